```python
import functools
import jax, jax.numpy as jnp
from jax import lax
import numpy as np

D_MODEL = 1024
BATCH = 1
SEQ = 16384
DEPTH = 2
DEC_BATCH = 32
DEC_SEQ = 4
PAST_LEN = 16384
PAGE_SIZE = 128

D_MIX = D_MODEL
GLA_HEADS = 4
GLA_DK = D_MODEL // 16
GLA_DV = D_MODEL // 16
GLA_WIDTH = GLA_HEADS * GLA_DV
GLA_GATE_RANK = 16
GLA_GATE_TEMP = 16.0
GLA_CHUNK = 64
SC_WIDTH = D_MODEL // 4
CONV_W = 3
NSA_HEADS = 8
NSA_KV_HEADS = 2
NSA_HD = D_MODEL // 16
NSA_WIDTH = NSA_HEADS * NSA_HD
NSA_KV_WIDTH = 2 * NSA_KV_HEADS * NSA_HD
CMP_STRIDE = 16
CMP_BLOCK = 2 * CMP_STRIDE
CMP_HIDDEN = 128
SEL_BLOCK = 64
SEL_TOPN = 16
WINDOW = 512
Q_BLOCK = 128
D_FF = 2816
EPS = 1e-6
NEG = -1e30
TINY = 1e-30
FORCE = 1e9

IN_SIZES = (
    GLA_HEADS * GLA_DK, GLA_HEADS * GLA_DK, GLA_WIDTH, GLA_WIDTH, GLA_GATE_RANK,
    SC_WIDTH, SC_WIDTH, SC_WIDTH,
    NSA_WIDTH, NSA_KV_WIDTH, NSA_KV_WIDTH, NSA_KV_WIDTH, NSA_HEADS * 3,
)
D_IN = sum(IN_SIZES)

kernel_name = 'hymba_gla_shortconv_nsa_convffn_adaln_step'


def rmsnorm(x, g):
    xf = x.astype(jnp.float32)
    y = xf * lax.rsqrt(jnp.mean(xf * xf, axis=-1, keepdims=True) + EPS)
    return (y * g.astype(jnp.float32)).astype(x.dtype)


def masked_softmax(s, mask):
    s = jnp.where(mask, s.astype(jnp.float32), NEG)
    m = jnp.max(s, axis=-1, keepdims=True)
    e = jnp.where(mask, jnp.exp(s - m), 0.0)
    return e / jnp.maximum(jnp.sum(e, axis=-1, keepdims=True), TINY)


def causal_dwconv(u, hist, w):
    T = u.shape[1]
    up = jnp.concatenate([hist.astype(u.dtype), u], axis=1)
    y = w[0] * up[:, 0:T]
    for k in range(1, CONV_W):
        y = y + w[k] * up[:, k:k + T]
    return y, up[:, -(CONV_W - 1):]


def split_proj(proj):
    offs = []
    acc = 0
    for s in IN_SIZES[:-1]:
        acc += s
        offs.append(acc)
    return jnp.split(proj, offs, axis=-1)


def gla_scan(q, k, v, log_a, s0):
    B, T, H = q.shape[:3]
    C = min(GLA_CHUNK, T)
    nc = -(-T // C)
    pad = nc * C - T

    def prep(a):
        a = jnp.pad(a.astype(jnp.float32), ((0, 0), (0, pad), (0, 0), (0, 0)))
        return a.reshape(B, nc, C, H, a.shape[-1]).transpose(1, 0, 3, 2, 4)

    qs, ks, vs, as_ = prep(q), prep(k), prep(v), prep(log_a)
    causal = jnp.tril(jnp.ones((C, C), bool))[:, :, None]

    def step(S, xs):
        qc, kc, vc, ac = xs
        b = jnp.cumsum(ac, axis=2)
        b_last = b[:, :, -1]
        inter = jnp.einsum('bhid,bhde->bhie', qc * jnp.exp(b), S)
        diff = b[:, :, :, None, :] - b[:, :, None, :, :]
        decay = jnp.where(causal, jnp.exp(jnp.where(causal, diff, 0.0)), 0.0)
        A = jnp.einsum('bhid,bhjd,bhijd->bhij', qc, kc, decay)
        intra = jnp.einsum('bhij,bhje->bhie', A, vc)
        S = jnp.exp(b_last)[..., None] * S + jnp.einsum(
            'bhjd,bhje->bhde', kc * jnp.exp(b_last[:, :, None, :] - b), vc)
        return S, inter + intra

    S, o = lax.scan(step, s0.astype(jnp.float32), (qs, ks, vs, as_))
    o = o.transpose(1, 0, 3, 2, 4).reshape(B, nc * C, H, v.shape[-1])[:, :T]
    return o, S


def nsa_compress(kv, pos_emb, w1, w2):
    B, T = kv.shape[:2]
    sub = kv.reshape(B, T // CMP_STRIDE, CMP_STRIDE, 2, NSA_KV_HEADS, NSA_HD)
    lead = jnp.einsum('bnlkgd,kldh->bnkgh', sub, w1[:, :CMP_STRIDE])
    trail = jnp.einsum('bnlkgd,kldh->bnkgh', sub, w1[:, CMP_STRIDE:])
    bias = jnp.einsum('kld,kldh->kh', pos_emb, w1)
    hid = jax.nn.gelu(lead[:, :-1] + trail[:, 1:] + bias[:, None, :])
    return jnp.einsum('bnkgh,khd->bnkgd', hid, w2)


def nsa_attend(q, pos, gates, kc, vc, fetch, win_k, win_v, win_pos):
    B, Qb = q.shape[:2]
    G, R = NSA_KV_HEADS, NSA_HEADS // NSA_KV_HEADS
    qg = q.reshape(B, Qb, G, R, NSA_HD) * (NSA_HD ** -0.5)
    n_cmp = kc.shape[1]
    n_sel = (n_cmp + 1) * CMP_STRIDE // SEL_BLOCK
    cmp_end = jnp.arange(n_cmp) * CMP_STRIDE + (CMP_BLOCK - 1)
    m_cmp = cmp_end[None, :] <= pos[:, None]
    p_cmp = masked_softmax(jnp.einsum('bqgrd,bngd->bgrqn', qg, kc), m_cmp)
    o_cmp = jnp.einsum('bgrqn,bngd->bqgrd', p_cmp.astype(vc.dtype), vc)
    pg = jnp.pad(p_cmp.sum(axis=2), ((0, 0), (0, 0), (0, 0), (0, 1)))
    p_sub = pg + jnp.pad(pg[..., :-1], ((0, 0), (0, 0), (0, 0), (1, 0)))
    p_slc = p_sub.reshape(B, G, Qb, n_sel, SEL_BLOCK // CMP_STRIDE).sum(-1)
    blk = jnp.arange(n_sel)[None, :]
    cur = (pos // SEL_BLOCK)[:, None]
    forced = (blk == 0) | (blk == cur) | (blk == cur - 1)
    allowed = blk * SEL_BLOCK <= pos[:, None]
    score = jnp.where(forced, FORCE, jnp.where(allowed, p_slc, -1.0))
    k_sel = min(SEL_TOPN, n_sel)
    _, idx = lax.top_k(score, k_sel)
    tok = (idx[..., None] * SEL_BLOCK + jnp.arange(SEL_BLOCK)).reshape(B, G, Qb, k_sel * SEL_BLOCK)
    m_slc = tok <= pos[None, None, :, None]
    ks, vs = fetch(tok)
    p_s = masked_softmax(jnp.einsum('bqgrd,bgqsd->bgrqs', qg, ks), m_slc[:, :, None])
    o_slc = jnp.einsum('bgrqs,bgqsd->bqgrd', p_s.astype(vs.dtype), vs)
    rel = pos[:, None] - win_pos[None, :]
    m_win = (win_pos[None, :] >= 0) & (rel >= 0) & (rel <= WINDOW)
    p_w = masked_softmax(jnp.einsum('bqgrd,bkgd->bgrqk', qg, win_k), m_win)
    o_win = jnp.einsum('bgrqk,bkgd->bqgrd', p_w.astype(win_v.dtype), win_v)
    g = jax.nn.sigmoid(gates.astype(jnp.float32)).reshape(B, Qb, G, R, 3)
    o = g[..., 0:1] * o_cmp + g[..., 1:2] * o_slc + g[..., 2:3] * o_win
    return o.reshape(B, Qb, NSA_WIDTH).astype(q.dtype)


def nsa_prompt(q, kv_cmp, kv_slc, kv_win, gates, cmp_params):
    B, T = q.shape[:2]
    kvc = nsa_compress(kv_cmp, *cmp_params)
    kc, vc = kvc[:, :, 0], kvc[:, :, 1]
    slc_t = kv_slc.transpose(0, 3, 1, 2, 4)

    def fetch(tok):
        rows = jax.vmap(jax.vmap(lambda a, i: a[i]))(slc_t, tok)
        return rows[..., 0, :], rows[..., 1, :]

    win_pad = jnp.pad(kv_win, ((0, 0), (WINDOW, 0), (0, 0), (0, 0), (0, 0)))

    def block(n):
        s0 = n * Q_BLOCK
        pos = s0 + jnp.arange(Q_BLOCK)
        qb = lax.dynamic_slice_in_dim(q, s0, Q_BLOCK, axis=1)
        gb = lax.dynamic_slice_in_dim(gates, s0, Q_BLOCK, axis=1)
        wk = lax.dynamic_slice_in_dim(win_pad, s0, WINDOW + Q_BLOCK, axis=1)
        wpos = s0 - WINDOW + jnp.arange(WINDOW + Q_BLOCK)
        return nsa_attend(qb, pos, gb, kc, vc, fetch, wk[:, :, 0], wk[:, :, 1], wpos)

    o = lax.map(block, jnp.arange(T // Q_BLOCK))
    o = o.transpose(1, 0, 2, 3).reshape(B, T, NSA_WIDTH)
    return o, kv_win[:, -min(WINDOW, T):]


def nsa_sample(q, kv_cmp, kv_slc, kv_win, gates, pool_cmp, pool_slc, page_table, win_buf, layer, cmp_params):
    B, Tn = q.shape[:2]
    n_pages = page_table.shape[1]
    P = n_pages * PAGE_SIZE
    pad = -(-Tn // SEL_BLOCK) * SEL_BLOCK - Tn
    past_cmp = pool_cmp[layer, page_table].reshape(B, P, 2, NSA_KV_HEADS, NSA_HD)
    full_cmp = jnp.concatenate(
        [past_cmp, kv_cmp.astype(past_cmp.dtype), jnp.zeros((B, pad, 2, NSA_KV_HEADS, NSA_HD), past_cmp.dtype)], axis=1)
    kvc = nsa_compress(full_cmp, *cmp_params)
    kc, vc = kvc[:, :, 0], kvc[:, :, 1]
    pos = P + jnp.arange(Tn)
    new_t = kv_slc.transpose(0, 3, 1, 2, 4)

    def fetch(tok):
        past = tok < P
        page = jax.vmap(lambda row, t: row[t])(page_table, jnp.minimum(tok // PAGE_SIZE, n_pages - 1))
        off = tok % PAGE_SIZE
        g_idx = jnp.arange(NSA_KV_HEADS)[None, :, None, None]
        past_rows = pool_slc[layer, page, off, :, g_idx]
        new_rows = jax.vmap(jax.vmap(lambda a, i: a[i]))(new_t, jnp.clip(tok - P, 0, Tn - 1))
        rows = jnp.where(past[..., None, None], past_rows, new_rows.astype(past_rows.dtype))
        return rows[..., 0, :], rows[..., 1, :]

    Wb = win_buf.shape[1]
    win_all = jnp.concatenate([win_buf, kv_win.astype(win_buf.dtype)], axis=1)
    wpos = P - Wb + jnp.arange(Wb + Tn)
    o = nsa_attend(q, pos, gates, kc, vc, fetch, win_all[:, :, 0], win_all[:, :, 1], wpos)
    return o, win_all[:, -min(WINDOW, Wb + Tn):]


def trunk_layer(x, c, weights, gla_s0, sc_hist, ffn_hist, nsa_apply):
    (mod_w, mod_b, norm_mix, norm_ffn, w_in, gla_wa2, gla_ba, gla_norm, sc_conv,
     w_out, ffn_up, ffn_conv, ffn_down) = weights
    B, T, _ = x.shape
    mod = jax.nn.silu(c) @ mod_w + mod_b
    ssh1, ssc1, sgt1, ssh2, ssc2, sgt2 = jnp.split(mod[:, None, :], 6, axis=-1)
    h = rmsnorm(x, norm_mix) * (1 + ssc1) + ssh1
    (gq, gk, gv, gg, ga, sb, scc, shh, nq, ncmp, nslc, nwin, ngate) = split_proj(h @ w_in)
    heads = lambda a, d: a.reshape(B, T, -1, d)
    log_a = jax.nn.log_sigmoid((ga @ gla_wa2 + gla_ba).astype(jnp.float32)) / GLA_GATE_TEMP
    o_gla, s_gla = gla_scan(heads(gq, GLA_DK) * (GLA_DK ** -0.5), heads(gk, GLA_DK),
                            heads(gv, GLA_DV), heads(log_a, GLA_DK), gla_s0)
    o_gla = rmsnorm(o_gla, gla_norm).astype(x.dtype).reshape(B, T, GLA_WIDTH) * jax.nn.silu(gg)
    conv_out, sc_state = causal_dwconv(scc * shh, sc_hist, sc_conv)
    o_sc = sb * conv_out
    kvr = lambda a: a.reshape(B, T, 2, NSA_KV_HEADS, NSA_HD)
    kv_cmp, kv_slc, kv_win = kvr(ncmp), kvr(nslc), kvr(nwin)
    o_nsa, win_state = nsa_apply(heads(nq, NSA_HD), kv_cmp, kv_slc, kv_win, heads(ngate, 3))
    x = x + sgt1 * (jnp.concatenate([o_gla, o_sc, o_nsa], axis=-1) @ w_out)
    h = rmsnorm(x, norm_ffn) * (1 + ssc2) + ssh2
    up, ffn_state = causal_dwconv(h @ ffn_up, ffn_hist, ffn_conv)
    a, b = jnp.split(up, 2, axis=-1)
    x = x + sgt2 * ((jax.nn.silu(a) * b) @ ffn_down)
    return x, kv_cmp, kv_slc, win_state, s_gla.astype(gla_s0.dtype), sc_state, ffn_state


def setup_inputs(seed: int = 0) -> dict:
    key = jax.random.key(seed)
    ks = jax.random.split(key, 32)
    n_pages = PAST_LEN // PAGE_SIZE
    n_used = DEC_BATCH * n_pages
    n_pool = (n_used * 5) // 4
    win_buf = min(WINDOW, PAST_LEN)
    nrm = lambda k, shape, s=1.0: s * jax.random.normal(k, shape, jnp.float32)
    page_table = jax.random.permutation(ks[8], n_pool)[:n_used].reshape(DEC_BATCH, n_pages).astype(jnp.int32)
    return {
        'x_prompt': nrm(ks[0], (BATCH, SEQ, D_MODEL)),
        'x_sample': nrm(ks[1], (DEC_BATCH, DEC_SEQ, D_MODEL)),
        'cache_nsa_cmp': nrm(ks[2], (DEPTH, n_pool, PAGE_SIZE, 2, NSA_KV_HEADS, NSA_HD)),
        'cache_nsa_slc': nrm(ks[3], (DEPTH, n_pool, PAGE_SIZE, 2, NSA_KV_HEADS, NSA_HD)),
        'cache_nsa_win': nrm(ks[4], (DEPTH, DEC_BATCH, win_buf, 2, NSA_KV_HEADS, NSA_HD)),
        'state_gla': nrm(ks[5], (DEPTH, DEC_BATCH, GLA_HEADS, GLA_DK, GLA_DV)),
        'state_shortconv': nrm(ks[6], (DEPTH, DEC_BATCH, CONV_W - 1, SC_WIDTH)),
        'state_ffn_conv': nrm(ks[7], (DEPTH, DEC_BATCH, CONV_W - 1, 2 * D_FF), 0.5),
        'page_table': page_table,
        'c_prompt': nrm(ks[9], (BATCH, D_MODEL)),
        'c_sample': nrm(ks[10], (DEC_BATCH, D_MODEL)),
        'mod_w': nrm(ks[11], (DEPTH, D_MODEL, 6 * D_MODEL), 0.5 * D_MODEL ** -0.5),
        'mod_b': nrm(ks[12], (DEPTH, 6 * D_MODEL), 0.01),
        'norm_mix': 1.0 + nrm(ks[13], (DEPTH, D_MODEL), 0.05),
        'norm_ffn': 1.0 + nrm(ks[14], (DEPTH, D_MODEL), 0.05),
        'w_in': nrm(ks[15], (DEPTH, D_MODEL, D_IN), D_MODEL ** -0.5),
        'gla_wa2': nrm(ks[16], (DEPTH, GLA_GATE_RANK, GLA_HEADS * GLA_DK), GLA_GATE_RANK ** -0.5),
        'gla_ba': nrm(ks[17], (DEPTH, GLA_HEADS * GLA_DK), 0.1),
        'gla_norm': 1.0 + nrm(ks[18], (DEPTH, GLA_DV), 0.05),
        'sc_conv': nrm(ks[19], (DEPTH, CONV_W, SC_WIDTH), CONV_W ** -0.5),
        'nsa_cmp_pos': nrm(ks[20], (DEPTH, 2, CMP_BLOCK, NSA_HD), 0.5),
        'nsa_cmp_w1': nrm(ks[21], (DEPTH, 2, CMP_BLOCK, NSA_HD, CMP_HIDDEN), (CMP_BLOCK * NSA_HD) ** -0.5),
        'nsa_cmp_w2': nrm(ks[22], (DEPTH, 2, CMP_HIDDEN, NSA_HD), CMP_HIDDEN ** -0.5),
        'w_out': nrm(ks[23], (DEPTH, D_MIX, D_MODEL), D_MIX ** -0.5),
        'ffn_up': nrm(ks[24], (DEPTH, D_MODEL, 2 * D_FF), D_MODEL ** -0.5),
        'ffn_conv': nrm(ks[25], (DEPTH, CONV_W, 2 * D_FF), CONV_W ** -0.5),
        'ffn_down': nrm(ks[26], (DEPTH, D_FF, D_MODEL), D_FF ** -0.5),
        'norm_final': 1.0 + nrm(ks[27], (D_MODEL,), 0.05),
    }


def reference(x_prompt, x_sample, cache_nsa_cmp, cache_nsa_slc, cache_nsa_win, state_gla,
              state_shortconv, state_ffn_conv, page_table, c_prompt, c_sample,
              mod_w, mod_b, norm_mix, norm_ffn, w_in, gla_wa2, gla_ba, gla_norm, sc_conv,
              nsa_cmp_pos, nsa_cmp_w1, nsa_cmp_w2, w_out, ffn_up, ffn_conv, ffn_down, norm_final):
    xp, xs = x_prompt, x_sample
    bp = xp.shape[0]
    cmp_p, cmp_s, slc_p, slc_s, win_p, win_s = [], [], [], [], [], []
    gla_p, gla_s, sc_p, sc_s, ffn_p, ffn_s = [], [], [], [], [], []
    for l in range(DEPTH):
        weights = (mod_w[l], mod_b[l], norm_mix[l], norm_ffn[l], w_in[l], gla_wa2[l], gla_ba[l],
                   gla_norm[l], sc_conv[l], w_out[l], ffn_up[l], ffn_conv[l], ffn_down[l])
        cmp_params = (nsa_cmp_pos[l], nsa_cmp_w1[l], nsa_cmp_w2[l])
        xp, kc, ksl, wst, gst, sst, fst = trunk_layer(
            xp, c_prompt, weights,
            jnp.zeros((bp, GLA_HEADS, GLA_DK, GLA_DV), xp.dtype),
            jnp.zeros((bp, CONV_W - 1, SC_WIDTH), xp.dtype),
            jnp.zeros((bp, CONV_W - 1, 2 * D_FF), xp.dtype),
            functools.partial(nsa_prompt, cmp_params=cmp_params))
        cmp_p.append(kc); slc_p.append(ksl); win_p.append(wst)
        gla_p.append(gst); sc_p.append(sst); ffn_p.append(fst)
        xs, kc, ksl, wst, gst, sst, fst = trunk_layer(
            xs, c_sample, weights, state_gla[l], state_shortconv[l], state_ffn_conv[l],
            functools.partial(nsa_sample, pool_cmp=cache_nsa_cmp, pool_slc=cache_nsa_slc,
                              page_table=page_table, win_buf=cache_nsa_win[l], layer=l,
                              cmp_params=cmp_params))
        cmp_s.append(kc); slc_s.append(ksl); win_s.append(wst)
        gla_s.append(gst); sc_s.append(sst); ffn_s.append(fst)
    y_prompt = rmsnorm(xp, norm_final)
    y_sample = rmsnorm(xs, norm_final)
    return (y_prompt, y_sample,
            jnp.stack(cmp_p), jnp.stack(cmp_s), jnp.stack(slc_p), jnp.stack(slc_s),
            jnp.stack(win_p), jnp.stack(win_s), jnp.stack(gla_p), jnp.stack(gla_s),
            jnp.stack(sc_p), jnp.stack(sc_s), jnp.stack(ffn_p), jnp.stack(ffn_s))
```

```python
import functools

import jax
import jax.numpy as jnp
from jax import lax
from jax.experimental import pallas as pl
from jax.experimental.pallas import tpu as pltpu

D_MODEL = 1024
DEPTH = 2
PAGE_SIZE = 128
GLA_HEADS = 4
GLA_DK = D_MODEL // 16
GLA_DV = D_MODEL // 16
GLA_WIDTH = GLA_HEADS * GLA_DV
GLA_GATE_RANK = 16
GLA_GATE_TEMP = 16.0
GLA_CHUNK = 64
SC_WIDTH = D_MODEL // 4
CONV_W = 3
NSA_HEADS = 8
NSA_KV_HEADS = 2
NSA_HD = D_MODEL // 16
NSA_WIDTH = NSA_HEADS * NSA_HD
NSA_KV_WIDTH = 2 * NSA_KV_HEADS * NSA_HD
CMP_STRIDE = 16
CMP_BLOCK = 2 * CMP_STRIDE
CMP_HIDDEN = 128
SEL_BLOCK = 64
SEL_TOPN = 16
WINDOW = 512
Q_BLOCK = 128
D_FF = 2816
EPS = 1e-6
NEG = -1e30
TINY = 1e-30
FORCE = 1e9

IN_SIZES = (
    GLA_HEADS * GLA_DK, GLA_HEADS * GLA_DK, GLA_WIDTH, GLA_WIDTH, GLA_GATE_RANK,
    SC_WIDTH, SC_WIDTH, SC_WIDTH,
    NSA_WIDTH, NSA_KV_WIDTH, NSA_KV_WIDTH, NSA_KV_WIDTH, NSA_HEADS * 3,
)

LANES = 128
SUBLANES = 8
VMEM_LIMIT_BYTES = 56 * 1024 * 1024

PROJ_ORDER = (0, 1, 2, 3, 5, 6, 7, 8, 9, 10, 11, 4, 12)


def _round_up(n, m):
    return -(-n // m) * m


def _proj_layout():
    src, acc = [], 0
    for s in IN_SIZES:
        src.append(acc)
        acc += s
    offs, dst = {}, 0
    for p in PROJ_ORDER:
        offs[p] = dst
        dst += _round_up(IN_SIZES[p], LANES)
    return src, offs, dst


PROJ_SRC, PROJ_OFF, PROJ_WIDTH = _proj_layout()


def _pack_w_in(w_in):
    out = jnp.zeros((D_MODEL, PROJ_WIDTH), jnp.bfloat16)
    for p in PROJ_ORDER:
        piece = w_in[:, PROJ_SRC[p]:PROJ_SRC[p] + IN_SIZES[p]].astype(jnp.bfloat16)
        out = lax.dynamic_update_slice(out, piece, (0, PROJ_OFF[p]))
    return out


def _proj_piece(proj, p):
    return proj[..., PROJ_OFF[p]:PROJ_OFF[p] + IN_SIZES[p]]


def _mod_kernel(c_ref, w_ref, b_ref, o_ref):
    c = c_ref[...]
    a = c * jax.nn.sigmoid(c)
    o_ref[0] = jnp.dot(a, w_ref[0], preferred_element_type=jnp.float32,
                       precision=lax.Precision.HIGHEST) + b_ref[0]


def _modulation(c_all, mod_w, mod_b):
    rows = c_all.shape[0]
    tn = 1024
    n = mod_w.shape[-1]
    return pl.pallas_call(
        _mod_kernel,
        grid=(DEPTH, n // tn),
        in_specs=[
            pl.BlockSpec((rows, D_MODEL), lambda l, j: (0, 0)),
            pl.BlockSpec((1, D_MODEL, tn), lambda l, j: (l, 0, j)),
            pl.BlockSpec((1, 1, tn), lambda l, j: (l, 0, j)),
        ],
        out_specs=pl.BlockSpec((1, rows, tn), lambda l, j: (l, 0, j)),
        out_shape=jax.ShapeDtypeStruct((DEPTH, rows, n), jnp.float32),
        name="adaln_modulation",
    )(c_all, mod_w, mod_b.reshape(DEPTH, 1, n))


def _norm_mod(x, g, sc, sh):
    r = lax.rsqrt(jnp.mean(x * x, axis=-1, keepdims=True) + EPS)
    return (x * r * g) * (1.0 + sc) + sh


def _in_proj_kernel(x_ref, g_ref, sc_ref, sh_ref, w_ref, o_ref):
    h = _norm_mod(x_ref[...], g_ref[...], sc_ref[...], sh_ref[...])
    o_ref[...] = jnp.dot(h.astype(jnp.bfloat16), w_ref[...], preferred_element_type=jnp.float32)


def _row_spec(tm, per_row):
    if per_row:
        return pl.BlockSpec((tm, D_MODEL), lambda i: (i, 0))
    return pl.BlockSpec((1, D_MODEL), lambda i: (0, 0))


def _resident(shape):
    return pl.BlockSpec(shape, lambda i: (0,) * len(shape), pipeline_mode=pl.Buffered(1))


def _in_proj(x, g, sc, sh, w_packed, tm):
    m = x.shape[0]
    per_row = sc.shape[0] != 1
    return pl.pallas_call(
        _in_proj_kernel,
        grid=(m // tm,),
        in_specs=[
            pl.BlockSpec((tm, D_MODEL), lambda i: (i, 0)),
            _resident((1, D_MODEL)),
            _row_spec(tm, per_row),
            _row_spec(tm, per_row),
            _resident((D_MODEL, PROJ_WIDTH)),
        ],
        out_specs=pl.BlockSpec((tm, PROJ_WIDTH), lambda i: (i, 0)),
        out_shape=jax.ShapeDtypeStruct((m, PROJ_WIDTH), jnp.float32),
        compiler_params=pltpu.CompilerParams(
            dimension_semantics=("arbitrary",), vmem_limit_bytes=VMEM_LIMIT_BYTES),
        name="norm_in_proj",
    )(x, g, sc, sh, w_packed)


def _out_proj_kernel(x_ref, mix_ref, gt_ref, w_ref, o_ref):
    y = jnp.dot(mix_ref[...].astype(jnp.bfloat16), w_ref[...], preferred_element_type=jnp.float32)
    o_ref[...] = x_ref[...] + gt_ref[...] * y


def _out_proj(x, mix, gt, w_bf16, tm):
    m = x.shape[0]
    per_row = gt.shape[0] != 1
    return pl.pallas_call(
        _out_proj_kernel,
        grid=(m // tm,),
        in_specs=[
            pl.BlockSpec((tm, D_MODEL), lambda i: (i, 0)),
            pl.BlockSpec((tm, D_MODEL), lambda i: (i, 0)),
            _row_spec(tm, per_row),
            _resident((D_MODEL, D_MODEL)),
        ],
        out_specs=pl.BlockSpec((tm, D_MODEL), lambda i: (i, 0)),
        out_shape=jax.ShapeDtypeStruct((m, D_MODEL), jnp.float32),
        compiler_params=pltpu.CompilerParams(
            dimension_semantics=("arbitrary",), vmem_limit_bytes=VMEM_LIMIT_BYTES),
        name="out_proj_residual",
    )(x, mix, gt, w_bf16)


FFN_UP_CHUNK = 512
FFN_ACT_CHUNK = 256
HIST_ROWS = SUBLANES


def _ffn_kernel(*refs, tm, group, final):
    grouped = group > 0
    if grouped:
        (x_ref, g_ref, sc_ref, sh_ref, gt_ref, wup_ref, cw_ref, wdn_ref, gf_ref,
         h1_ref, h2_ref, o_ref, st_ref, up_s) = refs
    else:
        (x_ref, g_ref, sc_ref, sh_ref, gt_ref, wup_ref, cw_ref, wdn_ref, gf_ref,
         h0_ref, o_ref, st_ref, up_s) = refs

        @pl.when(pl.program_id(0) == 0)
        def _():
            up_s[0:HIST_ROWS, :] = h0_ref[...]

    x = x_ref[...]
    h = _norm_mod(x, g_ref[...], sc_ref[...], sh_ref[...]).astype(jnp.bfloat16)
    for c in range(2 * D_FF // FFN_UP_CHUNK):
        cols = slice(c * FFN_UP_CHUNK, (c + 1) * FFN_UP_CHUNK)
        up_s[HIST_ROWS:HIST_ROWS + tm, cols] = jnp.dot(
            h, wup_ref[:, cols], preferred_element_type=jnp.float32)

    if grouped:
        t = lax.broadcasted_iota(jnp.int32, (tm, 1), 0) % group

    def conv(cols):
        cur = up_s[HIST_ROWS:HIST_ROWS + tm, cols]
        p1 = up_s[HIST_ROWS - 1:HIST_ROWS - 1 + tm, cols]
        p2 = up_s[HIST_ROWS - 2:HIST_ROWS - 2 + tm, cols]
        if grouped:
            p1 = jnp.where(t == 0, h1_ref[:, cols], p1)
            p2 = jnp.where(t <= 1, h2_ref[:, cols], p2)
        return cw_ref[0:1, cols] * p2 + cw_ref[1:2, cols] * p1 + cw_ref[2:3, cols] * cur

    acc = jnp.zeros((tm, D_MODEL), jnp.float32)
    for c in range(D_FF // FFN_ACT_CHUNK):
        a = conv(slice(c * FFN_ACT_CHUNK, (c + 1) * FFN_ACT_CHUNK))
        b = conv(slice(D_FF + c * FFN_ACT_CHUNK, D_FF + (c + 1) * FFN_ACT_CHUNK))
        act = (a * jax.nn.sigmoid(a) * b).astype(jnp.bfloat16)
        acc = acc + jnp.dot(act, wdn_ref[c * FFN_ACT_CHUNK:(c + 1) * FFN_ACT_CHUNK, :],
                            preferred_element_type=jnp.float32)
    y = x + gt_ref[...] * acc
    if final:
        r = lax.rsqrt(jnp.mean(y * y, axis=-1, keepdims=True) + EPS)
        y = y * r * gf_ref[...]
    o_ref[...] = y

    if grouped:
        st_ref[...] = up_s[HIST_ROWS:HIST_ROWS + tm, :]
    else:
        tail = up_s[tm:tm + HIST_ROWS, :]
        st_ref[...] = tail
        up_s[0:HIST_ROWS, :] = tail


def _ffn(x, g, sc, sh, gt, wup, cw, wdn, g_final, hist, tm, group, final):
    m = x.shape[0]
    grouped = group > 0
    per_row = sc.shape[0] != 1
    ff2 = 2 * D_FF
    in_specs = [
        pl.BlockSpec((tm, D_MODEL), lambda i: (i, 0)),
        _resident((1, D_MODEL)),
        _row_spec(tm, per_row), _row_spec(tm, per_row), _row_spec(tm, per_row),
        _resident((D_MODEL, ff2)),
        _resident((CONV_W, ff2)),
        _resident((D_FF, D_MODEL)),
        _resident((1, D_MODEL)),
    ]
    if grouped:
        assert m == tm
        in_specs += [_resident((tm, ff2)), _resident((tm, ff2))]
        hist_args = tuple(hist)
        st_rows = tm
    else:
        in_specs += [_resident((HIST_ROWS, ff2))]
        hist_args = (hist,)
        st_rows = HIST_ROWS
    return pl.pallas_call(
        functools.partial(_ffn_kernel, tm=tm, group=group, final=final),
        grid=(m // tm,),
        in_specs=in_specs,
        out_specs=[pl.BlockSpec((tm, D_MODEL), lambda i: (i, 0)),
                   pl.BlockSpec((st_rows, ff2), lambda i: (0, 0))],
        out_shape=[jax.ShapeDtypeStruct((m, D_MODEL), jnp.float32),
                   jax.ShapeDtypeStruct((st_rows, ff2), jnp.float32)],
        scratch_shapes=[pltpu.VMEM((HIST_ROWS + tm, ff2), jnp.float32)],
        compiler_params=pltpu.CompilerParams(
            dimension_semantics=("arbitrary",), vmem_limit_bytes=VMEM_LIMIT_BYTES),
        name="conv_ffn",
    )(x, g, sc, sh, gt, wup, cw, wdn, g_final, *hist_args)


def _rmsnorm(x, g):
    xf = x.astype(jnp.float32)
    y = xf * lax.rsqrt(jnp.mean(xf * xf, axis=-1, keepdims=True) + EPS)
    return (y * g.astype(jnp.float32)).astype(x.dtype)


def _masked_softmax(s, mask):
    s = jnp.where(mask, s.astype(jnp.float32), NEG)
    m = jnp.max(s, axis=-1, keepdims=True)
    e = jnp.where(mask, jnp.exp(s - m), 0.0)
    return e / jnp.maximum(jnp.sum(e, axis=-1, keepdims=True), TINY)


def _causal_dwconv(u, hist, w):
    T = u.shape[1]
    up = jnp.concatenate([hist.astype(u.dtype), u], axis=1)
    y = w[0] * up[:, 0:T]
    for k in range(1, CONV_W):
        y = y + w[k] * up[:, k:k + T]
    return y, up[:, -(CONV_W - 1):]


def _gla_scan(q, k, v, log_a, s0):
    B, T, H = q.shape[:3]
    C = min(GLA_CHUNK, T)
    nc = -(-T // C)
    pad = nc * C - T

    def prep(a):
        a = jnp.pad(a.astype(jnp.float32), ((0, 0), (0, pad), (0, 0), (0, 0)))
        return a.reshape(B, nc, C, H, a.shape[-1]).transpose(1, 0, 3, 2, 4)

    qs, ks, vs, as_ = prep(q), prep(k), prep(v), prep(log_a)
    causal = jnp.tril(jnp.ones((C, C), bool))[:, :, None]

    def step(S, xs):
        qc, kc, vc, ac = xs
        b = jnp.cumsum(ac, axis=2)
        b_last = b[:, :, -1]
        inter = jnp.einsum('bhid,bhde->bhie', qc * jnp.exp(b), S)
        diff = b[:, :, :, None, :] - b[:, :, None, :, :]
        decay = jnp.where(causal, jnp.exp(jnp.where(causal, diff, 0.0)), 0.0)
        A = jnp.einsum('bhid,bhjd,bhijd->bhij', qc, kc, decay)
        intra = jnp.einsum('bhij,bhje->bhie', A, vc)
        S = jnp.exp(b_last)[..., None] * S + jnp.einsum(
            'bhjd,bhje->bhde', kc * jnp.exp(b_last[:, :, None, :] - b), vc)
        return S, inter + intra

    S, o = lax.scan(step, s0.astype(jnp.float32), (qs, ks, vs, as_))
    o = o.transpose(1, 0, 3, 2, 4).reshape(B, nc * C, H, v.shape[-1])[:, :T]
    return o, S


def _nsa_compress(kv, pos_emb, w1, w2):
    B, T = kv.shape[:2]
    sub = kv.reshape(B, T // CMP_STRIDE, CMP_STRIDE, 2, NSA_KV_HEADS, NSA_HD)
    lead = jnp.einsum('bnlkgd,kldh->bnkgh', sub, w1[:, :CMP_STRIDE])
    trail = jnp.einsum('bnlkgd,kldh->bnkgh', sub, w1[:, CMP_STRIDE:])
    bias = jnp.einsum('kld,kldh->kh', pos_emb, w1)
    hid = jax.nn.gelu(lead[:, :-1] + trail[:, 1:] + bias[:, None, :])
    return jnp.einsum('bnkgh,khd->bnkgd', hid, w2)


def _nsa_attend(q, pos, gates, kc, vc, fetch, win_k, win_v, win_pos):
    B, Qb = q.shape[:2]
    G, R = NSA_KV_HEADS, NSA_HEADS // NSA_KV_HEADS
    qg = q.reshape(B, Qb, G, R, NSA_HD) * (NSA_HD ** -0.5)
    n_cmp = kc.shape[1]
    n_sel = (n_cmp + 1) * CMP_STRIDE // SEL_BLOCK
    cmp_end = jnp.arange(n_cmp) * CMP_STRIDE + (CMP_BLOCK - 1)
    m_cmp = cmp_end[None, :] <= pos[:, None]
    p_cmp = _masked_softmax(jnp.einsum('bqgrd,bngd->bgrqn', qg, kc), m_cmp)
    o_cmp = jnp.einsum('bgrqn,bngd->bqgrd', p_cmp.astype(vc.dtype), vc)
    pg = jnp.pad(p_cmp.sum(axis=2), ((0, 0), (0, 0), (0, 0), (0, 1)))
    p_sub = pg + jnp.pad(pg[..., :-1], ((0, 0), (0, 0), (0, 0), (1, 0)))
    p_slc = p_sub.reshape(B, G, Qb, n_sel, SEL_BLOCK // CMP_STRIDE).sum(-1)
    blk = jnp.arange(n_sel)[None, :]
    cur = (pos // SEL_BLOCK)[:, None]
    forced = (blk == 0) | (blk == cur) | (blk == cur - 1)
    allowed = blk * SEL_BLOCK <= pos[:, None]
    score = jnp.where(forced, FORCE, jnp.where(allowed, p_slc, -1.0))
    k_sel = min(SEL_TOPN, n_sel)
    _, idx = lax.top_k(score, k_sel)
    tok = (idx[..., None] * SEL_BLOCK + jnp.arange(SEL_BLOCK)).reshape(B, G, Qb, k_sel * SEL_BLOCK)
    m_slc = tok <= pos[None, None, :, None]
    ks, vs = fetch(tok)
    p_s = _masked_softmax(jnp.einsum('bqgrd,bgqsd->bgrqs', qg, ks), m_slc[:, :, None])
    o_slc = jnp.einsum('bgrqs,bgqsd->bqgrd', p_s.astype(vs.dtype), vs)
    rel = pos[:, None] - win_pos[None, :]
    m_win = (win_pos[None, :] >= 0) & (rel >= 0) & (rel <= WINDOW)
    p_w = _masked_softmax(jnp.einsum('bqgrd,bkgd->bgrqk', qg, win_k), m_win)
    o_win = jnp.einsum('bgrqk,bkgd->bqgrd', p_w.astype(win_v.dtype), win_v)
    g = jax.nn.sigmoid(gates.astype(jnp.float32)).reshape(B, Qb, G, R, 3)
    o = g[..., 0:1] * o_cmp + g[..., 1:2] * o_slc + g[..., 2:3] * o_win
    return o.reshape(B, Qb, NSA_WIDTH).astype(q.dtype)


def _nsa_prompt(q, kv_cmp, kv_slc, kv_win, gates, cmp_params):
    B, T = q.shape[:2]
    kvc = _nsa_compress(kv_cmp, *cmp_params)
    kc, vc = kvc[:, :, 0], kvc[:, :, 1]
    slc_t = kv_slc.transpose(0, 3, 1, 2, 4)

    def fetch(tok):
        rows = jax.vmap(jax.vmap(lambda a, i: a[i]))(slc_t, tok)
        return rows[..., 0, :], rows[..., 1, :]

    win_pad = jnp.pad(kv_win, ((0, 0), (WINDOW, 0), (0, 0), (0, 0), (0, 0)))

    def block(n):
        s0 = n * Q_BLOCK
        pos = s0 + jnp.arange(Q_BLOCK)
        qb = lax.dynamic_slice_in_dim(q, s0, Q_BLOCK, axis=1)
        gb = lax.dynamic_slice_in_dim(gates, s0, Q_BLOCK, axis=1)
        wk = lax.dynamic_slice_in_dim(win_pad, s0, WINDOW + Q_BLOCK, axis=1)
        wpos = s0 - WINDOW + jnp.arange(WINDOW + Q_BLOCK)
        return _nsa_attend(qb, pos, gb, kc, vc, fetch, wk[:, :, 0], wk[:, :, 1], wpos)

    o = lax.map(block, jnp.arange(T // Q_BLOCK))
    o = o.transpose(1, 0, 2, 3).reshape(B, T, NSA_WIDTH)
    return o, kv_win[:, -min(WINDOW, T):]


def _nsa_sample(q, kv_cmp, kv_slc, kv_win, gates, pool_cmp, pool_slc, page_table, win_buf, layer, cmp_params):
    B, Tn = q.shape[:2]
    n_pages = page_table.shape[1]
    P = n_pages * PAGE_SIZE
    pad = -(-Tn // SEL_BLOCK) * SEL_BLOCK - Tn
    past_cmp = pool_cmp[layer, page_table].reshape(B, P, 2, NSA_KV_HEADS, NSA_HD)
    full_cmp = jnp.concatenate(
        [past_cmp, kv_cmp.astype(past_cmp.dtype), jnp.zeros((B, pad, 2, NSA_KV_HEADS, NSA_HD), past_cmp.dtype)], axis=1)
    kvc = _nsa_compress(full_cmp, *cmp_params)
    kc, vc = kvc[:, :, 0], kvc[:, :, 1]
    pos = P + jnp.arange(Tn)
    new_t = kv_slc.transpose(0, 3, 1, 2, 4)

    def fetch(tok):
        past = tok < P
        page = jax.vmap(lambda row, t: row[t])(page_table, jnp.minimum(tok // PAGE_SIZE, n_pages - 1))
        off = tok % PAGE_SIZE
        g_idx = jnp.arange(NSA_KV_HEADS)[None, :, None, None]
        past_rows = pool_slc[layer, page, off, :, g_idx]
        new_rows = jax.vmap(jax.vmap(lambda a, i: a[i]))(new_t, jnp.clip(tok - P, 0, Tn - 1))
        rows = jnp.where(past[..., None, None], past_rows, new_rows.astype(past_rows.dtype))
        return rows[..., 0, :], rows[..., 1, :]

    Wb = win_buf.shape[1]
    win_all = jnp.concatenate([win_buf, kv_win.astype(win_buf.dtype)], axis=1)
    wpos = P - Wb + jnp.arange(Wb + Tn)
    o = _nsa_attend(q, pos, gates, kc, vc, fetch, win_all[:, :, 0], win_all[:, :, 1], wpos)
    return o, win_all[:, -min(WINDOW, Wb + Tn):]


def _expand_rows(v, t):
    if v.shape[0] == 1:
        return v
    return jnp.repeat(v, t, axis=0)


def _trunk_layer(x, mod, lw, gla_s0, sc_hist, ffn_hist, nsa_apply, tm, final, g_final):
    B, T, _ = x.shape
    m = B * T
    grouped = B > 1
    ssh1, ssc1, sgt1, ssh2, ssc2, sgt2 = [_expand_rows(v, T) for v in jnp.split(mod, 6, axis=-1)]
    x2 = x.reshape(m, D_MODEL)
    proj = _in_proj(x2, lw['norm_mix'], ssc1, ssh1, lw['w_in'], tm).reshape(B, T, PROJ_WIDTH)
    gq, gk, gv, gg, ga, sb, scc, shh, nq, ncmp, nslc, nwin, ngate = [_proj_piece(proj, p) for p in range(13)]
    heads = lambda a, d: a.reshape(B, T, -1, d)
    log_a = jax.nn.log_sigmoid((ga @ lw['gla_wa2'] + lw['gla_ba']).astype(jnp.float32)) / GLA_GATE_TEMP
    o_gla, s_gla = _gla_scan(heads(gq, GLA_DK) * (GLA_DK ** -0.5), heads(gk, GLA_DK),
                             heads(gv, GLA_DV), heads(log_a, GLA_DK), gla_s0)
    o_gla = _rmsnorm(o_gla, lw['gla_norm']).astype(x.dtype).reshape(B, T, GLA_WIDTH) * jax.nn.silu(gg)
    conv_out, sc_state = _causal_dwconv(scc * shh, sc_hist, lw['sc_conv'])
    o_sc = sb * conv_out
    kvr = lambda a: a.reshape(B, T, 2, NSA_KV_HEADS, NSA_HD)
    kv_cmp, kv_slc, kv_win = kvr(ncmp), kvr(nslc), kvr(nwin)
    o_nsa, win_state = nsa_apply(heads(nq, NSA_HD), kv_cmp, kv_slc, kv_win, heads(ngate, 3))
    mix = jnp.concatenate([o_gla, o_sc, o_nsa], axis=-1).reshape(m, D_MODEL)
    x2 = _out_proj(x2, mix, sgt1, lw['w_out'], tm)

    ff2 = 2 * D_FF
    if grouped:
        zero = jnp.zeros((B, 1, ff2), jnp.float32)
        h1 = jnp.concatenate([ffn_hist[:, 1:2], zero, zero, zero], axis=1).reshape(m, ff2)
        h2 = jnp.concatenate([ffn_hist[:, 0:1], ffn_hist[:, 1:2], zero, zero], axis=1).reshape(m, ff2)
        hist = (h1, h2)
    else:
        hist = jnp.concatenate(
            [jnp.zeros((HIST_ROWS - (CONV_W - 1), ff2), jnp.float32), ffn_hist[0]], axis=0)
    y, st = _ffn(x2, lw['norm_ffn'], ssc2, ssh2, sgt2, lw['ffn_up'], lw['ffn_conv'], lw['ffn_down'],
                 g_final, hist, tm, T if grouped else 0, final)
    if grouped:
        ffn_state = st.reshape(B, T, ff2)[:, -(CONV_W - 1):]
    else:
        ffn_state = st[None, -(CONV_W - 1):]
    return (y.reshape(B, T, D_MODEL), kv_cmp, kv_slc, win_state, s_gla.astype(gla_s0.dtype), sc_state, ffn_state)


def kernel(x_prompt, x_sample, cache_nsa_cmp, cache_nsa_slc, cache_nsa_win, state_gla, state_shortconv, state_ffn_conv, page_table, c_prompt, c_sample, mod_w, mod_b, norm_mix, norm_ffn, w_in, gla_wa2, gla_ba, gla_norm, sc_conv, nsa_cmp_pos, nsa_cmp_w1, nsa_cmp_w2, w_out, ffn_up, ffn_conv, ffn_down, norm_final):
    xp, xs = x_prompt, x_sample
    bp, bs = xp.shape[0], xs.shape[0]
    assert bp == 1 and xs.shape[1] == 4
    c_rows = _round_up(bp + bs, SUBLANES)
    c_all = jnp.concatenate([c_prompt, c_sample, jnp.zeros((c_rows - bp - bs, D_MODEL), jnp.float32)], axis=0)
    mod_all = _modulation(c_all, mod_w, mod_b)
    g_final = norm_final.reshape(1, D_MODEL)
    outs = [[] for _ in range(12)]
    for l in range(DEPTH):
        lw = dict(
            norm_mix=norm_mix[l].reshape(1, D_MODEL), norm_ffn=norm_ffn[l].reshape(1, D_MODEL),
            w_in=_pack_w_in(w_in[l]), gla_wa2=gla_wa2[l], gla_ba=gla_ba[l], gla_norm=gla_norm[l],
            sc_conv=sc_conv[l], w_out=w_out[l].astype(jnp.bfloat16),
            ffn_up=ffn_up[l].astype(jnp.bfloat16), ffn_conv=ffn_conv[l],
            ffn_down=ffn_down[l].astype(jnp.bfloat16))
        cmp_params = (nsa_cmp_pos[l], nsa_cmp_w1[l], nsa_cmp_w2[l])
        final = l == DEPTH - 1
        res_p = _trunk_layer(
            xp, mod_all[l, 0:bp], lw,
            jnp.zeros((bp, GLA_HEADS, GLA_DK, GLA_DV), xp.dtype),
            jnp.zeros((bp, CONV_W - 1, SC_WIDTH), xp.dtype),
            jnp.zeros((bp, CONV_W - 1, 2 * D_FF), xp.dtype),
            functools.partial(_nsa_prompt, cmp_params=cmp_params), 256, final, g_final)
        res_s = _trunk_layer(
            xs, mod_all[l, bp:bp + bs], lw, state_gla[l], state_shortconv[l], state_ffn_conv[l],
            functools.partial(_nsa_sample, pool_cmp=cache_nsa_cmp, pool_slc=cache_nsa_slc,
                              page_table=page_table, win_buf=cache_nsa_win[l], layer=l,
                              cmp_params=cmp_params), bs * xs.shape[1], final, g_final)
        xp, xs = res_p[0], res_s[0]
        for k in range(6):
            outs[2 * k].append(res_p[k + 1])
            outs[2 * k + 1].append(res_s[k + 1])
    return (xp, xs) + tuple(jnp.stack(o) for o in outs)
```

```python
import functools

import jax
import jax.numpy as jnp
from jax import lax
from jax.experimental import pallas as pl
from jax.experimental.pallas import tpu as pltpu

D_MODEL = 1024
DEPTH = 2
PAGE_SIZE = 128
GLA_HEADS = 4
GLA_DK = D_MODEL // 16
GLA_DV = D_MODEL // 16
GLA_WIDTH = GLA_HEADS * GLA_DV
GLA_GATE_RANK = 16
GLA_GATE_TEMP = 16.0
GLA_CHUNK = 64
SC_WIDTH = D_MODEL // 4
CONV_W = 3
NSA_HEADS = 8
NSA_KV_HEADS = 2
NSA_HD = D_MODEL // 16
NSA_WIDTH = NSA_HEADS * NSA_HD
NSA_KV_WIDTH = 2 * NSA_KV_HEADS * NSA_HD
CMP_STRIDE = 16
CMP_BLOCK = 2 * CMP_STRIDE
CMP_HIDDEN = 128
SEL_BLOCK = 64
SEL_TOPN = 16
WINDOW = 512
Q_BLOCK = 128
D_FF = 2816
EPS = 1e-6
NEG = -1e30
TINY = 1e-30
FORCE = 1e9

IN_SIZES = (
    GLA_HEADS * GLA_DK, GLA_HEADS * GLA_DK, GLA_WIDTH, GLA_WIDTH, GLA_GATE_RANK,
    SC_WIDTH, SC_WIDTH, SC_WIDTH,
    NSA_WIDTH, NSA_KV_WIDTH, NSA_KV_WIDTH, NSA_KV_WIDTH, NSA_HEADS * 3,
)

LANES = 128
SUBLANES = 8
VMEM_LIMIT_BYTES = 56 * 1024 * 1024

PROJ_ORDER = (0, 1, 2, 3, 5, 6, 7, 8, 9, 10, 11, 4, 12)


def _round_up(n, m):
    return -(-n // m) * m


def _proj_layout():
    src, acc = [], 0
    for s in IN_SIZES:
        src.append(acc)
        acc += s
    offs, dst = {}, 0
    for p in PROJ_ORDER:
        offs[p] = dst
        dst += _round_up(IN_SIZES[p], LANES)
    return src, offs, dst


PROJ_SRC, PROJ_OFF, PROJ_WIDTH = _proj_layout()


def _pack_w_in(w_in):
    out = jnp.zeros((D_MODEL, PROJ_WIDTH), jnp.bfloat16)
    for p in PROJ_ORDER:
        piece = w_in[:, PROJ_SRC[p]:PROJ_SRC[p] + IN_SIZES[p]].astype(jnp.bfloat16)
        out = lax.dynamic_update_slice(out, piece, (0, PROJ_OFF[p]))
    return out


def _proj_piece(proj, p):
    return proj[..., PROJ_OFF[p]:PROJ_OFF[p] + IN_SIZES[p]]


def _mod_kernel(c_ref, w_ref, b_ref, o_ref):
    c = c_ref[...]
    a = c * jax.nn.sigmoid(c)
    o_ref[0] = jnp.dot(a, w_ref[0], preferred_element_type=jnp.float32,
                       precision=lax.Precision.HIGHEST) + b_ref[0]


def _modulation(c_all, mod_w, mod_b):
    rows = c_all.shape[0]
    tn = 1024
    n = mod_w.shape[-1]
    return pl.pallas_call(
        _mod_kernel,
        grid=(DEPTH, n // tn),
        in_specs=[
            pl.BlockSpec((rows, D_MODEL), lambda l, j: (0, 0)),
            pl.BlockSpec((1, D_MODEL, tn), lambda l, j: (l, 0, j)),
            pl.BlockSpec((1, 1, tn), lambda l, j: (l, 0, j)),
        ],
        out_specs=pl.BlockSpec((1, rows, tn), lambda l, j: (l, 0, j)),
        out_shape=jax.ShapeDtypeStruct((DEPTH, rows, n), jnp.float32),
        name="adaln_modulation",
    )(c_all, mod_w, mod_b.reshape(DEPTH, 1, n))


def _norm_mod(x, g, sc, sh):
    r = lax.rsqrt(jnp.mean(x * x, axis=-1, keepdims=True) + EPS)
    return (x * r * g) * (1.0 + sc) + sh


def _in_proj_kernel(x_ref, g_ref, sc_ref, sh_ref, w_ref, o_ref):
    h = _norm_mod(x_ref[...], g_ref[...], sc_ref[...], sh_ref[...])
    o_ref[...] = jnp.dot(h.astype(jnp.bfloat16), w_ref[...], preferred_element_type=jnp.float32)


def _row_spec(tm, per_row):
    if per_row:
        return pl.BlockSpec((tm, D_MODEL), lambda i: (i, 0))
    return pl.BlockSpec((1, D_MODEL), lambda i: (0, 0))


def _resident(shape):
    return pl.BlockSpec(shape, lambda i: (0,) * len(shape), pipeline_mode=pl.Buffered(1))


def _in_proj(x, g, sc, sh, w_packed, tm):
    m = x.shape[0]
    per_row = sc.shape[0] != 1
    return pl.pallas_call(
        _in_proj_kernel,
        grid=(m // tm,),
        in_specs=[
            pl.BlockSpec((tm, D_MODEL), lambda i: (i, 0)),
            _resident((1, D_MODEL)),
            _row_spec(tm, per_row),
            _row_spec(tm, per_row),
            _resident((D_MODEL, PROJ_WIDTH)),
        ],
        out_specs=pl.BlockSpec((tm, PROJ_WIDTH), lambda i: (i, 0)),
        out_shape=jax.ShapeDtypeStruct((m, PROJ_WIDTH), jnp.float32),
        compiler_params=pltpu.CompilerParams(
            dimension_semantics=("arbitrary",), vmem_limit_bytes=VMEM_LIMIT_BYTES),
        name="norm_in_proj",
    )(x, g, sc, sh, w_packed)


def _out_proj_kernel(x_ref, mix_ref, gt_ref, w_ref, o_ref):
    y = jnp.dot(mix_ref[...].astype(jnp.bfloat16), w_ref[...], preferred_element_type=jnp.float32)
    o_ref[...] = x_ref[...] + gt_ref[...] * y


def _out_proj(x, mix, gt, w_bf16, tm):
    m = x.shape[0]
    per_row = gt.shape[0] != 1
    return pl.pallas_call(
        _out_proj_kernel,
        grid=(m // tm,),
        in_specs=[
            pl.BlockSpec((tm, D_MODEL), lambda i: (i, 0)),
            pl.BlockSpec((tm, D_MODEL), lambda i: (i, 0)),
            _row_spec(tm, per_row),
            _resident((D_MODEL, D_MODEL)),
        ],
        out_specs=pl.BlockSpec((tm, D_MODEL), lambda i: (i, 0)),
        out_shape=jax.ShapeDtypeStruct((m, D_MODEL), jnp.float32),
        compiler_params=pltpu.CompilerParams(
            dimension_semantics=("arbitrary",), vmem_limit_bytes=VMEM_LIMIT_BYTES),
        name="out_proj_residual",
    )(x, mix, gt, w_bf16)


FFN_UP_CHUNK = 512
FFN_ACT_CHUNK = 256
HIST_ROWS = SUBLANES


def _ffn_kernel(*refs, tm, group, final):
    grouped = group > 0
    if grouped:
        (x_ref, g_ref, sc_ref, sh_ref, gt_ref, wup_ref, cw_ref, wdn_ref, gf_ref,
         h1_ref, h2_ref, o_ref, st_ref, up_s) = refs
    else:
        (x_ref, g_ref, sc_ref, sh_ref, gt_ref, wup_ref, cw_ref, wdn_ref, gf_ref,
         h0_ref, o_ref, st_ref, up_s) = refs

        @pl.when(pl.program_id(0) == 0)
        def _():
            up_s[0:HIST_ROWS, :] = h0_ref[...]

    x = x_ref[...]
    h = _norm_mod(x, g_ref[...], sc_ref[...], sh_ref[...]).astype(jnp.bfloat16)
    for c in range(2 * D_FF // FFN_UP_CHUNK):
        cols = slice(c * FFN_UP_CHUNK, (c + 1) * FFN_UP_CHUNK)
        up_s[HIST_ROWS:HIST_ROWS + tm, cols] = jnp.dot(
            h, wup_ref[:, cols], preferred_element_type=jnp.float32)

    if grouped:
        t = lax.broadcasted_iota(jnp.int32, (tm, 1), 0) % group

    def conv(cols):
        cur = up_s[HIST_ROWS:HIST_ROWS + tm, cols]
        p1 = up_s[HIST_ROWS - 1:HIST_ROWS - 1 + tm, cols]
        p2 = up_s[HIST_ROWS - 2:HIST_ROWS - 2 + tm, cols]
        if grouped:
            p1 = jnp.where(t == 0, h1_ref[:, cols], p1)
            p2 = jnp.where(t <= 1, h2_ref[:, cols], p2)
        return cw_ref[0:1, cols] * p2 + cw_ref[1:2, cols] * p1 + cw_ref[2:3, cols] * cur

    acc = jnp.zeros((tm, D_MODEL), jnp.float32)
    for c in range(D_FF // FFN_ACT_CHUNK):
        a = conv(slice(c * FFN_ACT_CHUNK, (c + 1) * FFN_ACT_CHUNK))
        b = conv(slice(D_FF + c * FFN_ACT_CHUNK, D_FF + (c + 1) * FFN_ACT_CHUNK))
        act = (a * jax.nn.sigmoid(a) * b).astype(jnp.bfloat16)
        acc = acc + jnp.dot(act, wdn_ref[c * FFN_ACT_CHUNK:(c + 1) * FFN_ACT_CHUNK, :],
                            preferred_element_type=jnp.float32)
    y = x + gt_ref[...] * acc
    if final:
        r = lax.rsqrt(jnp.mean(y * y, axis=-1, keepdims=True) + EPS)
        y = y * r * gf_ref[...]
    o_ref[...] = y

    if grouped:
        st_ref[...] = up_s[HIST_ROWS:HIST_ROWS + tm, :]
    else:
        tail = up_s[tm:tm + HIST_ROWS, :]
        st_ref[...] = tail
        up_s[0:HIST_ROWS, :] = tail


def _ffn(x, g, sc, sh, gt, wup, cw, wdn, g_final, hist, tm, group, final):
    m = x.shape[0]
    grouped = group > 0
    per_row = sc.shape[0] != 1
    ff2 = 2 * D_FF
    in_specs = [
        pl.BlockSpec((tm, D_MODEL), lambda i: (i, 0)),
        _resident((1, D_MODEL)),
        _row_spec(tm, per_row), _row_spec(tm, per_row), _row_spec(tm, per_row),
        _resident((D_MODEL, ff2)),
        _resident((CONV_W, ff2)),
        _resident((D_FF, D_MODEL)),
        _resident((1, D_MODEL)),
    ]
    if grouped:
        assert m == tm
        in_specs += [_resident((tm, ff2)), _resident((tm, ff2))]
        hist_args = tuple(hist)
        st_rows = tm
    else:
        in_specs += [_resident((HIST_ROWS, ff2))]
        hist_args = (hist,)
        st_rows = HIST_ROWS
    return pl.pallas_call(
        functools.partial(_ffn_kernel, tm=tm, group=group, final=final),
        grid=(m // tm,),
        in_specs=in_specs,
        out_specs=[pl.BlockSpec((tm, D_MODEL), lambda i: (i, 0)),
                   pl.BlockSpec((st_rows, ff2), lambda i: (0, 0))],
        out_shape=[jax.ShapeDtypeStruct((m, D_MODEL), jnp.float32),
                   jax.ShapeDtypeStruct((st_rows, ff2), jnp.float32)],
        scratch_shapes=[pltpu.VMEM((HIST_ROWS + tm, ff2), jnp.float32)],
        compiler_params=pltpu.CompilerParams(
            dimension_semantics=("arbitrary",), vmem_limit_bytes=VMEM_LIMIT_BYTES),
        name="conv_ffn",
    )(x, g, sc, sh, gt, wup, cw, wdn, g_final, *hist_args)


CMP_ROW = CMP_STRIDE * NSA_KV_WIDTH
CMP_HID = 2 * NSA_KV_HEADS * CMP_HIDDEN


def _gelu_tanh(x):
    return 0.5 * x * (1.0 + jnp.tanh(0.7978845608028654 * (x + 0.044715 * (x * x * x))))


def _compress_kernel(x_ref, xn_ref, pos_ref, wl_ref, wt_ref, w2_ref, o_ref, tr_s, *, tm):
    bf = jnp.bfloat16
    f32 = jnp.float32
    x = x_ref[...].astype(bf)
    lead = jnp.dot(x, wl_ref[...], preferred_element_type=f32)
    tr_s[0:tm, :] = jnp.dot(x, wt_ref[...], preferred_element_type=f32)
    tr_s[tm:tm + SUBLANES, :] = jnp.dot(xn_ref[...].astype(bf), wt_ref[...], preferred_element_type=f32)
    bias = (jnp.dot(pos_ref[0].astype(bf), wl_ref[...], preferred_element_type=f32)
            + jnp.dot(pos_ref[1].astype(bf), wt_ref[...], preferred_element_type=f32))[0:1, :]
    hid = _gelu_tanh(lead + tr_s[1:tm + 1, :] + bias)
    o_ref[...] = jnp.dot(hid.astype(bf), w2_ref[...], preferred_element_type=f32)


def _compress_weights(pos_emb, w1, w2):
    eye = jnp.eye(2, dtype=jnp.float32)
    def expand(w):
        return jnp.einsum('kldh,kK,gG->lkgdKGh', w, eye, eye).reshape(CMP_ROW, CMP_HID).astype(jnp.bfloat16)
    wl, wt = expand(w1[:, :CMP_STRIDE]), expand(w1[:, CMP_STRIDE:])
    w2b = jnp.einsum('khd,kK,gG->kghKGd', w2, eye, eye).reshape(CMP_HID, NSA_KV_WIDTH).astype(jnp.bfloat16)
    def pos_row(p):
        r = jnp.broadcast_to(p.transpose(1, 0, 2)[:, :, None, :], (CMP_STRIDE, 2, NSA_KV_HEADS, NSA_HD))
        return jnp.zeros((SUBLANES, CMP_ROW), jnp.float32).at[0].set(r.reshape(CMP_ROW))
    pos = jnp.stack([pos_row(pos_emb[:, :CMP_STRIDE]), pos_row(pos_emb[:, CMP_STRIDE:])])
    return pos, wl, wt, w2b


def _compress(x, cw, tm):
    pos, wl, wt, w2b = cw
    n = x.shape[0]
    nb8 = n // SUBLANES
    return pl.pallas_call(
        functools.partial(_compress_kernel, tm=tm),
        grid=(n // tm,),
        in_specs=[
            pl.BlockSpec((tm, CMP_ROW), lambda i: (i, 0)),
            pl.BlockSpec((SUBLANES, CMP_ROW), lambda i: (jnp.minimum((i + 1) * (tm // SUBLANES), nb8 - 1), 0)),
            _resident((2, SUBLANES, CMP_ROW)),
            _resident((CMP_ROW, CMP_HID)), _resident((CMP_ROW, CMP_HID)),
            _resident((CMP_HID, NSA_KV_WIDTH)),
        ],
        out_specs=pl.BlockSpec((tm, NSA_KV_WIDTH), lambda i: (i, 0)),
        out_shape=jax.ShapeDtypeStruct((n, NSA_KV_WIDTH), jnp.float32),
        scratch_shapes=[pltpu.VMEM((tm + SUBLANES, CMP_HID), jnp.float32)],
        compiler_params=pltpu.CompilerParams(
            dimension_semantics=("arbitrary",), vmem_limit_bytes=VMEM_LIMIT_BYTES),
        name="nsa_compress",
    )(x, x, pos, wl, wt, w2b)


NSA_R = NSA_HEADS // NSA_KV_HEADS
QL = NSA_R * Q_BLOCK
QLL = NSA_KV_HEADS * QL
SLC_TILE = 512
BLK_PER_TILE = SLC_TILE // SEL_BLOCK
WIN_TILE = WINDOW + Q_BLOCK
CMP_PER_SEL = SEL_BLOCK // CMP_STRIDE
M_INIT = -1e29


def _tile_lanes(v, reps):
    return jnp.concatenate([v] * reps, axis=1)


def _nsa_prompt_kernel(qbd_ref, gate_ref, kc_ref, vct_ref, kslc_ref, vtslc_ref, kwin_ref, vtwin_ref,
                       o_ref, sc_s, sel_s, m_s, l_s, acc_s, *, n_sel):
    f32, bf = jnp.float32, jnp.bfloat16
    G, HD = NSA_KV_HEADS, NSA_HD
    n = pl.program_id(0)
    qbd = qbd_ref[0]
    lane = lax.broadcasted_iota(jnp.int32, (1, Q_BLOCK), 1)
    pos_q = n * Q_BLOCK + lane
    pos_l = _tile_lanes(pos_q, QLL // Q_BLOCK)
    jrow = lax.broadcasted_iota(jnp.int32, (n_sel, 1), 0)

    s_c, mk_c = [], []
    m = jnp.full((1, QLL), NEG, f32)
    for c in range(CMP_PER_SEL):
        s = jnp.dot(kc_ref[c * n_sel:(c + 1) * n_sel, :], qbd, preferred_element_type=f32)
        mk = jrow * SEL_BLOCK + (c * CMP_STRIDE + CMP_BLOCK - 1) <= pos_l
        s = jnp.where(mk, s, NEG)
        m = jnp.maximum(m, jnp.max(s, axis=0, keepdims=True))
        s_c.append(s)
        mk_c.append(mk)
    e_c = [jnp.where(mk_c[c], jnp.exp(s_c[c] - m), 0.0) for c in range(CMP_PER_SEL)]
    l = e_c[0].sum(axis=0, keepdims=True)
    for c in range(1, CMP_PER_SEL):
        l = l + e_c[c].sum(axis=0, keepdims=True)
    inv = 1.0 / jnp.maximum(l, TINY)
    o_cmp = [jnp.zeros((HD, QL), f32) for _ in range(G)]
    pg = []
    for c in range(CMP_PER_SEL):
        p = e_c[c] * inv
        pb = p.astype(bf)
        for g in range(G):
            o_cmp[g] = o_cmp[g] + jnp.dot(vct_ref[g * HD:(g + 1) * HD, c * n_sel:(c + 1) * n_sel],
                                          pb[:, g * QL:(g + 1) * QL], preferred_element_type=f32)
        pg.append([sum(p[:, g * QL + r * Q_BLOCK:g * QL + (r + 1) * Q_BLOCK] for r in range(NSA_R))
                   for g in range(G)])
    o_cmp = jnp.concatenate(o_cmp, axis=1)

    cur = pos_q // SEL_BLOCK
    forced = (jrow == 0) | (jrow == cur) | (jrow == cur - 1)
    allowed = jrow * SEL_BLOCK <= pos_q
    jrow_f = jrow.astype(f32)
    for g in range(G):
        last = pg[CMP_PER_SEL - 1][g]
        prev = jnp.where(jrow == 0, 0.0, pltpu.roll(last, 1, 0))
        inner = pg[0][g]
        for c in range(1, CMP_PER_SEL - 1):
            inner = inner + pg[c][g]
        p_slc = 2.0 * inner + last + prev
        sc_s[g] = jnp.where(forced, FORCE, jnp.where(allowed, p_slc, -1.0))
        sel_s[g] = jnp.zeros((n_sel, Q_BLOCK), f32)

    def pick(_, carry):
        for g in range(G):
            s = sc_s[g]
            top = jnp.max(s, axis=0, keepdims=True)
            first = jnp.min(jnp.where(s == top, jrow_f, float(n_sel)), axis=0, keepdims=True)
            hit = jrow_f == first
            sc_s[g] = jnp.where(hit, -jnp.inf, s)
            sel_s[g] = jnp.where(hit, 1.0, sel_s[g])
        return carry

    lax.fori_loop(0, min(SEL_TOPN, n_sel), pick, 0)
    for g in range(G):
        sel_s[g] = jnp.where(allowed, sel_s[g], 0.0)

    def reset():
        m_s[...] = jnp.full((1, QLL), M_INIT, f32)
        l_s[...] = jnp.zeros((1, QLL), f32)
        acc_s[...] = jnp.zeros((HD, QLL), f32)

    def update(s_blocks, vt_ref, start, rows):
        m_old = m_s[...]
        m_new = m_old
        for s in s_blocks:
            m_new = jnp.maximum(m_new, jnp.max(s, axis=0, keepdims=True))
        alpha = jnp.exp(m_old - m_new)
        e_blocks = [jnp.exp(s - m_new) for s in s_blocks]
        l_new = l_s[...] * alpha
        for e in e_blocks:
            l_new = l_new + e.sum(axis=0, keepdims=True)
        pt = jnp.concatenate([e.astype(bf) for e in e_blocks], axis=0)
        pv = [jnp.dot(vt_ref[g * HD:(g + 1) * HD, pl.ds(start, rows)], pt[:, g * QL:(g + 1) * QL],
                      preferred_element_type=f32) for g in range(G)]
        acc_s[...] = acc_s[...] * alpha + jnp.concatenate(pv, axis=1)
        m_s[...] = m_new
        l_s[...] = l_new

    def finish():
        return acc_s[...] * (1.0 / jnp.maximum(l_s[...], TINY))

    def slc_tile(kt, causal):
        start = pl.multiple_of(kt * SLC_TILE, SLC_TILE)
        s = jnp.dot(kslc_ref[pl.ds(start, SLC_TILE), :], qbd, preferred_element_type=f32)
        selb = [sel_s[g, pl.ds(pl.multiple_of(kt * BLK_PER_TILE, BLK_PER_TILE), BLK_PER_TILE), :]
                for g in range(G)]
        blocks = []
        for i in range(BLK_PER_TILE):
            keep = jnp.concatenate([_tile_lanes(selb[g][i:i + 1, :], NSA_R) for g in range(G)], axis=1) > 0.5
            if causal:
                tok = start + i * SEL_BLOCK + lax.broadcasted_iota(jnp.int32, (SEL_BLOCK, 1), 0)
                keep = keep & (tok <= pos_l)
            blocks.append(jnp.where(keep, s[i * SEL_BLOCK:(i + 1) * SEL_BLOCK, :], NEG))
        update(blocks, vtslc_ref, start, SLC_TILE)

    reset()
    diag = (n * Q_BLOCK) // SLC_TILE

    def slc_body(kt, carry):
        slc_tile(kt, False)
        return carry

    lax.fori_loop(0, diag, slc_body, 0)
    slc_tile(diag, True)
    o_slc = finish()

    reset()
    wstart = pl.multiple_of(jnp.maximum(n * Q_BLOCK - WINDOW, 0), Q_BLOCK)
    s = jnp.dot(kwin_ref[pl.ds(wstart, WIN_TILE), :], qbd, preferred_element_type=f32)
    rel = pos_q - (wstart + lax.broadcasted_iota(jnp.int32, (WIN_TILE, 1), 0))
    bias = _tile_lanes(jnp.where((rel >= 0) & (rel <= WINDOW), 0.0, NEG), QLL // Q_BLOCK)
    update([s + bias], vtwin_ref, wstart, WIN_TILE)
    o_win = finish()

    gate = jax.nn.sigmoid(gate_ref[0])
    o_ref[0] = gate[0:1, :] * o_cmp + gate[1:2, :] * o_slc + gate[2:3, :] * o_win


def _nsa_prompt_attention(nq, ngate, kvc, nslc, nwin):
    T = nq.shape[0]
    nb, n_sel = T // Q_BLOCK, T // SEL_BLOCK
    G, R, HD = NSA_KV_HEADS, NSA_R, NSA_HD
    bf = jnp.bfloat16
    half = G * HD
    qt = (nq * (HD ** -0.5)).reshape(nb, Q_BLOCK, G, R, HD).transpose(0, 2, 4, 3, 1)
    qbd = jnp.einsum('ngdrq,gh->ngdhrq', qt, jnp.eye(G, dtype=jnp.float32)).reshape(nb, half, QLL).astype(bf)
    gate = ngate.reshape(nb, Q_BLOCK, G, R, 3).transpose(0, 4, 2, 3, 1).reshape(nb, 3, QLL)
    kvp = kvc.reshape(n_sel, CMP_PER_SEL, NSA_KV_WIDTH).transpose(1, 0, 2).reshape(T // CMP_STRIDE, NSA_KV_WIDTH)
    kc, vct = kvp[:, :half].astype(bf), kvp[:, half:].T.astype(bf)
    kslc, vtslc = nslc[:, :half].astype(bf), nslc[:, half:].T.astype(bf)
    kwin, vtwin = nwin[:, :half].astype(bf), nwin[:, half:].T.astype(bf)
    out = pl.pallas_call(
        functools.partial(_nsa_prompt_kernel, n_sel=n_sel),
        grid=(nb,),
        in_specs=[
            pl.BlockSpec((1, half, QLL), lambda i: (i, 0, 0)),
            pl.BlockSpec((1, 3, QLL), lambda i: (i, 0, 0)),
            _resident((T // CMP_STRIDE, half)), _resident((half, T // CMP_STRIDE)),
            _resident((T, half)), _resident((half, T)),
            _resident((T, half)), _resident((half, T)),
        ],
        out_specs=pl.BlockSpec((1, HD, QLL), lambda i: (i, 0, 0)),
        out_shape=jax.ShapeDtypeStruct((nb, HD, QLL), jnp.float32),
        scratch_shapes=[
            pltpu.VMEM((G, n_sel, Q_BLOCK), jnp.float32),
            pltpu.VMEM((G, n_sel, Q_BLOCK), jnp.float32),
            pltpu.VMEM((1, QLL), jnp.float32),
            pltpu.VMEM((1, QLL), jnp.float32),
            pltpu.VMEM((HD, QLL), jnp.float32),
        ],
        compiler_params=pltpu.CompilerParams(
            dimension_semantics=("arbitrary",), vmem_limit_bytes=VMEM_LIMIT_BYTES),
        name="nsa_prompt_attention",
    )(qbd, gate, kc, vct, kslc, vtslc, kwin, vtwin)
    return out.reshape(nb, HD, G, R, Q_BLOCK).transpose(0, 4, 2, 3, 1).reshape(T, NSA_WIDTH)


def _rmsnorm(x, g):
    xf = x.astype(jnp.float32)
    y = xf * lax.rsqrt(jnp.mean(xf * xf, axis=-1, keepdims=True) + EPS)
    return (y * g.astype(jnp.float32)).astype(x.dtype)


def _masked_softmax(s, mask):
    s = jnp.where(mask, s.astype(jnp.float32), NEG)
    m = jnp.max(s, axis=-1, keepdims=True)
    e = jnp.where(mask, jnp.exp(s - m), 0.0)
    return e / jnp.maximum(jnp.sum(e, axis=-1, keepdims=True), TINY)


def _causal_dwconv(u, hist, w):
    T = u.shape[1]
    up = jnp.concatenate([hist.astype(u.dtype), u], axis=1)
    y = w[0] * up[:, 0:T]
    for k in range(1, CONV_W):
        y = y + w[k] * up[:, k:k + T]
    return y, up[:, -(CONV_W - 1):]


def _gla_scan(q, k, v, log_a, s0):
    B, T, H = q.shape[:3]
    C = min(GLA_CHUNK, T)
    nc = -(-T // C)
    pad = nc * C - T

    def prep(a):
        a = jnp.pad(a.astype(jnp.float32), ((0, 0), (0, pad), (0, 0), (0, 0)))
        return a.reshape(B, nc, C, H, a.shape[-1]).transpose(1, 0, 3, 2, 4)

    qs, ks, vs, as_ = prep(q), prep(k), prep(v), prep(log_a)
    causal = jnp.tril(jnp.ones((C, C), bool))[:, :, None]

    def step(S, xs):
        qc, kc, vc, ac = xs
        b = jnp.cumsum(ac, axis=2)
        b_last = b[:, :, -1]
        inter = jnp.einsum('bhid,bhde->bhie', qc * jnp.exp(b), S)
        diff = b[:, :, :, None, :] - b[:, :, None, :, :]
        decay = jnp.where(causal, jnp.exp(jnp.where(causal, diff, 0.0)), 0.0)
        A = jnp.einsum('bhid,bhjd,bhijd->bhij', qc, kc, decay)
        intra = jnp.einsum('bhij,bhje->bhie', A, vc)
        S = jnp.exp(b_last)[..., None] * S + jnp.einsum(
            'bhjd,bhje->bhde', kc * jnp.exp(b_last[:, :, None, :] - b), vc)
        return S, inter + intra

    S, o = lax.scan(step, s0.astype(jnp.float32), (qs, ks, vs, as_))
    o = o.transpose(1, 0, 3, 2, 4).reshape(B, nc * C, H, v.shape[-1])[:, :T]
    return o, S


def _nsa_compress(kv, pos_emb, w1, w2):
    B, T = kv.shape[:2]
    sub = kv.reshape(B, T // CMP_STRIDE, CMP_STRIDE, 2, NSA_KV_HEADS, NSA_HD)
    lead = jnp.einsum('bnlkgd,kldh->bnkgh', sub, w1[:, :CMP_STRIDE])
    trail = jnp.einsum('bnlkgd,kldh->bnkgh', sub, w1[:, CMP_STRIDE:])
    bias = jnp.einsum('kld,kldh->kh', pos_emb, w1)
    hid = jax.nn.gelu(lead[:, :-1] + trail[:, 1:] + bias[:, None, :])
    return jnp.einsum('bnkgh,khd->bnkgd', hid, w2)


def _nsa_attend(q, pos, gates, kc, vc, fetch, win_k, win_v, win_pos):
    B, Qb = q.shape[:2]
    G, R = NSA_KV_HEADS, NSA_HEADS // NSA_KV_HEADS
    qg = q.reshape(B, Qb, G, R, NSA_HD) * (NSA_HD ** -0.5)
    n_cmp = kc.shape[1]
    n_sel = (n_cmp + 1) * CMP_STRIDE // SEL_BLOCK
    cmp_end = jnp.arange(n_cmp) * CMP_STRIDE + (CMP_BLOCK - 1)
    m_cmp = cmp_end[None, :] <= pos[:, None]
    p_cmp = _masked_softmax(jnp.einsum('bqgrd,bngd->bgrqn', qg, kc), m_cmp)
    o_cmp = jnp.einsum('bgrqn,bngd->bqgrd', p_cmp.astype(vc.dtype), vc)
    pg = jnp.pad(p_cmp.sum(axis=2), ((0, 0), (0, 0), (0, 0), (0, 1)))
    p_sub = pg + jnp.pad(pg[..., :-1], ((0, 0), (0, 0), (0, 0), (1, 0)))
    p_slc = p_sub.reshape(B, G, Qb, n_sel, SEL_BLOCK // CMP_STRIDE).sum(-1)
    blk = jnp.arange(n_sel)[None, :]
    cur = (pos // SEL_BLOCK)[:, None]
    forced = (blk == 0) | (blk == cur) | (blk == cur - 1)
    allowed = blk * SEL_BLOCK <= pos[:, None]
    score = jnp.where(forced, FORCE, jnp.where(allowed, p_slc, -1.0))
    k_sel = min(SEL_TOPN, n_sel)
    _, idx = lax.top_k(score, k_sel)
    tok = (idx[..., None] * SEL_BLOCK + jnp.arange(SEL_BLOCK)).reshape(B, G, Qb, k_sel * SEL_BLOCK)
    m_slc = tok <= pos[None, None, :, None]
    ks, vs = fetch(tok)
    p_s = _masked_softmax(jnp.einsum('bqgrd,bgqsd->bgrqs', qg, ks), m_slc[:, :, None])
    o_slc = jnp.einsum('bgrqs,bgqsd->bqgrd', p_s.astype(vs.dtype), vs)
    rel = pos[:, None] - win_pos[None, :]
    m_win = (win_pos[None, :] >= 0) & (rel >= 0) & (rel <= WINDOW)
    p_w = _masked_softmax(jnp.einsum('bqgrd,bkgd->bgrqk', qg, win_k), m_win)
    o_win = jnp.einsum('bgrqk,bkgd->bqgrd', p_w.astype(win_v.dtype), win_v)
    g = jax.nn.sigmoid(gates.astype(jnp.float32)).reshape(B, Qb, G, R, 3)
    o = g[..., 0:1] * o_cmp + g[..., 1:2] * o_slc + g[..., 2:3] * o_win
    return o.reshape(B, Qb, NSA_WIDTH).astype(q.dtype)


def _nsa_prompt(q, kv_cmp, kv_slc, kv_win, gates, cmp_w):
    B, T = q.shape[:2]
    assert B == 1 and T % SLC_TILE == 0 and T >= WIN_TILE
    n_rows = T // CMP_STRIDE
    kvc = _compress(kv_cmp.reshape(n_rows, CMP_ROW), cmp_w, min(256, n_rows))
    o = _nsa_prompt_attention(q.reshape(T, NSA_WIDTH), gates.reshape(T, 3 * NSA_HEADS), kvc,
                              kv_slc.reshape(T, NSA_KV_WIDTH), kv_win.reshape(T, NSA_KV_WIDTH))
    return o.reshape(B, T, NSA_WIDTH), kv_win[:, -min(WINDOW, T):]


def _nsa_sample(q, kv_cmp, kv_slc, kv_win, gates, pool_cmp, pool_slc, page_table, win_buf, layer, cmp_params):
    B, Tn = q.shape[:2]
    n_pages = page_table.shape[1]
    P = n_pages * PAGE_SIZE
    pad = -(-Tn // SEL_BLOCK) * SEL_BLOCK - Tn
    past_cmp = pool_cmp[layer, page_table].reshape(B, P, 2, NSA_KV_HEADS, NSA_HD)
    full_cmp = jnp.concatenate(
        [past_cmp, kv_cmp.astype(past_cmp.dtype), jnp.zeros((B, pad, 2, NSA_KV_HEADS, NSA_HD), past_cmp.dtype)], axis=1)
    kvc = _nsa_compress(full_cmp, *cmp_params)
    kc, vc = kvc[:, :, 0], kvc[:, :, 1]
    pos = P + jnp.arange(Tn)
    new_t = kv_slc.transpose(0, 3, 1, 2, 4)

    def fetch(tok):
        past = tok < P
        page = jax.vmap(lambda row, t: row[t])(page_table, jnp.minimum(tok // PAGE_SIZE, n_pages - 1))
        off = tok % PAGE_SIZE
        g_idx = jnp.arange(NSA_KV_HEADS)[None, :, None, None]
        past_rows = pool_slc[layer, page, off, :, g_idx]
        new_rows = jax.vmap(jax.vmap(lambda a, i: a[i]))(new_t, jnp.clip(tok - P, 0, Tn - 1))
        rows = jnp.where(past[..., None, None], past_rows, new_rows.astype(past_rows.dtype))
        return rows[..., 0, :], rows[..., 1, :]

    Wb = win_buf.shape[1]
    win_all = jnp.concatenate([win_buf, kv_win.astype(win_buf.dtype)], axis=1)
    wpos = P - Wb + jnp.arange(Wb + Tn)
    o = _nsa_attend(q, pos, gates, kc, vc, fetch, win_all[:, :, 0], win_all[:, :, 1], wpos)
    return o, win_all[:, -min(WINDOW, Wb + Tn):]


def _expand_rows(v, t):
    if v.shape[0] == 1:
        return v
    return jnp.repeat(v, t, axis=0)


def _trunk_layer(x, mod, lw, gla_s0, sc_hist, ffn_hist, nsa_apply, tm, final, g_final):
    B, T, _ = x.shape
    m = B * T
    grouped = B > 1
    ssh1, ssc1, sgt1, ssh2, ssc2, sgt2 = [_expand_rows(v, T) for v in jnp.split(mod, 6, axis=-1)]
    x2 = x.reshape(m, D_MODEL)
    proj = _in_proj(x2, lw['norm_mix'], ssc1, ssh1, lw['w_in'], tm).reshape(B, T, PROJ_WIDTH)
    gq, gk, gv, gg, ga, sb, scc, shh, nq, ncmp, nslc, nwin, ngate = [_proj_piece(proj, p) for p in range(13)]
    heads = lambda a, d: a.reshape(B, T, -1, d)
    log_a = jax.nn.log_sigmoid((ga @ lw['gla_wa2'] + lw['gla_ba']).astype(jnp.float32)) / GLA_GATE_TEMP
    o_gla, s_gla = _gla_scan(heads(gq, GLA_DK) * (GLA_DK ** -0.5), heads(gk, GLA_DK),
                             heads(gv, GLA_DV), heads(log_a, GLA_DK), gla_s0)
    o_gla = _rmsnorm(o_gla, lw['gla_norm']).astype(x.dtype).reshape(B, T, GLA_WIDTH) * jax.nn.silu(gg)
    conv_out, sc_state = _causal_dwconv(scc * shh, sc_hist, lw['sc_conv'])
    o_sc = sb * conv_out
    kvr = lambda a: a.reshape(B, T, 2, NSA_KV_HEADS, NSA_HD)
    kv_cmp, kv_slc, kv_win = kvr(ncmp), kvr(nslc), kvr(nwin)
    o_nsa, win_state = nsa_apply(heads(nq, NSA_HD), kv_cmp, kv_slc, kv_win, heads(ngate, 3))
    mix = jnp.concatenate([o_gla, o_sc, o_nsa], axis=-1).reshape(m, D_MODEL)
    x2 = _out_proj(x2, mix, sgt1, lw['w_out'], tm)

    ff2 = 2 * D_FF
    if grouped:
        zero = jnp.zeros((B, 1, ff2), jnp.float32)
        h1 = jnp.concatenate([ffn_hist[:, 1:2], zero, zero, zero], axis=1).reshape(m, ff2)
        h2 = jnp.concatenate([ffn_hist[:, 0:1], ffn_hist[:, 1:2], zero, zero], axis=1).reshape(m, ff2)
        hist = (h1, h2)
    else:
        hist = jnp.concatenate(
            [jnp.zeros((HIST_ROWS - (CONV_W - 1), ff2), jnp.float32), ffn_hist[0]], axis=0)
    y, st = _ffn(x2, lw['norm_ffn'], ssc2, ssh2, sgt2, lw['ffn_up'], lw['ffn_conv'], lw['ffn_down'],
                 g_final, hist, tm, T if grouped else 0, final)
    if grouped:
        ffn_state = st.reshape(B, T, ff2)[:, -(CONV_W - 1):]
    else:
        ffn_state = st[None, -(CONV_W - 1):]
    return (y.reshape(B, T, D_MODEL), kv_cmp, kv_slc, win_state, s_gla.astype(gla_s0.dtype), sc_state, ffn_state)


def kernel(x_prompt, x_sample, cache_nsa_cmp, cache_nsa_slc, cache_nsa_win, state_gla, state_shortconv, state_ffn_conv, page_table, c_prompt, c_sample, mod_w, mod_b, norm_mix, norm_ffn, w_in, gla_wa2, gla_ba, gla_norm, sc_conv, nsa_cmp_pos, nsa_cmp_w1, nsa_cmp_w2, w_out, ffn_up, ffn_conv, ffn_down, norm_final):
    xp, xs = x_prompt, x_sample
    bp, bs = xp.shape[0], xs.shape[0]
    assert bp == 1 and xs.shape[1] == 4
    c_rows = _round_up(bp + bs, SUBLANES)
    c_all = jnp.concatenate([c_prompt, c_sample, jnp.zeros((c_rows - bp - bs, D_MODEL), jnp.float32)], axis=0)
    mod_all = _modulation(c_all, mod_w, mod_b)
    g_final = norm_final.reshape(1, D_MODEL)
    outs = [[] for _ in range(12)]
    for l in range(DEPTH):
        lw = dict(
            norm_mix=norm_mix[l].reshape(1, D_MODEL), norm_ffn=norm_ffn[l].reshape(1, D_MODEL),
            w_in=_pack_w_in(w_in[l]), gla_wa2=gla_wa2[l], gla_ba=gla_ba[l], gla_norm=gla_norm[l],
            sc_conv=sc_conv[l], w_out=w_out[l].astype(jnp.bfloat16),
            ffn_up=ffn_up[l].astype(jnp.bfloat16), ffn_conv=ffn_conv[l],
            ffn_down=ffn_down[l].astype(jnp.bfloat16))
        cmp_params = (nsa_cmp_pos[l], nsa_cmp_w1[l], nsa_cmp_w2[l])
        cmp_w = _compress_weights(*cmp_params)
        final = l == DEPTH - 1
        res_p = _trunk_layer(
            xp, mod_all[l, 0:bp], lw,
            jnp.zeros((bp, GLA_HEADS, GLA_DK, GLA_DV), xp.dtype),
            jnp.zeros((bp, CONV_W - 1, SC_WIDTH), xp.dtype),
            jnp.zeros((bp, CONV_W - 1, 2 * D_FF), xp.dtype),
            functools.partial(_nsa_prompt, cmp_w=cmp_w), 256, final, g_final)
        res_s = _trunk_layer(
            xs, mod_all[l, bp:bp + bs], lw, state_gla[l], state_shortconv[l], state_ffn_conv[l],
            functools.partial(_nsa_sample, pool_cmp=cache_nsa_cmp, pool_slc=cache_nsa_slc,
                              page_table=page_table, win_buf=cache_nsa_win[l], layer=l,
                              cmp_params=cmp_params), bs * xs.shape[1], final, g_final)
        xp, xs = res_p[0], res_s[0]
        for k in range(6):
            outs[2 * k].append(res_p[k + 1])
            outs[2 * k + 1].append(res_s[k + 1])
    return (xp, xs) + tuple(jnp.stack(o) for o in outs)
```

```python
import functools

import jax
import jax.numpy as jnp
from jax import lax
from jax.experimental import pallas as pl
from jax.experimental.pallas import tpu as pltpu

D_MODEL = 1024
DEPTH = 2
PAGE_SIZE = 128
GLA_HEADS = 4
GLA_DK = D_MODEL // 16
GLA_DV = D_MODEL // 16
GLA_WIDTH = GLA_HEADS * GLA_DV
GLA_GATE_RANK = 16
GLA_GATE_TEMP = 16.0
GLA_CHUNK = 64
SC_WIDTH = D_MODEL // 4
CONV_W = 3
NSA_HEADS = 8
NSA_KV_HEADS = 2
NSA_HD = D_MODEL // 16
NSA_WIDTH = NSA_HEADS * NSA_HD
NSA_KV_WIDTH = 2 * NSA_KV_HEADS * NSA_HD
CMP_STRIDE = 16
CMP_BLOCK = 2 * CMP_STRIDE
CMP_HIDDEN = 128
SEL_BLOCK = 64
SEL_TOPN = 16
WINDOW = 512
Q_BLOCK = 128
D_FF = 2816
EPS = 1e-6
NEG = -1e30
TINY = 1e-30
FORCE = 1e9

IN_SIZES = (
    GLA_HEADS * GLA_DK, GLA_HEADS * GLA_DK, GLA_WIDTH, GLA_WIDTH, GLA_GATE_RANK,
    SC_WIDTH, SC_WIDTH, SC_WIDTH,
    NSA_WIDTH, NSA_KV_WIDTH, NSA_KV_WIDTH, NSA_KV_WIDTH, NSA_HEADS * 3,
)

LANES = 128
SUBLANES = 8
VMEM_LIMIT_BYTES = 56 * 1024 * 1024

PROJ_ORDER = (0, 1, 2, 3, 5, 6, 7, 8, 9, 10, 11, 4, 12)


def _round_up(n, m):
    return -(-n // m) * m


def _proj_layout():
    src, acc = [], 0
    for s in IN_SIZES:
        src.append(acc)
        acc += s
    offs, dst = {}, 0
    for p in PROJ_ORDER:
        offs[p] = dst
        dst += _round_up(IN_SIZES[p], LANES)
    return src, offs, dst


PROJ_SRC, PROJ_OFF, PROJ_WIDTH = _proj_layout()


def _pack_w_in(w_in):
    out = jnp.zeros((D_MODEL, PROJ_WIDTH), jnp.bfloat16)
    for p in PROJ_ORDER:
        piece = w_in[:, PROJ_SRC[p]:PROJ_SRC[p] + IN_SIZES[p]].astype(jnp.bfloat16)
        out = lax.dynamic_update_slice(out, piece, (0, PROJ_OFF[p]))
    return out


def _proj_piece(proj, p):
    return proj[..., PROJ_OFF[p]:PROJ_OFF[p] + IN_SIZES[p]]


def _mod_kernel(c_ref, w_ref, b_ref, o_ref):
    c = c_ref[...]
    a = c * jax.nn.sigmoid(c)
    o_ref[0] = jnp.dot(a, w_ref[0], preferred_element_type=jnp.float32,
                       precision=lax.Precision.HIGHEST) + b_ref[0]


def _modulation(c_all, mod_w, mod_b):
    rows = c_all.shape[0]
    tn = 1024
    n = mod_w.shape[-1]
    return pl.pallas_call(
        _mod_kernel,
        grid=(DEPTH, n // tn),
        in_specs=[
            pl.BlockSpec((rows, D_MODEL), lambda l, j: (0, 0)),
            pl.BlockSpec((1, D_MODEL, tn), lambda l, j: (l, 0, j)),
            pl.BlockSpec((1, 1, tn), lambda l, j: (l, 0, j)),
        ],
        out_specs=pl.BlockSpec((1, rows, tn), lambda l, j: (l, 0, j)),
        out_shape=jax.ShapeDtypeStruct((DEPTH, rows, n), jnp.float32),
        name="adaln_modulation",
    )(c_all, mod_w, mod_b.reshape(DEPTH, 1, n))


def _norm_mod(x, g, sc, sh):
    r = lax.rsqrt(jnp.mean(x * x, axis=-1, keepdims=True) + EPS)
    return (x * r * g) * (1.0 + sc) + sh


def _in_proj_kernel(x_ref, g_ref, sc_ref, sh_ref, w_ref, o_ref):
    h = _norm_mod(x_ref[...], g_ref[...], sc_ref[...], sh_ref[...])
    o_ref[...] = jnp.dot(h.astype(jnp.bfloat16), w_ref[...], preferred_element_type=jnp.float32)


def _row_spec(tm, per_row):
    if per_row:
        return pl.BlockSpec((tm, D_MODEL), lambda i: (i, 0))
    return pl.BlockSpec((1, D_MODEL), lambda i: (0, 0))


def _resident(shape):
    return pl.BlockSpec(shape, lambda i: (0,) * len(shape), pipeline_mode=pl.Buffered(1))


def _in_proj(x, g, sc, sh, w_packed, tm):
    m = x.shape[0]
    per_row = sc.shape[0] != 1
    return pl.pallas_call(
        _in_proj_kernel,
        grid=(m // tm,),
        in_specs=[
            pl.BlockSpec((tm, D_MODEL), lambda i: (i, 0)),
            _resident((1, D_MODEL)),
            _row_spec(tm, per_row),
            _row_spec(tm, per_row),
            _resident((D_MODEL, PROJ_WIDTH)),
        ],
        out_specs=pl.BlockSpec((tm, PROJ_WIDTH), lambda i: (i, 0)),
        out_shape=jax.ShapeDtypeStruct((m, PROJ_WIDTH), jnp.float32),
        compiler_params=pltpu.CompilerParams(
            dimension_semantics=("arbitrary",), vmem_limit_bytes=VMEM_LIMIT_BYTES),
        name="norm_in_proj",
    )(x, g, sc, sh, w_packed)


def _out_proj_kernel(x_ref, mix_ref, gt_ref, w_ref, o_ref):
    y = jnp.dot(mix_ref[...].astype(jnp.bfloat16), w_ref[...], preferred_element_type=jnp.float32)
    o_ref[...] = x_ref[...] + gt_ref[...] * y


def _out_proj(x, mix, gt, w_bf16, tm):
    m = x.shape[0]
    per_row = gt.shape[0] != 1
    return pl.pallas_call(
        _out_proj_kernel,
        grid=(m // tm,),
        in_specs=[
            pl.BlockSpec((tm, D_MODEL), lambda i: (i, 0)),
            pl.BlockSpec((tm, D_MODEL), lambda i: (i, 0)),
            _row_spec(tm, per_row),
            _resident((D_MODEL, D_MODEL)),
        ],
        out_specs=pl.BlockSpec((tm, D_MODEL), lambda i: (i, 0)),
        out_shape=jax.ShapeDtypeStruct((m, D_MODEL), jnp.float32),
        compiler_params=pltpu.CompilerParams(
            dimension_semantics=("arbitrary",), vmem_limit_bytes=VMEM_LIMIT_BYTES),
        name="out_proj_residual",
    )(x, mix, gt, w_bf16)


FFN_UP_CHUNK = 512
FFN_ACT_CHUNK = 256
HIST_ROWS = SUBLANES


def _ffn_kernel(*refs, tm, group, final):
    grouped = group > 0
    if grouped:
        (x_ref, g_ref, sc_ref, sh_ref, gt_ref, wup_ref, cw_ref, wdn_ref, gf_ref,
         h1_ref, h2_ref, o_ref, st_ref, up_s) = refs
    else:
        (x_ref, g_ref, sc_ref, sh_ref, gt_ref, wup_ref, cw_ref, wdn_ref, gf_ref,
         h0_ref, o_ref, st_ref, up_s) = refs

        @pl.when(pl.program_id(0) == 0)
        def _():
            up_s[0:HIST_ROWS, :] = h0_ref[...]

    x = x_ref[...]
    h = _norm_mod(x, g_ref[...], sc_ref[...], sh_ref[...]).astype(jnp.bfloat16)
    for c in range(2 * D_FF // FFN_UP_CHUNK):
        cols = slice(c * FFN_UP_CHUNK, (c + 1) * FFN_UP_CHUNK)
        up_s[HIST_ROWS:HIST_ROWS + tm, cols] = jnp.dot(
            h, wup_ref[:, cols], preferred_element_type=jnp.float32)

    if grouped:
        t = lax.broadcasted_iota(jnp.int32, (tm, 1), 0) % group

    def conv(cols):
        cur = up_s[HIST_ROWS:HIST_ROWS + tm, cols]
        p1 = up_s[HIST_ROWS - 1:HIST_ROWS - 1 + tm, cols]
        p2 = up_s[HIST_ROWS - 2:HIST_ROWS - 2 + tm, cols]
        if grouped:
            p1 = jnp.where(t == 0, h1_ref[:, cols], p1)
            p2 = jnp.where(t <= 1, h2_ref[:, cols], p2)
        return cw_ref[0:1, cols] * p2 + cw_ref[1:2, cols] * p1 + cw_ref[2:3, cols] * cur

    acc = jnp.zeros((tm, D_MODEL), jnp.float32)
    for c in range(D_FF // FFN_ACT_CHUNK):
        a = conv(slice(c * FFN_ACT_CHUNK, (c + 1) * FFN_ACT_CHUNK))
        b = conv(slice(D_FF + c * FFN_ACT_CHUNK, D_FF + (c + 1) * FFN_ACT_CHUNK))
        act = (a * jax.nn.sigmoid(a) * b).astype(jnp.bfloat16)
        acc = acc + jnp.dot(act, wdn_ref[c * FFN_ACT_CHUNK:(c + 1) * FFN_ACT_CHUNK, :],
                            preferred_element_type=jnp.float32)
    y = x + gt_ref[...] * acc
    if final:
        r = lax.rsqrt(jnp.mean(y * y, axis=-1, keepdims=True) + EPS)
        y = y * r * gf_ref[...]
    o_ref[...] = y

    if grouped:
        st_ref[...] = up_s[HIST_ROWS:HIST_ROWS + tm, :]
    else:
        tail = up_s[tm:tm + HIST_ROWS, :]
        st_ref[...] = tail
        up_s[0:HIST_ROWS, :] = tail


def _ffn(x, g, sc, sh, gt, wup, cw, wdn, g_final, hist, tm, group, final):
    m = x.shape[0]
    grouped = group > 0
    per_row = sc.shape[0] != 1
    ff2 = 2 * D_FF
    in_specs = [
        pl.BlockSpec((tm, D_MODEL), lambda i: (i, 0)),
        _resident((1, D_MODEL)),
        _row_spec(tm, per_row), _row_spec(tm, per_row), _row_spec(tm, per_row),
        _resident((D_MODEL, ff2)),
        _resident((CONV_W, ff2)),
        _resident((D_FF, D_MODEL)),
        _resident((1, D_MODEL)),
    ]
    if grouped:
        assert m == tm
        in_specs += [_resident((tm, ff2)), _resident((tm, ff2))]
        hist_args = tuple(hist)
        st_rows = tm
    else:
        in_specs += [_resident((HIST_ROWS, ff2))]
        hist_args = (hist,)
        st_rows = HIST_ROWS
    return pl.pallas_call(
        functools.partial(_ffn_kernel, tm=tm, group=group, final=final),
        grid=(m // tm,),
        in_specs=in_specs,
        out_specs=[pl.BlockSpec((tm, D_MODEL), lambda i: (i, 0)),
                   pl.BlockSpec((st_rows, ff2), lambda i: (0, 0))],
        out_shape=[jax.ShapeDtypeStruct((m, D_MODEL), jnp.float32),
                   jax.ShapeDtypeStruct((st_rows, ff2), jnp.float32)],
        scratch_shapes=[pltpu.VMEM((HIST_ROWS + tm, ff2), jnp.float32)],
        compiler_params=pltpu.CompilerParams(
            dimension_semantics=("arbitrary",), vmem_limit_bytes=VMEM_LIMIT_BYTES),
        name="conv_ffn",
    )(x, g, sc, sh, gt, wup, cw, wdn, g_final, *hist_args)


CMP_ROW = CMP_STRIDE * NSA_KV_WIDTH
CMP_HID = 2 * NSA_KV_HEADS * CMP_HIDDEN


def _gelu_tanh(x):
    return 0.5 * x * (1.0 + jnp.tanh(0.7978845608028654 * (x + 0.044715 * (x * x * x))))


def _compress_kernel(x_ref, xn_ref, pos_ref, wl_ref, wt_ref, w2_ref, o_ref, tr_s, *, tm):
    bf = jnp.bfloat16
    f32 = jnp.float32
    x = x_ref[...].astype(bf)
    lead = jnp.dot(x, wl_ref[...], preferred_element_type=f32)
    tr_s[0:tm, :] = jnp.dot(x, wt_ref[...], preferred_element_type=f32)
    tr_s[tm:tm + SUBLANES, :] = jnp.dot(xn_ref[...].astype(bf), wt_ref[...], preferred_element_type=f32)
    bias = (jnp.dot(pos_ref[0].astype(bf), wl_ref[...], preferred_element_type=f32)
            + jnp.dot(pos_ref[1].astype(bf), wt_ref[...], preferred_element_type=f32))[0:1, :]
    hid = _gelu_tanh(lead + tr_s[1:tm + 1, :] + bias)
    o_ref[...] = jnp.dot(hid.astype(bf), w2_ref[...], preferred_element_type=f32)


def _compress_weights(pos_emb, w1, w2):
    eye = jnp.eye(2, dtype=jnp.float32)
    def expand(w):
        return jnp.einsum('kldh,kK,gG->lkgdKGh', w, eye, eye).reshape(CMP_ROW, CMP_HID).astype(jnp.bfloat16)
    wl, wt = expand(w1[:, :CMP_STRIDE]), expand(w1[:, CMP_STRIDE:])
    w2b = jnp.einsum('khd,kK,gG->kghKGd', w2, eye, eye).reshape(CMP_HID, NSA_KV_WIDTH).astype(jnp.bfloat16)
    def pos_row(p):
        r = jnp.broadcast_to(p.transpose(1, 0, 2)[:, :, None, :], (CMP_STRIDE, 2, NSA_KV_HEADS, NSA_HD))
        return jnp.zeros((SUBLANES, CMP_ROW), jnp.float32).at[0].set(r.reshape(CMP_ROW))
    pos = jnp.stack([pos_row(pos_emb[:, :CMP_STRIDE]), pos_row(pos_emb[:, CMP_STRIDE:])])
    return pos, wl, wt, w2b


def _compress(x, cw, tm):
    pos, wl, wt, w2b = cw
    n = x.shape[0]
    nb8 = n // SUBLANES
    return pl.pallas_call(
        functools.partial(_compress_kernel, tm=tm),
        grid=(n // tm,),
        in_specs=[
            pl.BlockSpec((tm, CMP_ROW), lambda i: (i, 0)),
            pl.BlockSpec((SUBLANES, CMP_ROW), lambda i: (jnp.minimum((i + 1) * (tm // SUBLANES), nb8 - 1), 0)),
            _resident((2, SUBLANES, CMP_ROW)),
            _resident((CMP_ROW, CMP_HID)), _resident((CMP_ROW, CMP_HID)),
            _resident((CMP_HID, NSA_KV_WIDTH)),
        ],
        out_specs=pl.BlockSpec((tm, NSA_KV_WIDTH), lambda i: (i, 0)),
        out_shape=jax.ShapeDtypeStruct((n, NSA_KV_WIDTH), jnp.float32),
        scratch_shapes=[pltpu.VMEM((tm + SUBLANES, CMP_HID), jnp.float32)],
        compiler_params=pltpu.CompilerParams(
            dimension_semantics=("arbitrary",), vmem_limit_bytes=VMEM_LIMIT_BYTES),
        name="nsa_compress",
    )(x, x, pos, wl, wt, w2b)


NSA_R = NSA_HEADS // NSA_KV_HEADS
QL = NSA_R * Q_BLOCK
QLL = NSA_KV_HEADS * QL
SLC_TILE = 512
BLK_PER_TILE = SLC_TILE // SEL_BLOCK
WIN_TILE = WINDOW + Q_BLOCK
CMP_PER_SEL = SEL_BLOCK // CMP_STRIDE
M_INIT = -1e29


def _tile_lanes(v, reps):
    return jnp.concatenate([v] * reps, axis=1)


def _nsa_prompt_kernel(qbd_ref, gate_ref, kc_ref, vct_ref, kslc_ref, vtslc_ref, kwin_ref, vtwin_ref,
                       o_ref, sc_s, sel_s, m_s, l_s, acc_s, *, n_sel):
    f32, bf = jnp.float32, jnp.bfloat16
    G, HD = NSA_KV_HEADS, NSA_HD
    n = pl.program_id(0)
    qbd = qbd_ref[0]
    lane = lax.broadcasted_iota(jnp.int32, (1, Q_BLOCK), 1)
    pos_q = n * Q_BLOCK + lane
    pos_l = _tile_lanes(pos_q, QLL // Q_BLOCK)
    jrow = lax.broadcasted_iota(jnp.int32, (n_sel, 1), 0)

    s_c, mk_c = [], []
    m = jnp.full((1, QLL), NEG, f32)
    for c in range(CMP_PER_SEL):
        s = jnp.dot(kc_ref[c * n_sel:(c + 1) * n_sel, :], qbd, preferred_element_type=f32)
        mk = jrow * SEL_BLOCK + (c * CMP_STRIDE + CMP_BLOCK - 1) <= pos_l
        s = jnp.where(mk, s, NEG)
        m = jnp.maximum(m, jnp.max(s, axis=0, keepdims=True))
        s_c.append(s)
        mk_c.append(mk)
    e_c = [jnp.where(mk_c[c], jnp.exp(s_c[c] - m), 0.0) for c in range(CMP_PER_SEL)]
    l = e_c[0].sum(axis=0, keepdims=True)
    for c in range(1, CMP_PER_SEL):
        l = l + e_c[c].sum(axis=0, keepdims=True)
    inv = 1.0 / jnp.maximum(l, TINY)
    o_cmp = [jnp.zeros((HD, QL), f32) for _ in range(G)]
    pg = []
    for c in range(CMP_PER_SEL):
        p = e_c[c] * inv
        pb = p.astype(bf)
        for g in range(G):
            o_cmp[g] = o_cmp[g] + jnp.dot(vct_ref[g * HD:(g + 1) * HD, c * n_sel:(c + 1) * n_sel],
                                          pb[:, g * QL:(g + 1) * QL], preferred_element_type=f32)
        pg.append([sum(p[:, g * QL + r * Q_BLOCK:g * QL + (r + 1) * Q_BLOCK] for r in range(NSA_R))
                   for g in range(G)])
    o_cmp = jnp.concatenate(o_cmp, axis=1)

    cur = pos_q // SEL_BLOCK
    forced = (jrow == 0) | (jrow == cur) | (jrow == cur - 1)
    allowed = jrow * SEL_BLOCK <= pos_q
    jrow_f = jrow.astype(f32)
    for g in range(G):
        last = pg[CMP_PER_SEL - 1][g]
        prev = jnp.where(jrow == 0, 0.0, pltpu.roll(last, 1, 0))
        inner = pg[0][g]
        for c in range(1, CMP_PER_SEL - 1):
            inner = inner + pg[c][g]
        p_slc = 2.0 * inner + last + prev
        sc_s[g] = jnp.where(forced, FORCE, jnp.where(allowed, p_slc, -1.0))
        sel_s[g] = jnp.zeros((n_sel, Q_BLOCK), f32)

    def pick(_, carry):
        for g in range(G):
            s = sc_s[g]
            top = jnp.max(s, axis=0, keepdims=True)
            first = jnp.min(jnp.where(s == top, jrow_f, float(n_sel)), axis=0, keepdims=True)
            hit = jrow_f == first
            sc_s[g] = jnp.where(hit, -jnp.inf, s)
            sel_s[g] = jnp.where(hit, 1.0, sel_s[g])
        return carry

    lax.fori_loop(0, min(SEL_TOPN, n_sel), pick, 0)
    for g in range(G):
        sel_s[g] = jnp.where(allowed, sel_s[g], 0.0)

    def reset():
        m_s[...] = jnp.full((1, QLL), M_INIT, f32)
        l_s[...] = jnp.zeros((1, QLL), f32)
        acc_s[...] = jnp.zeros((HD, QLL), f32)

    def update(s_blocks, vt_ref, start, rows):
        m_old = m_s[...]
        m_new = m_old
        for s in s_blocks:
            m_new = jnp.maximum(m_new, jnp.max(s, axis=0, keepdims=True))
        alpha = jnp.exp(m_old - m_new)
        e_blocks = [jnp.exp(s - m_new) for s in s_blocks]
        l_new = l_s[...] * alpha
        for e in e_blocks:
            l_new = l_new + e.sum(axis=0, keepdims=True)
        pt = jnp.concatenate([e.astype(bf) for e in e_blocks], axis=0)
        pv = [jnp.dot(vt_ref[g * HD:(g + 1) * HD, pl.ds(start, rows)], pt[:, g * QL:(g + 1) * QL],
                      preferred_element_type=f32) for g in range(G)]
        acc_s[...] = acc_s[...] * alpha + jnp.concatenate(pv, axis=1)
        m_s[...] = m_new
        l_s[...] = l_new

    def finish():
        return acc_s[...] * (1.0 / jnp.maximum(l_s[...], TINY))

    def slc_tile(kt, causal):
        start = pl.multiple_of(kt * SLC_TILE, SLC_TILE)
        s = jnp.dot(kslc_ref[pl.ds(start, SLC_TILE), :], qbd, preferred_element_type=f32)
        selb = [sel_s[g, pl.ds(pl.multiple_of(kt * BLK_PER_TILE, BLK_PER_TILE), BLK_PER_TILE), :]
                for g in range(G)]
        blocks = []
        for i in range(BLK_PER_TILE):
            keep = jnp.concatenate([_tile_lanes(selb[g][i:i + 1, :], NSA_R) for g in range(G)], axis=1) > 0.5
            if causal:
                tok = start + i * SEL_BLOCK + lax.broadcasted_iota(jnp.int32, (SEL_BLOCK, 1), 0)
                keep = keep & (tok <= pos_l)
            blocks.append(jnp.where(keep, s[i * SEL_BLOCK:(i + 1) * SEL_BLOCK, :], NEG))
        update(blocks, vtslc_ref, start, SLC_TILE)

    reset()
    diag = (n * Q_BLOCK) // SLC_TILE

    def slc_body(kt, carry):
        slc_tile(kt, False)
        return carry

    lax.fori_loop(0, diag, slc_body, 0)
    slc_tile(diag, True)
    o_slc = finish()

    reset()
    wstart = pl.multiple_of(jnp.maximum(n * Q_BLOCK - WINDOW, 0), Q_BLOCK)
    s = jnp.dot(kwin_ref[pl.ds(wstart, WIN_TILE), :], qbd, preferred_element_type=f32)
    rel = pos_q - (wstart + lax.broadcasted_iota(jnp.int32, (WIN_TILE, 1), 0))
    bias = _tile_lanes(jnp.where((rel >= 0) & (rel <= WINDOW), 0.0, NEG), QLL // Q_BLOCK)
    update([s + bias], vtwin_ref, wstart, WIN_TILE)
    o_win = finish()

    gate = jax.nn.sigmoid(gate_ref[0])
    o_ref[0] = gate[0:1, :] * o_cmp + gate[1:2, :] * o_slc + gate[2:3, :] * o_win


def _nsa_prompt_attention(nq, ngate, kvc, nslc, nwin):
    T = nq.shape[0]
    nb, n_sel = T // Q_BLOCK, T // SEL_BLOCK
    G, R, HD = NSA_KV_HEADS, NSA_R, NSA_HD
    bf = jnp.bfloat16
    half = G * HD
    qt = (nq * (HD ** -0.5)).reshape(nb, Q_BLOCK, G, R, HD).transpose(0, 2, 4, 3, 1)
    qbd = jnp.einsum('ngdrq,gh->ngdhrq', qt, jnp.eye(G, dtype=jnp.float32)).reshape(nb, half, QLL).astype(bf)
    gate = ngate.reshape(nb, Q_BLOCK, G, R, 3).transpose(0, 4, 2, 3, 1).reshape(nb, 3, QLL)
    kvp = kvc.reshape(n_sel, CMP_PER_SEL, NSA_KV_WIDTH).transpose(1, 0, 2).reshape(T // CMP_STRIDE, NSA_KV_WIDTH)
    kc, vct = kvp[:, :half].astype(bf), kvp[:, half:].T.astype(bf)
    kslc, vtslc = nslc[:, :half].astype(bf), nslc[:, half:].T.astype(bf)
    kwin, vtwin = nwin[:, :half].astype(bf), nwin[:, half:].T.astype(bf)
    out = pl.pallas_call(
        functools.partial(_nsa_prompt_kernel, n_sel=n_sel),
        grid=(nb,),
        in_specs=[
            pl.BlockSpec((1, half, QLL), lambda i: (i, 0, 0)),
            pl.BlockSpec((1, 3, QLL), lambda i: (i, 0, 0)),
            _resident((T // CMP_STRIDE, half)), _resident((half, T // CMP_STRIDE)),
            _resident((T, half)), _resident((half, T)),
            _resident((T, half)), _resident((half, T)),
        ],
        out_specs=pl.BlockSpec((1, HD, QLL), lambda i: (i, 0, 0)),
        out_shape=jax.ShapeDtypeStruct((nb, HD, QLL), jnp.float32),
        scratch_shapes=[
            pltpu.VMEM((G, n_sel, Q_BLOCK), jnp.float32),
            pltpu.VMEM((G, n_sel, Q_BLOCK), jnp.float32),
            pltpu.VMEM((1, QLL), jnp.float32),
            pltpu.VMEM((1, QLL), jnp.float32),
            pltpu.VMEM((HD, QLL), jnp.float32),
        ],
        compiler_params=pltpu.CompilerParams(
            dimension_semantics=("arbitrary",), vmem_limit_bytes=VMEM_LIMIT_BYTES),
        name="nsa_prompt_attention",
    )(qbd, gate, kc, vct, kslc, vtslc, kwin, vtwin)
    return out.reshape(nb, HD, G, R, Q_BLOCK).transpose(0, 4, 2, 3, 1).reshape(T, NSA_WIDTH)


GLA_LANES = GLA_HEADS * GLA_DK
GLA_KERNEL_CHUNK = 32
GLA_BLOCK_ROWS = 512


def _head_block_mask(dtype):
    r = lax.broadcasted_iota(jnp.int32, (GLA_LANES, GLA_LANES), 0) // GLA_DK
    c = lax.broadcasted_iota(jnp.int32, (GLA_LANES, GLA_LANES), 1) // GLA_DK
    return (r == c).astype(dtype)


def _gla_kernel(q_ref, k_ref, v_ref, gg_ref, ga_ref, wa_ref, ba_ref, gn_ref, st0_ref, o_ref, st_ref,
                st_s, kp_s, bp_s, vp_s, *, c, n_chunks, valid_rows):
    f32, bf = jnp.float32, jnp.bfloat16

    @pl.when(pl.program_id(1) == 0)
    def _():
        st_s[...] = st0_ref[0]
        zeros = jnp.zeros((c, GLA_LANES), f32)
        kp_s[0:c, :] = zeros
        bp_s[0:c, :] = zeros
        vp_s[0:c, :] = zeros

    row = lax.broadcasted_iota(jnp.int32, (c, 1), 0)
    tril = (lax.broadcasted_iota(jnp.int32, (c, c), 0) >= lax.broadcasted_iota(jnp.int32, (c, c), 1)).astype(f32)
    ones_blk = _head_block_mask(bf)
    blk_f32 = _head_block_mask(f32)
    contract_last = (((1,), (1,)), ((), ()))
    contract_rows = (((0,), (0,)), ((), ()))

    def head_sum(x):
        hi = x.astype(bf)
        lo = (x - hi.astype(f32)).astype(bf)
        return (jnp.dot(hi, ones_blk, preferred_element_type=f32)
                + jnp.dot(lo, ones_blk, preferred_element_type=f32))

    def chunk(ch, carry):
        rows = pl.ds(pl.multiple_of(ch * c, c), c)
        q = q_ref[rows, :] * (GLA_DK ** -0.5)
        k = k_ref[rows, :]
        v = v_ref[rows, :]
        z = jnp.dot(ga_ref[rows, :].astype(bf), wa_ref[...], preferred_element_type=f32) + ba_ref[...]
        la = (jnp.minimum(z, 0.0) - jnp.log1p(jnp.exp(-jnp.abs(z)))) / GLA_GATE_TEMP
        if valid_rows < c:
            la = jnp.where(row < valid_rows, la, 0.0)
        b = jnp.dot(tril, la, preferred_element_type=f32, precision=lax.Precision.HIGHEST)
        st = st_s[...]
        o = lax.dot_general((q * jnp.exp(b)).astype(bf), st.astype(bf), contract_last,
                            preferred_element_type=f32)
        kp_s[c:2 * c, :] = k
        bp_s[c:2 * c, :] = b
        vp_s[c:2 * c, :] = v
        for d in range(c):
            if d == 0:
                term = q * k
                vr = v
            else:
                ok = row >= d
                kr = kp_s[c - d:2 * c - d, :]
                br = bp_s[c - d:2 * c - d, :]
                vr = vp_s[c - d:2 * c - d, :]
                term = jnp.where(ok, q * kr * jnp.exp(jnp.where(ok, b - br, 0.0)), 0.0)
            o = o + jnp.dot(term.astype(bf), ones_blk, preferred_element_type=f32) * vr
        ms = head_sum(o * o) * (1.0 / GLA_DV)
        g = gg_ref[rows, :]
        o_ref[rows, :] = o * lax.rsqrt(ms + EPS) * gn_ref[...] * (g * jax.nn.sigmoid(g))
        b_last = b[c - 1:c, :]
        ke = k * jnp.exp(b_last - b)
        upd = lax.dot_general(v.astype(bf), ke.astype(bf), contract_rows, preferred_element_type=f32)
        st_s[...] = st * jnp.exp(b_last) + upd * blk_f32
        return carry

    lax.fori_loop(0, n_chunks, chunk, 0)
    st_ref[0] = st_s[...]


def _gla(proj, nbatch, tb, c, valid_rows, wa_pad, ba, gn, st0):
    m = proj.shape[0]
    nblk = m // nbatch // tb
    colblk = lambda p, w: PROJ_OFF[p] // w
    row_map = lambda j: (lambda b, i: (b * nblk + i, j))
    return pl.pallas_call(
        functools.partial(_gla_kernel, c=c, n_chunks=tb // c, valid_rows=valid_rows),
        grid=(nbatch, nblk),
        in_specs=[
            pl.BlockSpec((tb, GLA_LANES), row_map(colblk(0, GLA_LANES))),
            pl.BlockSpec((tb, GLA_LANES), row_map(colblk(1, GLA_LANES))),
            pl.BlockSpec((tb, GLA_LANES), row_map(colblk(2, GLA_LANES))),
            pl.BlockSpec((tb, GLA_LANES), row_map(colblk(3, GLA_LANES))),
            pl.BlockSpec((tb, LANES), row_map(colblk(4, LANES))),
            pl.BlockSpec((LANES, GLA_LANES), lambda b, i: (0, 0)),
            pl.BlockSpec((1, GLA_LANES), lambda b, i: (0, 0)),
            pl.BlockSpec((1, GLA_LANES), lambda b, i: (0, 0)),
            pl.BlockSpec((1, GLA_LANES, GLA_LANES), lambda b, i: (b, 0, 0)),
        ],
        out_specs=[pl.BlockSpec((tb, GLA_LANES), lambda b, i: (b * nblk + i, 0)),
                   pl.BlockSpec((1, GLA_LANES, GLA_LANES), lambda b, i: (b, 0, 0))],
        out_shape=[jax.ShapeDtypeStruct((m, GLA_LANES), jnp.float32),
                   jax.ShapeDtypeStruct((nbatch, GLA_LANES, GLA_LANES), jnp.float32)],
        scratch_shapes=[pltpu.VMEM((GLA_LANES, GLA_LANES), jnp.float32),
                        pltpu.VMEM((2 * c, GLA_LANES), jnp.float32),
                        pltpu.VMEM((2 * c, GLA_LANES), jnp.float32),
                        pltpu.VMEM((2 * c, GLA_LANES), jnp.float32)],
        compiler_params=pltpu.CompilerParams(
            dimension_semantics=("arbitrary", "arbitrary"), vmem_limit_bytes=VMEM_LIMIT_BYTES),
        name="gla_scan",
    )(proj, proj, proj, proj, proj, wa_pad, ba, gn, st0)


def _gla_state_in(s0):
    eye = jnp.eye(GLA_HEADS, dtype=jnp.float32)
    return jnp.einsum('bhde,hg->bhegd', s0.astype(jnp.float32), eye).reshape(-1, GLA_LANES, GLA_LANES)


def _gla_state_out(st):
    blocks = [st[:, h * GLA_DV:(h + 1) * GLA_DV, h * GLA_DK:(h + 1) * GLA_DK] for h in range(GLA_HEADS)]
    return jnp.stack(blocks, axis=1).transpose(0, 1, 3, 2)


def _rmsnorm(x, g):
    xf = x.astype(jnp.float32)
    y = xf * lax.rsqrt(jnp.mean(xf * xf, axis=-1, keepdims=True) + EPS)
    return (y * g.astype(jnp.float32)).astype(x.dtype)


def _masked_softmax(s, mask):
    s = jnp.where(mask, s.astype(jnp.float32), NEG)
    m = jnp.max(s, axis=-1, keepdims=True)
    e = jnp.where(mask, jnp.exp(s - m), 0.0)
    return e / jnp.maximum(jnp.sum(e, axis=-1, keepdims=True), TINY)


def _causal_dwconv(u, hist, w):
    T = u.shape[1]
    up = jnp.concatenate([hist.astype(u.dtype), u], axis=1)
    y = w[0] * up[:, 0:T]
    for k in range(1, CONV_W):
        y = y + w[k] * up[:, k:k + T]
    return y, up[:, -(CONV_W - 1):]


def _gla_scan(q, k, v, log_a, s0):
    B, T, H = q.shape[:3]
    C = min(GLA_CHUNK, T)
    nc = -(-T // C)
    pad = nc * C - T

    def prep(a):
        a = jnp.pad(a.astype(jnp.float32), ((0, 0), (0, pad), (0, 0), (0, 0)))
        return a.reshape(B, nc, C, H, a.shape[-1]).transpose(1, 0, 3, 2, 4)

    qs, ks, vs, as_ = prep(q), prep(k), prep(v), prep(log_a)
    causal = jnp.tril(jnp.ones((C, C), bool))[:, :, None]

    def step(S, xs):
        qc, kc, vc, ac = xs
        b = jnp.cumsum(ac, axis=2)
        b_last = b[:, :, -1]
        inter = jnp.einsum('bhid,bhde->bhie', qc * jnp.exp(b), S)
        diff = b[:, :, :, None, :] - b[:, :, None, :, :]
        decay = jnp.where(causal, jnp.exp(jnp.where(causal, diff, 0.0)), 0.0)
        A = jnp.einsum('bhid,bhjd,bhijd->bhij', qc, kc, decay)
        intra = jnp.einsum('bhij,bhje->bhie', A, vc)
        S = jnp.exp(b_last)[..., None] * S + jnp.einsum(
            'bhjd,bhje->bhde', kc * jnp.exp(b_last[:, :, None, :] - b), vc)
        return S, inter + intra

    S, o = lax.scan(step, s0.astype(jnp.float32), (qs, ks, vs, as_))
    o = o.transpose(1, 0, 3, 2, 4).reshape(B, nc * C, H, v.shape[-1])[:, :T]
    return o, S


def _nsa_compress(kv, pos_emb, w1, w2):
    B, T = kv.shape[:2]
    sub = kv.reshape(B, T // CMP_STRIDE, CMP_STRIDE, 2, NSA_KV_HEADS, NSA_HD)
    lead = jnp.einsum('bnlkgd,kldh->bnkgh', sub, w1[:, :CMP_STRIDE])
    trail = jnp.einsum('bnlkgd,kldh->bnkgh', sub, w1[:, CMP_STRIDE:])
    bias = jnp.einsum('kld,kldh->kh', pos_emb, w1)
    hid = jax.nn.gelu(lead[:, :-1] + trail[:, 1:] + bias[:, None, :])
    return jnp.einsum('bnkgh,khd->bnkgd', hid, w2)


def _nsa_attend(q, pos, gates, kc, vc, fetch, win_k, win_v, win_pos):
    B, Qb = q.shape[:2]
    G, R = NSA_KV_HEADS, NSA_HEADS // NSA_KV_HEADS
    qg = q.reshape(B, Qb, G, R, NSA_HD) * (NSA_HD ** -0.5)
    n_cmp = kc.shape[1]
    n_sel = (n_cmp + 1) * CMP_STRIDE // SEL_BLOCK
    cmp_end = jnp.arange(n_cmp) * CMP_STRIDE + (CMP_BLOCK - 1)
    m_cmp = cmp_end[None, :] <= pos[:, None]
    p_cmp = _masked_softmax(jnp.einsum('bqgrd,bngd->bgrqn', qg, kc), m_cmp)
    o_cmp = jnp.einsum('bgrqn,bngd->bqgrd', p_cmp.astype(vc.dtype), vc)
    pg = jnp.pad(p_cmp.sum(axis=2), ((0, 0), (0, 0), (0, 0), (0, 1)))
    p_sub = pg + jnp.pad(pg[..., :-1], ((0, 0), (0, 0), (0, 0), (1, 0)))
    p_slc = p_sub.reshape(B, G, Qb, n_sel, SEL_BLOCK // CMP_STRIDE).sum(-1)
    blk = jnp.arange(n_sel)[None, :]
    cur = (pos // SEL_BLOCK)[:, None]
    forced = (blk == 0) | (blk == cur) | (blk == cur - 1)
    allowed = blk * SEL_BLOCK <= pos[:, None]
    score = jnp.where(forced, FORCE, jnp.where(allowed, p_slc, -1.0))
    k_sel = min(SEL_TOPN, n_sel)
    _, idx = lax.top_k(score, k_sel)
    tok = (idx[..., None] * SEL_BLOCK + jnp.arange(SEL_BLOCK)).reshape(B, G, Qb, k_sel * SEL_BLOCK)
    m_slc = tok <= pos[None, None, :, None]
    ks, vs = fetch(tok)
    p_s = _masked_softmax(jnp.einsum('bqgrd,bgqsd->bgrqs', qg, ks), m_slc[:, :, None])
    o_slc = jnp.einsum('bgrqs,bgqsd->bqgrd', p_s.astype(vs.dtype), vs)
    rel = pos[:, None] - win_pos[None, :]
    m_win = (win_pos[None, :] >= 0) & (rel >= 0) & (rel <= WINDOW)
    p_w = _masked_softmax(jnp.einsum('bqgrd,bkgd->bgrqk', qg, win_k), m_win)
    o_win = jnp.einsum('bgrqk,bkgd->bqgrd', p_w.astype(win_v.dtype), win_v)
    g = jax.nn.sigmoid(gates.astype(jnp.float32)).reshape(B, Qb, G, R, 3)
    o = g[..., 0:1] * o_cmp + g[..., 1:2] * o_slc + g[..., 2:3] * o_win
    return o.reshape(B, Qb, NSA_WIDTH).astype(q.dtype)


def _nsa_prompt(q, kv_cmp, kv_slc, kv_win, gates, cmp_w):
    B, T = q.shape[:2]
    assert B == 1 and T % SLC_TILE == 0 and T >= WIN_TILE
    n_rows = T // CMP_STRIDE
    kvc = _compress(kv_cmp.reshape(n_rows, CMP_ROW), cmp_w, min(256, n_rows))
    o = _nsa_prompt_attention(q.reshape(T, NSA_WIDTH), gates.reshape(T, 3 * NSA_HEADS), kvc,
                              kv_slc.reshape(T, NSA_KV_WIDTH), kv_win.reshape(T, NSA_KV_WIDTH))
    return o.reshape(B, T, NSA_WIDTH), kv_win[:, -min(WINDOW, T):]


def _nsa_sample(q, kv_cmp, kv_slc, kv_win, gates, pool_cmp, pool_slc, page_table, win_buf, layer, cmp_params):
    B, Tn = q.shape[:2]
    n_pages = page_table.shape[1]
    P = n_pages * PAGE_SIZE
    pad = -(-Tn // SEL_BLOCK) * SEL_BLOCK - Tn
    past_cmp = pool_cmp[layer, page_table].reshape(B, P, 2, NSA_KV_HEADS, NSA_HD)
    full_cmp = jnp.concatenate(
        [past_cmp, kv_cmp.astype(past_cmp.dtype), jnp.zeros((B, pad, 2, NSA_KV_HEADS, NSA_HD), past_cmp.dtype)], axis=1)
    kvc = _nsa_compress(full_cmp, *cmp_params)
    kc, vc = kvc[:, :, 0], kvc[:, :, 1]
    pos = P + jnp.arange(Tn)
    new_t = kv_slc.transpose(0, 3, 1, 2, 4)

    def fetch(tok):
        past = tok < P
        page = jax.vmap(lambda row, t: row[t])(page_table, jnp.minimum(tok // PAGE_SIZE, n_pages - 1))
        off = tok % PAGE_SIZE
        g_idx = jnp.arange(NSA_KV_HEADS)[None, :, None, None]
        past_rows = pool_slc[layer, page, off, :, g_idx]
        new_rows = jax.vmap(jax.vmap(lambda a, i: a[i]))(new_t, jnp.clip(tok - P, 0, Tn - 1))
        rows = jnp.where(past[..., None, None], past_rows, new_rows.astype(past_rows.dtype))
        return rows[..., 0, :], rows[..., 1, :]

    Wb = win_buf.shape[1]
    win_all = jnp.concatenate([win_buf, kv_win.astype(win_buf.dtype)], axis=1)
    wpos = P - Wb + jnp.arange(Wb + Tn)
    o = _nsa_attend(q, pos, gates, kc, vc, fetch, win_all[:, :, 0], win_all[:, :, 1], wpos)
    return o, win_all[:, -min(WINDOW, Wb + Tn):]


def _expand_rows(v, t):
    if v.shape[0] == 1:
        return v
    return jnp.repeat(v, t, axis=0)


def _trunk_layer(x, mod, lw, gla_s0, sc_hist, ffn_hist, nsa_apply, tm, final, g_final):
    B, T, _ = x.shape
    m = B * T
    grouped = B > 1
    ssh1, ssc1, sgt1, ssh2, ssc2, sgt2 = [_expand_rows(v, T) for v in jnp.split(mod, 6, axis=-1)]
    x2 = x.reshape(m, D_MODEL)
    proj = _in_proj(x2, lw['norm_mix'], ssc1, ssh1, lw['w_in'], tm).reshape(B, T, PROJ_WIDTH)
    gq, gk, gv, gg, ga, sb, scc, shh, nq, ncmp, nslc, nwin, ngate = [_proj_piece(proj, p) for p in range(13)]
    heads = lambda a, d: a.reshape(B, T, -1, d)
    if grouped:
        t_pad = _round_up(T, SUBLANES)
        gla_in = jnp.pad(proj, ((0, 0), (0, t_pad - T), (0, 0))).reshape(B * t_pad, PROJ_WIDTH)
        tb = chunk = t_pad
    else:
        t_pad, gla_in, tb, chunk = T, proj.reshape(m, PROJ_WIDTH), GLA_BLOCK_ROWS, GLA_KERNEL_CHUNK
    o_gla, st_gla = _gla(gla_in, B, tb, chunk, T if grouped else chunk,
                         lw['gla_wa'], lw['gla_ba'], lw['gla_norm'], _gla_state_in(gla_s0))
    o_gla = o_gla.reshape(B, t_pad, GLA_WIDTH)[:, :T]
    s_gla = _gla_state_out(st_gla)
    conv_out, sc_state = _causal_dwconv(scc * shh, sc_hist, lw['sc_conv'])
    o_sc = sb * conv_out
    kvr = lambda a: a.reshape(B, T, 2, NSA_KV_HEADS, NSA_HD)
    kv_cmp, kv_slc, kv_win = kvr(ncmp), kvr(nslc), kvr(nwin)
    o_nsa, win_state = nsa_apply(heads(nq, NSA_HD), kv_cmp, kv_slc, kv_win, heads(ngate, 3))
    mix = jnp.concatenate([o_gla, o_sc, o_nsa], axis=-1).reshape(m, D_MODEL)
    x2 = _out_proj(x2, mix, sgt1, lw['w_out'], tm)

    ff2 = 2 * D_FF
    if grouped:
        zero = jnp.zeros((B, 1, ff2), jnp.float32)
        h1 = jnp.concatenate([ffn_hist[:, 1:2], zero, zero, zero], axis=1).reshape(m, ff2)
        h2 = jnp.concatenate([ffn_hist[:, 0:1], ffn_hist[:, 1:2], zero, zero], axis=1).reshape(m, ff2)
        hist = (h1, h2)
    else:
        hist = jnp.concatenate(
            [jnp.zeros((HIST_ROWS - (CONV_W - 1), ff2), jnp.float32), ffn_hist[0]], axis=0)
    y, st = _ffn(x2, lw['norm_ffn'], ssc2, ssh2, sgt2, lw['ffn_up'], lw['ffn_conv'], lw['ffn_down'],
                 g_final, hist, tm, T if grouped else 0, final)
    if grouped:
        ffn_state = st.reshape(B, T, ff2)[:, -(CONV_W - 1):]
    else:
        ffn_state = st[None, -(CONV_W - 1):]
    return (y.reshape(B, T, D_MODEL), kv_cmp, kv_slc, win_state, s_gla.astype(gla_s0.dtype), sc_state, ffn_state)


def kernel(x_prompt, x_sample, cache_nsa_cmp, cache_nsa_slc, cache_nsa_win, state_gla, state_shortconv, state_ffn_conv, page_table, c_prompt, c_sample, mod_w, mod_b, norm_mix, norm_ffn, w_in, gla_wa2, gla_ba, gla_norm, sc_conv, nsa_cmp_pos, nsa_cmp_w1, nsa_cmp_w2, w_out, ffn_up, ffn_conv, ffn_down, norm_final):
    xp, xs = x_prompt, x_sample
    bp, bs = xp.shape[0], xs.shape[0]
    assert bp == 1 and xs.shape[1] == 4
    c_rows = _round_up(bp + bs, SUBLANES)
    c_all = jnp.concatenate([c_prompt, c_sample, jnp.zeros((c_rows - bp - bs, D_MODEL), jnp.float32)], axis=0)
    mod_all = _modulation(c_all, mod_w, mod_b)
    g_final = norm_final.reshape(1, D_MODEL)
    outs = [[] for _ in range(12)]
    for l in range(DEPTH):
        lw = dict(
            norm_mix=norm_mix[l].reshape(1, D_MODEL), norm_ffn=norm_ffn[l].reshape(1, D_MODEL),
            w_in=_pack_w_in(w_in[l]),
            gla_wa=jnp.zeros((LANES, GLA_LANES), jnp.bfloat16).at[:GLA_GATE_RANK].set(
                gla_wa2[l].astype(jnp.bfloat16)),
            gla_ba=gla_ba[l].reshape(1, GLA_LANES),
            gla_norm=jnp.tile(gla_norm[l], GLA_HEADS).reshape(1, GLA_LANES),
            sc_conv=sc_conv[l], w_out=w_out[l].astype(jnp.bfloat16),
            ffn_up=ffn_up[l].astype(jnp.bfloat16), ffn_conv=ffn_conv[l],
            ffn_down=ffn_down[l].astype(jnp.bfloat16))
        cmp_params = (nsa_cmp_pos[l], nsa_cmp_w1[l], nsa_cmp_w2[l])
        cmp_w = _compress_weights(*cmp_params)
        final = l == DEPTH - 1
        res_p = _trunk_layer(
            xp, mod_all[l, 0:bp], lw,
            jnp.zeros((bp, GLA_HEADS, GLA_DK, GLA_DV), xp.dtype),
            jnp.zeros((bp, CONV_W - 1, SC_WIDTH), xp.dtype),
            jnp.zeros((bp, CONV_W - 1, 2 * D_FF), xp.dtype),
            functools.partial(_nsa_prompt, cmp_w=cmp_w), 256, final, g_final)
        res_s = _trunk_layer(
            xs, mod_all[l, bp:bp + bs], lw, state_gla[l], state_shortconv[l], state_ffn_conv[l],
            functools.partial(_nsa_sample, pool_cmp=cache_nsa_cmp, pool_slc=cache_nsa_slc,
                              page_table=page_table, win_buf=cache_nsa_win[l], layer=l,
                              cmp_params=cmp_params), bs * xs.shape[1], final, g_final)
        xp, xs = res_p[0], res_s[0]
        for k in range(6):
            outs[2 * k].append(res_p[k + 1])
            outs[2 * k + 1].append(res_s[k + 1])
    return (xp, xs) + tuple(jnp.stack(o) for o in outs)
```

```python
import functools

import jax
import jax.numpy as jnp
from jax import lax
from jax.experimental import pallas as pl
from jax.experimental.pallas import tpu as pltpu

D_MODEL = 1024
DEPTH = 2
PAGE_SIZE = 128
GLA_HEADS = 4
GLA_DK = D_MODEL // 16
GLA_DV = D_MODEL // 16
GLA_WIDTH = GLA_HEADS * GLA_DV
GLA_GATE_RANK = 16
GLA_GATE_TEMP = 16.0
GLA_CHUNK = 64
SC_WIDTH = D_MODEL // 4
CONV_W = 3
NSA_HEADS = 8
NSA_KV_HEADS = 2
NSA_HD = D_MODEL // 16
NSA_WIDTH = NSA_HEADS * NSA_HD
NSA_KV_WIDTH = 2 * NSA_KV_HEADS * NSA_HD
CMP_STRIDE = 16
CMP_BLOCK = 2 * CMP_STRIDE
CMP_HIDDEN = 128
SEL_BLOCK = 64
SEL_TOPN = 16
WINDOW = 512
Q_BLOCK = 128
D_FF = 2816
EPS = 1e-6
NEG = -1e30
TINY = 1e-30
FORCE = 1e9

IN_SIZES = (
    GLA_HEADS * GLA_DK, GLA_HEADS * GLA_DK, GLA_WIDTH, GLA_WIDTH, GLA_GATE_RANK,
    SC_WIDTH, SC_WIDTH, SC_WIDTH,
    NSA_WIDTH, NSA_KV_WIDTH, NSA_KV_WIDTH, NSA_KV_WIDTH, NSA_HEADS * 3,
)

LANES = 128
SUBLANES = 8
BF16_SUBLANES = 16
VMEM_LIMIT_BYTES = 56 * 1024 * 1024

PROJ_ORDER = (0, 1, 2, 3, 5, 6, 7, 8, 9, 10, 11, 4, 12)


def _round_up(n, m):
    return -(-n // m) * m


def _proj_layout():
    src, acc = [], 0
    for s in IN_SIZES:
        src.append(acc)
        acc += s
    offs, dst = {}, 0
    for p in PROJ_ORDER:
        offs[p] = dst
        dst += _round_up(IN_SIZES[p], LANES)
    return src, offs, dst


PROJ_SRC, PROJ_OFF, PROJ_WIDTH = _proj_layout()


def _pack_w_in(w_in):
    out = jnp.zeros((D_MODEL, PROJ_WIDTH), jnp.bfloat16)
    for p in PROJ_ORDER:
        piece = w_in[:, PROJ_SRC[p]:PROJ_SRC[p] + IN_SIZES[p]].astype(jnp.bfloat16)
        out = lax.dynamic_update_slice(out, piece, (0, PROJ_OFF[p]))
    return out


def _proj_piece(proj, p):
    return proj[..., PROJ_OFF[p]:PROJ_OFF[p] + IN_SIZES[p]]


def _mod_kernel(c_ref, w_ref, b_ref, o_ref):
    c = c_ref[...]
    a = c * jax.nn.sigmoid(c)
    o_ref[0] = jnp.dot(a, w_ref[0], preferred_element_type=jnp.float32,
                       precision=lax.Precision.HIGHEST) + b_ref[0]


def _modulation(c_all, mod_w, mod_b):
    rows = c_all.shape[0]
    tn = 1024
    n = mod_w.shape[-1]
    return pl.pallas_call(
        _mod_kernel,
        grid=(DEPTH, n // tn),
        in_specs=[
            pl.BlockSpec((rows, D_MODEL), lambda l, j: (0, 0)),
            pl.BlockSpec((1, D_MODEL, tn), lambda l, j: (l, 0, j)),
            pl.BlockSpec((1, 1, tn), lambda l, j: (l, 0, j)),
        ],
        out_specs=pl.BlockSpec((1, rows, tn), lambda l, j: (l, 0, j)),
        out_shape=jax.ShapeDtypeStruct((DEPTH, rows, n), jnp.float32),
        name="adaln_modulation",
    )(c_all, mod_w, mod_b.reshape(DEPTH, 1, n))


def _norm_mod(x, g, sc, sh):
    r = lax.rsqrt(jnp.mean(x * x, axis=-1, keepdims=True) + EPS)
    return (x * r * g) * (1.0 + sc) + sh


def _in_proj_kernel(x_ref, g_ref, sc_ref, sh_ref, w_ref, o_ref):
    h = _norm_mod(x_ref[...], g_ref[...], sc_ref[...], sh_ref[...])
    o_ref[...] = jnp.dot(h.astype(jnp.bfloat16), w_ref[...], preferred_element_type=jnp.float32)


def _row_spec(tm, per_row):
    if per_row:
        return pl.BlockSpec((tm, D_MODEL), lambda i: (i, 0))
    return pl.BlockSpec((1, D_MODEL), lambda i: (0, 0))


def _resident(shape):
    return pl.BlockSpec(shape, lambda i: (0,) * len(shape), pipeline_mode=pl.Buffered(1))


def _in_proj(x, g, sc, sh, w_packed, tm):
    m = x.shape[0]
    per_row = sc.shape[0] != 1
    return pl.pallas_call(
        _in_proj_kernel,
        grid=(m // tm,),
        in_specs=[
            pl.BlockSpec((tm, D_MODEL), lambda i: (i, 0)),
            _resident((1, D_MODEL)),
            _row_spec(tm, per_row),
            _row_spec(tm, per_row),
            _resident((D_MODEL, PROJ_WIDTH)),
        ],
        out_specs=pl.BlockSpec((tm, PROJ_WIDTH), lambda i: (i, 0)),
        out_shape=jax.ShapeDtypeStruct((m, PROJ_WIDTH), jnp.float32),
        compiler_params=pltpu.CompilerParams(
            dimension_semantics=("arbitrary",), vmem_limit_bytes=VMEM_LIMIT_BYTES),
        name="norm_in_proj",
    )(x, g, sc, sh, w_packed)


def _out_proj_kernel(x_ref, mix_ref, gt_ref, w_ref, o_ref):
    y = jnp.dot(mix_ref[...].astype(jnp.bfloat16), w_ref[...], preferred_element_type=jnp.float32)
    o_ref[...] = x_ref[...] + gt_ref[...] * y


def _out_proj(x, mix, gt, w_bf16, tm):
    m = x.shape[0]
    per_row = gt.shape[0] != 1
    return pl.pallas_call(
        _out_proj_kernel,
        grid=(m // tm,),
        in_specs=[
            pl.BlockSpec((tm, D_MODEL), lambda i: (i, 0)),
            pl.BlockSpec((tm, D_MODEL), lambda i: (i, 0)),
            _row_spec(tm, per_row),
            _resident((D_MODEL, D_MODEL)),
        ],
        out_specs=pl.BlockSpec((tm, D_MODEL), lambda i: (i, 0)),
        out_shape=jax.ShapeDtypeStruct((m, D_MODEL), jnp.float32),
        compiler_params=pltpu.CompilerParams(
            dimension_semantics=("arbitrary",), vmem_limit_bytes=VMEM_LIMIT_BYTES),
        name="out_proj_residual",
    )(x, mix, gt, w_bf16)


FFN_UP_CHUNK = 512
FFN_ACT_CHUNK = 256
HIST_ROWS = SUBLANES


def _ffn_kernel(*refs, tm, group, final):
    grouped = group > 0
    if grouped:
        (x_ref, g_ref, sc_ref, sh_ref, gt_ref, wup_ref, cw_ref, wdn_ref, gf_ref,
         h1_ref, h2_ref, o_ref, st_ref, up_s) = refs
    else:
        (x_ref, g_ref, sc_ref, sh_ref, gt_ref, wup_ref, cw_ref, wdn_ref, gf_ref,
         h0_ref, o_ref, st_ref, up_s) = refs

        @pl.when(pl.program_id(0) == 0)
        def _():
            up_s[0:HIST_ROWS, :] = h0_ref[...]

    x = x_ref[...]
    h = _norm_mod(x, g_ref[...], sc_ref[...], sh_ref[...]).astype(jnp.bfloat16)
    for c in range(2 * D_FF // FFN_UP_CHUNK):
        cols = slice(c * FFN_UP_CHUNK, (c + 1) * FFN_UP_CHUNK)
        up_s[HIST_ROWS:HIST_ROWS + tm, cols] = jnp.dot(
            h, wup_ref[:, cols], preferred_element_type=jnp.float32)

    if grouped:
        t = lax.broadcasted_iota(jnp.int32, (tm, 1), 0) % group

    def conv(cols):
        cur = up_s[HIST_ROWS:HIST_ROWS + tm, cols]
        p1 = up_s[HIST_ROWS - 1:HIST_ROWS - 1 + tm, cols]
        p2 = up_s[HIST_ROWS - 2:HIST_ROWS - 2 + tm, cols]
        if grouped:
            p1 = jnp.where(t == 0, h1_ref[:, cols], p1)
            p2 = jnp.where(t <= 1, h2_ref[:, cols], p2)
        return cw_ref[0:1, cols] * p2 + cw_ref[1:2, cols] * p1 + cw_ref[2:3, cols] * cur

    acc = jnp.zeros((tm, D_MODEL), jnp.float32)
    for c in range(D_FF // FFN_ACT_CHUNK):
        a = conv(slice(c * FFN_ACT_CHUNK, (c + 1) * FFN_ACT_CHUNK))
        b = conv(slice(D_FF + c * FFN_ACT_CHUNK, D_FF + (c + 1) * FFN_ACT_CHUNK))
        act = (a * jax.nn.sigmoid(a) * b).astype(jnp.bfloat16)
        acc = acc + jnp.dot(act, wdn_ref[c * FFN_ACT_CHUNK:(c + 1) * FFN_ACT_CHUNK, :],
                            preferred_element_type=jnp.float32)
    y = x + gt_ref[...] * acc
    if final:
        r = lax.rsqrt(jnp.mean(y * y, axis=-1, keepdims=True) + EPS)
        y = y * r * gf_ref[...]
    o_ref[...] = y

    if grouped:
        st_ref[...] = up_s[HIST_ROWS:HIST_ROWS + tm, :]
    else:
        tail = up_s[tm:tm + HIST_ROWS, :]
        st_ref[...] = tail
        up_s[0:HIST_ROWS, :] = tail


def _ffn(x, g, sc, sh, gt, wup, cw, wdn, g_final, hist, tm, group, final):
    m = x.shape[0]
    grouped = group > 0
    per_row = sc.shape[0] != 1
    ff2 = 2 * D_FF
    in_specs = [
        pl.BlockSpec((tm, D_MODEL), lambda i: (i, 0)),
        _resident((1, D_MODEL)),
        _row_spec(tm, per_row), _row_spec(tm, per_row), _row_spec(tm, per_row),
        _resident((D_MODEL, ff2)),
        _resident((CONV_W, ff2)),
        _resident((D_FF, D_MODEL)),
        _resident((1, D_MODEL)),
    ]
    if grouped:
        assert m == tm
        in_specs += [_resident((tm, ff2)), _resident((tm, ff2))]
        hist_args = tuple(hist)
        st_rows = tm
    else:
        in_specs += [_resident((HIST_ROWS, ff2))]
        hist_args = (hist,)
        st_rows = HIST_ROWS
    return pl.pallas_call(
        functools.partial(_ffn_kernel, tm=tm, group=group, final=final),
        grid=(m // tm,),
        in_specs=in_specs,
        out_specs=[pl.BlockSpec((tm, D_MODEL), lambda i: (i, 0)),
                   pl.BlockSpec((st_rows, ff2), lambda i: (0, 0))],
        out_shape=[jax.ShapeDtypeStruct((m, D_MODEL), jnp.float32),
                   jax.ShapeDtypeStruct((st_rows, ff2), jnp.float32)],
        scratch_shapes=[pltpu.VMEM((HIST_ROWS + tm, ff2), jnp.float32)],
        compiler_params=pltpu.CompilerParams(
            dimension_semantics=("arbitrary",), vmem_limit_bytes=VMEM_LIMIT_BYTES),
        name="conv_ffn",
    )(x, g, sc, sh, gt, wup, cw, wdn, g_final, *hist_args)


CMP_ROW = CMP_STRIDE * NSA_KV_WIDTH
CMP_HID = 2 * NSA_KV_HEADS * CMP_HIDDEN


def _gelu_tanh(x):
    return 0.5 * x * (1.0 + jnp.tanh(0.7978845608028654 * (x + 0.044715 * (x * x * x))))


def _compress_kernel(x_ref, xn_ref, pos_ref, wl_ref, wt_ref, w2_ref, o_ref, tr_s, *, tm):
    bf = jnp.bfloat16
    f32 = jnp.float32
    x = x_ref[...].astype(bf)
    lead = jnp.dot(x, wl_ref[...], preferred_element_type=f32)
    tr_s[0:tm, :] = jnp.dot(x, wt_ref[...], preferred_element_type=f32)
    tr_s[tm:tm + SUBLANES, :] = jnp.dot(xn_ref[...].astype(bf), wt_ref[...], preferred_element_type=f32)
    bias = (jnp.dot(pos_ref[0].astype(bf), wl_ref[...], preferred_element_type=f32)
            + jnp.dot(pos_ref[1].astype(bf), wt_ref[...], preferred_element_type=f32))[0:1, :]
    hid = _gelu_tanh(lead + tr_s[1:tm + 1, :] + bias)
    o_ref[...] = jnp.dot(hid.astype(bf), w2_ref[...], preferred_element_type=f32)


def _compress_weights(pos_emb, w1, w2):
    eye = jnp.eye(2, dtype=jnp.float32)
    def expand(w):
        return jnp.einsum('kldh,kK,gG->lkgdKGh', w, eye, eye).reshape(CMP_ROW, CMP_HID).astype(jnp.bfloat16)
    wl, wt = expand(w1[:, :CMP_STRIDE]), expand(w1[:, CMP_STRIDE:])
    w2b = jnp.einsum('khd,kK,gG->kghKGd', w2, eye, eye).reshape(CMP_HID, NSA_KV_WIDTH).astype(jnp.bfloat16)
    def pos_row(p):
        r = jnp.broadcast_to(p.transpose(1, 0, 2)[:, :, None, :], (CMP_STRIDE, 2, NSA_KV_HEADS, NSA_HD))
        return jnp.zeros((SUBLANES, CMP_ROW), jnp.float32).at[0].set(r.reshape(CMP_ROW))
    pos = jnp.stack([pos_row(pos_emb[:, :CMP_STRIDE]), pos_row(pos_emb[:, CMP_STRIDE:])])
    return pos, wl, wt, w2b


def _compress(x, cw, tm):
    pos, wl, wt, w2b = cw
    n = x.shape[0]
    nb8 = n // SUBLANES
    return pl.pallas_call(
        functools.partial(_compress_kernel, tm=tm),
        grid=(n // tm,),
        in_specs=[
            pl.BlockSpec((tm, CMP_ROW), lambda i: (i, 0)),
            pl.BlockSpec((SUBLANES, CMP_ROW), lambda i: (jnp.minimum((i + 1) * (tm // SUBLANES), nb8 - 1), 0)),
            _resident((2, SUBLANES, CMP_ROW)),
            _resident((CMP_ROW, CMP_HID)), _resident((CMP_ROW, CMP_HID)),
            _resident((CMP_HID, NSA_KV_WIDTH)),
        ],
        out_specs=pl.BlockSpec((tm, NSA_KV_WIDTH), lambda i: (i, 0)),
        out_shape=jax.ShapeDtypeStruct((n, NSA_KV_WIDTH), jnp.float32),
        scratch_shapes=[pltpu.VMEM((tm + SUBLANES, CMP_HID), jnp.float32)],
        compiler_params=pltpu.CompilerParams(
            dimension_semantics=("arbitrary",), vmem_limit_bytes=VMEM_LIMIT_BYTES),
        name="nsa_compress",
    )(x, x, pos, wl, wt, w2b)


NSA_R = NSA_HEADS // NSA_KV_HEADS
QL = NSA_R * Q_BLOCK
QLL = NSA_KV_HEADS * QL
SLC_TILE = 512
BLK_PER_TILE = SLC_TILE // SEL_BLOCK
WIN_TILE = WINDOW + Q_BLOCK
CMP_PER_SEL = SEL_BLOCK // CMP_STRIDE
M_INIT = -1e29


def _tile_lanes(v, reps):
    return jnp.concatenate([v] * reps, axis=1)


def _nsa_prompt_kernel(qbd_ref, gate_ref, kc_ref, vct_ref, kslc_ref, vtslc_ref, kwin_ref, vtwin_ref,
                       o_ref, sc_s, sel_s, m_s, l_s, acc_s, *, n_sel):
    f32, bf = jnp.float32, jnp.bfloat16
    G, HD = NSA_KV_HEADS, NSA_HD
    n = pl.program_id(0)
    qbd = qbd_ref[0]
    lane = lax.broadcasted_iota(jnp.int32, (1, Q_BLOCK), 1)
    pos_q = n * Q_BLOCK + lane
    pos_l = _tile_lanes(pos_q, QLL // Q_BLOCK)
    jrow = lax.broadcasted_iota(jnp.int32, (n_sel, 1), 0)

    s_c, mk_c = [], []
    m = jnp.full((1, QLL), NEG, f32)
    for c in range(CMP_PER_SEL):
        s = jnp.dot(kc_ref[c * n_sel:(c + 1) * n_sel, :], qbd, preferred_element_type=f32)
        mk = jrow * SEL_BLOCK + (c * CMP_STRIDE + CMP_BLOCK - 1) <= pos_l
        s = jnp.where(mk, s, NEG)
        m = jnp.maximum(m, jnp.max(s, axis=0, keepdims=True))
        s_c.append(s)
        mk_c.append(mk)
    e_c = [jnp.where(mk_c[c], jnp.exp(s_c[c] - m), 0.0) for c in range(CMP_PER_SEL)]
    l = e_c[0].sum(axis=0, keepdims=True)
    for c in range(1, CMP_PER_SEL):
        l = l + e_c[c].sum(axis=0, keepdims=True)
    inv = 1.0 / jnp.maximum(l, TINY)
    o_cmp = [jnp.zeros((HD, QL), f32) for _ in range(G)]
    pg = []
    for c in range(CMP_PER_SEL):
        p = e_c[c] * inv
        pb = p.astype(bf)
        for g in range(G):
            o_cmp[g] = o_cmp[g] + jnp.dot(vct_ref[g * HD:(g + 1) * HD, c * n_sel:(c + 1) * n_sel],
                                          pb[:, g * QL:(g + 1) * QL], preferred_element_type=f32)
        pg.append([sum(p[:, g * QL + r * Q_BLOCK:g * QL + (r + 1) * Q_BLOCK] for r in range(NSA_R))
                   for g in range(G)])
    o_cmp = jnp.concatenate(o_cmp, axis=1)

    cur = pos_q // SEL_BLOCK
    forced = (jrow == 0) | (jrow == cur) | (jrow == cur - 1)
    allowed = jrow * SEL_BLOCK <= pos_q
    jrow_f = jrow.astype(f32)
    for g in range(G):
        last = pg[CMP_PER_SEL - 1][g]
        prev = jnp.where(jrow == 0, 0.0, pltpu.roll(last, 1, 0))
        inner = pg[0][g]
        for c in range(1, CMP_PER_SEL - 1):
            inner = inner + pg[c][g]
        p_slc = 2.0 * inner + last + prev
        sc_s[g] = jnp.where(forced, FORCE, jnp.where(allowed, p_slc, -1.0))
        sel_s[g] = jnp.zeros((n_sel, Q_BLOCK), f32)

    def pick(_, carry):
        for g in range(G):
            s = sc_s[g]
            top = jnp.max(s, axis=0, keepdims=True)
            first = jnp.min(jnp.where(s == top, jrow_f, float(n_sel)), axis=0, keepdims=True)
            hit = jrow_f == first
            sc_s[g] = jnp.where(hit, -jnp.inf, s)
            sel_s[g] = jnp.where(hit, 1.0, sel_s[g])
        return carry

    lax.fori_loop(0, min(SEL_TOPN, n_sel), pick, 0)
    for g in range(G):
        sel_s[g] = jnp.where(allowed, sel_s[g], 0.0)

    def reset():
        m_s[...] = jnp.full((1, QLL), M_INIT, f32)
        l_s[...] = jnp.zeros((1, QLL), f32)
        acc_s[...] = jnp.zeros((HD, QLL), f32)

    def update(s_blocks, vt_ref, start, rows):
        m_old = m_s[...]
        m_new = m_old
        for s in s_blocks:
            m_new = jnp.maximum(m_new, jnp.max(s, axis=0, keepdims=True))
        alpha = jnp.exp(m_old - m_new)
        e_blocks = [jnp.exp(s - m_new) for s in s_blocks]
        l_new = l_s[...] * alpha
        for e in e_blocks:
            l_new = l_new + e.sum(axis=0, keepdims=True)
        pt = jnp.concatenate([e.astype(bf) for e in e_blocks], axis=0)
        pv = [jnp.dot(vt_ref[g * HD:(g + 1) * HD, pl.ds(start, rows)], pt[:, g * QL:(g + 1) * QL],
                      preferred_element_type=f32) for g in range(G)]
        acc_s[...] = acc_s[...] * alpha + jnp.concatenate(pv, axis=1)
        m_s[...] = m_new
        l_s[...] = l_new

    def finish():
        return acc_s[...] * (1.0 / jnp.maximum(l_s[...], TINY))

    def slc_tile(kt, causal):
        start = pl.multiple_of(kt * SLC_TILE, SLC_TILE)
        s = jnp.dot(kslc_ref[pl.ds(start, SLC_TILE), :], qbd, preferred_element_type=f32)
        selb = [sel_s[g, pl.ds(pl.multiple_of(kt * BLK_PER_TILE, BLK_PER_TILE), BLK_PER_TILE), :]
                for g in range(G)]
        blocks = []
        for i in range(BLK_PER_TILE):
            keep = jnp.concatenate([_tile_lanes(selb[g][i:i + 1, :], NSA_R) for g in range(G)], axis=1) > 0.5
            if causal:
                tok = start + i * SEL_BLOCK + lax.broadcasted_iota(jnp.int32, (SEL_BLOCK, 1), 0)
                keep = keep & (tok <= pos_l)
            blocks.append(jnp.where(keep, s[i * SEL_BLOCK:(i + 1) * SEL_BLOCK, :], NEG))
        update(blocks, vtslc_ref, start, SLC_TILE)

    reset()
    diag = (n * Q_BLOCK) // SLC_TILE

    def slc_body(kt, carry):
        slc_tile(kt, False)
        return carry

    lax.fori_loop(0, diag, slc_body, 0)
    slc_tile(diag, True)
    o_slc = finish()

    reset()
    wstart = pl.multiple_of(jnp.maximum(n * Q_BLOCK - WINDOW, 0), Q_BLOCK)
    s = jnp.dot(kwin_ref[pl.ds(wstart, WIN_TILE), :], qbd, preferred_element_type=f32)
    rel = pos_q - (wstart + lax.broadcasted_iota(jnp.int32, (WIN_TILE, 1), 0))
    bias = _tile_lanes(jnp.where((rel >= 0) & (rel <= WINDOW), 0.0, NEG), QLL // Q_BLOCK)
    update([s + bias], vtwin_ref, wstart, WIN_TILE)
    o_win = finish()

    gate = jax.nn.sigmoid(gate_ref[0])
    o_ref[0] = gate[0:1, :] * o_cmp + gate[1:2, :] * o_slc + gate[2:3, :] * o_win


def _nsa_prompt_attention(nq, ngate, kvc, nslc, nwin):
    T = nq.shape[0]
    nb, n_sel = T // Q_BLOCK, T // SEL_BLOCK
    G, R, HD = NSA_KV_HEADS, NSA_R, NSA_HD
    bf = jnp.bfloat16
    half = G * HD
    qt = (nq * (HD ** -0.5)).reshape(nb, Q_BLOCK, G, R, HD).transpose(0, 2, 4, 3, 1)
    qbd = jnp.einsum('ngdrq,gh->ngdhrq', qt, jnp.eye(G, dtype=jnp.float32)).reshape(nb, half, QLL).astype(bf)
    gate = ngate.reshape(nb, Q_BLOCK, G, R, 3).transpose(0, 4, 2, 3, 1).reshape(nb, 3, QLL)
    kvp = kvc.reshape(n_sel, CMP_PER_SEL, NSA_KV_WIDTH).transpose(1, 0, 2).reshape(T // CMP_STRIDE, NSA_KV_WIDTH)
    kc, vct = kvp[:, :half].astype(bf), kvp[:, half:].T.astype(bf)
    kslc, vtslc = nslc[:, :half].astype(bf), nslc[:, half:].T.astype(bf)
    kwin, vtwin = nwin[:, :half].astype(bf), nwin[:, half:].T.astype(bf)
    out = pl.pallas_call(
        functools.partial(_nsa_prompt_kernel, n_sel=n_sel),
        grid=(nb,),
        in_specs=[
            pl.BlockSpec((1, half, QLL), lambda i: (i, 0, 0)),
            pl.BlockSpec((1, 3, QLL), lambda i: (i, 0, 0)),
            _resident((T // CMP_STRIDE, half)), _resident((half, T // CMP_STRIDE)),
            _resident((T, half)), _resident((half, T)),
            _resident((T, half)), _resident((half, T)),
        ],
        out_specs=pl.BlockSpec((1, HD, QLL), lambda i: (i, 0, 0)),
        out_shape=jax.ShapeDtypeStruct((nb, HD, QLL), jnp.float32),
        scratch_shapes=[
            pltpu.VMEM((G, n_sel, Q_BLOCK), jnp.float32),
            pltpu.VMEM((G, n_sel, Q_BLOCK), jnp.float32),
            pltpu.VMEM((1, QLL), jnp.float32),
            pltpu.VMEM((1, QLL), jnp.float32),
            pltpu.VMEM((HD, QLL), jnp.float32),
        ],
        compiler_params=pltpu.CompilerParams(
            dimension_semantics=("arbitrary",), vmem_limit_bytes=VMEM_LIMIT_BYTES),
        name="nsa_prompt_attention",
    )(qbd, gate, kc, vct, kslc, vtslc, kwin, vtwin)
    return out.reshape(nb, HD, G, R, Q_BLOCK).transpose(0, 4, 2, 3, 1).reshape(T, NSA_WIDTH)


PAGE_ROWS = PAGE_SIZE // CMP_STRIDE
SQ = 16
SAMPLE_LANES = NSA_HEADS * SQ
HALF_PAGES = 64


def _page_copy(pool_ref, buf_ref, sem_ref, page, slot, row0, rows):
    return pltpu.make_async_copy(pool_ref.at[page], buf_ref.at[slot, pl.ds(row0, rows), :], sem_ref.at[slot])


def _gather_schedule(issue_fn, wait_fn):
    s = pl.program_id(0)

    @pl.when(s == 0)
    def _():
        issue_fn(s, 0)

    @pl.when(s + 1 < pl.num_programs(0))
    def _():
        issue_fn(s + 1, (s + 1) % 2)

    wait_fn(s, s % 2)


def _compress_paged_kernel(pt_ref, pool_ref, pos_ref, wl_ref, wt_ref, w2_ref, o_ref, buf, sem, tr_s,
                           *, page_base, n_pages):
    bf, f32 = jnp.bfloat16, jnp.float32
    rows = HALF_PAGES * PAGE_ROWS

    def copies(step, slot, fn):
        b, half = step // 2, step % 2

        def body(i, c):
            p = jnp.minimum(half * HALF_PAGES + i, n_pages - 1)
            fn(_page_copy(pool_ref, buf, sem, page_base + pt_ref[b, p], slot, i * PAGE_ROWS, PAGE_ROWS))
            return c

        lax.fori_loop(0, HALF_PAGES + 1, body, 0)

    _gather_schedule(lambda s, slot: copies(s, slot, lambda cp: cp.start()),
                     lambda s, slot: copies(s, slot, lambda cp: cp.wait()))
    slot = pl.program_id(0) % 2
    x = buf[slot].astype(bf)
    lead = jnp.dot(x[0:rows], wl_ref[...], preferred_element_type=f32)
    tr_s[...] = jnp.dot(x, wt_ref[...], preferred_element_type=f32)
    bias = (jnp.dot(pos_ref[0].astype(bf), wl_ref[...], preferred_element_type=f32)
            + jnp.dot(pos_ref[1].astype(bf), wt_ref[...], preferred_element_type=f32))[0:1, :]
    hid = _gelu_tanh(lead + tr_s[1:rows + 1, :] + bias)
    o_ref[...] = jnp.dot(hid.astype(bf), w2_ref[...], preferred_element_type=f32)


def _compress_paged(page_table, pool, layer, cw):
    pos, wl, wt, w2b = cw
    nbatch, n_pages = page_table.shape
    assert n_pages == 2 * HALF_PAGES
    rows = HALF_PAGES * PAGE_ROWS
    const = lambda shape: pl.BlockSpec(shape, lambda s, pt: (0,) * len(shape), pipeline_mode=pl.Buffered(1))
    return pl.pallas_call(
        functools.partial(_compress_paged_kernel, page_base=layer * (pool.shape[0] // DEPTH), n_pages=n_pages),
        grid_spec=pltpu.PrefetchScalarGridSpec(
            num_scalar_prefetch=1,
            grid=(2 * nbatch,),
            in_specs=[
                pl.BlockSpec(memory_space=pl.ANY),
                const((2, SUBLANES, CMP_ROW)),
                const((CMP_ROW, CMP_HID)), const((CMP_ROW, CMP_HID)),
                const((CMP_HID, NSA_KV_WIDTH)),
            ],
            out_specs=pl.BlockSpec((rows, NSA_KV_WIDTH), lambda s, pt: (s, 0)),
            scratch_shapes=[
                pltpu.VMEM((2, rows + PAGE_ROWS, CMP_ROW), jnp.float32),
                pltpu.SemaphoreType.DMA((2,)),
                pltpu.VMEM((rows + PAGE_ROWS, CMP_HID), jnp.float32),
            ]),
        out_shape=jax.ShapeDtypeStruct((2 * nbatch * rows, NSA_KV_WIDTH), jnp.float32),
        compiler_params=pltpu.CompilerParams(
            dimension_semantics=("arbitrary",), vmem_limit_bytes=VMEM_LIMIT_BYTES),
        name="nsa_compress_paged",
    )(page_table, pool, pos, wl, wt, w2b)


def _nsa_sample_kernel(pt_ref, pool_ref, qbd_ref, gate_ref, kc_ref, vc_ref, nslc_ref, wcache_ref, nwin_ref,
                       o_ref, buf, sem, sc_s, sel_s, m_s, l_s, acc_s, *, page_base, n_pages, n_sel, t_new):
    f32, bf = jnp.float32, jnp.bfloat16
    HD, LN = NSA_HD, SAMPLE_LANES
    half = NSA_KV_HEADS * HD
    past = n_pages * PAGE_SIZE
    contract_rows = (((0,), (0,)), ((), ()))

    def copies(step, slot, fn):
        def body(p, c):
            fn(_page_copy(pool_ref, buf, sem, page_base + pt_ref[step, p], slot, p * PAGE_SIZE, PAGE_SIZE))
            return c

        lax.fori_loop(0, n_pages, body, 0)

    _gather_schedule(lambda s, slot: copies(s, slot, lambda cp: cp.start()),
                     lambda s, slot: copies(s, slot, lambda cp: cp.wait()))
    slot = pl.program_id(0) % 2

    qbd = qbd_ref[0]
    lane = lax.broadcasted_iota(jnp.int32, (1, LN), 1)
    pos_l = past + lane % SQ
    group0 = lane < LN // NSA_KV_HEADS
    jrow = lax.broadcasted_iota(jnp.int32, (n_sel, 1), 0)
    jrow_f = jrow.astype(f32)

    def group_rows(full):
        return jnp.where(group0, full[0:HD, :], full[HD:2 * HD, :])

    def pv(v, pt):
        return group_rows(lax.dot_general(v.astype(bf), pt, contract_rows, preferred_element_type=f32))

    li = lax.broadcasted_iota(jnp.int32, (LN, LN), 0)
    lj = lax.broadcasted_iota(jnp.int32, (LN, LN), 1)
    same = ((li // (NSA_R * SQ) == lj // (NSA_R * SQ)) & (li % SQ == lj % SQ)).astype(bf)

    def head_sum(p):
        hi = p.astype(bf)
        r1 = p - hi.astype(f32)
        mid = r1.astype(bf)
        lo = (r1 - mid.astype(f32)).astype(bf)
        return (jnp.dot(hi, same, preferred_element_type=f32) + jnp.dot(mid, same, preferred_element_type=f32)
                + jnp.dot(lo, same, preferred_element_type=f32))

    s_c, mk_c = [], []
    m = jnp.full((1, LN), NEG, f32)
    for c in range(CMP_PER_SEL):
        s = jnp.dot(kc_ref[0, c * n_sel:(c + 1) * n_sel, :], qbd, preferred_element_type=f32)
        mk = jrow * SEL_BLOCK + (c * CMP_STRIDE + CMP_BLOCK - 1) <= pos_l
        s = jnp.where(mk, s, NEG)
        m = jnp.maximum(m, jnp.max(s, axis=0, keepdims=True))
        s_c.append(s)
        mk_c.append(mk)
    e_c = [jnp.where(mk_c[c], jnp.exp(s_c[c] - m), 0.0) for c in range(CMP_PER_SEL)]
    l = e_c[0].sum(axis=0, keepdims=True)
    for c in range(1, CMP_PER_SEL):
        l = l + e_c[c].sum(axis=0, keepdims=True)
    inv = 1.0 / jnp.maximum(l, TINY)
    o_cmp = jnp.zeros((HD, LN), f32)
    pg = []
    for c in range(CMP_PER_SEL):
        p = e_c[c] * inv
        o_cmp = o_cmp + pv(vc_ref[0, c * n_sel:(c + 1) * n_sel, :], p.astype(bf))
        pg.append(head_sum(p))

    cur = pos_l // SEL_BLOCK
    forced = (jrow == 0) | (jrow == cur) | (jrow == cur - 1)
    allowed = jrow * SEL_BLOCK <= pos_l
    last = pg[CMP_PER_SEL - 1]
    prev = jnp.where(jrow == 0, 0.0, pltpu.roll(last, 1, 0))
    inner = pg[0]
    for c in range(1, CMP_PER_SEL - 1):
        inner = inner + pg[c]
    sc_s[...] = jnp.where(forced, FORCE, jnp.where(allowed, 2.0 * inner + last + prev, -1.0))
    sel_s[...] = jnp.zeros((n_sel, LN), f32)

    def pick(_, carry):
        s = sc_s[...]
        top = jnp.max(s, axis=0, keepdims=True)
        first = jnp.min(jnp.where(s == top, jrow_f, float(n_sel)), axis=0, keepdims=True)
        hit = jrow_f == first
        sc_s[...] = jnp.where(hit, -jnp.inf, s)
        sel_s[...] = jnp.where(hit, 1.0, sel_s[...])
        return carry

    lax.fori_loop(0, SEL_TOPN, pick, 0)
    sel_s[...] = jnp.where(allowed, sel_s[...], 0.0)

    def reset():
        m_s[...] = jnp.full((1, LN), M_INIT, f32)
        l_s[...] = jnp.zeros((1, LN), f32)
        acc_s[...] = jnp.zeros((HD, LN), f32)

    def update(s_blocks, v):
        m_old = m_s[...]
        m_new = m_old
        for s in s_blocks:
            m_new = jnp.maximum(m_new, jnp.max(s, axis=0, keepdims=True))
        alpha = jnp.exp(m_old - m_new)
        e_blocks = [jnp.exp(s - m_new) for s in s_blocks]
        l_new = l_s[...] * alpha
        for e in e_blocks:
            l_new = l_new + e.sum(axis=0, keepdims=True)
        pt = e_blocks[0] if len(e_blocks) == 1 else jnp.concatenate(e_blocks, axis=0)
        acc_s[...] = acc_s[...] * alpha + pv(v, pt.astype(bf))
        m_s[...] = m_new
        l_s[...] = l_new

    def finish():
        return acc_s[...] * (1.0 / jnp.maximum(l_s[...], TINY))

    def scores(kv):
        return jnp.dot(kv[:, 0:half].astype(bf), qbd, preferred_element_type=f32)

    reset()

    def slc_body(kt, carry):
        kv = buf[slot, pl.ds(pl.multiple_of(kt * SLC_TILE, SLC_TILE), SLC_TILE), :]
        s = scores(kv)
        selb = sel_s[pl.ds(pl.multiple_of(kt * BLK_PER_TILE, BLK_PER_TILE), BLK_PER_TILE), :]
        blocks = [jnp.where(selb[i:i + 1, :] > 0.5, s[i * SEL_BLOCK:(i + 1) * SEL_BLOCK, :], NEG)
                  for i in range(BLK_PER_TILE)]
        update(blocks, kv[:, half:])
        return carry

    lax.fori_loop(0, past // SLC_TILE, slc_body, 0)
    rows_new = lax.broadcasted_iota(jnp.int32, (SEL_BLOCK, 1), 0)
    kv = nslc_ref[0]
    keep = (sel_s[past // SEL_BLOCK:past // SEL_BLOCK + 1, :] > 0.5) & (past + rows_new <= pos_l)
    update([jnp.where(keep, scores(kv), NEG)], kv[:, half:])
    o_slc = finish()

    reset()
    kv = wcache_ref[0, 0]
    wb = kv.shape[0]
    rel = pos_l - (past - wb + lax.broadcasted_iota(jnp.int32, (wb, 1), 0))
    update([jnp.where((rel >= 0) & (rel <= WINDOW), scores(kv), NEG)], kv[:, half:])
    kv = nwin_ref[0]
    rel = pos_l - (past + rows_new)
    update([jnp.where((rel >= 0) & (rel <= WINDOW) & (rows_new < t_new), scores(kv), NEG)], kv[:, half:])
    o_win = finish()

    gate = jax.nn.sigmoid(gate_ref[0])
    o_ref[0] = gate[0:1, :] * o_cmp + gate[1:2, :] * o_slc + gate[2:3, :] * o_win


def _pad_rows(a, rows):
    return jnp.pad(a, ((0, 0), (0, rows - a.shape[1]), (0, 0)))


def _nsa_sample(q, kv_cmp, kv_slc, kv_win, gates, pool_cmp, pool_slc, page_table, win_cache, layer, cmp_w):
    B, Tn = q.shape[:2]
    G, R, HD = NSA_KV_HEADS, NSA_R, NSA_HD
    n_pages = page_table.shape[1]
    n_pool = pool_cmp.shape[1]
    past = n_pages * PAGE_SIZE
    wb = win_cache.shape[2]
    assert Tn <= SQ and Tn <= SEL_BLOCK and past % SLC_TILE == 0 and wb % SUBLANES == 0
    bf = jnp.bfloat16
    half = G * HD
    kvc = _compress_paged(page_table, pool_cmp.reshape(DEPTH * n_pool, PAGE_ROWS, CMP_ROW), layer, cmp_w)
    n_blk = past // CMP_STRIDE
    n_sel = _round_up(past // SEL_BLOCK + 1, BF16_SUBLANES)
    kvp = kvc.reshape(B, n_blk // CMP_PER_SEL, CMP_PER_SEL, NSA_KV_WIDTH).transpose(0, 2, 1, 3)
    kvp = jnp.pad(kvp, ((0, 0), (0, 0), (0, n_sel - n_blk // CMP_PER_SEL), (0, 0)))
    kvp = kvp.reshape(B, CMP_PER_SEL * n_sel, NSA_KV_WIDTH).astype(bf)
    kc, vc = kvp[..., :half], kvp[..., half:]
    qt = jnp.pad((q * (HD ** -0.5)).reshape(B, Tn, G, R, HD), ((0, 0), (0, SQ - Tn), (0, 0), (0, 0), (0, 0)))
    qt = qt.transpose(0, 2, 4, 3, 1)
    qbd = jnp.einsum('bgdrq,gh->bgdhrq', qt, jnp.eye(G, dtype=jnp.float32)).reshape(B, half, SAMPLE_LANES).astype(bf)
    gate = jnp.pad(gates.reshape(B, Tn, G, R, 3), ((0, 0), (0, SQ - Tn), (0, 0), (0, 0), (0, 0)))
    gate = gate.transpose(0, 4, 2, 3, 1).reshape(B, 3, SAMPLE_LANES)
    nslc = _pad_rows(kv_slc.reshape(B, Tn, NSA_KV_WIDTH), SEL_BLOCK)
    nwin = _pad_rows(kv_win.reshape(B, Tn, NSA_KV_WIDTH), SEL_BLOCK)
    per_b = lambda shape: pl.BlockSpec((1,) + shape, lambda b, pt: (b,) + (0,) * len(shape))
    out = pl.pallas_call(
        functools.partial(_nsa_sample_kernel, page_base=layer * n_pool, n_pages=n_pages, n_sel=n_sel, t_new=Tn),
        grid_spec=pltpu.PrefetchScalarGridSpec(
            num_scalar_prefetch=1,
            grid=(B,),
            in_specs=[
                pl.BlockSpec(memory_space=pl.ANY),
                per_b((half, SAMPLE_LANES)), per_b((3, SAMPLE_LANES)),
                per_b((CMP_PER_SEL * n_sel, half)), per_b((CMP_PER_SEL * n_sel, half)),
                per_b((SEL_BLOCK, NSA_KV_WIDTH)),
                pl.BlockSpec((1, 1, wb, NSA_KV_WIDTH), lambda b, pt: (layer, b, 0, 0)),
                per_b((SEL_BLOCK, NSA_KV_WIDTH)),
            ],
            out_specs=per_b((HD, SAMPLE_LANES)),
            scratch_shapes=[
                pltpu.VMEM((2, past, NSA_KV_WIDTH), jnp.float32),
                pltpu.SemaphoreType.DMA((2,)),
                pltpu.VMEM((n_sel, SAMPLE_LANES), jnp.float32),
                pltpu.VMEM((n_sel, SAMPLE_LANES), jnp.float32),
                pltpu.VMEM((1, SAMPLE_LANES), jnp.float32),
                pltpu.VMEM((1, SAMPLE_LANES), jnp.float32),
                pltpu.VMEM((HD, SAMPLE_LANES), jnp.float32),
            ]),
        out_shape=jax.ShapeDtypeStruct((B, HD, SAMPLE_LANES), jnp.float32),
        compiler_params=pltpu.CompilerParams(
            dimension_semantics=("arbitrary",), vmem_limit_bytes=VMEM_LIMIT_BYTES),
        name="nsa_sample_attention",
    )(page_table, pool_slc.reshape(DEPTH * n_pool, PAGE_SIZE, NSA_KV_WIDTH), qbd, gate, kc, vc, nslc,
      win_cache.reshape(DEPTH, B, wb, NSA_KV_WIDTH), nwin)
    o = out.reshape(B, HD, G, R, SQ)[..., :Tn].transpose(0, 4, 2, 3, 1).reshape(B, Tn, NSA_WIDTH)
    win_all = jnp.concatenate([win_cache[layer], kv_win.astype(win_cache.dtype)], axis=1)
    return o, win_all[:, -min(WINDOW, wb + Tn):]


GLA_LANES = GLA_HEADS * GLA_DK
GLA_KERNEL_CHUNK = 32
GLA_BLOCK_ROWS = 512


def _head_block_mask(dtype):
    r = lax.broadcasted_iota(jnp.int32, (GLA_LANES, GLA_LANES), 0) // GLA_DK
    c = lax.broadcasted_iota(jnp.int32, (GLA_LANES, GLA_LANES), 1) // GLA_DK
    return (r == c).astype(dtype)


def _gla_kernel(q_ref, k_ref, v_ref, gg_ref, ga_ref, wa_ref, ba_ref, gn_ref, st0_ref, o_ref, st_ref,
                st_s, kp_s, bp_s, vp_s, *, c, n_chunks, valid_rows):
    f32, bf = jnp.float32, jnp.bfloat16

    @pl.when(pl.program_id(1) == 0)
    def _():
        st_s[...] = st0_ref[0]
        zeros = jnp.zeros((c, GLA_LANES), f32)
        kp_s[0:c, :] = zeros
        bp_s[0:c, :] = zeros
        vp_s[0:c, :] = zeros

    row = lax.broadcasted_iota(jnp.int32, (c, 1), 0)
    tril = (lax.broadcasted_iota(jnp.int32, (c, c), 0) >= lax.broadcasted_iota(jnp.int32, (c, c), 1)).astype(f32)
    ones_blk = _head_block_mask(bf)
    blk_f32 = _head_block_mask(f32)
    contract_last = (((1,), (1,)), ((), ()))
    contract_rows = (((0,), (0,)), ((), ()))

    def head_sum(x):
        hi = x.astype(bf)
        lo = (x - hi.astype(f32)).astype(bf)
        return (jnp.dot(hi, ones_blk, preferred_element_type=f32)
                + jnp.dot(lo, ones_blk, preferred_element_type=f32))

    def chunk(ch, carry):
        rows = pl.ds(pl.multiple_of(ch * c, c), c)
        q = q_ref[rows, :] * (GLA_DK ** -0.5)
        k = k_ref[rows, :]
        v = v_ref[rows, :]
        z = jnp.dot(ga_ref[rows, :].astype(bf), wa_ref[...], preferred_element_type=f32) + ba_ref[...]
        la = (jnp.minimum(z, 0.0) - jnp.log1p(jnp.exp(-jnp.abs(z)))) / GLA_GATE_TEMP
        if valid_rows < c:
            la = jnp.where(row < valid_rows, la, 0.0)
        b = jnp.dot(tril, la, preferred_element_type=f32, precision=lax.Precision.HIGHEST)
        st = st_s[...]
        o = lax.dot_general((q * jnp.exp(b)).astype(bf), st.astype(bf), contract_last,
                            preferred_element_type=f32)
        kp_s[c:2 * c, :] = k
        bp_s[c:2 * c, :] = b
        vp_s[c:2 * c, :] = v
        for d in range(c):
            if d == 0:
                term = q * k
                vr = v
            else:
                ok = row >= d
                kr = kp_s[c - d:2 * c - d, :]
                br = bp_s[c - d:2 * c - d, :]
                vr = vp_s[c - d:2 * c - d, :]
                term = jnp.where(ok, q * kr * jnp.exp(jnp.where(ok, b - br, 0.0)), 0.0)
            o = o + jnp.dot(term.astype(bf), ones_blk, preferred_element_type=f32) * vr
        ms = head_sum(o * o) * (1.0 / GLA_DV)
        g = gg_ref[rows, :]
        o_ref[rows, :] = o * lax.rsqrt(ms + EPS) * gn_ref[...] * (g * jax.nn.sigmoid(g))
        b_last = b[c - 1:c, :]
        ke = k * jnp.exp(b_last - b)
        upd = lax.dot_general(v.astype(bf), ke.astype(bf), contract_rows, preferred_element_type=f32)
        st_s[...] = st * jnp.exp(b_last) + upd * blk_f32
        return carry

    lax.fori_loop(0, n_chunks, chunk, 0)
    st_ref[0] = st_s[...]


def _gla(proj, nbatch, tb, c, valid_rows, wa_pad, ba, gn, st0):
    m = proj.shape[0]
    nblk = m // nbatch // tb
    colblk = lambda p, w: PROJ_OFF[p] // w
    row_map = lambda j: (lambda b, i: (b * nblk + i, j))
    return pl.pallas_call(
        functools.partial(_gla_kernel, c=c, n_chunks=tb // c, valid_rows=valid_rows),
        grid=(nbatch, nblk),
        in_specs=[
            pl.BlockSpec((tb, GLA_LANES), row_map(colblk(0, GLA_LANES))),
            pl.BlockSpec((tb, GLA_LANES), row_map(colblk(1, GLA_LANES))),
            pl.BlockSpec((tb, GLA_LANES), row_map(colblk(2, GLA_LANES))),
            pl.BlockSpec((tb, GLA_LANES), row_map(colblk(3, GLA_LANES))),
            pl.BlockSpec((tb, LANES), row_map(colblk(4, LANES))),
            pl.BlockSpec((LANES, GLA_LANES), lambda b, i: (0, 0)),
            pl.BlockSpec((1, GLA_LANES), lambda b, i: (0, 0)),
            pl.BlockSpec((1, GLA_LANES), lambda b, i: (0, 0)),
            pl.BlockSpec((1, GLA_LANES, GLA_LANES), lambda b, i: (b, 0, 0)),
        ],
        out_specs=[pl.BlockSpec((tb, GLA_LANES), lambda b, i: (b * nblk + i, 0)),
                   pl.BlockSpec((1, GLA_LANES, GLA_LANES), lambda b, i: (b, 0, 0))],
        out_shape=[jax.ShapeDtypeStruct((m, GLA_LANES), jnp.float32),
                   jax.ShapeDtypeStruct((nbatch, GLA_LANES, GLA_LANES), jnp.float32)],
        scratch_shapes=[pltpu.VMEM((GLA_LANES, GLA_LANES), jnp.float32),
                        pltpu.VMEM((2 * c, GLA_LANES), jnp.float32),
                        pltpu.VMEM((2 * c, GLA_LANES), jnp.float32),
                        pltpu.VMEM((2 * c, GLA_LANES), jnp.float32)],
        compiler_params=pltpu.CompilerParams(
            dimension_semantics=("arbitrary", "arbitrary"), vmem_limit_bytes=VMEM_LIMIT_BYTES),
        name="gla_scan",
    )(proj, proj, proj, proj, proj, wa_pad, ba, gn, st0)


def _gla_state_in(s0):
    eye = jnp.eye(GLA_HEADS, dtype=jnp.float32)
    return jnp.einsum('bhde,hg->bhegd', s0.astype(jnp.float32), eye).reshape(-1, GLA_LANES, GLA_LANES)


def _gla_state_out(st):
    blocks = [st[:, h * GLA_DV:(h + 1) * GLA_DV, h * GLA_DK:(h + 1) * GLA_DK] for h in range(GLA_HEADS)]
    return jnp.stack(blocks, axis=1).transpose(0, 1, 3, 2)


def _causal_dwconv(u, hist, w):
    T = u.shape[1]
    up = jnp.concatenate([hist.astype(u.dtype), u], axis=1)
    y = w[0] * up[:, 0:T]
    for k in range(1, CONV_W):
        y = y + w[k] * up[:, k:k + T]
    return y, up[:, -(CONV_W - 1):]


def _nsa_prompt(q, kv_cmp, kv_slc, kv_win, gates, cmp_w):
    B, T = q.shape[:2]
    assert B == 1 and T % SLC_TILE == 0 and T >= WIN_TILE
    n_rows = T // CMP_STRIDE
    kvc = _compress(kv_cmp.reshape(n_rows, CMP_ROW), cmp_w, min(256, n_rows))
    o = _nsa_prompt_attention(q.reshape(T, NSA_WIDTH), gates.reshape(T, 3 * NSA_HEADS), kvc,
                              kv_slc.reshape(T, NSA_KV_WIDTH), kv_win.reshape(T, NSA_KV_WIDTH))
    return o.reshape(B, T, NSA_WIDTH), kv_win[:, -min(WINDOW, T):]


def _expand_rows(v, t):
    if v.shape[0] == 1:
        return v
    return jnp.repeat(v, t, axis=0)


def _trunk_layer(x, mod, lw, gla_s0, sc_hist, ffn_hist, nsa_apply, tm, final, g_final):
    B, T, _ = x.shape
    m = B * T
    grouped = B > 1
    ssh1, ssc1, sgt1, ssh2, ssc2, sgt2 = [_expand_rows(v, T) for v in jnp.split(mod, 6, axis=-1)]
    x2 = x.reshape(m, D_MODEL)
    proj = _in_proj(x2, lw['norm_mix'], ssc1, ssh1, lw['w_in'], tm).reshape(B, T, PROJ_WIDTH)
    gq, gk, gv, gg, ga, sb, scc, shh, nq, ncmp, nslc, nwin, ngate = [_proj_piece(proj, p) for p in range(13)]
    heads = lambda a, d: a.reshape(B, T, -1, d)
    if grouped:
        t_pad = _round_up(T, SUBLANES)
        gla_in = jnp.pad(proj, ((0, 0), (0, t_pad - T), (0, 0))).reshape(B * t_pad, PROJ_WIDTH)
        tb = chunk = t_pad
    else:
        t_pad, gla_in, tb, chunk = T, proj.reshape(m, PROJ_WIDTH), GLA_BLOCK_ROWS, GLA_KERNEL_CHUNK
    o_gla, st_gla = _gla(gla_in, B, tb, chunk, T if grouped else chunk,
                         lw['gla_wa'], lw['gla_ba'], lw['gla_norm'], _gla_state_in(gla_s0))
    o_gla = o_gla.reshape(B, t_pad, GLA_WIDTH)[:, :T]
    s_gla = _gla_state_out(st_gla)
    conv_out, sc_state = _causal_dwconv(scc * shh, sc_hist, lw['sc_conv'])
    o_sc = sb * conv_out
    kvr = lambda a: a.reshape(B, T, 2, NSA_KV_HEADS, NSA_HD)
    kv_cmp, kv_slc, kv_win = kvr(ncmp), kvr(nslc), kvr(nwin)
    o_nsa, win_state = nsa_apply(heads(nq, NSA_HD), kv_cmp, kv_slc, kv_win, heads(ngate, 3))
    mix = jnp.concatenate([o_gla, o_sc, o_nsa], axis=-1).reshape(m, D_MODEL)
    x2 = _out_proj(x2, mix, sgt1, lw['w_out'], tm)

    ff2 = 2 * D_FF
    if grouped:
        zero = jnp.zeros((B, 1, ff2), jnp.float32)
        h1 = jnp.concatenate([ffn_hist[:, 1:2], zero, zero, zero], axis=1).reshape(m, ff2)
        h2 = jnp.concatenate([ffn_hist[:, 0:1], ffn_hist[:, 1:2], zero, zero], axis=1).reshape(m, ff2)
        hist = (h1, h2)
    else:
        hist = jnp.concatenate(
            [jnp.zeros((HIST_ROWS - (CONV_W - 1), ff2), jnp.float32), ffn_hist[0]], axis=0)
    y, st = _ffn(x2, lw['norm_ffn'], ssc2, ssh2, sgt2, lw['ffn_up'], lw['ffn_conv'], lw['ffn_down'],
                 g_final, hist, tm, T if grouped else 0, final)
    if grouped:
        ffn_state = st.reshape(B, T, ff2)[:, -(CONV_W - 1):]
    else:
        ffn_state = st[None, -(CONV_W - 1):]
    return (y.reshape(B, T, D_MODEL), kv_cmp, kv_slc, win_state, s_gla.astype(gla_s0.dtype), sc_state, ffn_state)


def kernel(x_prompt, x_sample, cache_nsa_cmp, cache_nsa_slc, cache_nsa_win, state_gla, state_shortconv, state_ffn_conv, page_table, c_prompt, c_sample, mod_w, mod_b, norm_mix, norm_ffn, w_in, gla_wa2, gla_ba, gla_norm, sc_conv, nsa_cmp_pos, nsa_cmp_w1, nsa_cmp_w2, w_out, ffn_up, ffn_conv, ffn_down, norm_final):
    xp, xs = x_prompt, x_sample
    bp, bs = xp.shape[0], xs.shape[0]
    assert bp == 1 and xs.shape[1] == 4
    c_rows = _round_up(bp + bs, SUBLANES)
    c_all = jnp.concatenate([c_prompt, c_sample, jnp.zeros((c_rows - bp - bs, D_MODEL), jnp.float32)], axis=0)
    mod_all = _modulation(c_all, mod_w, mod_b)
    g_final = norm_final.reshape(1, D_MODEL)
    outs = [[] for _ in range(12)]
    for l in range(DEPTH):
        lw = dict(
            norm_mix=norm_mix[l].reshape(1, D_MODEL), norm_ffn=norm_ffn[l].reshape(1, D_MODEL),
            w_in=_pack_w_in(w_in[l]),
            gla_wa=jnp.zeros((LANES, GLA_LANES), jnp.bfloat16).at[:GLA_GATE_RANK].set(
                gla_wa2[l].astype(jnp.bfloat16)),
            gla_ba=gla_ba[l].reshape(1, GLA_LANES),
            gla_norm=jnp.tile(gla_norm[l], GLA_HEADS).reshape(1, GLA_LANES),
            sc_conv=sc_conv[l], w_out=w_out[l].astype(jnp.bfloat16),
            ffn_up=ffn_up[l].astype(jnp.bfloat16), ffn_conv=ffn_conv[l],
            ffn_down=ffn_down[l].astype(jnp.bfloat16))
        cmp_params = (nsa_cmp_pos[l], nsa_cmp_w1[l], nsa_cmp_w2[l])
        cmp_w = _compress_weights(*cmp_params)
        final = l == DEPTH - 1
        res_p = _trunk_layer(
            xp, mod_all[l, 0:bp], lw,
            jnp.zeros((bp, GLA_HEADS, GLA_DK, GLA_DV), xp.dtype),
            jnp.zeros((bp, CONV_W - 1, SC_WIDTH), xp.dtype),
            jnp.zeros((bp, CONV_W - 1, 2 * D_FF), xp.dtype),
            functools.partial(_nsa_prompt, cmp_w=cmp_w), 256, final, g_final)
        res_s = _trunk_layer(
            xs, mod_all[l, bp:bp + bs], lw, state_gla[l], state_shortconv[l], state_ffn_conv[l],
            functools.partial(_nsa_sample, pool_cmp=cache_nsa_cmp, pool_slc=cache_nsa_slc,
                              page_table=page_table, win_cache=cache_nsa_win, layer=l,
                              cmp_w=cmp_w), bs * xs.shape[1], final, g_final)
        xp, xs = res_p[0], res_s[0]
        for k in range(6):
            outs[2 * k].append(res_p[k + 1])
            outs[2 * k + 1].append(res_s[k + 1])
    return (xp, xs) + tuple(jnp.stack(o) for o in outs)
```

```python
import functools

import jax
import jax.numpy as jnp
from jax import lax
from jax.experimental import pallas as pl
from jax.experimental.pallas import tpu as pltpu

D_MODEL = 1024
DEPTH = 2
PAGE_SIZE = 128
GLA_HEADS = 4
GLA_DK = D_MODEL // 16
GLA_DV = D_MODEL // 16
GLA_WIDTH = GLA_HEADS * GLA_DV
GLA_GATE_RANK = 16
GLA_GATE_TEMP = 16.0
GLA_CHUNK = 64
SC_WIDTH = D_MODEL // 4
CONV_W = 3
NSA_HEADS = 8
NSA_KV_HEADS = 2
NSA_HD = D_MODEL // 16
NSA_WIDTH = NSA_HEADS * NSA_HD
NSA_KV_WIDTH = 2 * NSA_KV_HEADS * NSA_HD
CMP_STRIDE = 16
CMP_BLOCK = 2 * CMP_STRIDE
CMP_HIDDEN = 128
SEL_BLOCK = 64
SEL_TOPN = 16
WINDOW = 512
Q_BLOCK = 128
D_FF = 2816
EPS = 1e-6
NEG = -1e30
TINY = 1e-30
FORCE = 1e9

IN_SIZES = (
    GLA_HEADS * GLA_DK, GLA_HEADS * GLA_DK, GLA_WIDTH, GLA_WIDTH, GLA_GATE_RANK,
    SC_WIDTH, SC_WIDTH, SC_WIDTH,
    NSA_WIDTH, NSA_KV_WIDTH, NSA_KV_WIDTH, NSA_KV_WIDTH, NSA_HEADS * 3,
)

LANES = 128
SUBLANES = 8
BF16_SUBLANES = 16
VMEM_LIMIT_BYTES = 56 * 1024 * 1024

PROJ_ORDER = (0, 1, 2, 3, 5, 6, 7, 8, 9, 10, 11, 4, 12)


def _round_up(n, m):
    return -(-n // m) * m


def _proj_layout():
    src, acc = [], 0
    for s in IN_SIZES:
        src.append(acc)
        acc += s
    offs, dst = {}, 0
    for p in PROJ_ORDER:
        offs[p] = dst
        dst += _round_up(IN_SIZES[p], LANES)
    return src, offs, dst


PROJ_SRC, PROJ_OFF, PROJ_WIDTH = _proj_layout()


def _pack_w_in(w_in):
    out = jnp.zeros((D_MODEL, PROJ_WIDTH), jnp.bfloat16)
    for p in PROJ_ORDER:
        piece = w_in[:, PROJ_SRC[p]:PROJ_SRC[p] + IN_SIZES[p]].astype(jnp.bfloat16)
        out = lax.dynamic_update_slice(out, piece, (0, PROJ_OFF[p]))
    return out


def _proj_piece(proj, p):
    return proj[..., PROJ_OFF[p]:PROJ_OFF[p] + IN_SIZES[p]]


def _mod_kernel(c_ref, w_ref, b_ref, o_ref):
    c = c_ref[...]
    a = c * jax.nn.sigmoid(c)
    o_ref[0] = jnp.dot(a, w_ref[0], preferred_element_type=jnp.float32,
                       precision=lax.Precision.HIGHEST) + b_ref[0]


def _modulation(c_all, mod_w, mod_b):
    rows = c_all.shape[0]
    tn = 1024
    n = mod_w.shape[-1]
    return pl.pallas_call(
        _mod_kernel,
        grid=(DEPTH, n // tn),
        in_specs=[
            pl.BlockSpec((rows, D_MODEL), lambda l, j: (0, 0)),
            pl.BlockSpec((1, D_MODEL, tn), lambda l, j: (l, 0, j)),
            pl.BlockSpec((1, 1, tn), lambda l, j: (l, 0, j)),
        ],
        out_specs=pl.BlockSpec((1, rows, tn), lambda l, j: (l, 0, j)),
        out_shape=jax.ShapeDtypeStruct((DEPTH, rows, n), jnp.float32),
        name="adaln_modulation",
    )(c_all, mod_w, mod_b.reshape(DEPTH, 1, n))


def _norm_mod(x, g, sc, sh):
    r = lax.rsqrt(jnp.mean(x * x, axis=-1, keepdims=True) + EPS)
    return (x * r * g) * (1.0 + sc) + sh


def _in_proj_kernel(x_ref, g_ref, sc_ref, sh_ref, w_ref, o_ref):
    h = _norm_mod(x_ref[...], g_ref[...], sc_ref[...], sh_ref[...])
    o_ref[...] = jnp.dot(h.astype(jnp.bfloat16), w_ref[...], preferred_element_type=jnp.float32)


def _row_spec(tm, per_row):
    if per_row:
        return pl.BlockSpec((tm, D_MODEL), lambda i: (i, 0))
    return pl.BlockSpec((1, D_MODEL), lambda i: (0, 0))


def _resident(shape):
    return pl.BlockSpec(shape, lambda i: (0,) * len(shape), pipeline_mode=pl.Buffered(1))


def _in_proj(x, g, sc, sh, w_packed, tm):
    m = x.shape[0]
    per_row = sc.shape[0] != 1
    return pl.pallas_call(
        _in_proj_kernel,
        grid=(m // tm,),
        in_specs=[
            pl.BlockSpec((tm, D_MODEL), lambda i: (i, 0)),
            _resident((1, D_MODEL)),
            _row_spec(tm, per_row),
            _row_spec(tm, per_row),
            _resident((D_MODEL, PROJ_WIDTH)),
        ],
        out_specs=pl.BlockSpec((tm, PROJ_WIDTH), lambda i: (i, 0)),
        out_shape=jax.ShapeDtypeStruct((m, PROJ_WIDTH), jnp.float32),
        compiler_params=pltpu.CompilerParams(
            dimension_semantics=("arbitrary",), vmem_limit_bytes=VMEM_LIMIT_BYTES),
        name="norm_in_proj",
    )(x, g, sc, sh, w_packed)


def _out_proj_kernel(x_ref, mix_ref, gt_ref, w_ref, o_ref):
    y = jnp.dot(mix_ref[...].astype(jnp.bfloat16), w_ref[...], preferred_element_type=jnp.float32)
    o_ref[...] = x_ref[...] + gt_ref[...] * y


def _out_proj(x, mix, gt, w_bf16, tm):
    m = x.shape[0]
    per_row = gt.shape[0] != 1
    return pl.pallas_call(
        _out_proj_kernel,
        grid=(m // tm,),
        in_specs=[
            pl.BlockSpec((tm, D_MODEL), lambda i: (i, 0)),
            pl.BlockSpec((tm, D_MODEL), lambda i: (i, 0)),
            _row_spec(tm, per_row),
            _resident((D_MODEL, D_MODEL)),
        ],
        out_specs=pl.BlockSpec((tm, D_MODEL), lambda i: (i, 0)),
        out_shape=jax.ShapeDtypeStruct((m, D_MODEL), jnp.float32),
        compiler_params=pltpu.CompilerParams(
            dimension_semantics=("arbitrary",), vmem_limit_bytes=VMEM_LIMIT_BYTES),
        name="out_proj_residual",
    )(x, mix, gt, w_bf16)


FFN_UP_CHUNK = 512
FFN_ACT_CHUNK = 256
HIST_ROWS = SUBLANES


def _ffn_kernel(*refs, tm, group, final):
    grouped = group > 0
    if grouped:
        (x_ref, g_ref, sc_ref, sh_ref, gt_ref, wup_ref, cw_ref, wdn_ref, gf_ref,
         h1_ref, h2_ref, o_ref, st_ref, up_s) = refs
    else:
        (x_ref, g_ref, sc_ref, sh_ref, gt_ref, wup_ref, cw_ref, wdn_ref, gf_ref,
         h0_ref, o_ref, st_ref, up_s) = refs

        @pl.when(pl.program_id(0) == 0)
        def _():
            up_s[0:HIST_ROWS, :] = h0_ref[...]

    x = x_ref[...]
    h = _norm_mod(x, g_ref[...], sc_ref[...], sh_ref[...]).astype(jnp.bfloat16)
    for c in range(2 * D_FF // FFN_UP_CHUNK):
        cols = slice(c * FFN_UP_CHUNK, (c + 1) * FFN_UP_CHUNK)
        up_s[HIST_ROWS:HIST_ROWS + tm, cols] = jnp.dot(
            h, wup_ref[:, cols], preferred_element_type=jnp.float32)

    if grouped:
        t = lax.broadcasted_iota(jnp.int32, (tm, 1), 0) % group

    def conv(cols):
        cur = up_s[HIST_ROWS:HIST_ROWS + tm, cols]
        p1 = up_s[HIST_ROWS - 1:HIST_ROWS - 1 + tm, cols]
        p2 = up_s[HIST_ROWS - 2:HIST_ROWS - 2 + tm, cols]
        if grouped:
            p1 = jnp.where(t == 0, h1_ref[:, cols], p1)
            p2 = jnp.where(t <= 1, h2_ref[:, cols], p2)
        return cw_ref[0:1, cols] * p2 + cw_ref[1:2, cols] * p1 + cw_ref[2:3, cols] * cur

    acc = jnp.zeros((tm, D_MODEL), jnp.float32)
    for c in range(D_FF // FFN_ACT_CHUNK):
        a = conv(slice(c * FFN_ACT_CHUNK, (c + 1) * FFN_ACT_CHUNK))
        b = conv(slice(D_FF + c * FFN_ACT_CHUNK, D_FF + (c + 1) * FFN_ACT_CHUNK))
        act = (a * jax.nn.sigmoid(a) * b).astype(jnp.bfloat16)
        acc = acc + jnp.dot(act, wdn_ref[c * FFN_ACT_CHUNK:(c + 1) * FFN_ACT_CHUNK, :],
                            preferred_element_type=jnp.float32)
    y = x + gt_ref[...] * acc
    if final:
        r = lax.rsqrt(jnp.mean(y * y, axis=-1, keepdims=True) + EPS)
        y = y * r * gf_ref[...]
    o_ref[...] = y

    if grouped:
        st_ref[...] = up_s[HIST_ROWS:HIST_ROWS + tm, :]
    else:
        tail = up_s[tm:tm + HIST_ROWS, :]
        st_ref[...] = tail
        up_s[0:HIST_ROWS, :] = tail


def _ffn(x, g, sc, sh, gt, wup, cw, wdn, g_final, hist, tm, group, final):
    m = x.shape[0]
    grouped = group > 0
    per_row = sc.shape[0] != 1
    ff2 = 2 * D_FF
    in_specs = [
        pl.BlockSpec((tm, D_MODEL), lambda i: (i, 0)),
        _resident((1, D_MODEL)),
        _row_spec(tm, per_row), _row_spec(tm, per_row), _row_spec(tm, per_row),
        _resident((D_MODEL, ff2)),
        _resident((CONV_W, ff2)),
        _resident((D_FF, D_MODEL)),
        _resident((1, D_MODEL)),
    ]
    if grouped:
        assert m == tm
        in_specs += [_resident((tm, ff2)), _resident((tm, ff2))]
        hist_args = tuple(hist)
        st_rows = tm
    else:
        in_specs += [_resident((HIST_ROWS, ff2))]
        hist_args = (hist,)
        st_rows = HIST_ROWS
    return pl.pallas_call(
        functools.partial(_ffn_kernel, tm=tm, group=group, final=final),
        grid=(m // tm,),
        in_specs=in_specs,
        out_specs=[pl.BlockSpec((tm, D_MODEL), lambda i: (i, 0)),
                   pl.BlockSpec((st_rows, ff2), lambda i: (0, 0))],
        out_shape=[jax.ShapeDtypeStruct((m, D_MODEL), jnp.float32),
                   jax.ShapeDtypeStruct((st_rows, ff2), jnp.float32)],
        scratch_shapes=[pltpu.VMEM((HIST_ROWS + tm, ff2), jnp.float32)],
        compiler_params=pltpu.CompilerParams(
            dimension_semantics=("arbitrary",), vmem_limit_bytes=VMEM_LIMIT_BYTES),
        name="conv_ffn",
    )(x, g, sc, sh, gt, wup, cw, wdn, g_final, *hist_args)


CMP_ROW = CMP_STRIDE * NSA_KV_WIDTH
CMP_HID = 2 * NSA_KV_HEADS * CMP_HIDDEN


def _gelu_tanh(x):
    return 0.5 * x * (1.0 + jnp.tanh(0.7978845608028654 * (x + 0.044715 * (x * x * x))))


def _compress_kernel(x_ref, xn_ref, pos_ref, wl_ref, wt_ref, w2_ref, o_ref, tr_s, *, tm):
    bf = jnp.bfloat16
    f32 = jnp.float32
    x = x_ref[...].astype(bf)
    lead = jnp.dot(x, wl_ref[...], preferred_element_type=f32)
    tr_s[0:tm, :] = jnp.dot(x, wt_ref[...], preferred_element_type=f32)
    tr_s[tm:tm + SUBLANES, :] = jnp.dot(xn_ref[...].astype(bf), wt_ref[...], preferred_element_type=f32)
    bias = (jnp.dot(pos_ref[0].astype(bf), wl_ref[...], preferred_element_type=f32)
            + jnp.dot(pos_ref[1].astype(bf), wt_ref[...], preferred_element_type=f32))[0:1, :]
    hid = _gelu_tanh(lead + tr_s[1:tm + 1, :] + bias)
    o_ref[...] = jnp.dot(hid.astype(bf), w2_ref[...], preferred_element_type=f32)


def _compress_weights(pos_emb, w1, w2):
    eye = jnp.eye(2, dtype=jnp.float32)
    w1f = jnp.einsum('kldh,kK,gG->lkgdKGh', w1, eye, eye).reshape(CMP_BLOCK, NSA_KV_WIDTH, CMP_HID)
    w1f = w1f.astype(jnp.bfloat16)
    w2b = jnp.einsum('khd,kK,gG->kghKGd', w2, eye, eye).reshape(CMP_HID, NSA_KV_WIDTH).astype(jnp.bfloat16)
    posf = jnp.broadcast_to(pos_emb.transpose(1, 0, 2)[:, :, None, :], (CMP_BLOCK, 2, NSA_KV_HEADS, NSA_HD))
    posf = posf.reshape(CMP_BLOCK, NSA_KV_WIDTH)
    pos = jnp.zeros((CMP_BLOCK, SUBLANES, NSA_KV_WIDTH), jnp.float32).at[:, 0].set(posf)
    pos_rows = jnp.zeros((2, SUBLANES, CMP_ROW), jnp.float32).at[:, 0].set(posf.reshape(2, CMP_ROW))
    w1kv = jnp.einsum('kldh,gG->lkgdGh', w1, eye).reshape(
        CMP_BLOCK, 2, NSA_KV_WIDTH // 2, CMP_HID // 2).astype(jnp.bfloat16)
    return dict(pos=pos, w1=w1kv, w2=w2b, pos_rows=pos_rows,
                wl=w1f[:CMP_STRIDE].reshape(CMP_ROW, CMP_HID), wt=w1f[CMP_STRIDE:].reshape(CMP_ROW, CMP_HID))


def _compress(x, cw, tm):
    pos, wl, wt, w2b = cw['pos_rows'], cw['wl'], cw['wt'], cw['w2']
    n = x.shape[0]
    nb8 = n // SUBLANES
    return pl.pallas_call(
        functools.partial(_compress_kernel, tm=tm),
        grid=(n // tm,),
        in_specs=[
            pl.BlockSpec((tm, CMP_ROW), lambda i: (i, 0)),
            pl.BlockSpec((SUBLANES, CMP_ROW), lambda i: (jnp.minimum((i + 1) * (tm // SUBLANES), nb8 - 1), 0)),
            _resident((2, SUBLANES, CMP_ROW)),
            _resident((CMP_ROW, CMP_HID)), _resident((CMP_ROW, CMP_HID)),
            _resident((CMP_HID, NSA_KV_WIDTH)),
        ],
        out_specs=pl.BlockSpec((tm, NSA_KV_WIDTH), lambda i: (i, 0)),
        out_shape=jax.ShapeDtypeStruct((n, NSA_KV_WIDTH), jnp.float32),
        scratch_shapes=[pltpu.VMEM((tm + SUBLANES, CMP_HID), jnp.float32)],
        compiler_params=pltpu.CompilerParams(
            dimension_semantics=("arbitrary",), vmem_limit_bytes=VMEM_LIMIT_BYTES),
        name="nsa_compress",
    )(x, x, pos, wl, wt, w2b)


NSA_R = NSA_HEADS // NSA_KV_HEADS
QL = NSA_R * Q_BLOCK
QLL = NSA_KV_HEADS * QL
SLC_TILE = 512
BLK_PER_TILE = SLC_TILE // SEL_BLOCK
WIN_TILE = WINDOW + Q_BLOCK
CMP_PER_SEL = SEL_BLOCK // CMP_STRIDE
M_INIT = -1e29


def _tile_lanes(v, reps):
    return jnp.concatenate([v] * reps, axis=1)


def _nsa_prompt_kernel(qbd_ref, gate_ref, kc_ref, vct_ref, kslc_ref, vtslc_ref, kwin_ref, vtwin_ref,
                       o_ref, sc_s, sel_s, m_s, l_s, acc_s, *, n_sel):
    f32, bf = jnp.float32, jnp.bfloat16
    G, HD = NSA_KV_HEADS, NSA_HD
    n = pl.program_id(0)
    qbd = qbd_ref[0]
    lane = lax.broadcasted_iota(jnp.int32, (1, Q_BLOCK), 1)
    pos_q = n * Q_BLOCK + lane
    pos_l = _tile_lanes(pos_q, QLL // Q_BLOCK)
    jrow = lax.broadcasted_iota(jnp.int32, (n_sel, 1), 0)

    s_c, mk_c = [], []
    m = jnp.full((1, QLL), NEG, f32)
    for c in range(CMP_PER_SEL):
        s = jnp.dot(kc_ref[c * n_sel:(c + 1) * n_sel, :], qbd, preferred_element_type=f32)
        mk = jrow * SEL_BLOCK + (c * CMP_STRIDE + CMP_BLOCK - 1) <= pos_l
        s = jnp.where(mk, s, NEG)
        m = jnp.maximum(m, jnp.max(s, axis=0, keepdims=True))
        s_c.append(s)
        mk_c.append(mk)
    e_c = [jnp.where(mk_c[c], jnp.exp(s_c[c] - m), 0.0) for c in range(CMP_PER_SEL)]
    l = e_c[0].sum(axis=0, keepdims=True)
    for c in range(1, CMP_PER_SEL):
        l = l + e_c[c].sum(axis=0, keepdims=True)
    inv = 1.0 / jnp.maximum(l, TINY)
    o_cmp = [jnp.zeros((HD, QL), f32) for _ in range(G)]
    pg = []
    for c in range(CMP_PER_SEL):
        p = e_c[c] * inv
        pb = p.astype(bf)
        for g in range(G):
            o_cmp[g] = o_cmp[g] + jnp.dot(vct_ref[g * HD:(g + 1) * HD, c * n_sel:(c + 1) * n_sel],
                                          pb[:, g * QL:(g + 1) * QL], preferred_element_type=f32)
        pg.append([sum(p[:, g * QL + r * Q_BLOCK:g * QL + (r + 1) * Q_BLOCK] for r in range(NSA_R))
                   for g in range(G)])
    o_cmp = jnp.concatenate(o_cmp, axis=1)

    cur = pos_q // SEL_BLOCK
    forced = (jrow == 0) | (jrow == cur) | (jrow == cur - 1)
    allowed = jrow * SEL_BLOCK <= pos_q
    jrow_f = jrow.astype(f32)
    for g in range(G):
        last = pg[CMP_PER_SEL - 1][g]
        prev = jnp.where(jrow == 0, 0.0, pltpu.roll(last, 1, 0))
        inner = pg[0][g]
        for c in range(1, CMP_PER_SEL - 1):
            inner = inner + pg[c][g]
        p_slc = 2.0 * inner + last + prev
        sc_s[g] = jnp.where(forced, FORCE, jnp.where(allowed, p_slc, -1.0))
        sel_s[g] = jnp.zeros((n_sel, Q_BLOCK), f32)

    def pick(_, carry):
        for g in range(G):
            s = sc_s[g]
            top = jnp.max(s, axis=0, keepdims=True)
            first = jnp.min(jnp.where(s == top, jrow_f, float(n_sel)), axis=0, keepdims=True)
            hit = jrow_f == first
            sc_s[g] = jnp.where(hit, -jnp.inf, s)
            sel_s[g] = jnp.where(hit, 1.0, sel_s[g])
        return carry

    lax.fori_loop(0, min(SEL_TOPN, n_sel), pick, 0)
    for g in range(G):
        sel_s[g] = jnp.where(allowed, sel_s[g], 0.0)

    def reset():
        m_s[...] = jnp.full((1, QLL), M_INIT, f32)
        l_s[...] = jnp.zeros((1, QLL), f32)
        acc_s[...] = jnp.zeros((HD, QLL), f32)

    def update(s_blocks, vt_ref, start, rows):
        m_old = m_s[...]
        m_new = m_old
        for s in s_blocks:
            m_new = jnp.maximum(m_new, jnp.max(s, axis=0, keepdims=True))
        alpha = jnp.exp(m_old - m_new)
        e_blocks = [jnp.exp(s - m_new) for s in s_blocks]
        l_new = l_s[...] * alpha
        for e in e_blocks:
            l_new = l_new + e.sum(axis=0, keepdims=True)
        pt = jnp.concatenate([e.astype(bf) for e in e_blocks], axis=0)
        pv = [jnp.dot(vt_ref[g * HD:(g + 1) * HD, pl.ds(start, rows)], pt[:, g * QL:(g + 1) * QL],
                      preferred_element_type=f32) for g in range(G)]
        acc_s[...] = acc_s[...] * alpha + jnp.concatenate(pv, axis=1)
        m_s[...] = m_new
        l_s[...] = l_new

    def finish():
        return acc_s[...] * (1.0 / jnp.maximum(l_s[...], TINY))

    def slc_tile(kt, causal):
        start = pl.multiple_of(kt * SLC_TILE, SLC_TILE)
        s = jnp.dot(kslc_ref[pl.ds(start, SLC_TILE), :], qbd, preferred_element_type=f32)
        selb = [sel_s[g, pl.ds(pl.multiple_of(kt * BLK_PER_TILE, BLK_PER_TILE), BLK_PER_TILE), :]
                for g in range(G)]
        blocks = []
        for i in range(BLK_PER_TILE):
            keep = jnp.concatenate([_tile_lanes(selb[g][i:i + 1, :], NSA_R) for g in range(G)], axis=1) > 0.5
            if causal:
                tok = start + i * SEL_BLOCK + lax.broadcasted_iota(jnp.int32, (SEL_BLOCK, 1), 0)
                keep = keep & (tok <= pos_l)
            blocks.append(jnp.where(keep, s[i * SEL_BLOCK:(i + 1) * SEL_BLOCK, :], NEG))
        update(blocks, vtslc_ref, start, SLC_TILE)

    reset()
    diag = (n * Q_BLOCK) // SLC_TILE

    def slc_body(kt, carry):
        slc_tile(kt, False)
        return carry

    lax.fori_loop(0, diag, slc_body, 0)
    slc_tile(diag, True)
    o_slc = finish()

    reset()
    wstart = pl.multiple_of(jnp.maximum(n * Q_BLOCK - WINDOW, 0), Q_BLOCK)
    s = jnp.dot(kwin_ref[pl.ds(wstart, WIN_TILE), :], qbd, preferred_element_type=f32)
    rel = pos_q - (wstart + lax.broadcasted_iota(jnp.int32, (WIN_TILE, 1), 0))
    bias = _tile_lanes(jnp.where((rel >= 0) & (rel <= WINDOW), 0.0, NEG), QLL // Q_BLOCK)
    update([s + bias], vtwin_ref, wstart, WIN_TILE)
    o_win = finish()

    gate = jax.nn.sigmoid(gate_ref[0])
    o_ref[0] = gate[0:1, :] * o_cmp + gate[1:2, :] * o_slc + gate[2:3, :] * o_win


def _nsa_prompt_attention(nq, ngate, kvc, nslc, nwin):
    T = nq.shape[0]
    nb, n_sel = T // Q_BLOCK, T // SEL_BLOCK
    G, R, HD = NSA_KV_HEADS, NSA_R, NSA_HD
    bf = jnp.bfloat16
    half = G * HD
    qt = (nq * (HD ** -0.5)).reshape(nb, Q_BLOCK, G, R, HD).transpose(0, 2, 4, 3, 1)
    qbd = jnp.einsum('ngdrq,gh->ngdhrq', qt, jnp.eye(G, dtype=jnp.float32)).reshape(nb, half, QLL).astype(bf)
    gate = ngate.reshape(nb, Q_BLOCK, G, R, 3).transpose(0, 4, 2, 3, 1).reshape(nb, 3, QLL)
    kvp = kvc.reshape(n_sel, CMP_PER_SEL, NSA_KV_WIDTH).transpose(1, 0, 2).reshape(T // CMP_STRIDE, NSA_KV_WIDTH)
    kc, vct = kvp[:, :half].astype(bf), kvp[:, half:].T.astype(bf)
    kslc, vtslc = nslc[:, :half].astype(bf), nslc[:, half:].T.astype(bf)
    kwin, vtwin = nwin[:, :half].astype(bf), nwin[:, half:].T.astype(bf)
    out = pl.pallas_call(
        functools.partial(_nsa_prompt_kernel, n_sel=n_sel),
        grid=(nb,),
        in_specs=[
            pl.BlockSpec((1, half, QLL), lambda i: (i, 0, 0)),
            pl.BlockSpec((1, 3, QLL), lambda i: (i, 0, 0)),
            _resident((T // CMP_STRIDE, half)), _resident((half, T // CMP_STRIDE)),
            _resident((T, half)), _resident((half, T)),
            _resident((T, half)), _resident((half, T)),
        ],
        out_specs=pl.BlockSpec((1, HD, QLL), lambda i: (i, 0, 0)),
        out_shape=jax.ShapeDtypeStruct((nb, HD, QLL), jnp.float32),
        scratch_shapes=[
            pltpu.VMEM((G, n_sel, Q_BLOCK), jnp.float32),
            pltpu.VMEM((G, n_sel, Q_BLOCK), jnp.float32),
            pltpu.VMEM((1, QLL), jnp.float32),
            pltpu.VMEM((1, QLL), jnp.float32),
            pltpu.VMEM((HD, QLL), jnp.float32),
        ],
        compiler_params=pltpu.CompilerParams(
            dimension_semantics=("arbitrary",), vmem_limit_bytes=VMEM_LIMIT_BYTES),
        name="nsa_prompt_attention",
    )(qbd, gate, kc, vct, kslc, vtslc, kwin, vtwin)
    return out.reshape(nb, HD, G, R, Q_BLOCK).transpose(0, 4, 2, 3, 1).reshape(T, NSA_WIDTH)


PAGE_ROWS = PAGE_SIZE // CMP_STRIDE
SQ = 16
SAMPLE_LANES = NSA_HEADS * SQ
HALF_PAGES = 64


def _page_view(pool):
    d, n_pool = pool.shape[:2]
    return pool.transpose(0, 1, 3, 4, 5, 2).reshape(d * n_pool, 2, NSA_KV_HEADS * NSA_HD, pool.shape[2])


def _page_copy(pool_ref, buf_ref, sem_ref, page, slot, idx):
    return pltpu.make_async_copy(pool_ref.at[page], buf_ref.at[slot, idx], sem_ref.at[slot])


def _gather_schedule(issue_fn, wait_fn):
    s = pl.program_id(0)

    @pl.when(s == 0)
    def _():
        issue_fn(s, 0)

    @pl.when(s + 1 < pl.num_programs(0))
    def _():
        issue_fn(s + 1, (s + 1) % 2)

    wait_fn(s, s % 2)


def _compress_paged_kernel(pt_ref, pool_ref, pos_ref, w1_ref, w2_ref, o_ref, buf, sem, tok_s,
                           *, page_base, n_pages):
    bf, f32 = jnp.bfloat16, jnp.float32
    rows = HALF_PAGES * PAGE_ROWS

    def copies(step, slot, fn):
        b, half = step // 2, step % 2

        def body(i, c):
            p = jnp.minimum(half * HALF_PAGES + i, n_pages - 1)
            fn(_page_copy(pool_ref, buf, sem, page_base + pt_ref[b, p], slot, i))
            return c

        lax.fori_loop(0, HALF_PAGES + 1, body, 0)

    _gather_schedule(lambda s, slot: copies(s, slot, lambda cp: cp.start()),
                     lambda s, slot: copies(s, slot, lambda cp: cp.wait()))
    slot = pl.program_id(0) % 2

    half = NSA_KV_HEADS * NSA_HD

    def to_token_rows(i, c):
        dst = pl.ds(pl.multiple_of(i * PAGE_SIZE, PAGE_SIZE), PAGE_SIZE)
        for kv in range(2):
            tok_s[kv, dst, :] = buf[slot, i, kv].T
        return c

    lax.fori_loop(0, HALF_PAGES + 1, to_token_rows, 0)
    acc = [jnp.zeros((rows, CMP_HID // 2), f32) for _ in range(2)]
    bias = [jnp.zeros((SUBLANES, CMP_HID // 2), f32) for _ in range(2)]
    for l in range(CMP_BLOCK):
        for kv in range(2):
            x = tok_s.at[kv][pl.ds(l, rows, stride=CMP_STRIDE), :].astype(bf)
            acc[kv] = acc[kv] + jnp.dot(x, w1_ref[l, kv], preferred_element_type=f32)
            bias[kv] = bias[kv] + jnp.dot(pos_ref[l, :, kv * half:(kv + 1) * half].astype(bf), w1_ref[l, kv],
                                          preferred_element_type=f32)
    hid = _gelu_tanh(jnp.concatenate(acc, axis=1) + jnp.concatenate(bias, axis=1)[0:1, :])
    o_ref[...] = jnp.dot(hid.astype(bf), w2_ref[...], preferred_element_type=f32)


def _compress_paged(page_table, pool, layer, cw):
    nbatch, n_pages = page_table.shape
    assert n_pages == 2 * HALF_PAGES
    rows = HALF_PAGES * PAGE_ROWS
    const = lambda shape: pl.BlockSpec(shape, lambda s, pt: (0,) * len(shape), pipeline_mode=pl.Buffered(1))
    return pl.pallas_call(
        functools.partial(_compress_paged_kernel, page_base=layer * (pool.shape[0] // DEPTH), n_pages=n_pages),
        grid_spec=pltpu.PrefetchScalarGridSpec(
            num_scalar_prefetch=1,
            grid=(2 * nbatch,),
            in_specs=[
                pl.BlockSpec(memory_space=pl.ANY),
                const((CMP_BLOCK, SUBLANES, NSA_KV_WIDTH)),
                const((CMP_BLOCK, 2, NSA_KV_WIDTH // 2, CMP_HID // 2)),
                const((CMP_HID, NSA_KV_WIDTH)),
            ],
            out_specs=pl.BlockSpec((rows, NSA_KV_WIDTH), lambda s, pt: (s, 0)),
            scratch_shapes=[
                pltpu.VMEM((2, HALF_PAGES + 1) + pool.shape[1:], jnp.float32),
                pltpu.SemaphoreType.DMA((2,)),
                pltpu.VMEM((2, (HALF_PAGES + 1) * PAGE_SIZE, NSA_KV_WIDTH // 2), jnp.float32),
            ]),
        out_shape=jax.ShapeDtypeStruct((2 * nbatch * rows, NSA_KV_WIDTH), jnp.float32),
        compiler_params=pltpu.CompilerParams(
            dimension_semantics=("arbitrary",), vmem_limit_bytes=VMEM_LIMIT_BYTES),
        name="nsa_compress_paged",
    )(page_table, pool, cw['pos'], cw['w1'], cw['w2'])


def _nsa_sample_kernel(pt_ref, pool_ref, qbd_ref, gate_ref, kc_ref, vc_ref, nslc_ref, wcache_ref, nwin_ref,
                       o_ref, buf, sem, sc_s, sel_s, m_s, l_s, acc_s, *, page_base, n_pages, n_sel, t_new):
    f32, bf = jnp.float32, jnp.bfloat16
    HD, LN = NSA_HD, SAMPLE_LANES
    half = NSA_KV_HEADS * HD
    past = n_pages * PAGE_SIZE
    contract_rows = (((0,), (0,)), ((), ()))

    def copies(step, slot, fn):
        def body(p, c):
            fn(_page_copy(pool_ref, buf, sem, page_base + pt_ref[step, p], slot, p))
            return c

        lax.fori_loop(0, n_pages, body, 0)

    _gather_schedule(lambda s, slot: copies(s, slot, lambda cp: cp.start()),
                     lambda s, slot: copies(s, slot, lambda cp: cp.wait()))
    slot = pl.program_id(0) % 2

    qbd = qbd_ref[0]
    lane = lax.broadcasted_iota(jnp.int32, (1, LN), 1)
    pos_l = past + lane % SQ
    group0 = lane < LN // NSA_KV_HEADS
    jrow = lax.broadcasted_iota(jnp.int32, (n_sel, 1), 0)
    jrow_f = jrow.astype(f32)

    def group_rows(full):
        return jnp.where(group0, full[0:HD, :], full[HD:2 * HD, :])

    def pv(v, pt):
        return group_rows(lax.dot_general(v.astype(bf), pt, contract_rows, preferred_element_type=f32))

    def scores_t(kt):
        return lax.dot_general(kt.astype(bf), qbd, contract_rows, preferred_element_type=f32)

    li = lax.broadcasted_iota(jnp.int32, (LN, LN), 0)
    lj = lax.broadcasted_iota(jnp.int32, (LN, LN), 1)
    same = ((li // (NSA_R * SQ) == lj // (NSA_R * SQ)) & (li % SQ == lj % SQ)).astype(bf)

    def head_sum(p):
        hi = p.astype(bf)
        r1 = p - hi.astype(f32)
        mid = r1.astype(bf)
        lo = (r1 - mid.astype(f32)).astype(bf)
        return (jnp.dot(hi, same, preferred_element_type=f32) + jnp.dot(mid, same, preferred_element_type=f32)
                + jnp.dot(lo, same, preferred_element_type=f32))

    s_c, mk_c = [], []
    m = jnp.full((1, LN), NEG, f32)
    for c in range(CMP_PER_SEL):
        s = jnp.dot(kc_ref[0, c * n_sel:(c + 1) * n_sel, :], qbd, preferred_element_type=f32)
        mk = jrow * SEL_BLOCK + (c * CMP_STRIDE + CMP_BLOCK - 1) <= pos_l
        s = jnp.where(mk, s, NEG)
        m = jnp.maximum(m, jnp.max(s, axis=0, keepdims=True))
        s_c.append(s)
        mk_c.append(mk)
    e_c = [jnp.where(mk_c[c], jnp.exp(s_c[c] - m), 0.0) for c in range(CMP_PER_SEL)]
    l = e_c[0].sum(axis=0, keepdims=True)
    for c in range(1, CMP_PER_SEL):
        l = l + e_c[c].sum(axis=0, keepdims=True)
    inv = 1.0 / jnp.maximum(l, TINY)
    o_cmp = jnp.zeros((HD, LN), f32)
    pg = []
    for c in range(CMP_PER_SEL):
        p = e_c[c] * inv
        o_cmp = o_cmp + pv(vc_ref[0, c * n_sel:(c + 1) * n_sel, :], p.astype(bf))
        pg.append(head_sum(p))

    cur = pos_l // SEL_BLOCK
    forced = (jrow == 0) | (jrow == cur) | (jrow == cur - 1)
    allowed = jrow * SEL_BLOCK <= pos_l
    last = pg[CMP_PER_SEL - 1]
    prev = jnp.where(jrow == 0, 0.0, pltpu.roll(last, 1, 0))
    inner = pg[0]
    for c in range(1, CMP_PER_SEL - 1):
        inner = inner + pg[c]
    sc_s[...] = jnp.where(forced, FORCE, jnp.where(allowed, 2.0 * inner + last + prev, -1.0))
    sel_s[...] = jnp.zeros((n_sel, LN), f32)

    def pick(_, carry):
        s = sc_s[...]
        top = jnp.max(s, axis=0, keepdims=True)
        first = jnp.min(jnp.where(s == top, jrow_f, float(n_sel)), axis=0, keepdims=True)
        hit = jrow_f == first
        sc_s[...] = jnp.where(hit, -jnp.inf, s)
        sel_s[...] = jnp.where(hit, 1.0, sel_s[...])
        return carry

    lax.fori_loop(0, SEL_TOPN, pick, 0)
    sel_s[...] = jnp.where(allowed, sel_s[...], 0.0)

    def reset():
        m_s[...] = jnp.full((1, LN), M_INIT, f32)
        l_s[...] = jnp.zeros((1, LN), f32)
        acc_s[...] = jnp.zeros((HD, LN), f32)

    def update(s_blocks, pv_fn):
        m_old = m_s[...]
        m_new = m_old
        for s in s_blocks:
            m_new = jnp.maximum(m_new, jnp.max(s, axis=0, keepdims=True))
        alpha = jnp.exp(m_old - m_new)
        e_blocks = [jnp.exp(s - m_new) for s in s_blocks]
        l_new = l_s[...] * alpha
        for e in e_blocks:
            l_new = l_new + e.sum(axis=0, keepdims=True)
        pt = e_blocks[0] if len(e_blocks) == 1 else jnp.concatenate(e_blocks, axis=0)
        acc_s[...] = acc_s[...] * alpha + pv_fn(pt.astype(bf))
        m_s[...] = m_new
        l_s[...] = l_new

    def finish():
        return acc_s[...] * (1.0 / jnp.maximum(l_s[...], TINY))

    def scores(kv):
        return jnp.dot(kv[:, 0:half].astype(bf), qbd, preferred_element_type=f32)

    reset()
    pages_per_tile = SLC_TILE // PAGE_SIZE
    blk_per_page = PAGE_SIZE // SEL_BLOCK

    def slc_body(kt, carry):
        selb = sel_s[pl.ds(pl.multiple_of(kt * BLK_PER_TILE, BLK_PER_TILE), BLK_PER_TILE), :]
        blocks = []
        for j in range(pages_per_tile):
            s = scores_t(buf[slot, kt * pages_per_tile + j, 0])
            for h in range(blk_per_page):
                i = j * blk_per_page + h
                blocks.append(jnp.where(selb[i:i + 1, :] > 0.5, s[h * SEL_BLOCK:(h + 1) * SEL_BLOCK, :], NEG))

        def pv_pages(pt):
            full = jnp.zeros((2 * HD, LN), f32)
            for j in range(pages_per_tile):
                full = full + jnp.dot(buf[slot, kt * pages_per_tile + j, 1].astype(bf),
                                      pt[j * PAGE_SIZE:(j + 1) * PAGE_SIZE, :], preferred_element_type=f32)
            return group_rows(full)

        update(blocks, pv_pages)
        return carry

    lax.fori_loop(0, past // SLC_TILE, slc_body, 0)
    rows_new = lax.broadcasted_iota(jnp.int32, (SEL_BLOCK, 1), 0)
    kv = nslc_ref[0]
    keep = (sel_s[past // SEL_BLOCK:past // SEL_BLOCK + 1, :] > 0.5) & (past + rows_new <= pos_l)
    update([jnp.where(keep, scores(kv), NEG)], functools.partial(pv, kv[:, half:]))
    o_slc = finish()

    reset()
    wb = wcache_ref.shape[-1]
    rel = pos_l - (past - wb + lax.broadcasted_iota(jnp.int32, (wb, 1), 0))
    update([jnp.where((rel >= 0) & (rel <= WINDOW), scores_t(wcache_ref[0, 0, 0]), NEG)],
           lambda pt: group_rows(jnp.dot(wcache_ref[0, 0, 1].astype(bf), pt, preferred_element_type=f32)))
    kv = nwin_ref[0]
    rel = pos_l - (past + rows_new)
    update([jnp.where((rel >= 0) & (rel <= WINDOW) & (rows_new < t_new), scores(kv), NEG)],
           functools.partial(pv, kv[:, half:]))
    o_win = finish()

    gate = jax.nn.sigmoid(gate_ref[0])
    o_ref[0] = gate[0:1, :] * o_cmp + gate[1:2, :] * o_slc + gate[2:3, :] * o_win


def _pad_rows(a, rows):
    return jnp.pad(a, ((0, 0), (0, rows - a.shape[1]), (0, 0)))


def _nsa_sample(q, kv_cmp, kv_slc, kv_win, gates, pool_cmp, pool_slc, page_table, win_cache, layer, cmp_w):
    B, Tn = q.shape[:2]
    G, R, HD = NSA_KV_HEADS, NSA_R, NSA_HD
    n_pages = page_table.shape[1]
    n_pool = pool_cmp.shape[1]
    past = n_pages * PAGE_SIZE
    wb = win_cache.shape[2]
    assert Tn <= SQ and Tn <= SEL_BLOCK and past % SLC_TILE == 0 and wb % SUBLANES == 0
    bf = jnp.bfloat16
    half = G * HD
    kvc = _compress_paged(page_table, _page_view(pool_cmp), layer, cmp_w)
    n_blk = past // CMP_STRIDE
    n_sel = _round_up(past // SEL_BLOCK + 1, BF16_SUBLANES)
    kvp = kvc.reshape(B, n_blk // CMP_PER_SEL, CMP_PER_SEL, NSA_KV_WIDTH).transpose(0, 2, 1, 3)
    kvp = jnp.pad(kvp, ((0, 0), (0, 0), (0, n_sel - n_blk // CMP_PER_SEL), (0, 0)))
    kvp = kvp.reshape(B, CMP_PER_SEL * n_sel, NSA_KV_WIDTH).astype(bf)
    kc, vc = kvp[..., :half], kvp[..., half:]
    qt = jnp.pad((q * (HD ** -0.5)).reshape(B, Tn, G, R, HD), ((0, 0), (0, SQ - Tn), (0, 0), (0, 0), (0, 0)))
    qt = qt.transpose(0, 2, 4, 3, 1)
    qbd = jnp.einsum('bgdrq,gh->bgdhrq', qt, jnp.eye(G, dtype=jnp.float32)).reshape(B, half, SAMPLE_LANES).astype(bf)
    gate = jnp.pad(gates.reshape(B, Tn, G, R, 3), ((0, 0), (0, SQ - Tn), (0, 0), (0, 0), (0, 0)))
    gate = gate.transpose(0, 4, 2, 3, 1).reshape(B, 3, SAMPLE_LANES)
    nslc = _pad_rows(kv_slc.reshape(B, Tn, NSA_KV_WIDTH), SEL_BLOCK)
    nwin = _pad_rows(kv_win.reshape(B, Tn, NSA_KV_WIDTH), SEL_BLOCK)
    per_b = lambda shape: pl.BlockSpec((1,) + shape, lambda b, pt: (b,) + (0,) * len(shape))
    out = pl.pallas_call(
        functools.partial(_nsa_sample_kernel, page_base=layer * n_pool, n_pages=n_pages, n_sel=n_sel, t_new=Tn),
        grid_spec=pltpu.PrefetchScalarGridSpec(
            num_scalar_prefetch=1,
            grid=(B,),
            in_specs=[
                pl.BlockSpec(memory_space=pl.ANY),
                per_b((half, SAMPLE_LANES)), per_b((3, SAMPLE_LANES)),
                per_b((CMP_PER_SEL * n_sel, half)), per_b((CMP_PER_SEL * n_sel, half)),
                per_b((SEL_BLOCK, NSA_KV_WIDTH)),
                pl.BlockSpec((1, 1, 2, half, wb), lambda b, pt: (layer, b, 0, 0, 0)),
                per_b((SEL_BLOCK, NSA_KV_WIDTH)),
            ],
            out_specs=per_b((HD, SAMPLE_LANES)),
            scratch_shapes=[
                pltpu.VMEM((2, n_pages, 2, half, PAGE_SIZE), jnp.float32),
                pltpu.SemaphoreType.DMA((2,)),
                pltpu.VMEM((n_sel, SAMPLE_LANES), jnp.float32),
                pltpu.VMEM((n_sel, SAMPLE_LANES), jnp.float32),
                pltpu.VMEM((1, SAMPLE_LANES), jnp.float32),
                pltpu.VMEM((1, SAMPLE_LANES), jnp.float32),
                pltpu.VMEM((HD, SAMPLE_LANES), jnp.float32),
            ]),
        out_shape=jax.ShapeDtypeStruct((B, HD, SAMPLE_LANES), jnp.float32),
        compiler_params=pltpu.CompilerParams(
            dimension_semantics=("arbitrary",), vmem_limit_bytes=VMEM_LIMIT_BYTES),
        name="nsa_sample_attention",
    )(page_table, _page_view(pool_slc), qbd, gate, kc, vc, nslc,
      win_cache.transpose(0, 1, 3, 4, 5, 2).reshape(DEPTH, B, 2, half, wb), nwin)
    o = out.reshape(B, HD, G, R, SQ)[..., :Tn].transpose(0, 4, 2, 3, 1).reshape(B, Tn, NSA_WIDTH)
    win_all = jnp.concatenate([win_cache[layer], kv_win.astype(win_cache.dtype)], axis=1)
    return o, win_all[:, -min(WINDOW, wb + Tn):]


GLA_LANES = GLA_HEADS * GLA_DK
GLA_KERNEL_CHUNK = 32
GLA_BLOCK_ROWS = 512


def _head_block_mask(dtype):
    r = lax.broadcasted_iota(jnp.int32, (GLA_LANES, GLA_LANES), 0) // GLA_DK
    c = lax.broadcasted_iota(jnp.int32, (GLA_LANES, GLA_LANES), 1) // GLA_DK
    return (r == c).astype(dtype)


def _gla_kernel(q_ref, k_ref, v_ref, gg_ref, ga_ref, wa_ref, ba_ref, gn_ref, st0_ref, o_ref, st_ref,
                st_s, kp_s, bp_s, vp_s, *, c, n_chunks, valid_rows):
    f32, bf = jnp.float32, jnp.bfloat16

    @pl.when(pl.program_id(1) == 0)
    def _():
        st_s[...] = st0_ref[0]
        zeros = jnp.zeros((c, GLA_LANES), f32)
        kp_s[0:c, :] = zeros
        bp_s[0:c, :] = zeros
        vp_s[0:c, :] = zeros

    row = lax.broadcasted_iota(jnp.int32, (c, 1), 0)
    tril = (lax.broadcasted_iota(jnp.int32, (c, c), 0) >= lax.broadcasted_iota(jnp.int32, (c, c), 1)).astype(f32)
    ones_blk = _head_block_mask(bf)
    blk_f32 = _head_block_mask(f32)
    contract_last = (((1,), (1,)), ((), ()))
    contract_rows = (((0,), (0,)), ((), ()))

    def head_sum(x):
        hi = x.astype(bf)
        lo = (x - hi.astype(f32)).astype(bf)
        return (jnp.dot(hi, ones_blk, preferred_element_type=f32)
                + jnp.dot(lo, ones_blk, preferred_element_type=f32))

    def chunk(ch, carry):
        rows = pl.ds(pl.multiple_of(ch * c, c), c)
        q = q_ref[rows, :] * (GLA_DK ** -0.5)
        k = k_ref[rows, :]
        v = v_ref[rows, :]
        z = jnp.dot(ga_ref[rows, :].astype(bf), wa_ref[...], preferred_element_type=f32) + ba_ref[...]
        la = (jnp.minimum(z, 0.0) - jnp.log1p(jnp.exp(-jnp.abs(z)))) / GLA_GATE_TEMP
        if valid_rows < c:
            la = jnp.where(row < valid_rows, la, 0.0)
        b = jnp.dot(tril, la, preferred_element_type=f32, precision=lax.Precision.HIGHEST)
        st = st_s[...]
        o = lax.dot_general((q * jnp.exp(b)).astype(bf), st.astype(bf), contract_last,
                            preferred_element_type=f32)
        kp_s[c:2 * c, :] = k
        bp_s[c:2 * c, :] = b
        vp_s[c:2 * c, :] = v
        for d in range(c):
            if d == 0:
                term = q * k
                vr = v
            else:
                ok = row >= d
                kr = kp_s[c - d:2 * c - d, :]
                br = bp_s[c - d:2 * c - d, :]
                vr = vp_s[c - d:2 * c - d, :]
                term = jnp.where(ok, q * kr * jnp.exp(jnp.where(ok, b - br, 0.0)), 0.0)
            o = o + jnp.dot(term.astype(bf), ones_blk, preferred_element_type=f32) * vr
        ms = head_sum(o * o) * (1.0 / GLA_DV)
        g = gg_ref[rows, :]
        o_ref[rows, :] = o * lax.rsqrt(ms + EPS) * gn_ref[...] * (g * jax.nn.sigmoid(g))
        b_last = b[c - 1:c, :]
        ke = k * jnp.exp(b_last - b)
        upd = lax.dot_general(v.astype(bf), ke.astype(bf), contract_rows, preferred_element_type=f32)
        st_s[...] = st * jnp.exp(b_last) + upd * blk_f32
        return carry

    lax.fori_loop(0, n_chunks, chunk, 0)
    st_ref[0] = st_s[...]


def _gla(proj, nbatch, tb, c, valid_rows, wa_pad, ba, gn, st0):
    m = proj.shape[0]
    nblk = m // nbatch // tb
    colblk = lambda p, w: PROJ_OFF[p] // w
    row_map = lambda j: (lambda b, i: (b * nblk + i, j))
    return pl.pallas_call(
        functools.partial(_gla_kernel, c=c, n_chunks=tb // c, valid_rows=valid_rows),
        grid=(nbatch, nblk),
        in_specs=[
            pl.BlockSpec((tb, GLA_LANES), row_map(colblk(0, GLA_LANES))),
            pl.BlockSpec((tb, GLA_LANES), row_map(colblk(1, GLA_LANES))),
            pl.BlockSpec((tb, GLA_LANES), row_map(colblk(2, GLA_LANES))),
            pl.BlockSpec((tb, GLA_LANES), row_map(colblk(3, GLA_LANES))),
            pl.BlockSpec((tb, LANES), row_map(colblk(4, LANES))),
            pl.BlockSpec((LANES, GLA_LANES), lambda b, i: (0, 0)),
            pl.BlockSpec((1, GLA_LANES), lambda b, i: (0, 0)),
            pl.BlockSpec((1, GLA_LANES), lambda b, i: (0, 0)),
            pl.BlockSpec((1, GLA_LANES, GLA_LANES), lambda b, i: (b, 0, 0)),
        ],
        out_specs=[pl.BlockSpec((tb, GLA_LANES), lambda b, i: (b * nblk + i, 0)),
                   pl.BlockSpec((1, GLA_LANES, GLA_LANES), lambda b, i: (b, 0, 0))],
        out_shape=[jax.ShapeDtypeStruct((m, GLA_LANES), jnp.float32),
                   jax.ShapeDtypeStruct((nbatch, GLA_LANES, GLA_LANES), jnp.float32)],
        scratch_shapes=[pltpu.VMEM((GLA_LANES, GLA_LANES), jnp.float32),
                        pltpu.VMEM((2 * c, GLA_LANES), jnp.float32),
                        pltpu.VMEM((2 * c, GLA_LANES), jnp.float32),
                        pltpu.VMEM((2 * c, GLA_LANES), jnp.float32)],
        compiler_params=pltpu.CompilerParams(
            dimension_semantics=("arbitrary", "arbitrary"), vmem_limit_bytes=VMEM_LIMIT_BYTES),
        name="gla_scan",
    )(proj, proj, proj, proj, proj, wa_pad, ba, gn, st0)


def _gla_state_in(s0):
    eye = jnp.eye(GLA_HEADS, dtype=jnp.float32)
    return jnp.einsum('bhde,hg->bhegd', s0.astype(jnp.float32), eye).reshape(-1, GLA_LANES, GLA_LANES)


def _gla_state_out(st):
    blocks = [st[:, h * GLA_DV:(h + 1) * GLA_DV, h * GLA_DK:(h + 1) * GLA_DK] for h in range(GLA_HEADS)]
    return jnp.stack(blocks, axis=1).transpose(0, 1, 3, 2)


def _causal_dwconv(u, hist, w):
    T = u.shape[1]
    up = jnp.concatenate([hist.astype(u.dtype), u], axis=1)
    y = w[0] * up[:, 0:T]
    for k in range(1, CONV_W):
        y = y + w[k] * up[:, k:k + T]
    return y, up[:, -(CONV_W - 1):]


def _nsa_prompt(q, kv_cmp, kv_slc, kv_win, gates, cmp_w):
    B, T = q.shape[:2]
    assert B == 1 and T % SLC_TILE == 0 and T >= WIN_TILE
    n_rows = T // CMP_STRIDE
    kvc = _compress(kv_cmp.reshape(n_rows, CMP_ROW), cmp_w, min(256, n_rows))
    o = _nsa_prompt_attention(q.reshape(T, NSA_WIDTH), gates.reshape(T, 3 * NSA_HEADS), kvc,
                              kv_slc.reshape(T, NSA_KV_WIDTH), kv_win.reshape(T, NSA_KV_WIDTH))
    return o.reshape(B, T, NSA_WIDTH), kv_win[:, -min(WINDOW, T):]


def _expand_rows(v, t):
    if v.shape[0] == 1:
        return v
    return jnp.repeat(v, t, axis=0)


def _trunk_layer(x, mod, lw, gla_s0, sc_hist, ffn_hist, nsa_apply, tm, final, g_final):
    B, T, _ = x.shape
    m = B * T
    grouped = B > 1
    ssh1, ssc1, sgt1, ssh2, ssc2, sgt2 = [_expand_rows(v, T) for v in jnp.split(mod, 6, axis=-1)]
    x2 = x.reshape(m, D_MODEL)
    proj = _in_proj(x2, lw['norm_mix'], ssc1, ssh1, lw['w_in'], tm).reshape(B, T, PROJ_WIDTH)
    gq, gk, gv, gg, ga, sb, scc, shh, nq, ncmp, nslc, nwin, ngate = [_proj_piece(proj, p) for p in range(13)]
    heads = lambda a, d: a.reshape(B, T, -1, d)
    if grouped:
        t_pad = _round_up(T, SUBLANES)
        gla_in = jnp.pad(proj, ((0, 0), (0, t_pad - T), (0, 0))).reshape(B * t_pad, PROJ_WIDTH)
        tb = chunk = t_pad
    else:
        t_pad, gla_in, tb, chunk = T, proj.reshape(m, PROJ_WIDTH), GLA_BLOCK_ROWS, GLA_KERNEL_CHUNK
    o_gla, st_gla = _gla(gla_in, B, tb, chunk, T if grouped else chunk,
                         lw['gla_wa'], lw['gla_ba'], lw['gla_norm'], _gla_state_in(gla_s0))
    o_gla = o_gla.reshape(B, t_pad, GLA_WIDTH)[:, :T]
    s_gla = _gla_state_out(st_gla)
    conv_out, sc_state = _causal_dwconv(scc * shh, sc_hist, lw['sc_conv'])
    o_sc = sb * conv_out
    kvr = lambda a: a.reshape(B, T, 2, NSA_KV_HEADS, NSA_HD)
    kv_cmp, kv_slc, kv_win = kvr(ncmp), kvr(nslc), kvr(nwin)
    o_nsa, win_state = nsa_apply(heads(nq, NSA_HD), kv_cmp, kv_slc, kv_win, heads(ngate, 3))
    mix = jnp.concatenate([o_gla, o_sc, o_nsa], axis=-1).reshape(m, D_MODEL)
    x2 = _out_proj(x2, mix, sgt1, lw['w_out'], tm)

    ff2 = 2 * D_FF
    if grouped:
        zero = jnp.zeros((B, 1, ff2), jnp.float32)
        h1 = jnp.concatenate([ffn_hist[:, 1:2], zero, zero, zero], axis=1).reshape(m, ff2)
        h2 = jnp.concatenate([ffn_hist[:, 0:1], ffn_hist[:, 1:2], zero, zero], axis=1).reshape(m, ff2)
        hist = (h1, h2)
    else:
        hist = jnp.concatenate(
            [jnp.zeros((HIST_ROWS - (CONV_W - 1), ff2), jnp.float32), ffn_hist[0]], axis=0)
    y, st = _ffn(x2, lw['norm_ffn'], ssc2, ssh2, sgt2, lw['ffn_up'], lw['ffn_conv'], lw['ffn_down'],
                 g_final, hist, tm, T if grouped else 0, final)
    if grouped:
        ffn_state = st.reshape(B, T, ff2)[:, -(CONV_W - 1):]
    else:
        ffn_state = st[None, -(CONV_W - 1):]
    return (y.reshape(B, T, D_MODEL), kv_cmp, kv_slc, win_state, s_gla.astype(gla_s0.dtype), sc_state, ffn_state)


def kernel(x_prompt, x_sample, cache_nsa_cmp, cache_nsa_slc, cache_nsa_win, state_gla, state_shortconv, state_ffn_conv, page_table, c_prompt, c_sample, mod_w, mod_b, norm_mix, norm_ffn, w_in, gla_wa2, gla_ba, gla_norm, sc_conv, nsa_cmp_pos, nsa_cmp_w1, nsa_cmp_w2, w_out, ffn_up, ffn_conv, ffn_down, norm_final):
    xp, xs = x_prompt, x_sample
    bp, bs = xp.shape[0], xs.shape[0]
    assert bp == 1 and xs.shape[1] == 4
    c_rows = _round_up(bp + bs, SUBLANES)
    c_all = jnp.concatenate([c_prompt, c_sample, jnp.zeros((c_rows - bp - bs, D_MODEL), jnp.float32)], axis=0)
    mod_all = _modulation(c_all, mod_w, mod_b)
    g_final = norm_final.reshape(1, D_MODEL)
    outs = [[] for _ in range(12)]
    for l in range(DEPTH):
        lw = dict(
            norm_mix=norm_mix[l].reshape(1, D_MODEL), norm_ffn=norm_ffn[l].reshape(1, D_MODEL),
            w_in=_pack_w_in(w_in[l]),
            gla_wa=jnp.zeros((LANES, GLA_LANES), jnp.bfloat16).at[:GLA_GATE_RANK].set(
                gla_wa2[l].astype(jnp.bfloat16)),
            gla_ba=gla_ba[l].reshape(1, GLA_LANES),
            gla_norm=jnp.tile(gla_norm[l], GLA_HEADS).reshape(1, GLA_LANES),
            sc_conv=sc_conv[l], w_out=w_out[l].astype(jnp.bfloat16),
            ffn_up=ffn_up[l].astype(jnp.bfloat16), ffn_conv=ffn_conv[l],
            ffn_down=ffn_down[l].astype(jnp.bfloat16))
        cmp_params = (nsa_cmp_pos[l], nsa_cmp_w1[l], nsa_cmp_w2[l])
        cmp_w = _compress_weights(*cmp_params)
        final = l == DEPTH - 1
        res_p = _trunk_layer(
            xp, mod_all[l, 0:bp], lw,
            jnp.zeros((bp, GLA_HEADS, GLA_DK, GLA_DV), xp.dtype),
            jnp.zeros((bp, CONV_W - 1, SC_WIDTH), xp.dtype),
            jnp.zeros((bp, CONV_W - 1, 2 * D_FF), xp.dtype),
            functools.partial(_nsa_prompt, cmp_w=cmp_w), 256, final, g_final)
        res_s = _trunk_layer(
            xs, mod_all[l, bp:bp + bs], lw, state_gla[l], state_shortconv[l], state_ffn_conv[l],
            functools.partial(_nsa_sample, pool_cmp=cache_nsa_cmp, pool_slc=cache_nsa_slc,
                              page_table=page_table, win_cache=cache_nsa_win, layer=l,
                              cmp_w=cmp_w), bs * xs.shape[1], final, g_final)
        xp, xs = res_p[0], res_s[0]
        for k in range(6):
            outs[2 * k].append(res_p[k + 1])
            outs[2 * k + 1].append(res_s[k + 1])
    return (xp, xs) + tuple(jnp.stack(o) for o in outs)
```

```python
import functools

import jax
import jax.numpy as jnp
from jax import lax
from jax.experimental import pallas as pl
from jax.experimental.pallas import tpu as pltpu

D_MODEL = 1024
DEPTH = 2
PAGE_SIZE = 128
GLA_HEADS = 4
GLA_DK = D_MODEL // 16
GLA_DV = D_MODEL // 16
GLA_WIDTH = GLA_HEADS * GLA_DV
GLA_GATE_RANK = 16
GLA_GATE_TEMP = 16.0
GLA_CHUNK = 64
SC_WIDTH = D_MODEL // 4
CONV_W = 3
NSA_HEADS = 8
NSA_KV_HEADS = 2
NSA_HD = D_MODEL // 16
NSA_WIDTH = NSA_HEADS * NSA_HD
NSA_KV_WIDTH = 2 * NSA_KV_HEADS * NSA_HD
CMP_STRIDE = 16
CMP_BLOCK = 2 * CMP_STRIDE
CMP_HIDDEN = 128
SEL_BLOCK = 64
SEL_TOPN = 16
WINDOW = 512
Q_BLOCK = 128
D_FF = 2816
EPS = 1e-6
NEG = -1e30
TINY = 1e-30
FORCE = 1e9

IN_SIZES = (
    GLA_HEADS * GLA_DK, GLA_HEADS * GLA_DK, GLA_WIDTH, GLA_WIDTH, GLA_GATE_RANK,
    SC_WIDTH, SC_WIDTH, SC_WIDTH,
    NSA_WIDTH, NSA_KV_WIDTH, NSA_KV_WIDTH, NSA_KV_WIDTH, NSA_HEADS * 3,
)

LANES = 128
SUBLANES = 8
BF16_SUBLANES = 16
VMEM_LIMIT_BYTES = 56 * 1024 * 1024

PROJ_ORDER = (0, 1, 2, 3, 5, 6, 7, 8, 9, 10, 11, 4, 12)


def _round_up(n, m):
    return -(-n // m) * m


def _proj_layout():
    src, acc = [], 0
    for s in IN_SIZES:
        src.append(acc)
        acc += s
    offs, dst = {}, 0
    for p in PROJ_ORDER:
        offs[p] = dst
        dst += _round_up(IN_SIZES[p], LANES)
    return src, offs, dst


PROJ_SRC, PROJ_OFF, PROJ_WIDTH = _proj_layout()


def _pack_w_in(w_in):
    out = jnp.zeros((D_MODEL, PROJ_WIDTH), jnp.bfloat16)
    for p in PROJ_ORDER:
        piece = w_in[:, PROJ_SRC[p]:PROJ_SRC[p] + IN_SIZES[p]].astype(jnp.bfloat16)
        out = lax.dynamic_update_slice(out, piece, (0, PROJ_OFF[p]))
    return out


def _proj_piece(proj, p):
    return proj[..., PROJ_OFF[p]:PROJ_OFF[p] + IN_SIZES[p]]


def _mod_kernel(c_ref, w_ref, b_ref, o_ref):
    c = c_ref[...]
    a = c * jax.nn.sigmoid(c)
    o_ref[0] = jnp.dot(a, w_ref[0], preferred_element_type=jnp.float32,
                       precision=lax.Precision.HIGHEST) + b_ref[0]


def _modulation(c_all, mod_w, mod_b):
    rows = c_all.shape[0]
    tn = 1024
    n = mod_w.shape[-1]
    return pl.pallas_call(
        _mod_kernel,
        grid=(DEPTH, n // tn),
        in_specs=[
            pl.BlockSpec((rows, D_MODEL), lambda l, j: (0, 0)),
            pl.BlockSpec((1, D_MODEL, tn), lambda l, j: (l, 0, j)),
            pl.BlockSpec((1, 1, tn), lambda l, j: (l, 0, j)),
        ],
        out_specs=pl.BlockSpec((1, rows, tn), lambda l, j: (l, 0, j)),
        out_shape=jax.ShapeDtypeStruct((DEPTH, rows, n), jnp.float32),
        name="adaln_modulation",
    )(c_all, mod_w, mod_b.reshape(DEPTH, 1, n))


def _norm_mod(x, g, sc, sh):
    r = lax.rsqrt(jnp.mean(x * x, axis=-1, keepdims=True) + EPS)
    return (x * r * g) * (1.0 + sc) + sh


def _in_proj_kernel(x_ref, g_ref, sc_ref, sh_ref, w_ref, o_ref):
    h = _norm_mod(x_ref[...], g_ref[...], sc_ref[...], sh_ref[...])
    o_ref[...] = jnp.dot(h.astype(jnp.bfloat16), w_ref[...], preferred_element_type=jnp.float32)


def _row_spec(tm, per_row):
    if per_row:
        return pl.BlockSpec((tm, D_MODEL), lambda i: (i, 0))
    return pl.BlockSpec((1, D_MODEL), lambda i: (0, 0))


def _resident(shape):
    return pl.BlockSpec(shape, lambda i: (0,) * len(shape), pipeline_mode=pl.Buffered(1))


def _in_proj(x, g, sc, sh, w_packed, tm):
    m = x.shape[0]
    per_row = sc.shape[0] != 1
    return pl.pallas_call(
        _in_proj_kernel,
        grid=(m // tm,),
        in_specs=[
            pl.BlockSpec((tm, D_MODEL), lambda i: (i, 0)),
            _resident((1, D_MODEL)),
            _row_spec(tm, per_row),
            _row_spec(tm, per_row),
            _resident((D_MODEL, PROJ_WIDTH)),
        ],
        out_specs=pl.BlockSpec((tm, PROJ_WIDTH), lambda i: (i, 0)),
        out_shape=jax.ShapeDtypeStruct((m, PROJ_WIDTH), jnp.float32),
        compiler_params=pltpu.CompilerParams(
            dimension_semantics=("arbitrary",), vmem_limit_bytes=VMEM_LIMIT_BYTES),
        name="norm_in_proj",
    )(x, g, sc, sh, w_packed)


def _out_proj_kernel(x_ref, mix_ref, gt_ref, w_ref, o_ref):
    y = jnp.dot(mix_ref[...].astype(jnp.bfloat16), w_ref[...], preferred_element_type=jnp.float32)
    o_ref[...] = x_ref[...] + gt_ref[...] * y


def _out_proj(x, mix, gt, w_bf16, tm):
    m = x.shape[0]
    per_row = gt.shape[0] != 1
    return pl.pallas_call(
        _out_proj_kernel,
        grid=(m // tm,),
        in_specs=[
            pl.BlockSpec((tm, D_MODEL), lambda i: (i, 0)),
            pl.BlockSpec((tm, D_MODEL), lambda i: (i, 0)),
            _row_spec(tm, per_row),
            _resident((D_MODEL, D_MODEL)),
        ],
        out_specs=pl.BlockSpec((tm, D_MODEL), lambda i: (i, 0)),
        out_shape=jax.ShapeDtypeStruct((m, D_MODEL), jnp.float32),
        compiler_params=pltpu.CompilerParams(
            dimension_semantics=("arbitrary",), vmem_limit_bytes=VMEM_LIMIT_BYTES),
        name="out_proj_residual",
    )(x, mix, gt, w_bf16)


FFN_UP_CHUNK = 512
FFN_ACT_CHUNK = 256
HIST_ROWS = SUBLANES


def _ffn_kernel(*refs, tm, group, final):
    grouped = group > 0
    if grouped:
        (x_ref, g_ref, sc_ref, sh_ref, gt_ref, wup_ref, cw_ref, wdn_ref, gf_ref,
         h1_ref, h2_ref, o_ref, st_ref, up_s) = refs
    else:
        (x_ref, g_ref, sc_ref, sh_ref, gt_ref, wup_ref, cw_ref, wdn_ref, gf_ref,
         h0_ref, o_ref, st_ref, up_s) = refs

        @pl.when(pl.program_id(0) == 0)
        def _():
            up_s[0:HIST_ROWS, :] = h0_ref[...]

    x = x_ref[...]
    h = _norm_mod(x, g_ref[...], sc_ref[...], sh_ref[...]).astype(jnp.bfloat16)
    for c in range(2 * D_FF // FFN_UP_CHUNK):
        cols = slice(c * FFN_UP_CHUNK, (c + 1) * FFN_UP_CHUNK)
        up_s[HIST_ROWS:HIST_ROWS + tm, cols] = jnp.dot(
            h, wup_ref[:, cols], preferred_element_type=jnp.float32)

    if grouped:
        t = lax.broadcasted_iota(jnp.int32, (tm, 1), 0) % group

    def conv(cols):
        cur = up_s[HIST_ROWS:HIST_ROWS + tm, cols]
        p1 = up_s[HIST_ROWS - 1:HIST_ROWS - 1 + tm, cols]
        p2 = up_s[HIST_ROWS - 2:HIST_ROWS - 2 + tm, cols]
        if grouped:
            p1 = jnp.where(t == 0, h1_ref[:, cols], p1)
            p2 = jnp.where(t <= 1, h2_ref[:, cols], p2)
        return cw_ref[0:1, cols] * p2 + cw_ref[1:2, cols] * p1 + cw_ref[2:3, cols] * cur

    acc = jnp.zeros((tm, D_MODEL), jnp.float32)
    for c in range(D_FF // FFN_ACT_CHUNK):
        a = conv(slice(c * FFN_ACT_CHUNK, (c + 1) * FFN_ACT_CHUNK))
        b = conv(slice(D_FF + c * FFN_ACT_CHUNK, D_FF + (c + 1) * FFN_ACT_CHUNK))
        act = (a * jax.nn.sigmoid(a) * b).astype(jnp.bfloat16)
        acc = acc + jnp.dot(act, wdn_ref[c * FFN_ACT_CHUNK:(c + 1) * FFN_ACT_CHUNK, :],
                            preferred_element_type=jnp.float32)
    y = x + gt_ref[...] * acc
    if final:
        r = lax.rsqrt(jnp.mean(y * y, axis=-1, keepdims=True) + EPS)
        y = y * r * gf_ref[...]
    o_ref[...] = y

    if grouped:
        st_ref[...] = up_s[HIST_ROWS:HIST_ROWS + tm, :]
    else:
        tail = up_s[tm:tm + HIST_ROWS, :]
        st_ref[...] = tail
        up_s[0:HIST_ROWS, :] = tail


def _ffn(x, g, sc, sh, gt, wup, cw, wdn, g_final, hist, tm, group, final):
    m = x.shape[0]
    grouped = group > 0
    per_row = sc.shape[0] != 1
    ff2 = 2 * D_FF
    in_specs = [
        pl.BlockSpec((tm, D_MODEL), lambda i: (i, 0)),
        _resident((1, D_MODEL)),
        _row_spec(tm, per_row), _row_spec(tm, per_row), _row_spec(tm, per_row),
        _resident((D_MODEL, ff2)),
        _resident((CONV_W, ff2)),
        _resident((D_FF, D_MODEL)),
        _resident((1, D_MODEL)),
    ]
    if grouped:
        assert m == tm
        in_specs += [_resident((tm, ff2)), _resident((tm, ff2))]
        hist_args = tuple(hist)
        st_rows = tm
    else:
        in_specs += [_resident((HIST_ROWS, ff2))]
        hist_args = (hist,)
        st_rows = HIST_ROWS
    return pl.pallas_call(
        functools.partial(_ffn_kernel, tm=tm, group=group, final=final),
        grid=(m // tm,),
        in_specs=in_specs,
        out_specs=[pl.BlockSpec((tm, D_MODEL), lambda i: (i, 0)),
                   pl.BlockSpec((st_rows, ff2), lambda i: (0, 0))],
        out_shape=[jax.ShapeDtypeStruct((m, D_MODEL), jnp.float32),
                   jax.ShapeDtypeStruct((st_rows, ff2), jnp.float32)],
        scratch_shapes=[pltpu.VMEM((HIST_ROWS + tm, ff2), jnp.float32)],
        compiler_params=pltpu.CompilerParams(
            dimension_semantics=("arbitrary",), vmem_limit_bytes=VMEM_LIMIT_BYTES),
        name="conv_ffn",
    )(x, g, sc, sh, gt, wup, cw, wdn, g_final, *hist_args)


CMP_ROW = CMP_STRIDE * NSA_KV_WIDTH
CMP_HID = 2 * NSA_KV_HEADS * CMP_HIDDEN


def _gelu_tanh(x):
    return 0.5 * x * (1.0 + jnp.tanh(0.7978845608028654 * (x + 0.044715 * (x * x * x))))


def _compress_kernel(x_ref, xn_ref, pos_ref, wl_ref, wt_ref, w2_ref, o_ref, tr_s, *, tm):
    bf = jnp.bfloat16
    f32 = jnp.float32
    x = x_ref[...].astype(bf)
    lead = jnp.dot(x, wl_ref[...], preferred_element_type=f32)
    tr_s[0:tm, :] = jnp.dot(x, wt_ref[...], preferred_element_type=f32)
    tr_s[tm:tm + SUBLANES, :] = jnp.dot(xn_ref[...].astype(bf), wt_ref[...], preferred_element_type=f32)
    bias = (jnp.dot(pos_ref[0].astype(bf), wl_ref[...], preferred_element_type=f32)
            + jnp.dot(pos_ref[1].astype(bf), wt_ref[...], preferred_element_type=f32))[0:1, :]
    hid = _gelu_tanh(lead + tr_s[1:tm + 1, :] + bias)
    o_ref[...] = jnp.dot(hid.astype(bf), w2_ref[...], preferred_element_type=f32)


def _compress_weights(pos_emb, w1, w2):
    eye = jnp.eye(2, dtype=jnp.float32)
    w1f = jnp.einsum('kldh,kK,gG->lkgdKGh', w1, eye, eye).reshape(CMP_BLOCK, NSA_KV_WIDTH, CMP_HID)
    w1f = w1f.astype(jnp.bfloat16)
    w2b = jnp.einsum('khd,kK,gG->kghKGd', w2, eye, eye).reshape(CMP_HID, NSA_KV_WIDTH).astype(jnp.bfloat16)
    posf = jnp.broadcast_to(pos_emb.transpose(1, 0, 2)[:, :, None, :], (CMP_BLOCK, 2, NSA_KV_HEADS, NSA_HD))
    posf = posf.reshape(CMP_BLOCK, NSA_KV_WIDTH)
    pos = jnp.zeros((CMP_BLOCK, SUBLANES, NSA_KV_WIDTH), jnp.float32).at[:, 0].set(posf)
    pos_rows = jnp.zeros((2, SUBLANES, CMP_ROW), jnp.float32).at[:, 0].set(posf.reshape(2, CMP_ROW))
    w1kv = jnp.einsum('kldh,gG->lkgdGh', w1, eye).reshape(
        CMP_BLOCK // 2, 2, 2, NSA_KV_WIDTH // 2, CMP_HID // 2).transpose(0, 2, 1, 3, 4).reshape(
        CMP_BLOCK // 2, 2, NSA_KV_WIDTH, CMP_HID // 2).astype(jnp.bfloat16)
    return dict(pos=pos, w1=w1kv, w2=w2b, pos_rows=pos_rows,
                wl=w1f[:CMP_STRIDE].reshape(CMP_ROW, CMP_HID), wt=w1f[CMP_STRIDE:].reshape(CMP_ROW, CMP_HID))


def _compress(x, cw, tm):
    pos, wl, wt, w2b = cw['pos_rows'], cw['wl'], cw['wt'], cw['w2']
    n = x.shape[0]
    nb8 = n // SUBLANES
    return pl.pallas_call(
        functools.partial(_compress_kernel, tm=tm),
        grid=(n // tm,),
        in_specs=[
            pl.BlockSpec((tm, CMP_ROW), lambda i: (i, 0)),
            pl.BlockSpec((SUBLANES, CMP_ROW), lambda i: (jnp.minimum((i + 1) * (tm // SUBLANES), nb8 - 1), 0)),
            _resident((2, SUBLANES, CMP_ROW)),
            _resident((CMP_ROW, CMP_HID)), _resident((CMP_ROW, CMP_HID)),
            _resident((CMP_HID, NSA_KV_WIDTH)),
        ],
        out_specs=pl.BlockSpec((tm, NSA_KV_WIDTH), lambda i: (i, 0)),
        out_shape=jax.ShapeDtypeStruct((n, NSA_KV_WIDTH), jnp.float32),
        scratch_shapes=[pltpu.VMEM((tm + SUBLANES, CMP_HID), jnp.float32)],
        compiler_params=pltpu.CompilerParams(
            dimension_semantics=("arbitrary",), vmem_limit_bytes=VMEM_LIMIT_BYTES),
        name="nsa_compress",
    )(x, x, pos, wl, wt, w2b)


NSA_R = NSA_HEADS // NSA_KV_HEADS
QL = NSA_R * Q_BLOCK
QLL = NSA_KV_HEADS * QL
SLC_TILE = 512
BLK_PER_TILE = SLC_TILE // SEL_BLOCK
WIN_TILE = WINDOW + Q_BLOCK
CMP_PER_SEL = SEL_BLOCK // CMP_STRIDE
M_INIT = -1e29
ONES_ROWS = BF16_SUBLANES


def _tile_lanes(v, reps):
    return jnp.concatenate([v] * reps, axis=1)


def _nsa_prompt_kernel(qbd_ref, gate_ref, kc_ref, vct_ref, kslc_ref, vtslc_ref, kwin_ref, vtwin_ref,
                       o_ref, sc_s, sel_s, m_s, l_s, acc_s, s_s, pt_s, *, n_sel):
    f32, bf = jnp.float32, jnp.bfloat16
    G, HD = NSA_KV_HEADS, NSA_HD
    n = pl.program_id(0)
    qbd = qbd_ref[0]
    lane = lax.broadcasted_iota(jnp.int32, (1, Q_BLOCK), 1)
    pos_q = n * Q_BLOCK + lane
    pos_l = _tile_lanes(pos_q, QLL // Q_BLOCK)
    jrow = lax.broadcasted_iota(jnp.int32, (n_sel, 1), 0)

    s_c, mk_c = [], []
    m = jnp.full((1, QLL), NEG, f32)
    for c in range(CMP_PER_SEL):
        s = jnp.dot(kc_ref[c * n_sel:(c + 1) * n_sel, :], qbd, preferred_element_type=f32)
        mk = jrow * SEL_BLOCK + (c * CMP_STRIDE + CMP_BLOCK - 1) <= pos_l
        s = jnp.where(mk, s, NEG)
        m = jnp.maximum(m, jnp.max(s, axis=0, keepdims=True))
        s_c.append(s)
        mk_c.append(mk)
    e_c = [jnp.where(mk_c[c], jnp.exp(s_c[c] - m), 0.0) for c in range(CMP_PER_SEL)]
    l = e_c[0].sum(axis=0, keepdims=True)
    for c in range(1, CMP_PER_SEL):
        l = l + e_c[c].sum(axis=0, keepdims=True)
    inv = 1.0 / jnp.maximum(l, TINY)
    o_cmp = [jnp.zeros((HD, QL), f32) for _ in range(G)]
    pg = []
    for c in range(CMP_PER_SEL):
        p = e_c[c] * inv
        pb = p.astype(bf)
        for g in range(G):
            o_cmp[g] = o_cmp[g] + jnp.dot(vct_ref[g * HD:(g + 1) * HD, c * n_sel:(c + 1) * n_sel],
                                          pb[:, g * QL:(g + 1) * QL], preferred_element_type=f32)
        pg.append([sum(p[:, g * QL + r * Q_BLOCK:g * QL + (r + 1) * Q_BLOCK] for r in range(NSA_R))
                   for g in range(G)])
    o_cmp = jnp.concatenate(o_cmp, axis=1)

    cur = pos_q // SEL_BLOCK
    forced = (jrow == 0) | (jrow == cur) | (jrow == cur - 1)
    allowed = jrow * SEL_BLOCK <= pos_q
    jrow_f = jrow.astype(f32)
    for g in range(G):
        last = pg[CMP_PER_SEL - 1][g]
        prev = jnp.where(jrow == 0, 0.0, pltpu.roll(last, 1, 0))
        inner = pg[0][g]
        for c in range(1, CMP_PER_SEL - 1):
            inner = inner + pg[c][g]
        p_slc = 2.0 * inner + last + prev
        sc_s[g] = jnp.where(forced, FORCE, jnp.where(allowed, p_slc, -1.0))
        sel_s[g] = jnp.zeros((n_sel, Q_BLOCK), f32)

    def pick(_, carry):
        for g in range(G):
            s = sc_s[g]
            top = jnp.max(s, axis=0, keepdims=True)
            first = jnp.min(jnp.where(s == top, jrow_f, float(n_sel)), axis=0, keepdims=True)
            hit = jrow_f == first
            sc_s[g] = jnp.where(hit, -jnp.inf, s)
            sel_s[g] = jnp.where(hit, 1.0, sel_s[g])
        return carry

    lax.fori_loop(0, min(SEL_TOPN, n_sel), pick, 0)
    for g in range(G):
        sel_s[g] = jnp.where(allowed & (sel_s[g] > 0.5), 0.0, NEG)

    def reset(st):
        m_s[st] = jnp.full((1, QLL), M_INIT, f32)
        l_s[st] = jnp.zeros((1, QLL), f32)
        acc_s[st] = jnp.zeros((HD, QLL), f32)

    def attend(tiles):
        stats = []
        for st, k_ref, _, start, rows, bias_fn in tiles:
            s = jnp.dot(k_ref[pl.ds(start, rows), :], qbd, preferred_element_type=f32)
            top = jnp.full((SUBLANES, QLL), NEG, f32)
            for i in range(rows // SEL_BLOCK):
                blk = slice(i * SEL_BLOCK, (i + 1) * SEL_BLOCK)
                sb = s[blk, :] + bias_fn(i)
                s_s[st, blk, :] = sb
                for r in range(SEL_BLOCK // SUBLANES):
                    top = jnp.maximum(top, sb[r * SUBLANES:(r + 1) * SUBLANES, :])
            m_old = m_s[st]
            m_new = jnp.maximum(m_old, jnp.max(top, axis=0, keepdims=True))
            stats.append((m_new, jnp.exp(m_old - m_new)))
        for (st, _, _, _, rows, _), (m_new, _) in zip(tiles, stats):
            for i in range(rows // SEL_BLOCK):
                blk = slice(i * SEL_BLOCK, (i + 1) * SEL_BLOCK)
                pt_s[st, blk, :] = jnp.exp(s_s[st, blk, :] - m_new).astype(bf)
        for (st, _, vta_ref, start, rows, _), (m_new, alpha) in zip(tiles, stats):
            pv, psum = [], []
            for g in range(G):
                r = jnp.dot(vta_ref[g, :, pl.ds(start, rows)], pt_s[st, 0:rows, g * QL:(g + 1) * QL],
                            preferred_element_type=f32)
                pv.append(r[0:HD, :])
                psum.append(r[HD:HD + 1, :])
            acc_s[st] = acc_s[st] * alpha + jnp.concatenate(pv, axis=1)
            l_s[st] = l_s[st] * alpha + jnp.concatenate(psum, axis=1)
            m_s[st] = m_new

    def finish(streams):
        if streams == 1:
            acc, l = acc_s[0], l_s[0]
        else:
            m = jnp.maximum(m_s[0], m_s[1])
            a0, a1 = jnp.exp(m_s[0] - m), jnp.exp(m_s[1] - m)
            acc, l = acc_s[0] * a0 + acc_s[1] * a1, l_s[0] * a0 + l_s[1] * a1
        return acc * (1.0 / jnp.maximum(l, TINY))

    def slc_tile(st, kt, causal):
        start = pl.multiple_of(kt * SLC_TILE, SLC_TILE)
        selb = [sel_s[g, pl.ds(pl.multiple_of(kt * BLK_PER_TILE, BLK_PER_TILE), BLK_PER_TILE), :]
                for g in range(G)]

        def bias_fn(i):
            row = jnp.concatenate([_tile_lanes(selb[g][i:i + 1, :], NSA_R) for g in range(G)], axis=1)
            if not causal:
                return row
            tok = start + i * SEL_BLOCK + lax.broadcasted_iota(jnp.int32, (SEL_BLOCK, 1), 0)
            return jnp.where(tok <= pos_l, row, NEG)

        return (st, kslc_ref, vtslc_ref, start, SLC_TILE, bias_fn)

    reset(0)
    reset(1)
    diag = (n * Q_BLOCK) // SLC_TILE

    def slc_pair(j, carry):
        attend([slc_tile(0, 2 * j, False), slc_tile(1, 2 * j + 1, False)])
        return carry

    lax.fori_loop(0, diag // 2, slc_pair, 0)

    @pl.when(diag % 2 == 1)
    def _():
        attend([slc_tile(0, diag - 1, False), slc_tile(1, diag, True)])

    @pl.when(diag % 2 == 0)
    def _():
        attend([slc_tile(1, diag, True)])

    o_slc = finish(2)

    reset(0)
    wstart = pl.multiple_of(jnp.maximum(n * Q_BLOCK - WINDOW, 0), Q_BLOCK)
    rel = pos_q - (wstart + lax.broadcasted_iota(jnp.int32, (WIN_TILE, 1), 0))
    wbias = jnp.where((rel >= 0) & (rel <= WINDOW), 0.0, NEG)
    attend([(0, kwin_ref, vtwin_ref, wstart, WIN_TILE,
             lambda i: _tile_lanes(wbias[i * SEL_BLOCK:(i + 1) * SEL_BLOCK, :], QLL // Q_BLOCK))])
    o_win = finish(1)

    gate = jax.nn.sigmoid(gate_ref[0])
    o_ref[0] = gate[0:1, :] * o_cmp + gate[1:2, :] * o_slc + gate[2:3, :] * o_win


def _nsa_prompt_attention(nq, ngate, kvc, nslc, nwin):
    T = nq.shape[0]
    nb, n_sel = T // Q_BLOCK, T // SEL_BLOCK
    G, R, HD = NSA_KV_HEADS, NSA_R, NSA_HD
    bf = jnp.bfloat16
    half = G * HD
    qt = (nq * (HD ** -0.5)).reshape(nb, Q_BLOCK, G, R, HD).transpose(0, 2, 4, 3, 1)
    qbd = jnp.einsum('ngdrq,gh->ngdhrq', qt, jnp.eye(G, dtype=jnp.float32)).reshape(nb, half, QLL).astype(bf)
    gate = ngate.reshape(nb, Q_BLOCK, G, R, 3).transpose(0, 4, 2, 3, 1).reshape(nb, 3, QLL)
    kvp = kvc.reshape(n_sel, CMP_PER_SEL, NSA_KV_WIDTH).transpose(1, 0, 2).reshape(T // CMP_STRIDE, NSA_KV_WIDTH)
    kc, vct = kvp[:, :half].astype(bf), kvp[:, half:].T.astype(bf)
    def vt_ones(v):
        vt = v.T.reshape(G, HD, T)
        return jnp.concatenate([vt, jnp.ones((G, ONES_ROWS, T), vt.dtype)], axis=1).astype(bf)
    kslc, vtslc = nslc[:, :half].astype(bf), vt_ones(nslc[:, half:])
    kwin, vtwin = nwin[:, :half].astype(bf), vt_ones(nwin[:, half:])
    out = pl.pallas_call(
        functools.partial(_nsa_prompt_kernel, n_sel=n_sel),
        grid=(nb,),
        in_specs=[
            pl.BlockSpec((1, half, QLL), lambda i: (i, 0, 0)),
            pl.BlockSpec((1, 3, QLL), lambda i: (i, 0, 0)),
            _resident((T // CMP_STRIDE, half)), _resident((half, T // CMP_STRIDE)),
            _resident((T, half)), _resident((G, HD + ONES_ROWS, T)),
            _resident((T, half)), _resident((G, HD + ONES_ROWS, T)),
        ],
        out_specs=pl.BlockSpec((1, HD, QLL), lambda i: (i, 0, 0)),
        out_shape=jax.ShapeDtypeStruct((nb, HD, QLL), jnp.float32),
        scratch_shapes=[
            pltpu.VMEM((G, n_sel, Q_BLOCK), jnp.float32),
            pltpu.VMEM((G, n_sel, Q_BLOCK), jnp.float32),
            pltpu.VMEM((2, 1, QLL), jnp.float32),
            pltpu.VMEM((2, 1, QLL), jnp.float32),
            pltpu.VMEM((2, HD, QLL), jnp.float32),
            pltpu.VMEM((2, WIN_TILE, QLL), jnp.float32),
            pltpu.VMEM((2, WIN_TILE, QLL), jnp.bfloat16),
        ],
        compiler_params=pltpu.CompilerParams(
            dimension_semantics=("arbitrary",), vmem_limit_bytes=VMEM_LIMIT_BYTES),
        name="nsa_prompt_attention",
    )(qbd, gate, kc, vct, kslc, vtslc, kwin, vtwin)
    return out.reshape(nb, HD, G, R, Q_BLOCK).transpose(0, 4, 2, 3, 1).reshape(T, NSA_WIDTH)


PAGE_ROWS = PAGE_SIZE // CMP_STRIDE
SQ = 16
SAMPLE_LANES = NSA_HEADS * SQ
HALF_PAGES = 64


def _page_view(pool):
    d, n_pool = pool.shape[:2]
    return pool.transpose(0, 1, 3, 4, 5, 2).reshape(d * n_pool, 2, NSA_KV_HEADS * NSA_HD, pool.shape[2])


def _page_copy(pool_ref, buf_ref, sem_ref, page, slot, idx):
    return pltpu.make_async_copy(pool_ref.at[page], buf_ref.at[slot, idx], sem_ref.at[slot])


def _gather_schedule(issue_fn, wait_fn):
    s = pl.program_id(0)

    @pl.when(s == 0)
    def _():
        issue_fn(s, 0)

    @pl.when(s + 1 < pl.num_programs(0))
    def _():
        issue_fn(s + 1, (s + 1) % 2)

    wait_fn(s, s % 2)


def _compress_paged_kernel(pt_ref, pool_ref, pos_ref, w1_ref, w2_ref, o_ref, buf, sem, tok_s,
                           *, page_base, n_pages):
    bf, f32 = jnp.bfloat16, jnp.float32
    rows = HALF_PAGES * PAGE_ROWS

    def copies(step, slot, fn):
        b, half = step // 2, step % 2

        def body(i, c):
            p = jnp.minimum(half * HALF_PAGES + i, n_pages - 1)
            fn(_page_copy(pool_ref, buf, sem, page_base + pt_ref[b, p], slot, i))
            return c

        lax.fori_loop(0, HALF_PAGES + 1, body, 0)

    _gather_schedule(lambda s, slot: copies(s, slot, lambda cp: cp.start()),
                     lambda s, slot: copies(s, slot, lambda cp: cp.wait()))
    slot = pl.program_id(0) % 2

    half = NSA_KV_HEADS * NSA_HD

    def to_token_rows(i, c):
        dst = pl.ds(pl.multiple_of(i * PAGE_SIZE, PAGE_SIZE), PAGE_SIZE)
        for kv in range(2):
            tok_s[kv, dst, :] = buf[slot, i, kv].T
        return c

    lax.fori_loop(0, HALF_PAGES + 1, to_token_rows, 0, unroll=5)
    acc = [jnp.zeros((rows, CMP_HID // 2), f32) for _ in range(2)]
    bias = [jnp.zeros((SUBLANES, CMP_HID // 2), f32) for _ in range(2)]
    for l in range(0, CMP_BLOCK, 2):
        for kv in range(2):
            x = jnp.concatenate([tok_s.at[kv][pl.ds(l + d, rows, stride=CMP_STRIDE), :] for d in range(2)],
                                axis=1).astype(bf)
            w = w1_ref[l // 2, kv]
            p = jnp.concatenate([pos_ref[l + d, :, kv * half:(kv + 1) * half] for d in range(2)], axis=1)
            acc[kv] = acc[kv] + jnp.dot(x, w, preferred_element_type=f32)
            bias[kv] = bias[kv] + jnp.dot(p.astype(bf), w, preferred_element_type=f32)
    hid = _gelu_tanh(jnp.concatenate(acc, axis=1) + jnp.concatenate(bias, axis=1)[0:1, :])
    o_ref[...] = jnp.dot(hid.astype(bf), w2_ref[...], preferred_element_type=f32)


def _compress_paged(page_table, pool, layer, cw):
    nbatch, n_pages = page_table.shape
    assert n_pages == 2 * HALF_PAGES
    rows = HALF_PAGES * PAGE_ROWS
    const = lambda shape: pl.BlockSpec(shape, lambda s, pt: (0,) * len(shape), pipeline_mode=pl.Buffered(1))
    return pl.pallas_call(
        functools.partial(_compress_paged_kernel, page_base=layer * (pool.shape[0] // DEPTH), n_pages=n_pages),
        grid_spec=pltpu.PrefetchScalarGridSpec(
            num_scalar_prefetch=1,
            grid=(2 * nbatch,),
            in_specs=[
                pl.BlockSpec(memory_space=pl.ANY),
                const((CMP_BLOCK, SUBLANES, NSA_KV_WIDTH)),
                const((CMP_BLOCK // 2, 2, NSA_KV_WIDTH, CMP_HID // 2)),
                const((CMP_HID, NSA_KV_WIDTH)),
            ],
            out_specs=pl.BlockSpec((rows, NSA_KV_WIDTH), lambda s, pt: (s, 0)),
            scratch_shapes=[
                pltpu.VMEM((2, HALF_PAGES + 1) + pool.shape[1:], jnp.float32),
                pltpu.SemaphoreType.DMA((2,)),
                pltpu.VMEM((2, (HALF_PAGES + 1) * PAGE_SIZE, NSA_KV_WIDTH // 2), jnp.float32),
            ]),
        out_shape=jax.ShapeDtypeStruct((2 * nbatch * rows, NSA_KV_WIDTH), jnp.float32),
        compiler_params=pltpu.CompilerParams(
            dimension_semantics=("arbitrary",), vmem_limit_bytes=VMEM_LIMIT_BYTES),
        name="nsa_compress_paged",
    )(page_table, pool, cw['pos'], cw['w1'], cw['w2'])


def _nsa_sample_kernel(pt_ref, pool_ref, qbd_ref, gate_ref, kc_ref, vc_ref, nslc_ref, wcache_ref, nwin_ref,
                       o_ref, buf, sem, sc_s, sel_s, m_s, l_s, acc_s, *, page_base, n_pages, n_sel, t_new):
    f32, bf = jnp.float32, jnp.bfloat16
    HD, LN = NSA_HD, SAMPLE_LANES
    half = NSA_KV_HEADS * HD
    past = n_pages * PAGE_SIZE
    contract_rows = (((0,), (0,)), ((), ()))

    def copies(step, slot, fn):
        def body(p, c):
            fn(_page_copy(pool_ref, buf, sem, page_base + pt_ref[step, p], slot, p))
            return c

        lax.fori_loop(0, n_pages, body, 0)

    _gather_schedule(lambda s, slot: copies(s, slot, lambda cp: cp.start()),
                     lambda s, slot: copies(s, slot, lambda cp: cp.wait()))
    slot = pl.program_id(0) % 2

    qbd = qbd_ref[0]
    lane = lax.broadcasted_iota(jnp.int32, (1, LN), 1)
    pos_l = past + lane % SQ
    group0 = lane < LN // NSA_KV_HEADS
    jrow = lax.broadcasted_iota(jnp.int32, (n_sel, 1), 0)
    jrow_f = jrow.astype(f32)

    def group_rows(full):
        return jnp.where(group0, full[0:HD, :], full[HD:2 * HD, :])

    def pv(v, pt):
        return group_rows(lax.dot_general(v.astype(bf), pt, contract_rows, preferred_element_type=f32))

    def scores_t(kt):
        return lax.dot_general(kt.astype(bf), qbd, contract_rows, preferred_element_type=f32)

    li = lax.broadcasted_iota(jnp.int32, (LN, LN), 0)
    lj = lax.broadcasted_iota(jnp.int32, (LN, LN), 1)
    same = ((li // (NSA_R * SQ) == lj // (NSA_R * SQ)) & (li % SQ == lj % SQ)).astype(bf)

    def head_sum(p):
        hi = p.astype(bf)
        r1 = p - hi.astype(f32)
        mid = r1.astype(bf)
        lo = (r1 - mid.astype(f32)).astype(bf)
        return (jnp.dot(hi, same, preferred_element_type=f32) + jnp.dot(mid, same, preferred_element_type=f32)
                + jnp.dot(lo, same, preferred_element_type=f32))

    s_c, mk_c = [], []
    m = jnp.full((1, LN), NEG, f32)
    for c in range(CMP_PER_SEL):
        s = jnp.dot(kc_ref[0, c * n_sel:(c + 1) * n_sel, :], qbd, preferred_element_type=f32)
        mk = jrow * SEL_BLOCK + (c * CMP_STRIDE + CMP_BLOCK - 1) <= pos_l
        s = jnp.where(mk, s, NEG)
        m = jnp.maximum(m, jnp.max(s, axis=0, keepdims=True))
        s_c.append(s)
        mk_c.append(mk)
    e_c = [jnp.where(mk_c[c], jnp.exp(s_c[c] - m), 0.0) for c in range(CMP_PER_SEL)]
    l = e_c[0].sum(axis=0, keepdims=True)
    for c in range(1, CMP_PER_SEL):
        l = l + e_c[c].sum(axis=0, keepdims=True)
    inv = 1.0 / jnp.maximum(l, TINY)
    o_cmp = jnp.zeros((HD, LN), f32)
    pg = []
    for c in range(CMP_PER_SEL):
        p = e_c[c] * inv
        o_cmp = o_cmp + pv(vc_ref[0, c * n_sel:(c + 1) * n_sel, :], p.astype(bf))
        pg.append(head_sum(p))

    cur = pos_l // SEL_BLOCK
    forced = (jrow == 0) | (jrow == cur) | (jrow == cur - 1)
    allowed = jrow * SEL_BLOCK <= pos_l
    last = pg[CMP_PER_SEL - 1]
    prev = jnp.where(jrow == 0, 0.0, pltpu.roll(last, 1, 0))
    inner = pg[0]
    for c in range(1, CMP_PER_SEL - 1):
        inner = inner + pg[c]
    sc_s[...] = jnp.where(forced, FORCE, jnp.where(allowed, 2.0 * inner + last + prev, -1.0))
    sel_s[...] = jnp.zeros((n_sel, LN), f32)

    def pick(_, carry):
        s = sc_s[...]
        top = jnp.max(s, axis=0, keepdims=True)
        first = jnp.min(jnp.where(s == top, jrow_f, float(n_sel)), axis=0, keepdims=True)
        hit = jrow_f == first
        sc_s[...] = jnp.where(hit, -jnp.inf, s)
        sel_s[...] = jnp.where(hit, 1.0, sel_s[...])
        return carry

    lax.fori_loop(0, SEL_TOPN, pick, 0)
    sel_s[...] = jnp.where(allowed, sel_s[...], 0.0)

    def reset():
        m_s[...] = jnp.full((1, LN), M_INIT, f32)
        l_s[...] = jnp.zeros((1, LN), f32)
        acc_s[...] = jnp.zeros((HD, LN), f32)

    def update(s_blocks, pv_fn):
        m_old = m_s[...]
        m_new = m_old
        for s in s_blocks:
            m_new = jnp.maximum(m_new, jnp.max(s, axis=0, keepdims=True))
        alpha = jnp.exp(m_old - m_new)
        e_blocks = [jnp.exp(s - m_new) for s in s_blocks]
        l_new = l_s[...] * alpha
        for e in e_blocks:
            l_new = l_new + e.sum(axis=0, keepdims=True)
        pt = e_blocks[0] if len(e_blocks) == 1 else jnp.concatenate(e_blocks, axis=0)
        acc_s[...] = acc_s[...] * alpha + pv_fn(pt.astype(bf))
        m_s[...] = m_new
        l_s[...] = l_new

    def finish():
        return acc_s[...] * (1.0 / jnp.maximum(l_s[...], TINY))

    def scores(kv):
        return jnp.dot(kv[:, 0:half].astype(bf), qbd, preferred_element_type=f32)

    reset()
    pages_per_tile = SLC_TILE // PAGE_SIZE
    blk_per_page = PAGE_SIZE // SEL_BLOCK

    def slc_body(kt, carry):
        selb = sel_s[pl.ds(pl.multiple_of(kt * BLK_PER_TILE, BLK_PER_TILE), BLK_PER_TILE), :]
        blocks = []
        for j in range(pages_per_tile):
            s = scores_t(buf[slot, kt * pages_per_tile + j, 0])
            for h in range(blk_per_page):
                i = j * blk_per_page + h
                blocks.append(jnp.where(selb[i:i + 1, :] > 0.5, s[h * SEL_BLOCK:(h + 1) * SEL_BLOCK, :], NEG))

        def pv_pages(pt):
            full = jnp.zeros((2 * HD, LN), f32)
            for j in range(pages_per_tile):
                full = full + jnp.dot(buf[slot, kt * pages_per_tile + j, 1].astype(bf),
                                      pt[j * PAGE_SIZE:(j + 1) * PAGE_SIZE, :], preferred_element_type=f32)
            return group_rows(full)

        update(blocks, pv_pages)
        return carry

    lax.fori_loop(0, past // SLC_TILE, slc_body, 0)
    rows_new = lax.broadcasted_iota(jnp.int32, (SEL_BLOCK, 1), 0)
    kv = nslc_ref[0]
    keep = (sel_s[past // SEL_BLOCK:past // SEL_BLOCK + 1, :] > 0.5) & (past + rows_new <= pos_l)
    update([jnp.where(keep, scores(kv), NEG)], functools.partial(pv, kv[:, half:]))
    o_slc = finish()

    reset()
    wb = wcache_ref.shape[-1]
    rel = pos_l - (past - wb + lax.broadcasted_iota(jnp.int32, (wb, 1), 0))
    update([jnp.where((rel >= 0) & (rel <= WINDOW), scores_t(wcache_ref[0, 0, 0]), NEG)],
           lambda pt: group_rows(jnp.dot(wcache_ref[0, 0, 1].astype(bf), pt, preferred_element_type=f32)))
    kv = nwin_ref[0]
    rel = pos_l - (past + rows_new)
    update([jnp.where((rel >= 0) & (rel <= WINDOW) & (rows_new < t_new), scores(kv), NEG)],
           functools.partial(pv, kv[:, half:]))
    o_win = finish()

    gate = jax.nn.sigmoid(gate_ref[0])
    o_ref[0] = gate[0:1, :] * o_cmp + gate[1:2, :] * o_slc + gate[2:3, :] * o_win


def _pad_rows(a, rows):
    return jnp.pad(a, ((0, 0), (0, rows - a.shape[1]), (0, 0)))


def _nsa_sample(q, kv_cmp, kv_slc, kv_win, gates, pool_cmp, pool_slc, page_table, win_cache, layer, cmp_w):
    B, Tn = q.shape[:2]
    G, R, HD = NSA_KV_HEADS, NSA_R, NSA_HD
    n_pages = page_table.shape[1]
    n_pool = pool_cmp.shape[1]
    past = n_pages * PAGE_SIZE
    wb = win_cache.shape[2]
    assert Tn <= SQ and Tn <= SEL_BLOCK and past % SLC_TILE == 0 and wb % SUBLANES == 0
    bf = jnp.bfloat16
    half = G * HD
    kvc = _compress_paged(page_table, _page_view(pool_cmp), layer, cmp_w)
    n_blk = past // CMP_STRIDE
    n_sel = _round_up(past // SEL_BLOCK + 1, BF16_SUBLANES)
    kvp = kvc.reshape(B, n_blk // CMP_PER_SEL, CMP_PER_SEL, NSA_KV_WIDTH).transpose(0, 2, 1, 3)
    kvp = jnp.pad(kvp, ((0, 0), (0, 0), (0, n_sel - n_blk // CMP_PER_SEL), (0, 0)))
    kvp = kvp.reshape(B, CMP_PER_SEL * n_sel, NSA_KV_WIDTH).astype(bf)
    kc, vc = kvp[..., :half], kvp[..., half:]
    qt = jnp.pad((q * (HD ** -0.5)).reshape(B, Tn, G, R, HD), ((0, 0), (0, SQ - Tn), (0, 0), (0, 0), (0, 0)))
    qt = qt.transpose(0, 2, 4, 3, 1)
    qbd = jnp.einsum('bgdrq,gh->bgdhrq', qt, jnp.eye(G, dtype=jnp.float32)).reshape(B, half, SAMPLE_LANES).astype(bf)
    gate = jnp.pad(gates.reshape(B, Tn, G, R, 3), ((0, 0), (0, SQ - Tn), (0, 0), (0, 0), (0, 0)))
    gate = gate.transpose(0, 4, 2, 3, 1).reshape(B, 3, SAMPLE_LANES)
    nslc = _pad_rows(kv_slc.reshape(B, Tn, NSA_KV_WIDTH), SEL_BLOCK)
    nwin = _pad_rows(kv_win.reshape(B, Tn, NSA_KV_WIDTH), SEL_BLOCK)
    per_b = lambda shape: pl.BlockSpec((1,) + shape, lambda b, pt: (b,) + (0,) * len(shape))
    out = pl.pallas_call(
        functools.partial(_nsa_sample_kernel, page_base=layer * n_pool, n_pages=n_pages, n_sel=n_sel, t_new=Tn),
        grid_spec=pltpu.PrefetchScalarGridSpec(
            num_scalar_prefetch=1,
            grid=(B,),
            in_specs=[
                pl.BlockSpec(memory_space=pl.ANY),
                per_b((half, SAMPLE_LANES)), per_b((3, SAMPLE_LANES)),
                per_b((CMP_PER_SEL * n_sel, half)), per_b((CMP_PER_SEL * n_sel, half)),
                per_b((SEL_BLOCK, NSA_KV_WIDTH)),
                pl.BlockSpec((1, 1, 2, half, wb), lambda b, pt: (layer, b, 0, 0, 0)),
                per_b((SEL_BLOCK, NSA_KV_WIDTH)),
            ],
            out_specs=per_b((HD, SAMPLE_LANES)),
            scratch_shapes=[
                pltpu.VMEM((2, n_pages, 2, half, PAGE_SIZE), jnp.float32),
                pltpu.SemaphoreType.DMA((2,)),
                pltpu.VMEM((n_sel, SAMPLE_LANES), jnp.float32),
                pltpu.VMEM((n_sel, SAMPLE_LANES), jnp.float32),
                pltpu.VMEM((1, SAMPLE_LANES), jnp.float32),
                pltpu.VMEM((1, SAMPLE_LANES), jnp.float32),
                pltpu.VMEM((HD, SAMPLE_LANES), jnp.float32),
            ]),
        out_shape=jax.ShapeDtypeStruct((B, HD, SAMPLE_LANES), jnp.float32),
        compiler_params=pltpu.CompilerParams(
            dimension_semantics=("arbitrary",), vmem_limit_bytes=VMEM_LIMIT_BYTES),
        name="nsa_sample_attention",
    )(page_table, _page_view(pool_slc), qbd, gate, kc, vc, nslc,
      win_cache.transpose(0, 1, 3, 4, 5, 2).reshape(DEPTH, B, 2, half, wb), nwin)
    o = out.reshape(B, HD, G, R, SQ)[..., :Tn].transpose(0, 4, 2, 3, 1).reshape(B, Tn, NSA_WIDTH)
    win_all = jnp.concatenate([win_cache[layer], kv_win.astype(win_cache.dtype)], axis=1)
    return o, win_all[:, -min(WINDOW, wb + Tn):]


GLA_LANES = GLA_HEADS * GLA_DK
GLA_KERNEL_CHUNK = 32
GLA_BLOCK_ROWS = 512


def _head_block_mask(dtype):
    r = lax.broadcasted_iota(jnp.int32, (GLA_LANES, GLA_LANES), 0) // GLA_DK
    c = lax.broadcasted_iota(jnp.int32, (GLA_LANES, GLA_LANES), 1) // GLA_DK
    return (r == c).astype(dtype)


def _gla_kernel(q_ref, k_ref, v_ref, gg_ref, ga_ref, wa_ref, ba_ref, gn_ref, st0_ref, o_ref, st_ref,
                st_s, kp_s, bp_s, vp_s, *, c, n_chunks, valid_rows):
    f32, bf = jnp.float32, jnp.bfloat16

    @pl.when(pl.program_id(1) == 0)
    def _():
        st_s[...] = st0_ref[0]
        zeros = jnp.zeros((c, GLA_LANES), f32)
        kp_s[0:c, :] = zeros
        bp_s[0:c, :] = zeros
        vp_s[0:c, :] = zeros

    row = lax.broadcasted_iota(jnp.int32, (c, 1), 0)
    tril = (lax.broadcasted_iota(jnp.int32, (c, c), 0) >= lax.broadcasted_iota(jnp.int32, (c, c), 1)).astype(f32)
    ones_blk = _head_block_mask(bf)
    blk_f32 = _head_block_mask(f32)
    contract_last = (((1,), (1,)), ((), ()))
    contract_rows = (((0,), (0,)), ((), ()))

    def head_sum(x):
        hi = x.astype(bf)
        lo = (x - hi.astype(f32)).astype(bf)
        return (jnp.dot(hi, ones_blk, preferred_element_type=f32)
                + jnp.dot(lo, ones_blk, preferred_element_type=f32))

    def chunk(ch, carry):
        rows = pl.ds(pl.multiple_of(ch * c, c), c)
        q = q_ref[rows, :] * (GLA_DK ** -0.5)
        k = k_ref[rows, :]
        v = v_ref[rows, :]
        z = jnp.dot(ga_ref[rows, :].astype(bf), wa_ref[...], preferred_element_type=f32) + ba_ref[...]
        la = (jnp.minimum(z, 0.0) - jnp.log1p(jnp.exp(-jnp.abs(z)))) / GLA_GATE_TEMP
        if valid_rows < c:
            la = jnp.where(row < valid_rows, la, 0.0)
        b = jnp.dot(tril, la, preferred_element_type=f32, precision=lax.Precision.HIGHEST)
        st = st_s[...]
        o = lax.dot_general((q * jnp.exp(b)).astype(bf), st.astype(bf), contract_last,
                            preferred_element_type=f32)
        kp_s[c:2 * c, :] = k
        bp_s[c:2 * c, :] = b
        vp_s[c:2 * c, :] = v
        for d in range(c):
            if d == 0:
                term = q * k
                vr = v
            else:
                ok = row >= d
                kr = kp_s[c - d:2 * c - d, :]
                br = bp_s[c - d:2 * c - d, :]
                vr = vp_s[c - d:2 * c - d, :]
                term = jnp.where(ok, q * kr * jnp.exp(jnp.where(ok, b - br, 0.0)), 0.0)
            o = o + jnp.dot(term.astype(bf), ones_blk, preferred_element_type=f32) * vr
        ms = head_sum(o * o) * (1.0 / GLA_DV)
        g = gg_ref[rows, :]
        o_ref[rows, :] = o * lax.rsqrt(ms + EPS) * gn_ref[...] * (g * jax.nn.sigmoid(g))
        b_last = b[c - 1:c, :]
        ke = k * jnp.exp(b_last - b)
        upd = lax.dot_general(v.astype(bf), ke.astype(bf), contract_rows, preferred_element_type=f32)
        st_s[...] = st * jnp.exp(b_last) + upd * blk_f32
        return carry

    lax.fori_loop(0, n_chunks, chunk, 0)
    st_ref[0] = st_s[...]


def _gla(proj, nbatch, tb, c, valid_rows, wa_pad, ba, gn, st0):
    m = proj.shape[0]
    nblk = m // nbatch // tb
    colblk = lambda p, w: PROJ_OFF[p] // w
    row_map = lambda j: (lambda b, i: (b * nblk + i, j))
    return pl.pallas_call(
        functools.partial(_gla_kernel, c=c, n_chunks=tb // c, valid_rows=valid_rows),
        grid=(nbatch, nblk),
        in_specs=[
            pl.BlockSpec((tb, GLA_LANES), row_map(colblk(0, GLA_LANES))),
            pl.BlockSpec((tb, GLA_LANES), row_map(colblk(1, GLA_LANES))),
            pl.BlockSpec((tb, GLA_LANES), row_map(colblk(2, GLA_LANES))),
            pl.BlockSpec((tb, GLA_LANES), row_map(colblk(3, GLA_LANES))),
            pl.BlockSpec((tb, LANES), row_map(colblk(4, LANES))),
            pl.BlockSpec((LANES, GLA_LANES), lambda b, i: (0, 0)),
            pl.BlockSpec((1, GLA_LANES), lambda b, i: (0, 0)),
            pl.BlockSpec((1, GLA_LANES), lambda b, i: (0, 0)),
            pl.BlockSpec((1, GLA_LANES, GLA_LANES), lambda b, i: (b, 0, 0)),
        ],
        out_specs=[pl.BlockSpec((tb, GLA_LANES), lambda b, i: (b * nblk + i, 0)),
                   pl.BlockSpec((1, GLA_LANES, GLA_LANES), lambda b, i: (b, 0, 0))],
        out_shape=[jax.ShapeDtypeStruct((m, GLA_LANES), jnp.float32),
                   jax.ShapeDtypeStruct((nbatch, GLA_LANES, GLA_LANES), jnp.float32)],
        scratch_shapes=[pltpu.VMEM((GLA_LANES, GLA_LANES), jnp.float32),
                        pltpu.VMEM((2 * c, GLA_LANES), jnp.float32),
                        pltpu.VMEM((2 * c, GLA_LANES), jnp.float32),
                        pltpu.VMEM((2 * c, GLA_LANES), jnp.float32)],
        compiler_params=pltpu.CompilerParams(
            dimension_semantics=("arbitrary", "arbitrary"), vmem_limit_bytes=VMEM_LIMIT_BYTES),
        name="gla_scan",
    )(proj, proj, proj, proj, proj, wa_pad, ba, gn, st0)


def _gla_state_in(s0):
    eye = jnp.eye(GLA_HEADS, dtype=jnp.float32)
    return jnp.einsum('bhde,hg->bhegd', s0.astype(jnp.float32), eye).reshape(-1, GLA_LANES, GLA_LANES)


def _gla_state_out(st):
    blocks = [st[:, h * GLA_DV:(h + 1) * GLA_DV, h * GLA_DK:(h + 1) * GLA_DK] for h in range(GLA_HEADS)]
    return jnp.stack(blocks, axis=1).transpose(0, 1, 3, 2)


def _causal_dwconv(u, hist, w):
    T = u.shape[1]
    up = jnp.concatenate([hist.astype(u.dtype), u], axis=1)
    y = w[0] * up[:, 0:T]
    for k in range(1, CONV_W):
        y = y + w[k] * up[:, k:k + T]
    return y, up[:, -(CONV_W - 1):]


def _nsa_prompt(q, kv_cmp, kv_slc, kv_win, gates, cmp_w):
    B, T = q.shape[:2]
    assert B == 1 and T % SLC_TILE == 0 and T >= WIN_TILE
    n_rows = T // CMP_STRIDE
    kvc = _compress(kv_cmp.reshape(n_rows, CMP_ROW), cmp_w, min(256, n_rows))
    o = _nsa_prompt_attention(q.reshape(T, NSA_WIDTH), gates.reshape(T, 3 * NSA_HEADS), kvc,
                              kv_slc.reshape(T, NSA_KV_WIDTH), kv_win.reshape(T, NSA_KV_WIDTH))
    return o.reshape(B, T, NSA_WIDTH), kv_win[:, -min(WINDOW, T):]


def _expand_rows(v, t):
    if v.shape[0] == 1:
        return v
    return jnp.repeat(v, t, axis=0)


def _trunk_layer(x, mod, lw, gla_s0, sc_hist, ffn_hist, nsa_apply, tm, final, g_final):
    B, T, _ = x.shape
    m = B * T
    grouped = B > 1
    ssh1, ssc1, sgt1, ssh2, ssc2, sgt2 = [_expand_rows(v, T) for v in jnp.split(mod, 6, axis=-1)]
    x2 = x.reshape(m, D_MODEL)
    proj = _in_proj(x2, lw['norm_mix'], ssc1, ssh1, lw['w_in'], tm).reshape(B, T, PROJ_WIDTH)
    gq, gk, gv, gg, ga, sb, scc, shh, nq, ncmp, nslc, nwin, ngate = [_proj_piece(proj, p) for p in range(13)]
    heads = lambda a, d: a.reshape(B, T, -1, d)
    if grouped:
        t_pad = _round_up(T, SUBLANES)
        gla_in = jnp.pad(proj, ((0, 0), (0, t_pad - T), (0, 0))).reshape(B * t_pad, PROJ_WIDTH)
        tb = chunk = t_pad
    else:
        t_pad, gla_in, tb, chunk = T, proj.reshape(m, PROJ_WIDTH), GLA_BLOCK_ROWS, GLA_KERNEL_CHUNK
    o_gla, st_gla = _gla(gla_in, B, tb, chunk, T if grouped else chunk,
                         lw['gla_wa'], lw['gla_ba'], lw['gla_norm'], _gla_state_in(gla_s0))
    o_gla = o_gla.reshape(B, t_pad, GLA_WIDTH)[:, :T]
    s_gla = _gla_state_out(st_gla)
    conv_out, sc_state = _causal_dwconv(scc * shh, sc_hist, lw['sc_conv'])
    o_sc = sb * conv_out
    kvr = lambda a: a.reshape(B, T, 2, NSA_KV_HEADS, NSA_HD)
    kv_cmp, kv_slc, kv_win = kvr(ncmp), kvr(nslc), kvr(nwin)
    o_nsa, win_state = nsa_apply(heads(nq, NSA_HD), kv_cmp, kv_slc, kv_win, heads(ngate, 3))
    mix = jnp.concatenate([o_gla, o_sc, o_nsa], axis=-1).reshape(m, D_MODEL)
    x2 = _out_proj(x2, mix, sgt1, lw['w_out'], tm)

    ff2 = 2 * D_FF
    if grouped:
        zero = jnp.zeros((B, 1, ff2), jnp.float32)
        h1 = jnp.concatenate([ffn_hist[:, 1:2], zero, zero, zero], axis=1).reshape(m, ff2)
        h2 = jnp.concatenate([ffn_hist[:, 0:1], ffn_hist[:, 1:2], zero, zero], axis=1).reshape(m, ff2)
        hist = (h1, h2)
    else:
        hist = jnp.concatenate(
            [jnp.zeros((HIST_ROWS - (CONV_W - 1), ff2), jnp.float32), ffn_hist[0]], axis=0)
    y, st = _ffn(x2, lw['norm_ffn'], ssc2, ssh2, sgt2, lw['ffn_up'], lw['ffn_conv'], lw['ffn_down'],
                 g_final, hist, tm, T if grouped else 0, final)
    if grouped:
        ffn_state = st.reshape(B, T, ff2)[:, -(CONV_W - 1):]
    else:
        ffn_state = st[None, -(CONV_W - 1):]
    return (y.reshape(B, T, D_MODEL), kv_cmp, kv_slc, win_state, s_gla.astype(gla_s0.dtype), sc_state, ffn_state)


def kernel(x_prompt, x_sample, cache_nsa_cmp, cache_nsa_slc, cache_nsa_win, state_gla, state_shortconv, state_ffn_conv, page_table, c_prompt, c_sample, mod_w, mod_b, norm_mix, norm_ffn, w_in, gla_wa2, gla_ba, gla_norm, sc_conv, nsa_cmp_pos, nsa_cmp_w1, nsa_cmp_w2, w_out, ffn_up, ffn_conv, ffn_down, norm_final):
    xp, xs = x_prompt, x_sample
    bp, bs = xp.shape[0], xs.shape[0]
    assert bp == 1 and xs.shape[1] == 4
    c_rows = _round_up(bp + bs, SUBLANES)
    c_all = jnp.concatenate([c_prompt, c_sample, jnp.zeros((c_rows - bp - bs, D_MODEL), jnp.float32)], axis=0)
    mod_all = _modulation(c_all, mod_w, mod_b)
    g_final = norm_final.reshape(1, D_MODEL)
    outs = [[] for _ in range(12)]
    for l in range(DEPTH):
        lw = dict(
            norm_mix=norm_mix[l].reshape(1, D_MODEL), norm_ffn=norm_ffn[l].reshape(1, D_MODEL),
            w_in=_pack_w_in(w_in[l]),
            gla_wa=jnp.zeros((LANES, GLA_LANES), jnp.bfloat16).at[:GLA_GATE_RANK].set(
                gla_wa2[l].astype(jnp.bfloat16)),
            gla_ba=gla_ba[l].reshape(1, GLA_LANES),
            gla_norm=jnp.tile(gla_norm[l], GLA_HEADS).reshape(1, GLA_LANES),
            sc_conv=sc_conv[l], w_out=w_out[l].astype(jnp.bfloat16),
            ffn_up=ffn_up[l].astype(jnp.bfloat16), ffn_conv=ffn_conv[l],
            ffn_down=ffn_down[l].astype(jnp.bfloat16))
        cmp_params = (nsa_cmp_pos[l], nsa_cmp_w1[l], nsa_cmp_w2[l])
        cmp_w = _compress_weights(*cmp_params)
        final = l == DEPTH - 1
        res_p = _trunk_layer(
            xp, mod_all[l, 0:bp], lw,
            jnp.zeros((bp, GLA_HEADS, GLA_DK, GLA_DV), xp.dtype),
            jnp.zeros((bp, CONV_W - 1, SC_WIDTH), xp.dtype),
            jnp.zeros((bp, CONV_W - 1, 2 * D_FF), xp.dtype),
            functools.partial(_nsa_prompt, cmp_w=cmp_w), 256, final, g_final)
        res_s = _trunk_layer(
            xs, mod_all[l, bp:bp + bs], lw, state_gla[l], state_shortconv[l], state_ffn_conv[l],
            functools.partial(_nsa_sample, pool_cmp=cache_nsa_cmp, pool_slc=cache_nsa_slc,
                              page_table=page_table, win_cache=cache_nsa_win, layer=l,
                              cmp_w=cmp_w), bs * xs.shape[1], final, g_final)
        xp, xs = res_p[0], res_s[0]
        for k in range(6):
            outs[2 * k].append(res_p[k + 1])
            outs[2 * k + 1].append(res_s[k + 1])
    return (xp, xs) + tuple(jnp.stack(o) for o in outs)
```

```python
import functools

import jax
import jax.numpy as jnp
from jax import lax
from jax.experimental import pallas as pl
from jax.experimental.pallas import tpu as pltpu

D_MODEL = 1024
DEPTH = 2
PAGE_SIZE = 128
GLA_HEADS = 4
GLA_DK = D_MODEL // 16
GLA_DV = D_MODEL // 16
GLA_WIDTH = GLA_HEADS * GLA_DV
GLA_GATE_RANK = 16
GLA_GATE_TEMP = 16.0
GLA_CHUNK = 64
SC_WIDTH = D_MODEL // 4
CONV_W = 3
NSA_HEADS = 8
NSA_KV_HEADS = 2
NSA_HD = D_MODEL // 16
NSA_WIDTH = NSA_HEADS * NSA_HD
NSA_KV_WIDTH = 2 * NSA_KV_HEADS * NSA_HD
CMP_STRIDE = 16
CMP_BLOCK = 2 * CMP_STRIDE
CMP_HIDDEN = 128
SEL_BLOCK = 64
SEL_TOPN = 16
WINDOW = 512
Q_BLOCK = 128
D_FF = 2816
EPS = 1e-6
NEG = -1e30
TINY = 1e-30
FORCE = 1e9

IN_SIZES = (
    GLA_HEADS * GLA_DK, GLA_HEADS * GLA_DK, GLA_WIDTH, GLA_WIDTH, GLA_GATE_RANK,
    SC_WIDTH, SC_WIDTH, SC_WIDTH,
    NSA_WIDTH, NSA_KV_WIDTH, NSA_KV_WIDTH, NSA_KV_WIDTH, NSA_HEADS * 3,
)

LANES = 128
SUBLANES = 8
BF16_SUBLANES = 16
VMEM_LIMIT_BYTES = 56 * 1024 * 1024

PROJ_ORDER = (0, 1, 2, 3, 5, 6, 7, 8, 9, 10, 11, 4, 12)


def _round_up(n, m):
    return -(-n // m) * m


def _proj_layout():
    src, acc = [], 0
    for s in IN_SIZES:
        src.append(acc)
        acc += s
    offs, dst = {}, 0
    for p in PROJ_ORDER:
        offs[p] = dst
        dst += _round_up(IN_SIZES[p], LANES)
    return src, offs, dst


PROJ_SRC, PROJ_OFF, PROJ_WIDTH = _proj_layout()


def _pack_w_in(w_in):
    out = jnp.zeros((D_MODEL, PROJ_WIDTH), jnp.bfloat16)
    for p in PROJ_ORDER:
        piece = w_in[:, PROJ_SRC[p]:PROJ_SRC[p] + IN_SIZES[p]].astype(jnp.bfloat16)
        out = lax.dynamic_update_slice(out, piece, (0, PROJ_OFF[p]))
    return out


def _proj_piece(proj, p):
    return proj[..., PROJ_OFF[p]:PROJ_OFF[p] + IN_SIZES[p]]


def _mod_kernel(c_ref, w_ref, b_ref, o_ref):
    c = c_ref[...]
    a = c * jax.nn.sigmoid(c)
    o_ref[0] = jnp.dot(a, w_ref[0], preferred_element_type=jnp.float32,
                       precision=lax.Precision.HIGHEST) + b_ref[0]


def _modulation(c_all, mod_w, mod_b):
    rows = c_all.shape[0]
    tn = 1024
    n = mod_w.shape[-1]
    return pl.pallas_call(
        _mod_kernel,
        grid=(DEPTH, n // tn),
        in_specs=[
            pl.BlockSpec((rows, D_MODEL), lambda l, j: (0, 0)),
            pl.BlockSpec((1, D_MODEL, tn), lambda l, j: (l, 0, j)),
            pl.BlockSpec((1, 1, tn), lambda l, j: (l, 0, j)),
        ],
        out_specs=pl.BlockSpec((1, rows, tn), lambda l, j: (l, 0, j)),
        out_shape=jax.ShapeDtypeStruct((DEPTH, rows, n), jnp.float32),
        name="adaln_modulation",
    )(c_all, mod_w, mod_b.reshape(DEPTH, 1, n))


def _norm_mod(x, g, sc, sh):
    r = lax.rsqrt(jnp.mean(x * x, axis=-1, keepdims=True) + EPS)
    return (x * r * g) * (1.0 + sc) + sh


def _in_proj_kernel(x_ref, g_ref, sc_ref, sh_ref, w_ref, o_ref):
    h = _norm_mod(x_ref[...], g_ref[...], sc_ref[...], sh_ref[...])
    o_ref[...] = jnp.dot(h.astype(jnp.bfloat16), w_ref[...], preferred_element_type=jnp.float32)


def _row_spec(tm, per_row):
    if per_row:
        return pl.BlockSpec((tm, D_MODEL), lambda i: (i, 0))
    return pl.BlockSpec((1, D_MODEL), lambda i: (0, 0))


def _resident(shape):
    return pl.BlockSpec(shape, lambda i: (0,) * len(shape), pipeline_mode=pl.Buffered(1))


def _in_proj(x, g, sc, sh, w_packed, tm):
    m = x.shape[0]
    per_row = sc.shape[0] != 1
    return pl.pallas_call(
        _in_proj_kernel,
        grid=(m // tm,),
        in_specs=[
            pl.BlockSpec((tm, D_MODEL), lambda i: (i, 0)),
            _resident((1, D_MODEL)),
            _row_spec(tm, per_row),
            _row_spec(tm, per_row),
            _resident((D_MODEL, PROJ_WIDTH)),
        ],
        out_specs=pl.BlockSpec((tm, PROJ_WIDTH), lambda i: (i, 0)),
        out_shape=jax.ShapeDtypeStruct((m, PROJ_WIDTH), jnp.float32),
        compiler_params=pltpu.CompilerParams(
            dimension_semantics=("arbitrary",), vmem_limit_bytes=VMEM_LIMIT_BYTES),
        name="norm_in_proj",
    )(x, g, sc, sh, w_packed)


def _out_proj_kernel(x_ref, mix_ref, gt_ref, w_ref, o_ref):
    y = jnp.dot(mix_ref[...].astype(jnp.bfloat16), w_ref[...], preferred_element_type=jnp.float32)
    o_ref[...] = x_ref[...] + gt_ref[...] * y


def _out_proj(x, mix, gt, w_bf16, tm):
    m = x.shape[0]
    per_row = gt.shape[0] != 1
    return pl.pallas_call(
        _out_proj_kernel,
        grid=(m // tm,),
        in_specs=[
            pl.BlockSpec((tm, D_MODEL), lambda i: (i, 0)),
            pl.BlockSpec((tm, D_MODEL), lambda i: (i, 0)),
            _row_spec(tm, per_row),
            _resident((D_MODEL, D_MODEL)),
        ],
        out_specs=pl.BlockSpec((tm, D_MODEL), lambda i: (i, 0)),
        out_shape=jax.ShapeDtypeStruct((m, D_MODEL), jnp.float32),
        compiler_params=pltpu.CompilerParams(
            dimension_semantics=("arbitrary",), vmem_limit_bytes=VMEM_LIMIT_BYTES),
        name="out_proj_residual",
    )(x, mix, gt, w_bf16)


FFN_UP_CHUNK = 512
FFN_ACT_CHUNK = 256
HIST_ROWS = SUBLANES


def _ffn_kernel(*refs, tm, group, final):
    grouped = group > 0
    if grouped:
        (x_ref, g_ref, sc_ref, sh_ref, gt_ref, wup_ref, cw_ref, wdn_ref, gf_ref,
         h1_ref, h2_ref, o_ref, st_ref, up_s) = refs
    else:
        (x_ref, g_ref, sc_ref, sh_ref, gt_ref, wup_ref, cw_ref, wdn_ref, gf_ref,
         h0_ref, o_ref, st_ref, up_s) = refs

        @pl.when(pl.program_id(0) == 0)
        def _():
            up_s[0:HIST_ROWS, :] = h0_ref[...]

    x = x_ref[...]
    h = _norm_mod(x, g_ref[...], sc_ref[...], sh_ref[...]).astype(jnp.bfloat16)
    for c in range(2 * D_FF // FFN_UP_CHUNK):
        cols = slice(c * FFN_UP_CHUNK, (c + 1) * FFN_UP_CHUNK)
        up_s[HIST_ROWS:HIST_ROWS + tm, cols] = jnp.dot(
            h, wup_ref[:, cols], preferred_element_type=jnp.float32)

    if grouped:
        t = lax.broadcasted_iota(jnp.int32, (tm, 1), 0) % group

    def conv(cols):
        cur = up_s[HIST_ROWS:HIST_ROWS + tm, cols]
        p1 = up_s[HIST_ROWS - 1:HIST_ROWS - 1 + tm, cols]
        p2 = up_s[HIST_ROWS - 2:HIST_ROWS - 2 + tm, cols]
        if grouped:
            p1 = jnp.where(t == 0, h1_ref[:, cols], p1)
            p2 = jnp.where(t <= 1, h2_ref[:, cols], p2)
        return cw_ref[0:1, cols] * p2 + cw_ref[1:2, cols] * p1 + cw_ref[2:3, cols] * cur

    acc = jnp.zeros((tm, D_MODEL), jnp.float32)
    for c in range(D_FF // FFN_ACT_CHUNK):
        a = conv(slice(c * FFN_ACT_CHUNK, (c + 1) * FFN_ACT_CHUNK))
        b = conv(slice(D_FF + c * FFN_ACT_CHUNK, D_FF + (c + 1) * FFN_ACT_CHUNK))
        act = (a * jax.nn.sigmoid(a) * b).astype(jnp.bfloat16)
        acc = acc + jnp.dot(act, wdn_ref[c * FFN_ACT_CHUNK:(c + 1) * FFN_ACT_CHUNK, :],
                            preferred_element_type=jnp.float32)
    y = x + gt_ref[...] * acc
    if final:
        r = lax.rsqrt(jnp.mean(y * y, axis=-1, keepdims=True) + EPS)
        y = y * r * gf_ref[...]
    o_ref[...] = y

    if grouped:
        st_ref[...] = up_s[HIST_ROWS:HIST_ROWS + tm, :]
    else:
        tail = up_s[tm:tm + HIST_ROWS, :]
        st_ref[...] = tail
        up_s[0:HIST_ROWS, :] = tail


def _ffn(x, g, sc, sh, gt, wup, cw, wdn, g_final, hist, tm, group, final):
    m = x.shape[0]
    grouped = group > 0
    per_row = sc.shape[0] != 1
    ff2 = 2 * D_FF
    in_specs = [
        pl.BlockSpec((tm, D_MODEL), lambda i: (i, 0)),
        _resident((1, D_MODEL)),
        _row_spec(tm, per_row), _row_spec(tm, per_row), _row_spec(tm, per_row),
        _resident((D_MODEL, ff2)),
        _resident((CONV_W, ff2)),
        _resident((D_FF, D_MODEL)),
        _resident((1, D_MODEL)),
    ]
    if grouped:
        assert m == tm
        in_specs += [_resident((tm, ff2)), _resident((tm, ff2))]
        hist_args = tuple(hist)
        st_rows = tm
    else:
        in_specs += [_resident((HIST_ROWS, ff2))]
        hist_args = (hist,)
        st_rows = HIST_ROWS
    return pl.pallas_call(
        functools.partial(_ffn_kernel, tm=tm, group=group, final=final),
        grid=(m // tm,),
        in_specs=in_specs,
        out_specs=[pl.BlockSpec((tm, D_MODEL), lambda i: (i, 0)),
                   pl.BlockSpec((st_rows, ff2), lambda i: (0, 0))],
        out_shape=[jax.ShapeDtypeStruct((m, D_MODEL), jnp.float32),
                   jax.ShapeDtypeStruct((st_rows, ff2), jnp.float32)],
        scratch_shapes=[pltpu.VMEM((HIST_ROWS + tm, ff2), jnp.float32)],
        compiler_params=pltpu.CompilerParams(
            dimension_semantics=("arbitrary",), vmem_limit_bytes=VMEM_LIMIT_BYTES),
        name="conv_ffn",
    )(x, g, sc, sh, gt, wup, cw, wdn, g_final, *hist_args)


CMP_ROW = CMP_STRIDE * NSA_KV_WIDTH
CMP_HID = 2 * NSA_KV_HEADS * CMP_HIDDEN


def _gelu_tanh(x):
    return 0.5 * x * (1.0 + jnp.tanh(0.7978845608028654 * (x + 0.044715 * (x * x * x))))


def _compress_kernel(x_ref, xn_ref, pos_ref, wl_ref, wt_ref, w2_ref, o_ref, tr_s, *, tm):
    bf = jnp.bfloat16
    f32 = jnp.float32
    x = x_ref[...].astype(bf)
    lead = jnp.dot(x, wl_ref[...], preferred_element_type=f32)
    tr_s[0:tm, :] = jnp.dot(x, wt_ref[...], preferred_element_type=f32)
    tr_s[tm:tm + SUBLANES, :] = jnp.dot(xn_ref[...].astype(bf), wt_ref[...], preferred_element_type=f32)
    bias = (jnp.dot(pos_ref[0].astype(bf), wl_ref[...], preferred_element_type=f32)
            + jnp.dot(pos_ref[1].astype(bf), wt_ref[...], preferred_element_type=f32))[0:1, :]
    hid = _gelu_tanh(lead + tr_s[1:tm + 1, :] + bias)
    o_ref[...] = jnp.dot(hid.astype(bf), w2_ref[...], preferred_element_type=f32)


def _compress_weights(pos_emb, w1, w2):
    eye = jnp.eye(2, dtype=jnp.float32)
    w1f = jnp.einsum('kldh,kK,gG->lkgdKGh', w1, eye, eye).reshape(CMP_BLOCK, NSA_KV_WIDTH, CMP_HID)
    w1f = w1f.astype(jnp.bfloat16)
    w2b = jnp.einsum('khd,kK,gG->kghKGd', w2, eye, eye).reshape(CMP_HID, NSA_KV_WIDTH).astype(jnp.bfloat16)
    posf = jnp.broadcast_to(pos_emb.transpose(1, 0, 2)[:, :, None, :], (CMP_BLOCK, 2, NSA_KV_HEADS, NSA_HD))
    posf = posf.reshape(CMP_BLOCK, NSA_KV_WIDTH)
    pos = jnp.zeros((CMP_BLOCK, SUBLANES, NSA_KV_WIDTH), jnp.float32).at[:, 0].set(posf)
    pos_rows = jnp.zeros((2, SUBLANES, CMP_ROW), jnp.float32).at[:, 0].set(posf.reshape(2, CMP_ROW))
    w1kv = jnp.einsum('kldh,gG->lkgdGh', w1, eye).reshape(
        CMP_BLOCK // 2, 2, 2, NSA_KV_WIDTH // 2, CMP_HID // 2).transpose(0, 2, 1, 3, 4).reshape(
        CMP_BLOCK // 2, 2, NSA_KV_WIDTH, CMP_HID // 2).astype(jnp.bfloat16)
    return dict(pos=pos, w1=w1kv, w2=w2b, pos_rows=pos_rows,
                wl=w1f[:CMP_STRIDE].reshape(CMP_ROW, CMP_HID), wt=w1f[CMP_STRIDE:].reshape(CMP_ROW, CMP_HID))


def _compress(x, cw, tm):
    pos, wl, wt, w2b = cw['pos_rows'], cw['wl'], cw['wt'], cw['w2']
    n = x.shape[0]
    nb8 = n // SUBLANES
    return pl.pallas_call(
        functools.partial(_compress_kernel, tm=tm),
        grid=(n // tm,),
        in_specs=[
            pl.BlockSpec((tm, CMP_ROW), lambda i: (i, 0)),
            pl.BlockSpec((SUBLANES, CMP_ROW), lambda i: (jnp.minimum((i + 1) * (tm // SUBLANES), nb8 - 1), 0)),
            _resident((2, SUBLANES, CMP_ROW)),
            _resident((CMP_ROW, CMP_HID)), _resident((CMP_ROW, CMP_HID)),
            _resident((CMP_HID, NSA_KV_WIDTH)),
        ],
        out_specs=pl.BlockSpec((tm, NSA_KV_WIDTH), lambda i: (i, 0)),
        out_shape=jax.ShapeDtypeStruct((n, NSA_KV_WIDTH), jnp.float32),
        scratch_shapes=[pltpu.VMEM((tm + SUBLANES, CMP_HID), jnp.float32)],
        compiler_params=pltpu.CompilerParams(
            dimension_semantics=("arbitrary",), vmem_limit_bytes=VMEM_LIMIT_BYTES),
        name="nsa_compress",
    )(x, x, pos, wl, wt, w2b)


NSA_R = NSA_HEADS // NSA_KV_HEADS
QL = NSA_R * Q_BLOCK
QLL = NSA_KV_HEADS * QL
SLC_TILE = 512
BLK_PER_TILE = SLC_TILE // SEL_BLOCK
WIN_TILE = WINDOW + Q_BLOCK
CMP_PER_SEL = SEL_BLOCK // CMP_STRIDE
M_INIT = -1e29
QK_SCALE = NSA_HD ** -0.5 * 1.4426950408889634
ONES_ROWS = BF16_SUBLANES


def _tile_lanes(v, reps):
    return jnp.concatenate([v] * reps, axis=1)


def _nsa_prompt_kernel(qbd_ref, gate_ref, kc_ref, vct_ref, kslc_ref, vtslc_ref, kwin_ref, vtwin_ref,
                       o_ref, sc_s, sel_s, m_s, l_s, acc_s, mt_s, al_s, s_s, pt_s, *, n_sel):
    f32, bf = jnp.float32, jnp.bfloat16
    G, HD = NSA_KV_HEADS, NSA_HD
    n = pl.program_id(0)
    qbd = qbd_ref[0]
    lane = lax.broadcasted_iota(jnp.int32, (1, Q_BLOCK), 1)
    pos_q = n * Q_BLOCK + lane
    pos_l = _tile_lanes(pos_q, QLL // Q_BLOCK)
    jrow = lax.broadcasted_iota(jnp.int32, (n_sel, 1), 0)

    s_c, mk_c = [], []
    m = jnp.full((1, QLL), NEG, f32)
    for c in range(CMP_PER_SEL):
        s = jnp.dot(kc_ref[c * n_sel:(c + 1) * n_sel, :], qbd, preferred_element_type=f32)
        mk = jrow * SEL_BLOCK + (c * CMP_STRIDE + CMP_BLOCK - 1) <= pos_l
        s = jnp.where(mk, s, NEG)
        m = jnp.maximum(m, jnp.max(s, axis=0, keepdims=True))
        s_c.append(s)
        mk_c.append(mk)
    e_c = [jnp.where(mk_c[c], jnp.exp2(s_c[c] - m), 0.0) for c in range(CMP_PER_SEL)]
    l = e_c[0].sum(axis=0, keepdims=True)
    for c in range(1, CMP_PER_SEL):
        l = l + e_c[c].sum(axis=0, keepdims=True)
    inv = 1.0 / jnp.maximum(l, TINY)
    o_cmp = [jnp.zeros((HD, QL), f32) for _ in range(G)]
    pg = []
    for c in range(CMP_PER_SEL):
        p = e_c[c] * inv
        pb = p.astype(bf)
        for g in range(G):
            o_cmp[g] = o_cmp[g] + jnp.dot(vct_ref[g * HD:(g + 1) * HD, c * n_sel:(c + 1) * n_sel],
                                          pb[:, g * QL:(g + 1) * QL], preferred_element_type=f32)
        pg.append([sum(p[:, g * QL + r * Q_BLOCK:g * QL + (r + 1) * Q_BLOCK] for r in range(NSA_R))
                   for g in range(G)])
    o_cmp = jnp.concatenate(o_cmp, axis=1)

    cur = pos_q // SEL_BLOCK
    forced = (jrow == 0) | (jrow == cur) | (jrow == cur - 1)
    allowed = jrow * SEL_BLOCK <= pos_q
    jrow_f = jrow.astype(f32)
    for g in range(G):
        last = pg[CMP_PER_SEL - 1][g]
        prev = jnp.where(jrow == 0, 0.0, pltpu.roll(last, 1, 0))
        inner = pg[0][g]
        for c in range(1, CMP_PER_SEL - 1):
            inner = inner + pg[c][g]
        p_slc = 2.0 * inner + last + prev
        sc_s[g] = jnp.where(forced, FORCE, jnp.where(allowed, p_slc, -1.0))
        sel_s[g] = jnp.zeros((n_sel, Q_BLOCK), f32)

    def pick(_, carry):
        for g in range(G):
            s = sc_s[g]
            top = jnp.max(s, axis=0, keepdims=True)
            first = jnp.min(jnp.where(s == top, jrow_f, float(n_sel)), axis=0, keepdims=True)
            hit = jrow_f == first
            sc_s[g] = jnp.where(hit, -jnp.inf, s)
            sel_s[g] = jnp.where(hit, 1.0, sel_s[g])
        return carry

    lax.fori_loop(0, min(SEL_TOPN, n_sel), pick, 0)
    for g in range(G):
        sel_s[g] = jnp.where(allowed & (sel_s[g] > 0.5), 0.0, NEG)

    def reset():
        m_s[...] = jnp.full((1, QLL), M_INIT, f32)
        l_s[...] = jnp.zeros((1, QLL), f32)
        acc_s[...] = jnp.zeros((HD, QLL), f32)

    def pass1(slot, k_ref, start, rows, bias_fn):
        s = jnp.dot(k_ref[pl.ds(start, rows), :], qbd, preferred_element_type=f32)
        top = jnp.full((SUBLANES, QLL), NEG, f32)
        for i in range(rows // SEL_BLOCK):
            blk = slice(i * SEL_BLOCK, (i + 1) * SEL_BLOCK)
            sb = s[blk, :] + bias_fn(i)
            s_s[slot, blk, :] = sb
            for r in range(SEL_BLOCK // SUBLANES):
                top = jnp.maximum(top, sb[r * SUBLANES:(r + 1) * SUBLANES, :])
        m_old = m_s[...]
        m_new = jnp.maximum(m_old, jnp.max(top, axis=0, keepdims=True))
        mt_s[slot] = m_new
        al_s[slot] = jnp.exp2(m_old - m_new)
        m_s[...] = m_new

    def pass2(slot, vta_ref, start, rows):
        m_new = mt_s[slot]
        for i in range(rows // SEL_BLOCK):
            blk = slice(i * SEL_BLOCK, (i + 1) * SEL_BLOCK)
            pt_s[slot, blk, :] = jnp.exp2(s_s[slot, blk, :] - m_new).astype(bf)
        pv, psum = [], []
        for g in range(G):
            r = jnp.dot(vta_ref[g, :, pl.ds(start, rows)], pt_s[slot, 0:rows, g * QL:(g + 1) * QL],
                        preferred_element_type=f32)
            pv.append(r[0:HD, :])
            psum.append(r[HD:HD + 1, :])
        alpha = al_s[slot]
        acc_s[...] = acc_s[...] * alpha + jnp.concatenate(pv, axis=1)
        l_s[...] = l_s[...] * alpha + jnp.concatenate(psum, axis=1)

    def finish():
        return acc_s[...] * (1.0 / jnp.maximum(l_s[...], TINY))

    def tile_start(kt):
        return pl.multiple_of(kt * SLC_TILE, SLC_TILE)

    def slc_pass1(slot, kt, causal):
        start = tile_start(kt)
        selb = [sel_s[g, pl.ds(pl.multiple_of(kt * BLK_PER_TILE, BLK_PER_TILE), BLK_PER_TILE), :]
                for g in range(G)]

        def bias_fn(i):
            row = jnp.concatenate([_tile_lanes(selb[g][i:i + 1, :], NSA_R) for g in range(G)], axis=1)
            if not causal:
                return row
            tok = start + i * SEL_BLOCK + lax.broadcasted_iota(jnp.int32, (SEL_BLOCK, 1), 0)
            return jnp.where(tok <= pos_l, row, NEG)

        pass1(slot, kslc_ref, start, SLC_TILE, bias_fn)

    def slc_pass2(slot, kt):
        pass2(slot, vtslc_ref, tile_start(kt), SLC_TILE)

    reset()
    diag = (n * Q_BLOCK) // SLC_TILE
    slc_pass1(0, diag, True)

    def slc_pair(j, carry):
        slc_pass1(1, 2 * j, False)
        slc_pass2(0, jnp.where(j == 0, diag, 2 * j - 1))
        slc_pass1(0, 2 * j + 1, False)
        slc_pass2(1, 2 * j)
        return carry

    pairs = diag // 2
    lax.fori_loop(0, pairs, slc_pair, 0)
    pending = jnp.where(pairs == 0, diag, 2 * pairs - 1)

    @pl.when(diag % 2 == 1)
    def _():
        slc_pass1(1, diag - 1, False)
        slc_pass2(0, pending)
        slc_pass2(1, diag - 1)

    @pl.when(diag % 2 == 0)
    def _():
        slc_pass2(0, pending)

    o_slc = finish()

    reset()
    wstart = pl.multiple_of(jnp.maximum(n * Q_BLOCK - WINDOW, 0), Q_BLOCK)
    rel = pos_q - (wstart + lax.broadcasted_iota(jnp.int32, (WIN_TILE, 1), 0))
    wbias = jnp.where((rel >= 0) & (rel <= WINDOW), 0.0, NEG)
    pass1(0, kwin_ref, wstart, WIN_TILE,
          lambda i: _tile_lanes(wbias[i * SEL_BLOCK:(i + 1) * SEL_BLOCK, :], QLL // Q_BLOCK))
    pass2(0, vtwin_ref, wstart, WIN_TILE)
    o_win = finish()

    gate = jax.nn.sigmoid(gate_ref[0])
    o_ref[0] = gate[0:1, :] * o_cmp + gate[1:2, :] * o_slc + gate[2:3, :] * o_win


def _nsa_prompt_attention(nq, ngate, kvc, nslc, nwin):
    T = nq.shape[0]
    nb, n_sel = T // Q_BLOCK, T // SEL_BLOCK
    G, R, HD = NSA_KV_HEADS, NSA_R, NSA_HD
    bf = jnp.bfloat16
    half = G * HD
    qt = (nq * QK_SCALE).reshape(nb, Q_BLOCK, G, R, HD).transpose(0, 2, 4, 3, 1)
    qbd = jnp.einsum('ngdrq,gh->ngdhrq', qt, jnp.eye(G, dtype=jnp.float32)).reshape(nb, half, QLL).astype(bf)
    gate = ngate.reshape(nb, Q_BLOCK, G, R, 3).transpose(0, 4, 2, 3, 1).reshape(nb, 3, QLL)
    kvp = kvc.reshape(n_sel, CMP_PER_SEL, NSA_KV_WIDTH).transpose(1, 0, 2).reshape(T // CMP_STRIDE, NSA_KV_WIDTH)
    kc, vct = kvp[:, :half].astype(bf), kvp[:, half:].T.astype(bf)
    def vt_ones(v):
        vt = v.T.reshape(G, HD, T)
        return jnp.concatenate([vt, jnp.ones((G, ONES_ROWS, T), vt.dtype)], axis=1).astype(bf)
    kslc, vtslc = nslc[:, :half].astype(bf), vt_ones(nslc[:, half:])
    kwin, vtwin = nwin[:, :half].astype(bf), vt_ones(nwin[:, half:])
    out = pl.pallas_call(
        functools.partial(_nsa_prompt_kernel, n_sel=n_sel),
        grid=(nb,),
        in_specs=[
            pl.BlockSpec((1, half, QLL), lambda i: (i, 0, 0)),
            pl.BlockSpec((1, 3, QLL), lambda i: (i, 0, 0)),
            _resident((T // CMP_STRIDE, half)), _resident((half, T // CMP_STRIDE)),
            _resident((T, half)), _resident((G, HD + ONES_ROWS, T)),
            _resident((T, half)), _resident((G, HD + ONES_ROWS, T)),
        ],
        out_specs=pl.BlockSpec((1, HD, QLL), lambda i: (i, 0, 0)),
        out_shape=jax.ShapeDtypeStruct((nb, HD, QLL), jnp.float32),
        scratch_shapes=[
            pltpu.VMEM((G, n_sel, Q_BLOCK), jnp.float32),
            pltpu.VMEM((G, n_sel, Q_BLOCK), jnp.float32),
            pltpu.VMEM((1, QLL), jnp.float32),
            pltpu.VMEM((1, QLL), jnp.float32),
            pltpu.VMEM((HD, QLL), jnp.float32),
            pltpu.VMEM((2, 1, QLL), jnp.float32),
            pltpu.VMEM((2, 1, QLL), jnp.float32),
            pltpu.VMEM((2, WIN_TILE, QLL), jnp.float32),
            pltpu.VMEM((2, WIN_TILE, QLL), jnp.bfloat16),
        ],
        compiler_params=pltpu.CompilerParams(
            dimension_semantics=("arbitrary",), vmem_limit_bytes=VMEM_LIMIT_BYTES),
        name="nsa_prompt_attention",
    )(qbd, gate, kc, vct, kslc, vtslc, kwin, vtwin)
    return out.reshape(nb, HD, G, R, Q_BLOCK).transpose(0, 4, 2, 3, 1).reshape(T, NSA_WIDTH)


PAGE_ROWS = PAGE_SIZE // CMP_STRIDE
SQ = 16
SAMPLE_LANES = NSA_HEADS * SQ
HALF_PAGES = 64
SAMPLE_TILE = 2048


def _page_view(pool):
    d, n_pool = pool.shape[:2]
    return pool.transpose(0, 1, 3, 4, 5, 2).reshape(d * n_pool, 2, NSA_KV_HEADS * NSA_HD, pool.shape[2])


def _page_copy(pool_ref, buf_ref, sem_ref, page, slot, idx):
    return pltpu.make_async_copy(pool_ref.at[page], buf_ref.at[slot, idx], sem_ref.at[slot])


def _gather_schedule(issue_fn, wait_fn):
    s = pl.program_id(0)

    @pl.when(s == 0)
    def _():
        issue_fn(s, 0)

    @pl.when(s + 1 < pl.num_programs(0))
    def _():
        issue_fn(s + 1, (s + 1) % 2)

    wait_fn(s, s % 2)


def _compress_paged_kernel(pt_ref, pool_ref, pos_ref, w1_ref, w2_ref, o_ref, buf, sem, tok_s,
                           *, page_base, n_pages):
    bf, f32 = jnp.bfloat16, jnp.float32
    rows = HALF_PAGES * PAGE_ROWS

    def copies(step, slot, fn):
        b, half = step // 2, step % 2

        def body(i, c):
            p = jnp.minimum(half * HALF_PAGES + i, n_pages - 1)
            fn(_page_copy(pool_ref, buf, sem, page_base + pt_ref[b, p], slot, i))
            return c

        lax.fori_loop(0, HALF_PAGES + 1, body, 0)

    _gather_schedule(lambda s, slot: copies(s, slot, lambda cp: cp.start()),
                     lambda s, slot: copies(s, slot, lambda cp: cp.wait()))
    slot = pl.program_id(0) % 2

    half = NSA_KV_HEADS * NSA_HD

    def to_token_rows(i, c):
        dst = pl.ds(pl.multiple_of(i * PAGE_SIZE, PAGE_SIZE), PAGE_SIZE)
        for kv in range(2):
            tok_s[kv, dst, :] = buf[slot, i, kv].T
        return c

    lax.fori_loop(0, HALF_PAGES + 1, to_token_rows, 0, unroll=5)
    acc = [jnp.zeros((rows, CMP_HID // 2), f32) for _ in range(2)]
    bias = [jnp.zeros((SUBLANES, CMP_HID // 2), f32) for _ in range(2)]
    for l in range(0, CMP_BLOCK, 2):
        for kv in range(2):
            x = jnp.concatenate([tok_s.at[kv][pl.ds(l + d, rows, stride=CMP_STRIDE), :] for d in range(2)],
                                axis=1).astype(bf)
            w = w1_ref[l // 2, kv]
            p = jnp.concatenate([pos_ref[l + d, :, kv * half:(kv + 1) * half] for d in range(2)], axis=1)
            acc[kv] = acc[kv] + jnp.dot(x, w, preferred_element_type=f32)
            bias[kv] = bias[kv] + jnp.dot(p.astype(bf), w, preferred_element_type=f32)
    hid = _gelu_tanh(jnp.concatenate(acc, axis=1) + jnp.concatenate(bias, axis=1)[0:1, :])
    o_ref[...] = jnp.dot(hid.astype(bf), w2_ref[...], preferred_element_type=f32)


def _compress_paged(page_table, pool, layer, cw):
    nbatch, n_pages = page_table.shape
    assert n_pages == 2 * HALF_PAGES
    rows = HALF_PAGES * PAGE_ROWS
    const = lambda shape: pl.BlockSpec(shape, lambda s, pt: (0,) * len(shape), pipeline_mode=pl.Buffered(1))
    return pl.pallas_call(
        functools.partial(_compress_paged_kernel, page_base=layer * (pool.shape[0] // DEPTH), n_pages=n_pages),
        grid_spec=pltpu.PrefetchScalarGridSpec(
            num_scalar_prefetch=1,
            grid=(2 * nbatch,),
            in_specs=[
                pl.BlockSpec(memory_space=pl.ANY),
                const((CMP_BLOCK, SUBLANES, NSA_KV_WIDTH)),
                const((CMP_BLOCK // 2, 2, NSA_KV_WIDTH, CMP_HID // 2)),
                const((CMP_HID, NSA_KV_WIDTH)),
            ],
            out_specs=pl.BlockSpec((rows, NSA_KV_WIDTH), lambda s, pt: (s, 0)),
            scratch_shapes=[
                pltpu.VMEM((2, HALF_PAGES + 1) + pool.shape[1:], jnp.float32),
                pltpu.SemaphoreType.DMA((2,)),
                pltpu.VMEM((2, (HALF_PAGES + 1) * PAGE_SIZE, NSA_KV_WIDTH // 2), jnp.float32),
            ]),
        out_shape=jax.ShapeDtypeStruct((2 * nbatch * rows, NSA_KV_WIDTH), jnp.float32),
        compiler_params=pltpu.CompilerParams(
            dimension_semantics=("arbitrary",), vmem_limit_bytes=VMEM_LIMIT_BYTES),
        name="nsa_compress_paged",
    )(page_table, pool, cw['pos'], cw['w1'], cw['w2'])


def _nsa_sample_kernel(pt_ref, pool_ref, qbd_ref, gate_ref, kc_ref, vc_ref, nslc_ref, wcache_ref, nwin_ref,
                       o_ref, buf, sem, sc_s, sel_s, m_s, l_s, acc_s, *, page_base, n_pages, n_sel, t_new):
    f32, bf = jnp.float32, jnp.bfloat16
    HD, LN = NSA_HD, SAMPLE_LANES
    half = NSA_KV_HEADS * HD
    past = n_pages * PAGE_SIZE
    contract_rows = (((0,), (0,)), ((), ()))

    def copies(step, slot, fn):
        def body(p, c):
            fn(_page_copy(pool_ref, buf, sem, page_base + pt_ref[step, p], slot, p))
            return c

        lax.fori_loop(0, n_pages, body, 0)

    _gather_schedule(lambda s, slot: copies(s, slot, lambda cp: cp.start()),
                     lambda s, slot: copies(s, slot, lambda cp: cp.wait()))
    slot = pl.program_id(0) % 2

    qbd = qbd_ref[0]
    lane = lax.broadcasted_iota(jnp.int32, (1, LN), 1)
    pos_l = past + lane % SQ
    group0 = lane < LN // NSA_KV_HEADS
    jrow = lax.broadcasted_iota(jnp.int32, (n_sel, 1), 0)
    jrow_f = jrow.astype(f32)

    def group_rows(full):
        return jnp.where(group0, full[0:HD, :], full[HD:2 * HD, :])

    def pv(v, pt):
        return group_rows(lax.dot_general(v.astype(bf), pt, contract_rows, preferred_element_type=f32))

    def scores_t(kt):
        return lax.dot_general(kt.astype(bf), qbd, contract_rows, preferred_element_type=f32)

    li = lax.broadcasted_iota(jnp.int32, (LN, LN), 0)
    lj = lax.broadcasted_iota(jnp.int32, (LN, LN), 1)
    same = ((li // (NSA_R * SQ) == lj // (NSA_R * SQ)) & (li % SQ == lj % SQ)).astype(bf)

    def head_sum(p):
        hi = p.astype(bf)
        r1 = p - hi.astype(f32)
        mid = r1.astype(bf)
        lo = (r1 - mid.astype(f32)).astype(bf)
        return (jnp.dot(hi, same, preferred_element_type=f32) + jnp.dot(mid, same, preferred_element_type=f32)
                + jnp.dot(lo, same, preferred_element_type=f32))

    s_c, mk_c = [], []
    m = jnp.full((1, LN), NEG, f32)
    for c in range(CMP_PER_SEL):
        s = jnp.dot(kc_ref[0, c * n_sel:(c + 1) * n_sel, :], qbd, preferred_element_type=f32)
        mk = jrow * SEL_BLOCK + (c * CMP_STRIDE + CMP_BLOCK - 1) <= pos_l
        s = jnp.where(mk, s, NEG)
        m = jnp.maximum(m, jnp.max(s, axis=0, keepdims=True))
        s_c.append(s)
        mk_c.append(mk)
    e_c = [jnp.where(mk_c[c], jnp.exp2(s_c[c] - m), 0.0) for c in range(CMP_PER_SEL)]
    l = e_c[0].sum(axis=0, keepdims=True)
    for c in range(1, CMP_PER_SEL):
        l = l + e_c[c].sum(axis=0, keepdims=True)
    inv = 1.0 / jnp.maximum(l, TINY)
    o_cmp = jnp.zeros((HD, LN), f32)
    pg = []
    for c in range(CMP_PER_SEL):
        p = e_c[c] * inv
        o_cmp = o_cmp + pv(vc_ref[0, c * n_sel:(c + 1) * n_sel, :], p.astype(bf))
        pg.append(head_sum(p))

    cur = pos_l // SEL_BLOCK
    forced = (jrow == 0) | (jrow == cur) | (jrow == cur - 1)
    allowed = jrow * SEL_BLOCK <= pos_l
    last = pg[CMP_PER_SEL - 1]
    prev = jnp.where(jrow == 0, 0.0, pltpu.roll(last, 1, 0))
    inner = pg[0]
    for c in range(1, CMP_PER_SEL - 1):
        inner = inner + pg[c]
    sc_s[...] = jnp.where(forced, FORCE, jnp.where(allowed, 2.0 * inner + last + prev, -1.0))
    sel_s[...] = jnp.zeros((n_sel, LN), f32)

    def pick(_, carry):
        s = sc_s[...]
        top = jnp.max(s, axis=0, keepdims=True)
        first = jnp.min(jnp.where(s == top, jrow_f, float(n_sel)), axis=0, keepdims=True)
        hit = jrow_f == first
        sc_s[...] = jnp.where(hit, -jnp.inf, s)
        sel_s[...] = jnp.where(hit, 1.0, sel_s[...])
        return carry

    lax.fori_loop(0, SEL_TOPN, pick, 0)
    sel_s[...] = jnp.where(allowed, sel_s[...], 0.0)

    def reset():
        m_s[...] = jnp.full((1, LN), M_INIT, f32)
        l_s[...] = jnp.zeros((1, LN), f32)
        acc_s[...] = jnp.zeros((HD, LN), f32)

    def update(s_blocks, pv_fn):
        m_old = m_s[...]
        m_new = m_old
        for s in s_blocks:
            m_new = jnp.maximum(m_new, jnp.max(s, axis=0, keepdims=True))
        alpha = jnp.exp2(m_old - m_new)
        e_blocks = [jnp.exp2(s - m_new) for s in s_blocks]
        l_new = l_s[...] * alpha
        for e in e_blocks:
            l_new = l_new + e.sum(axis=0, keepdims=True)
        pt = e_blocks[0] if len(e_blocks) == 1 else jnp.concatenate(e_blocks, axis=0)
        acc_s[...] = acc_s[...] * alpha + pv_fn(pt.astype(bf))
        m_s[...] = m_new
        l_s[...] = l_new

    def finish():
        return acc_s[...] * (1.0 / jnp.maximum(l_s[...], TINY))

    def scores(kv):
        return jnp.dot(kv[:, 0:half].astype(bf), qbd, preferred_element_type=f32)

    reset()
    pages_per_tile = SAMPLE_TILE // PAGE_SIZE
    blk_per_page = PAGE_SIZE // SEL_BLOCK
    blk_per_tile = SAMPLE_TILE // SEL_BLOCK

    def slc_body(kt, carry):
        selb = sel_s[pl.ds(pl.multiple_of(kt * blk_per_tile, blk_per_tile), blk_per_tile), :]
        blocks = []
        for j in range(pages_per_tile):
            s = scores_t(buf[slot, kt * pages_per_tile + j, 0])
            for h in range(blk_per_page):
                i = j * blk_per_page + h
                blocks.append(jnp.where(selb[i:i + 1, :] > 0.5, s[h * SEL_BLOCK:(h + 1) * SEL_BLOCK, :], NEG))

        def pv_pages(pt):
            full = jnp.zeros((2 * HD, LN), f32)
            for j in range(pages_per_tile):
                full = full + jnp.dot(buf[slot, kt * pages_per_tile + j, 1].astype(bf),
                                      pt[j * PAGE_SIZE:(j + 1) * PAGE_SIZE, :], preferred_element_type=f32)
            return group_rows(full)

        update(blocks, pv_pages)
        return carry

    lax.fori_loop(0, past // SAMPLE_TILE, slc_body, 0)
    rows_new = lax.broadcasted_iota(jnp.int32, (SEL_BLOCK, 1), 0)
    kv = nslc_ref[0]
    keep = (sel_s[past // SEL_BLOCK:past // SEL_BLOCK + 1, :] > 0.5) & (past + rows_new <= pos_l)
    update([jnp.where(keep, scores(kv), NEG)], functools.partial(pv, kv[:, half:]))
    o_slc = finish()

    reset()
    wb = wcache_ref.shape[-1]
    rel = pos_l - (past - wb + lax.broadcasted_iota(jnp.int32, (wb, 1), 0))
    update([jnp.where((rel >= 0) & (rel <= WINDOW), scores_t(wcache_ref[0, 0, 0]), NEG)],
           lambda pt: group_rows(jnp.dot(wcache_ref[0, 0, 1].astype(bf), pt, preferred_element_type=f32)))
    kv = nwin_ref[0]
    rel = pos_l - (past + rows_new)
    update([jnp.where((rel >= 0) & (rel <= WINDOW) & (rows_new < t_new), scores(kv), NEG)],
           functools.partial(pv, kv[:, half:]))
    o_win = finish()

    gate = jax.nn.sigmoid(gate_ref[0])
    o_ref[0] = gate[0:1, :] * o_cmp + gate[1:2, :] * o_slc + gate[2:3, :] * o_win


def _pad_rows(a, rows):
    return jnp.pad(a, ((0, 0), (0, rows - a.shape[1]), (0, 0)))


def _nsa_sample(q, kv_cmp, kv_slc, kv_win, gates, pool_cmp, pool_slc, page_table, win_cache, layer, cmp_w):
    B, Tn = q.shape[:2]
    G, R, HD = NSA_KV_HEADS, NSA_R, NSA_HD
    n_pages = page_table.shape[1]
    n_pool = pool_cmp.shape[1]
    past = n_pages * PAGE_SIZE
    wb = win_cache.shape[2]
    assert Tn <= SQ and Tn <= SEL_BLOCK and past % SAMPLE_TILE == 0 and wb % SUBLANES == 0
    bf = jnp.bfloat16
    half = G * HD
    kvc = _compress_paged(page_table, _page_view(pool_cmp), layer, cmp_w)
    n_blk = past // CMP_STRIDE
    n_sel = _round_up(past // SEL_BLOCK + 1, BF16_SUBLANES)
    kvp = kvc.reshape(B, n_blk // CMP_PER_SEL, CMP_PER_SEL, NSA_KV_WIDTH).transpose(0, 2, 1, 3)
    kvp = jnp.pad(kvp, ((0, 0), (0, 0), (0, n_sel - n_blk // CMP_PER_SEL), (0, 0)))
    kvp = kvp.reshape(B, CMP_PER_SEL * n_sel, NSA_KV_WIDTH).astype(bf)
    kc, vc = kvp[..., :half], kvp[..., half:]
    qt = jnp.pad((q * QK_SCALE).reshape(B, Tn, G, R, HD), ((0, 0), (0, SQ - Tn), (0, 0), (0, 0), (0, 0)))
    qt = qt.transpose(0, 2, 4, 3, 1)
    qbd = jnp.einsum('bgdrq,gh->bgdhrq', qt, jnp.eye(G, dtype=jnp.float32)).reshape(B, half, SAMPLE_LANES).astype(bf)
    gate = jnp.pad(gates.reshape(B, Tn, G, R, 3), ((0, 0), (0, SQ - Tn), (0, 0), (0, 0), (0, 0)))
    gate = gate.transpose(0, 4, 2, 3, 1).reshape(B, 3, SAMPLE_LANES)
    nslc = _pad_rows(kv_slc.reshape(B, Tn, NSA_KV_WIDTH), SEL_BLOCK)
    nwin = _pad_rows(kv_win.reshape(B, Tn, NSA_KV_WIDTH), SEL_BLOCK)
    per_b = lambda shape: pl.BlockSpec((1,) + shape, lambda b, pt: (b,) + (0,) * len(shape))
    out = pl.pallas_call(
        functools.partial(_nsa_sample_kernel, page_base=layer * n_pool, n_pages=n_pages, n_sel=n_sel, t_new=Tn),
        grid_spec=pltpu.PrefetchScalarGridSpec(
            num_scalar_prefetch=1,
            grid=(B,),
            in_specs=[
                pl.BlockSpec(memory_space=pl.ANY),
                per_b((half, SAMPLE_LANES)), per_b((3, SAMPLE_LANES)),
                per_b((CMP_PER_SEL * n_sel, half)), per_b((CMP_PER_SEL * n_sel, half)),
                per_b((SEL_BLOCK, NSA_KV_WIDTH)),
                pl.BlockSpec((1, 1, 2, half, wb), lambda b, pt: (layer, b, 0, 0, 0)),
                per_b((SEL_BLOCK, NSA_KV_WIDTH)),
            ],
            out_specs=per_b((HD, SAMPLE_LANES)),
            scratch_shapes=[
                pltpu.VMEM((2, n_pages, 2, half, PAGE_SIZE), jnp.float32),
                pltpu.SemaphoreType.DMA((2,)),
                pltpu.VMEM((n_sel, SAMPLE_LANES), jnp.float32),
                pltpu.VMEM((n_sel, SAMPLE_LANES), jnp.float32),
                pltpu.VMEM((1, SAMPLE_LANES), jnp.float32),
                pltpu.VMEM((1, SAMPLE_LANES), jnp.float32),
                pltpu.VMEM((HD, SAMPLE_LANES), jnp.float32),
            ]),
        out_shape=jax.ShapeDtypeStruct((B, HD, SAMPLE_LANES), jnp.float32),
        compiler_params=pltpu.CompilerParams(
            dimension_semantics=("arbitrary",), vmem_limit_bytes=VMEM_LIMIT_BYTES),
        name="nsa_sample_attention",
    )(page_table, _page_view(pool_slc), qbd, gate, kc, vc, nslc,
      win_cache.transpose(0, 1, 3, 4, 5, 2).reshape(DEPTH, B, 2, half, wb), nwin)
    o = out.reshape(B, HD, G, R, SQ)[..., :Tn].transpose(0, 4, 2, 3, 1).reshape(B, Tn, NSA_WIDTH)
    win_all = jnp.concatenate([win_cache[layer], kv_win.astype(win_cache.dtype)], axis=1)
    return o, win_all[:, -min(WINDOW, wb + Tn):]


GLA_LANES = GLA_HEADS * GLA_DK
GLA_KERNEL_CHUNK = 32
GLA_BLOCK_ROWS = 512


def _head_block_mask(dtype):
    r = lax.broadcasted_iota(jnp.int32, (GLA_LANES, GLA_LANES), 0) // GLA_DK
    c = lax.broadcasted_iota(jnp.int32, (GLA_LANES, GLA_LANES), 1) // GLA_DK
    return (r == c).astype(dtype)


def _gla_kernel(q_ref, k_ref, v_ref, gg_ref, ga_ref, wa_ref, ba_ref, gn_ref, st0_ref, o_ref, st_ref,
                st_s, kp_s, bp_s, vp_s, *, c, n_chunks, valid_rows):
    f32, bf = jnp.float32, jnp.bfloat16

    @pl.when(pl.program_id(1) == 0)
    def _():
        st_s[...] = st0_ref[0]
        zeros = jnp.zeros((c, GLA_LANES), f32)
        kp_s[0:c, :] = zeros
        bp_s[0:c, :] = zeros
        vp_s[0:c, :] = zeros

    row = lax.broadcasted_iota(jnp.int32, (c, 1), 0)
    tril = (lax.broadcasted_iota(jnp.int32, (c, c), 0) >= lax.broadcasted_iota(jnp.int32, (c, c), 1)).astype(f32)
    ones_blk = _head_block_mask(bf)
    blk_f32 = _head_block_mask(f32)
    contract_last = (((1,), (1,)), ((), ()))
    contract_rows = (((0,), (0,)), ((), ()))

    def head_sum(x):
        hi = x.astype(bf)
        lo = (x - hi.astype(f32)).astype(bf)
        return (jnp.dot(hi, ones_blk, preferred_element_type=f32)
                + jnp.dot(lo, ones_blk, preferred_element_type=f32))

    def chunk(ch, carry):
        rows = pl.ds(pl.multiple_of(ch * c, c), c)
        q = q_ref[rows, :] * (GLA_DK ** -0.5)
        k = k_ref[rows, :]
        v = v_ref[rows, :]
        z = jnp.dot(ga_ref[rows, :].astype(bf), wa_ref[...], preferred_element_type=f32) + ba_ref[...]
        la = (jnp.minimum(z, 0.0) - jnp.log1p(jnp.exp(-jnp.abs(z)))) / GLA_GATE_TEMP
        if valid_rows < c:
            la = jnp.where(row < valid_rows, la, 0.0)
        b = jnp.dot(tril, la, preferred_element_type=f32, precision=lax.Precision.HIGHEST)
        st = st_s[...]
        o = lax.dot_general((q * jnp.exp(b)).astype(bf), st.astype(bf), contract_last,
                            preferred_element_type=f32)
        kp_s[c:2 * c, :] = k
        bp_s[c:2 * c, :] = b
        vp_s[c:2 * c, :] = v
        terms = [q * k]
        for d in range(1, c):
            ok = row >= d
            kr = kp_s[c - d:2 * c - d, :]
            br = bp_s[c - d:2 * c - d, :]
            terms.append(jnp.where(ok, q * kr * jnp.exp(jnp.where(ok, b - br, 0.0)), 0.0))
        att = jnp.dot(jnp.concatenate(terms, axis=0).astype(bf), ones_blk, preferred_element_type=f32)
        for d in range(c):
            vr = v if d == 0 else vp_s[c - d:2 * c - d, :]
            o = o + att[d * c:(d + 1) * c, :] * vr
        ms = head_sum(o * o) * (1.0 / GLA_DV)
        g = gg_ref[rows, :]
        o_ref[rows, :] = o * lax.rsqrt(ms + EPS) * gn_ref[...] * (g * jax.nn.sigmoid(g))
        b_last = b[c - 1:c, :]
        ke = k * jnp.exp(b_last - b)
        upd = lax.dot_general(v.astype(bf), ke.astype(bf), contract_rows, preferred_element_type=f32)
        st_s[...] = st * jnp.exp(b_last) + upd * blk_f32
        return carry

    lax.fori_loop(0, n_chunks, chunk, 0, unroll=min(2, n_chunks))
    st_ref[0] = st_s[...]


def _gla(proj, nbatch, tb, c, valid_rows, wa_pad, ba, gn, st0):
    m = proj.shape[0]
    nblk = m // nbatch // tb
    colblk = lambda p, w: PROJ_OFF[p] // w
    row_map = lambda j: (lambda b, i: (b * nblk + i, j))
    return pl.pallas_call(
        functools.partial(_gla_kernel, c=c, n_chunks=tb // c, valid_rows=valid_rows),
        grid=(nbatch, nblk),
        in_specs=[
            pl.BlockSpec((tb, GLA_LANES), row_map(colblk(0, GLA_LANES))),
            pl.BlockSpec((tb, GLA_LANES), row_map(colblk(1, GLA_LANES))),
            pl.BlockSpec((tb, GLA_LANES), row_map(colblk(2, GLA_LANES))),
            pl.BlockSpec((tb, GLA_LANES), row_map(colblk(3, GLA_LANES))),
            pl.BlockSpec((tb, LANES), row_map(colblk(4, LANES))),
            pl.BlockSpec((LANES, GLA_LANES), lambda b, i: (0, 0)),
            pl.BlockSpec((1, GLA_LANES), lambda b, i: (0, 0)),
            pl.BlockSpec((1, GLA_LANES), lambda b, i: (0, 0)),
            pl.BlockSpec((1, GLA_LANES, GLA_LANES), lambda b, i: (b, 0, 0)),
        ],
        out_specs=[pl.BlockSpec((tb, GLA_LANES), lambda b, i: (b * nblk + i, 0)),
                   pl.BlockSpec((1, GLA_LANES, GLA_LANES), lambda b, i: (b, 0, 0))],
        out_shape=[jax.ShapeDtypeStruct((m, GLA_LANES), jnp.float32),
                   jax.ShapeDtypeStruct((nbatch, GLA_LANES, GLA_LANES), jnp.float32)],
        scratch_shapes=[pltpu.VMEM((GLA_LANES, GLA_LANES), jnp.float32),
                        pltpu.VMEM((2 * c, GLA_LANES), jnp.float32),
                        pltpu.VMEM((2 * c, GLA_LANES), jnp.float32),
                        pltpu.VMEM((2 * c, GLA_LANES), jnp.float32)],
        compiler_params=pltpu.CompilerParams(
            dimension_semantics=("arbitrary", "arbitrary"), vmem_limit_bytes=VMEM_LIMIT_BYTES),
        name="gla_scan",
    )(proj, proj, proj, proj, proj, wa_pad, ba, gn, st0)


def _gla_state_in(s0):
    eye = jnp.eye(GLA_HEADS, dtype=jnp.float32)
    return jnp.einsum('bhde,hg->bhegd', s0.astype(jnp.float32), eye).reshape(-1, GLA_LANES, GLA_LANES)


def _gla_state_out(st):
    blocks = [st[:, h * GLA_DV:(h + 1) * GLA_DV, h * GLA_DK:(h + 1) * GLA_DK] for h in range(GLA_HEADS)]
    return jnp.stack(blocks, axis=1).transpose(0, 1, 3, 2)


def _causal_dwconv(u, hist, w):
    T = u.shape[1]
    up = jnp.concatenate([hist.astype(u.dtype), u], axis=1)
    y = w[0] * up[:, 0:T]
    for k in range(1, CONV_W):
        y = y + w[k] * up[:, k:k + T]
    return y, up[:, -(CONV_W - 1):]


def _nsa_prompt(q, kv_cmp, kv_slc, kv_win, gates, cmp_w):
    B, T = q.shape[:2]
    assert B == 1 and T % SLC_TILE == 0 and T >= WIN_TILE
    n_rows = T // CMP_STRIDE
    kvc = _compress(kv_cmp.reshape(n_rows, CMP_ROW), cmp_w, min(256, n_rows))
    o = _nsa_prompt_attention(q.reshape(T, NSA_WIDTH), gates.reshape(T, 3 * NSA_HEADS), kvc,
                              kv_slc.reshape(T, NSA_KV_WIDTH), kv_win.reshape(T, NSA_KV_WIDTH))
    return o.reshape(B, T, NSA_WIDTH), kv_win[:, -min(WINDOW, T):]


def _expand_rows(v, t):
    if v.shape[0] == 1:
        return v
    return jnp.repeat(v, t, axis=0)


def _trunk_layer(x, mod, lw, gla_s0, sc_hist, ffn_hist, nsa_apply, tm, final, g_final):
    B, T, _ = x.shape
    m = B * T
    grouped = B > 1
    ssh1, ssc1, sgt1, ssh2, ssc2, sgt2 = [_expand_rows(v, T) for v in jnp.split(mod, 6, axis=-1)]
    x2 = x.reshape(m, D_MODEL)
    proj = _in_proj(x2, lw['norm_mix'], ssc1, ssh1, lw['w_in'], tm).reshape(B, T, PROJ_WIDTH)
    gq, gk, gv, gg, ga, sb, scc, shh, nq, ncmp, nslc, nwin, ngate = [_proj_piece(proj, p) for p in range(13)]
    heads = lambda a, d: a.reshape(B, T, -1, d)
    if grouped:
        t_pad = _round_up(T, SUBLANES)
        gla_in = jnp.pad(proj, ((0, 0), (0, t_pad - T), (0, 0))).reshape(B * t_pad, PROJ_WIDTH)
        tb = chunk = t_pad
    else:
        t_pad, gla_in, tb, chunk = T, proj.reshape(m, PROJ_WIDTH), GLA_BLOCK_ROWS, GLA_KERNEL_CHUNK
    o_gla, st_gla = _gla(gla_in, B, tb, chunk, T if grouped else chunk,
                         lw['gla_wa'], lw['gla_ba'], lw['gla_norm'], _gla_state_in(gla_s0))
    o_gla = o_gla.reshape(B, t_pad, GLA_WIDTH)[:, :T]
    s_gla = _gla_state_out(st_gla)
    conv_out, sc_state = _causal_dwconv(scc * shh, sc_hist, lw['sc_conv'])
    o_sc = sb * conv_out
    kvr = lambda a: a.reshape(B, T, 2, NSA_KV_HEADS, NSA_HD)
    kv_cmp, kv_slc, kv_win = kvr(ncmp), kvr(nslc), kvr(nwin)
    o_nsa, win_state = nsa_apply(heads(nq, NSA_HD), kv_cmp, kv_slc, kv_win, heads(ngate, 3))
    mix = jnp.concatenate([o_gla, o_sc, o_nsa], axis=-1).reshape(m, D_MODEL)
    x2 = _out_proj(x2, mix, sgt1, lw['w_out'], tm)

    ff2 = 2 * D_FF
    if grouped:
        zero = jnp.zeros((B, 1, ff2), jnp.float32)
        h1 = jnp.concatenate([ffn_hist[:, 1:2], zero, zero, zero], axis=1).reshape(m, ff2)
        h2 = jnp.concatenate([ffn_hist[:, 0:1], ffn_hist[:, 1:2], zero, zero], axis=1).reshape(m, ff2)
        hist = (h1, h2)
    else:
        hist = jnp.concatenate(
            [jnp.zeros((HIST_ROWS - (CONV_W - 1), ff2), jnp.float32), ffn_hist[0]], axis=0)
    y, st = _ffn(x2, lw['norm_ffn'], ssc2, ssh2, sgt2, lw['ffn_up'], lw['ffn_conv'], lw['ffn_down'],
                 g_final, hist, tm, T if grouped else 0, final)
    if grouped:
        ffn_state = st.reshape(B, T, ff2)[:, -(CONV_W - 1):]
    else:
        ffn_state = st[None, -(CONV_W - 1):]
    return (y.reshape(B, T, D_MODEL), kv_cmp, kv_slc, win_state, s_gla.astype(gla_s0.dtype), sc_state, ffn_state)


def kernel(x_prompt, x_sample, cache_nsa_cmp, cache_nsa_slc, cache_nsa_win, state_gla, state_shortconv, state_ffn_conv, page_table, c_prompt, c_sample, mod_w, mod_b, norm_mix, norm_ffn, w_in, gla_wa2, gla_ba, gla_norm, sc_conv, nsa_cmp_pos, nsa_cmp_w1, nsa_cmp_w2, w_out, ffn_up, ffn_conv, ffn_down, norm_final):
    xp, xs = x_prompt, x_sample
    bp, bs = xp.shape[0], xs.shape[0]
    assert bp == 1 and xs.shape[1] == 4
    c_rows = _round_up(bp + bs, SUBLANES)
    c_all = jnp.concatenate([c_prompt, c_sample, jnp.zeros((c_rows - bp - bs, D_MODEL), jnp.float32)], axis=0)
    mod_all = _modulation(c_all, mod_w, mod_b)
    g_final = norm_final.reshape(1, D_MODEL)
    outs = [[] for _ in range(12)]
    for l in range(DEPTH):
        lw = dict(
            norm_mix=norm_mix[l].reshape(1, D_MODEL), norm_ffn=norm_ffn[l].reshape(1, D_MODEL),
            w_in=_pack_w_in(w_in[l]),
            gla_wa=jnp.zeros((LANES, GLA_LANES), jnp.bfloat16).at[:GLA_GATE_RANK].set(
                gla_wa2[l].astype(jnp.bfloat16)),
            gla_ba=gla_ba[l].reshape(1, GLA_LANES),
            gla_norm=jnp.tile(gla_norm[l], GLA_HEADS).reshape(1, GLA_LANES),
            sc_conv=sc_conv[l], w_out=w_out[l].astype(jnp.bfloat16),
            ffn_up=ffn_up[l].astype(jnp.bfloat16), ffn_conv=ffn_conv[l],
            ffn_down=ffn_down[l].astype(jnp.bfloat16))
        cmp_params = (nsa_cmp_pos[l], nsa_cmp_w1[l], nsa_cmp_w2[l])
        cmp_w = _compress_weights(*cmp_params)
        final = l == DEPTH - 1
        res_p = _trunk_layer(
            xp, mod_all[l, 0:bp], lw,
            jnp.zeros((bp, GLA_HEADS, GLA_DK, GLA_DV), xp.dtype),
            jnp.zeros((bp, CONV_W - 1, SC_WIDTH), xp.dtype),
            jnp.zeros((bp, CONV_W - 1, 2 * D_FF), xp.dtype),
            functools.partial(_nsa_prompt, cmp_w=cmp_w), 256, final, g_final)
        res_s = _trunk_layer(
            xs, mod_all[l, bp:bp + bs], lw, state_gla[l], state_shortconv[l], state_ffn_conv[l],
            functools.partial(_nsa_sample, pool_cmp=cache_nsa_cmp, pool_slc=cache_nsa_slc,
                              page_table=page_table, win_cache=cache_nsa_win, layer=l,
                              cmp_w=cmp_w), bs * xs.shape[1], final, g_final)
        xp, xs = res_p[0], res_s[0]
        for k in range(6):
            outs[2 * k].append(res_p[k + 1])
            outs[2 * k + 1].append(res_s[k + 1])
    return (xp, xs) + tuple(jnp.stack(o) for o in outs)
```

```python
import functools

import jax
import jax.numpy as jnp
from jax import lax
from jax.experimental import pallas as pl
from jax.experimental.pallas import tpu as pltpu

D_MODEL = 1024
DEPTH = 2
PAGE_SIZE = 128
GLA_HEADS = 4
GLA_DK = D_MODEL // 16
GLA_DV = D_MODEL // 16
GLA_WIDTH = GLA_HEADS * GLA_DV
GLA_GATE_RANK = 16
GLA_GATE_TEMP = 16.0
GLA_CHUNK = 64
SC_WIDTH = D_MODEL // 4
CONV_W = 3
NSA_HEADS = 8
NSA_KV_HEADS = 2
NSA_HD = D_MODEL // 16
NSA_WIDTH = NSA_HEADS * NSA_HD
NSA_KV_WIDTH = 2 * NSA_KV_HEADS * NSA_HD
CMP_STRIDE = 16
CMP_BLOCK = 2 * CMP_STRIDE
CMP_HIDDEN = 128
SEL_BLOCK = 64
SEL_TOPN = 16
WINDOW = 512
Q_BLOCK = 128
D_FF = 2816
EPS = 1e-6
NEG = -1e30
TINY = 1e-30
FORCE = 1e9

IN_SIZES = (
    GLA_HEADS * GLA_DK, GLA_HEADS * GLA_DK, GLA_WIDTH, GLA_WIDTH, GLA_GATE_RANK,
    SC_WIDTH, SC_WIDTH, SC_WIDTH,
    NSA_WIDTH, NSA_KV_WIDTH, NSA_KV_WIDTH, NSA_KV_WIDTH, NSA_HEADS * 3,
)

LANES = 128
SUBLANES = 8
BF16_SUBLANES = 16
VMEM_LIMIT_BYTES = 56 * 1024 * 1024

PROJ_ORDER = (0, 1, 2, 3, 8, 5, 6, 7, 9, 10, 11, 4, 12)


def _round_up(n, m):
    return -(-n // m) * m


def _proj_layout():
    src, acc = [], 0
    for s in IN_SIZES:
        src.append(acc)
        acc += s
    offs, dst = {}, 0
    for p in PROJ_ORDER:
        offs[p] = dst
        dst += _round_up(IN_SIZES[p], LANES)
    return src, offs, dst


PROJ_SRC, PROJ_OFF, PROJ_WIDTH = _proj_layout()


def _pack_w_in(w_in):
    out = jnp.zeros((D_MODEL, PROJ_WIDTH), jnp.bfloat16)
    for p in PROJ_ORDER:
        piece = w_in[:, PROJ_SRC[p]:PROJ_SRC[p] + IN_SIZES[p]].astype(jnp.bfloat16)
        out = lax.dynamic_update_slice(out, piece, (0, PROJ_OFF[p]))
    return out


def _proj_piece(proj, p):
    return proj[..., PROJ_OFF[p]:PROJ_OFF[p] + IN_SIZES[p]]


def _mod_kernel(c_ref, w_ref, b_ref, o_ref):
    c = c_ref[...]
    a = c * jax.nn.sigmoid(c)
    o_ref[0] = jnp.dot(a, w_ref[0], preferred_element_type=jnp.float32,
                       precision=lax.Precision.HIGHEST) + b_ref[0]


def _modulation(c_all, mod_w, mod_b):
    rows = c_all.shape[0]
    tn = 1024
    n = mod_w.shape[-1]
    return pl.pallas_call(
        _mod_kernel,
        grid=(DEPTH, n // tn),
        in_specs=[
            pl.BlockSpec((rows, D_MODEL), lambda l, j: (0, 0)),
            pl.BlockSpec((1, D_MODEL, tn), lambda l, j: (l, 0, j)),
            pl.BlockSpec((1, 1, tn), lambda l, j: (l, 0, j)),
        ],
        out_specs=pl.BlockSpec((1, rows, tn), lambda l, j: (l, 0, j)),
        out_shape=jax.ShapeDtypeStruct((DEPTH, rows, n), jnp.float32),
        name="adaln_modulation",
    )(c_all, mod_w, mod_b.reshape(DEPTH, 1, n))


def _norm_mod(x, g, sc, sh):
    r = lax.rsqrt(jnp.mean(x * x, axis=-1, keepdims=True) + EPS)
    return (x * r * g) * (1.0 + sc) + sh


def _in_proj_kernel(x_ref, g_ref, sc_ref, sh_ref, w_ref, o_ref):
    h = _norm_mod(x_ref[...], g_ref[...], sc_ref[...], sh_ref[...])
    o_ref[...] = jnp.dot(h.astype(jnp.bfloat16), w_ref[...], preferred_element_type=jnp.float32)


def _row_spec(tm, per_row):
    if per_row:
        return pl.BlockSpec((tm, D_MODEL), lambda i: (i, 0))
    return pl.BlockSpec((1, D_MODEL), lambda i: (0, 0))


def _resident(shape):
    return pl.BlockSpec(shape, lambda i: (0,) * len(shape), pipeline_mode=pl.Buffered(1))


def _in_proj(x, g, sc, sh, w_packed, tm):
    m = x.shape[0]
    per_row = sc.shape[0] != 1
    return pl.pallas_call(
        _in_proj_kernel,
        grid=(m // tm,),
        in_specs=[
            pl.BlockSpec((tm, D_MODEL), lambda i: (i, 0)),
            _resident((1, D_MODEL)),
            _row_spec(tm, per_row),
            _row_spec(tm, per_row),
            _resident((D_MODEL, PROJ_WIDTH)),
        ],
        out_specs=pl.BlockSpec((tm, PROJ_WIDTH), lambda i: (i, 0)),
        out_shape=jax.ShapeDtypeStruct((m, PROJ_WIDTH), jnp.float32),
        compiler_params=pltpu.CompilerParams(
            dimension_semantics=("arbitrary",), vmem_limit_bytes=VMEM_LIMIT_BYTES),
        name="norm_in_proj",
    )(x, g, sc, sh, w_packed)


HIST_ROWS = SUBLANES


def _out_proj_kernel(*refs, tm, group):
    grouped = group > 0
    if grouped:
        (x_ref, gla_ref, sb_ref, scc_ref, shh_ref, nsa_ref, gt_ref, w_ref, cw_ref, h1_ref, h2_ref,
         o_ref, st_ref, u_s) = refs
    else:
        (x_ref, gla_ref, sb_ref, scc_ref, shh_ref, nsa_ref, gt_ref, w_ref, cw_ref, h0_ref,
         o_ref, st_ref, u_s) = refs

        @pl.when(pl.program_id(0) == 0)
        def _():
            u_s[0:HIST_ROWS, :] = h0_ref[...]

    u = scc_ref[...] * shh_ref[...]
    u_s[HIST_ROWS:HIST_ROWS + tm, :] = u
    p1 = u_s[HIST_ROWS - 1:HIST_ROWS - 1 + tm, :]
    p2 = u_s[HIST_ROWS - 2:HIST_ROWS - 2 + tm, :]
    if grouped:
        t = lax.broadcasted_iota(jnp.int32, (tm, 1), 0) % group
        p1 = jnp.where(t == 0, h1_ref[...], p1)
        p2 = jnp.where(t <= 1, h2_ref[...], p2)
    o_sc = sb_ref[...] * (cw_ref[0:1, :] * p2 + cw_ref[1:2, :] * p1 + cw_ref[2:3, :] * u)
    mix = jnp.concatenate([gla_ref[...], o_sc, nsa_ref[...]], axis=1).astype(jnp.bfloat16)
    o_ref[...] = x_ref[...] + gt_ref[...] * jnp.dot(mix, w_ref[...], preferred_element_type=jnp.float32)
    if grouped:
        st_ref[...] = u
    else:
        tail = u_s[tm:tm + HIST_ROWS, :]
        st_ref[...] = tail
        u_s[0:HIST_ROWS, :] = tail


def _out_proj(x, o_gla, proj, o_nsa, gt, w_bf16, cw, hist, tm, group):
    m = x.shape[0]
    per_row = gt.shape[0] != 1
    grouped = group > 0
    sc_cols = lambda p: pl.BlockSpec((tm, SC_WIDTH), lambda i: (i, PROJ_OFF[p] // SC_WIDTH))
    in_specs = [
        pl.BlockSpec((tm, D_MODEL), lambda i: (i, 0)),
        pl.BlockSpec((tm, GLA_WIDTH), lambda i: (i, 0)),
        sc_cols(5), sc_cols(6), sc_cols(7),
        pl.BlockSpec((tm, NSA_WIDTH), lambda i: (i, 0)),
        _row_spec(tm, per_row),
        _resident((D_MODEL, D_MODEL)),
        _resident((CONV_W, SC_WIDTH)),
    ]
    if grouped:
        assert m == tm
        in_specs += [_resident((tm, SC_WIDTH)), _resident((tm, SC_WIDTH))]
        hist_args, st_rows = tuple(hist), tm
    else:
        in_specs += [_resident((HIST_ROWS, SC_WIDTH))]
        hist_args, st_rows = (hist,), HIST_ROWS
    return pl.pallas_call(
        functools.partial(_out_proj_kernel, tm=tm, group=group),
        grid=(m // tm,),
        in_specs=in_specs,
        out_specs=[pl.BlockSpec((tm, D_MODEL), lambda i: (i, 0)),
                   pl.BlockSpec((st_rows, SC_WIDTH), lambda i: (0, 0))],
        out_shape=[jax.ShapeDtypeStruct((m, D_MODEL), jnp.float32),
                   jax.ShapeDtypeStruct((st_rows, SC_WIDTH), jnp.float32)],
        scratch_shapes=[pltpu.VMEM((HIST_ROWS + tm, SC_WIDTH), jnp.float32)],
        compiler_params=pltpu.CompilerParams(
            dimension_semantics=("arbitrary",), vmem_limit_bytes=VMEM_LIMIT_BYTES),
        name="shortconv_out_proj",
    )(x, o_gla, proj, proj, proj, o_nsa, gt, w_bf16, cw, *hist_args)


FFN_UP_CHUNK = 512
FFN_ACT_CHUNK = 256


def _ffn_kernel(*refs, tm, group, final):
    grouped = group > 0
    if grouped:
        (x_ref, g_ref, sc_ref, sh_ref, gt_ref, wup_ref, cw_ref, wdn_ref, gf_ref,
         h1_ref, h2_ref, o_ref, st_ref, up_s) = refs
    else:
        (x_ref, g_ref, sc_ref, sh_ref, gt_ref, wup_ref, cw_ref, wdn_ref, gf_ref,
         h0_ref, o_ref, st_ref, up_s) = refs

        @pl.when(pl.program_id(0) == 0)
        def _():
            up_s[0:HIST_ROWS, :] = h0_ref[...]

    x = x_ref[...]
    h = _norm_mod(x, g_ref[...], sc_ref[...], sh_ref[...]).astype(jnp.bfloat16)
    for c in range(2 * D_FF // FFN_UP_CHUNK):
        cols = slice(c * FFN_UP_CHUNK, (c + 1) * FFN_UP_CHUNK)
        up_s[HIST_ROWS:HIST_ROWS + tm, cols] = jnp.dot(
            h, wup_ref[:, cols], preferred_element_type=jnp.float32)

    if grouped:
        t = lax.broadcasted_iota(jnp.int32, (tm, 1), 0) % group

    def conv(cols):
        cur = up_s[HIST_ROWS:HIST_ROWS + tm, cols]
        p1 = up_s[HIST_ROWS - 1:HIST_ROWS - 1 + tm, cols]
        p2 = up_s[HIST_ROWS - 2:HIST_ROWS - 2 + tm, cols]
        if grouped:
            p1 = jnp.where(t == 0, h1_ref[:, cols], p1)
            p2 = jnp.where(t <= 1, h2_ref[:, cols], p2)
        return cw_ref[0:1, cols] * p2 + cw_ref[1:2, cols] * p1 + cw_ref[2:3, cols] * cur

    acc = jnp.zeros((tm, D_MODEL), jnp.float32)
    for c in range(D_FF // FFN_ACT_CHUNK):
        a = conv(slice(c * FFN_ACT_CHUNK, (c + 1) * FFN_ACT_CHUNK))
        b = conv(slice(D_FF + c * FFN_ACT_CHUNK, D_FF + (c + 1) * FFN_ACT_CHUNK))
        act = (a * jax.nn.sigmoid(a) * b).astype(jnp.bfloat16)
        acc = acc + jnp.dot(act, wdn_ref[c * FFN_ACT_CHUNK:(c + 1) * FFN_ACT_CHUNK, :],
                            preferred_element_type=jnp.float32)
    y = x + gt_ref[...] * acc
    if final:
        r = lax.rsqrt(jnp.mean(y * y, axis=-1, keepdims=True) + EPS)
        y = y * r * gf_ref[...]
    o_ref[...] = y

    if grouped:
        st_ref[...] = up_s[HIST_ROWS:HIST_ROWS + tm, :]
    else:
        tail = up_s[tm:tm + HIST_ROWS, :]
        st_ref[...] = tail
        up_s[0:HIST_ROWS, :] = tail


def _ffn(x, g, sc, sh, gt, wup, cw, wdn, g_final, hist, tm, group, final):
    m = x.shape[0]
    grouped = group > 0
    per_row = sc.shape[0] != 1
    ff2 = 2 * D_FF
    in_specs = [
        pl.BlockSpec((tm, D_MODEL), lambda i: (i, 0)),
        _resident((1, D_MODEL)),
        _row_spec(tm, per_row), _row_spec(tm, per_row), _row_spec(tm, per_row),
        _resident((D_MODEL, ff2)),
        _resident((CONV_W, ff2)),
        _resident((D_FF, D_MODEL)),
        _resident((1, D_MODEL)),
    ]
    if grouped:
        assert m == tm
        in_specs += [_resident((tm, ff2)), _resident((tm, ff2))]
        hist_args = tuple(hist)
        st_rows = tm
    else:
        in_specs += [_resident((HIST_ROWS, ff2))]
        hist_args = (hist,)
        st_rows = HIST_ROWS
    return pl.pallas_call(
        functools.partial(_ffn_kernel, tm=tm, group=group, final=final),
        grid=(m // tm,),
        in_specs=in_specs,
        out_specs=[pl.BlockSpec((tm, D_MODEL), lambda i: (i, 0)),
                   pl.BlockSpec((st_rows, ff2), lambda i: (0, 0))],
        out_shape=[jax.ShapeDtypeStruct((m, D_MODEL), jnp.float32),
                   jax.ShapeDtypeStruct((st_rows, ff2), jnp.float32)],
        scratch_shapes=[pltpu.VMEM((HIST_ROWS + tm, ff2), jnp.float32)],
        compiler_params=pltpu.CompilerParams(
            dimension_semantics=("arbitrary",), vmem_limit_bytes=VMEM_LIMIT_BYTES),
        name="conv_ffn",
    )(x, g, sc, sh, gt, wup, cw, wdn, g_final, *hist_args)


CMP_ROW = CMP_STRIDE * NSA_KV_WIDTH
CMP_HID = 2 * NSA_KV_HEADS * CMP_HIDDEN


def _gelu_tanh(x):
    return 0.5 * x * (1.0 + jnp.tanh(0.7978845608028654 * (x + 0.044715 * (x * x * x))))


def _compress_kernel(x_ref, xn_ref, pos_ref, wl_ref, wt_ref, w2_ref, o_ref, tr_s, *, tm):
    bf = jnp.bfloat16
    f32 = jnp.float32
    x = x_ref[...].astype(bf)
    lead = jnp.dot(x, wl_ref[...], preferred_element_type=f32)
    tr_s[0:tm, :] = jnp.dot(x, wt_ref[...], preferred_element_type=f32)
    tr_s[tm:tm + SUBLANES, :] = jnp.dot(xn_ref[...].astype(bf), wt_ref[...], preferred_element_type=f32)
    bias = (jnp.dot(pos_ref[0].astype(bf), wl_ref[...], preferred_element_type=f32)
            + jnp.dot(pos_ref[1].astype(bf), wt_ref[...], preferred_element_type=f32))[0:1, :]
    hid = _gelu_tanh(lead + tr_s[1:tm + 1, :] + bias)
    o_ref[...] = jnp.dot(hid.astype(bf), w2_ref[...], preferred_element_type=f32)


def _compress_weights(pos_emb, w1, w2):
    eye = jnp.eye(2, dtype=jnp.float32)
    w1f = jnp.einsum('kldh,kK,gG->lkgdKGh', w1, eye, eye).reshape(CMP_BLOCK, NSA_KV_WIDTH, CMP_HID)
    w1f = w1f.astype(jnp.bfloat16)
    w2b = jnp.einsum('khd,kK,gG->kghKGd', w2, eye, eye).reshape(CMP_HID, NSA_KV_WIDTH).astype(jnp.bfloat16)
    posf = jnp.broadcast_to(pos_emb.transpose(1, 0, 2)[:, :, None, :], (CMP_BLOCK, 2, NSA_KV_HEADS, NSA_HD))
    posf = posf.reshape(CMP_BLOCK, NSA_KV_WIDTH)
    pos = jnp.zeros((CMP_BLOCK, SUBLANES, NSA_KV_WIDTH), jnp.float32).at[:, 0].set(posf)
    pos_rows = jnp.zeros((2, SUBLANES, CMP_ROW), jnp.float32).at[:, 0].set(posf.reshape(2, CMP_ROW))
    w1kv = jnp.einsum('kldh,gG->lkgdGh', w1, eye).reshape(
        CMP_BLOCK // 2, 2, 2, NSA_KV_WIDTH // 2, CMP_HID // 2).transpose(0, 2, 1, 3, 4).reshape(
        CMP_BLOCK // 2, 2, NSA_KV_WIDTH, CMP_HID // 2).astype(jnp.bfloat16)
    return dict(pos=pos, w1=w1kv, w2=w2b, pos_rows=pos_rows,
                wl=w1f[:CMP_STRIDE].reshape(CMP_ROW, CMP_HID), wt=w1f[CMP_STRIDE:].reshape(CMP_ROW, CMP_HID))


def _compress(x, cw, tm):
    pos, wl, wt, w2b = cw['pos_rows'], cw['wl'], cw['wt'], cw['w2']
    n = x.shape[0]
    nb8 = n // SUBLANES
    return pl.pallas_call(
        functools.partial(_compress_kernel, tm=tm),
        grid=(n // tm,),
        in_specs=[
            pl.BlockSpec((tm, CMP_ROW), lambda i: (i, 0)),
            pl.BlockSpec((SUBLANES, CMP_ROW), lambda i: (jnp.minimum((i + 1) * (tm // SUBLANES), nb8 - 1), 0)),
            _resident((2, SUBLANES, CMP_ROW)),
            _resident((CMP_ROW, CMP_HID)), _resident((CMP_ROW, CMP_HID)),
            _resident((CMP_HID, NSA_KV_WIDTH)),
        ],
        out_specs=pl.BlockSpec((tm, NSA_KV_WIDTH), lambda i: (i, 0)),
        out_shape=jax.ShapeDtypeStruct((n, NSA_KV_WIDTH), jnp.float32),
        scratch_shapes=[pltpu.VMEM((tm + SUBLANES, CMP_HID), jnp.float32)],
        compiler_params=pltpu.CompilerParams(
            dimension_semantics=("arbitrary",), vmem_limit_bytes=VMEM_LIMIT_BYTES),
        name="nsa_compress",
    )(x, x, pos, wl, wt, w2b)


NSA_R = NSA_HEADS // NSA_KV_HEADS
QL = NSA_R * Q_BLOCK
QLL = NSA_KV_HEADS * QL
SLC_TILE = 512
BLK_PER_TILE = SLC_TILE // SEL_BLOCK
WIN_TILE = WINDOW + Q_BLOCK
CMP_PER_SEL = SEL_BLOCK // CMP_STRIDE
M_INIT = -1e29
QK_SCALE = NSA_HD ** -0.5 * 1.4426950408889634
ONES_ROWS = BF16_SUBLANES


def _tile_lanes(v, reps):
    return jnp.concatenate([v] * reps, axis=1)


def _nsa_prompt_kernel(q_ref, gate_ref, kc_ref, vct_ref, kslc_ref, vtslc_ref, kwin_ref, vtwin_ref,
                       o_ref, sc_s, sel_s, m_s, l_s, acc_s, mt_s, al_s, s_s, pt_s, *, n_sel):
    f32, bf = jnp.float32, jnp.bfloat16
    G, HD = NSA_KV_HEADS, NSA_HD
    n = pl.program_id(0)
    qn = q_ref[...] * QK_SCALE
    tq = [qn[:, j * LANES:(j + 1) * LANES].T for j in range(NSA_WIDTH // LANES)]
    heads_per_t = LANES // HD
    head_t = lambda h: tq[h // heads_per_t][(h % heads_per_t) * HD:(h % heads_per_t + 1) * HD, :]
    zero = jnp.zeros((HD, Q_BLOCK), f32)
    qbd = jnp.concatenate(
        [jnp.concatenate([head_t(g * NSA_R + r) if g == gp else zero for g in range(G) for r in range(NSA_R)],
                         axis=1) for gp in range(G)], axis=0).astype(bf)
    lane = lax.broadcasted_iota(jnp.int32, (1, Q_BLOCK), 1)
    pos_q = n * Q_BLOCK + lane
    pos_l = _tile_lanes(pos_q, QLL // Q_BLOCK)
    jrow = lax.broadcasted_iota(jnp.int32, (n_sel, 1), 0)

    s_c, mk_c = [], []
    m = jnp.full((1, QLL), NEG, f32)
    for c in range(CMP_PER_SEL):
        s = jnp.dot(kc_ref[c * n_sel:(c + 1) * n_sel, :], qbd, preferred_element_type=f32)
        mk = jrow * SEL_BLOCK + (c * CMP_STRIDE + CMP_BLOCK - 1) <= pos_l
        s = jnp.where(mk, s, NEG)
        m = jnp.maximum(m, jnp.max(s, axis=0, keepdims=True))
        s_c.append(s)
        mk_c.append(mk)
    e_c = [jnp.where(mk_c[c], jnp.exp2(s_c[c] - m), 0.0) for c in range(CMP_PER_SEL)]
    l = e_c[0].sum(axis=0, keepdims=True)
    for c in range(1, CMP_PER_SEL):
        l = l + e_c[c].sum(axis=0, keepdims=True)
    inv = 1.0 / jnp.maximum(l, TINY)
    o_cmp = [jnp.zeros((HD, QL), f32) for _ in range(G)]
    pg = []
    for c in range(CMP_PER_SEL):
        p = e_c[c] * inv
        pb = p.astype(bf)
        for g in range(G):
            o_cmp[g] = o_cmp[g] + jnp.dot(vct_ref[g * HD:(g + 1) * HD, c * n_sel:(c + 1) * n_sel],
                                          pb[:, g * QL:(g + 1) * QL], preferred_element_type=f32)
        pg.append([sum(p[:, g * QL + r * Q_BLOCK:g * QL + (r + 1) * Q_BLOCK] for r in range(NSA_R))
                   for g in range(G)])
    o_cmp = jnp.concatenate(o_cmp, axis=1)

    cur = pos_q // SEL_BLOCK
    forced = (jrow == 0) | (jrow == cur) | (jrow == cur - 1)
    allowed = jrow * SEL_BLOCK <= pos_q
    jrow_f = jrow.astype(f32)
    for g in range(G):
        last = pg[CMP_PER_SEL - 1][g]
        prev = jnp.where(jrow == 0, 0.0, pltpu.roll(last, 1, 0))
        inner = pg[0][g]
        for c in range(1, CMP_PER_SEL - 1):
            inner = inner + pg[c][g]
        p_slc = 2.0 * inner + last + prev
        sc_s[g] = jnp.where(forced, FORCE, jnp.where(allowed, p_slc, -1.0))
        sel_s[g] = jnp.zeros((n_sel, Q_BLOCK), f32)

    def pick(_, carry):
        for g in range(G):
            s = sc_s[g]
            top = jnp.max(s, axis=0, keepdims=True)
            first = jnp.min(jnp.where(s == top, jrow_f, float(n_sel)), axis=0, keepdims=True)
            hit = jrow_f == first
            sc_s[g] = jnp.where(hit, -jnp.inf, s)
            sel_s[g] = jnp.where(hit, 1.0, sel_s[g])
        return carry

    lax.fori_loop(0, min(SEL_TOPN, n_sel), pick, 0)
    for g in range(G):
        sel_s[g] = jnp.where(allowed & (sel_s[g] > 0.5), 0.0, NEG)

    def reset():
        m_s[...] = jnp.full((1, QLL), M_INIT, f32)
        l_s[...] = jnp.zeros((1, QLL), f32)
        acc_s[...] = jnp.zeros((HD, QLL), f32)

    def pass1(slot, k_ref, start, rows, bias_fn):
        s = jnp.dot(k_ref[pl.ds(start, rows), :].astype(bf), qbd, preferred_element_type=f32)
        top = jnp.full((SUBLANES, QLL), NEG, f32)
        for i in range(rows // SEL_BLOCK):
            blk = slice(i * SEL_BLOCK, (i + 1) * SEL_BLOCK)
            sb = s[blk, :] + bias_fn(i)
            s_s[slot, blk, :] = sb
            for r in range(SEL_BLOCK // SUBLANES):
                top = jnp.maximum(top, sb[r * SUBLANES:(r + 1) * SUBLANES, :])
        m_old = m_s[...]
        m_new = jnp.maximum(m_old, jnp.max(top, axis=0, keepdims=True))
        mt_s[slot] = m_new
        al_s[slot] = jnp.exp2(m_old - m_new)
        m_s[...] = m_new

    def pass2(slot, vta_ref, start, rows):
        m_new = mt_s[slot]
        for i in range(rows // SEL_BLOCK):
            blk = slice(i * SEL_BLOCK, (i + 1) * SEL_BLOCK)
            pt_s[slot, blk, :] = jnp.exp2(s_s[slot, blk, :] - m_new).astype(bf)
        pv, psum = [], []
        for g in range(G):
            r = jnp.dot(vta_ref[g, :, pl.ds(start, rows)], pt_s[slot, 0:rows, g * QL:(g + 1) * QL],
                        preferred_element_type=f32)
            pv.append(r[0:HD, :])
            psum.append(r[HD:HD + 1, :])
        alpha = al_s[slot]
        acc_s[...] = acc_s[...] * alpha + jnp.concatenate(pv, axis=1)
        l_s[...] = l_s[...] * alpha + jnp.concatenate(psum, axis=1)

    def finish():
        return acc_s[...] * (1.0 / jnp.maximum(l_s[...], TINY))

    def tile_start(kt):
        return pl.multiple_of(kt * SLC_TILE, SLC_TILE)

    def slc_pass1(slot, kt, causal):
        start = tile_start(kt)
        selb = [sel_s[g, pl.ds(pl.multiple_of(kt * BLK_PER_TILE, BLK_PER_TILE), BLK_PER_TILE), :]
                for g in range(G)]

        def bias_fn(i):
            row = jnp.concatenate([_tile_lanes(selb[g][i:i + 1, :], NSA_R) for g in range(G)], axis=1)
            if not causal:
                return row
            tok = start + i * SEL_BLOCK + lax.broadcasted_iota(jnp.int32, (SEL_BLOCK, 1), 0)
            return jnp.where(tok <= pos_l, row, NEG)

        pass1(slot, kslc_ref, start, SLC_TILE, bias_fn)

    def slc_pass2(slot, kt):
        pass2(slot, vtslc_ref, tile_start(kt), SLC_TILE)

    reset()
    diag = (n * Q_BLOCK) // SLC_TILE
    slc_pass1(0, diag, True)

    def slc_pair(j, carry):
        slc_pass1(1, 2 * j, False)
        slc_pass2(0, jnp.where(j == 0, diag, 2 * j - 1))
        slc_pass1(0, 2 * j + 1, False)
        slc_pass2(1, 2 * j)
        return carry

    pairs = diag // 2
    lax.fori_loop(0, pairs, slc_pair, 0)
    pending = jnp.where(pairs == 0, diag, 2 * pairs - 1)

    @pl.when(diag % 2 == 1)
    def _():
        slc_pass1(1, diag - 1, False)
        slc_pass2(0, pending)
        slc_pass2(1, diag - 1)

    @pl.when(diag % 2 == 0)
    def _():
        slc_pass2(0, pending)

    o_slc = finish()

    reset()
    wstart = pl.multiple_of(jnp.maximum(n * Q_BLOCK - WINDOW, 0), Q_BLOCK)
    rel = pos_q - (wstart + lax.broadcasted_iota(jnp.int32, (WIN_TILE, 1), 0))
    wbias = jnp.where((rel >= 0) & (rel <= WINDOW), 0.0, NEG)
    pass1(0, kwin_ref, wstart, WIN_TILE,
          lambda i: _tile_lanes(wbias[i * SEL_BLOCK:(i + 1) * SEL_BLOCK, :], QLL // Q_BLOCK))
    pass2(0, vtwin_ref, wstart, WIN_TILE)
    o_win = finish()

    gt = jax.nn.sigmoid(gate_ref[...].T)
    gate = lambda c: jnp.concatenate([gt[h * 3 + c:h * 3 + c + 1, :] for h in range(NSA_HEADS)], axis=1)
    o = gate(0) * o_cmp + gate(1) * o_slc + gate(2) * o_win
    o_ref[...] = jnp.concatenate(
        [jnp.concatenate([o[:, (heads_per_t * j + t) * Q_BLOCK:(heads_per_t * j + t + 1) * Q_BLOCK]
                          for t in range(heads_per_t)], axis=0).T for j in range(NSA_WIDTH // LANES)], axis=1)


def _nsa_prompt_attention(proj, kvc, nslc, nwin):
    T = proj.shape[0]
    nb, n_sel = T // Q_BLOCK, T // SEL_BLOCK
    G, R, HD = NSA_KV_HEADS, NSA_R, NSA_HD
    bf = jnp.bfloat16
    half = G * HD
    assert PROJ_OFF[8] % NSA_WIDTH == 0 and PROJ_OFF[12] % LANES == 0
    kvp = kvc.reshape(n_sel, CMP_PER_SEL, NSA_KV_WIDTH).transpose(1, 0, 2).reshape(T // CMP_STRIDE, NSA_KV_WIDTH)
    kc, vct = kvp[:, :half].astype(bf), kvp[:, half:].T.astype(bf)
    def vt_ones(v):
        vt = v.T.reshape(G, HD, T)
        return jnp.concatenate([vt, jnp.ones((G, ONES_ROWS, T), vt.dtype)], axis=1).astype(bf)
    vtslc, vtwin = vt_ones(nslc[:, half:]), vt_ones(nwin[:, half:])
    k_cols = lambda p: pl.BlockSpec((T, half), lambda i: (0, PROJ_OFF[p] // half), pipeline_mode=pl.Buffered(1))
    out = pl.pallas_call(
        functools.partial(_nsa_prompt_kernel, n_sel=n_sel),
        grid=(nb,),
        in_specs=[
            pl.BlockSpec((Q_BLOCK, NSA_WIDTH), lambda i: (i, PROJ_OFF[8] // NSA_WIDTH)),
            pl.BlockSpec((Q_BLOCK, LANES), lambda i: (i, PROJ_OFF[12] // LANES)),
            _resident((T // CMP_STRIDE, half)), _resident((half, T // CMP_STRIDE)),
            k_cols(10), _resident((G, HD + ONES_ROWS, T)),
            k_cols(11), _resident((G, HD + ONES_ROWS, T)),
        ],
        out_specs=pl.BlockSpec((Q_BLOCK, NSA_WIDTH), lambda i: (i, 0)),
        out_shape=jax.ShapeDtypeStruct((T, NSA_WIDTH), jnp.float32),
        scratch_shapes=[
            pltpu.VMEM((G, n_sel, Q_BLOCK), jnp.float32),
            pltpu.VMEM((G, n_sel, Q_BLOCK), jnp.float32),
            pltpu.VMEM((1, QLL), jnp.float32),
            pltpu.VMEM((1, QLL), jnp.float32),
            pltpu.VMEM((HD, QLL), jnp.float32),
            pltpu.VMEM((2, 1, QLL), jnp.float32),
            pltpu.VMEM((2, 1, QLL), jnp.float32),
            pltpu.VMEM((2, WIN_TILE, QLL), jnp.float32),
            pltpu.VMEM((2, WIN_TILE, QLL), jnp.bfloat16),
        ],
        compiler_params=pltpu.CompilerParams(
            dimension_semantics=("arbitrary",), vmem_limit_bytes=VMEM_LIMIT_BYTES),
        name="nsa_prompt_attention",
    )(proj, proj, kc, vct, proj, vtslc, proj, vtwin)
    return out


PAGE_ROWS = PAGE_SIZE // CMP_STRIDE
SQ = 16
SAMPLE_LANES = NSA_HEADS * SQ
HALF_PAGES = 64
SAMPLE_TILE = 2048


def _page_view(pool):
    d, n_pool = pool.shape[:2]
    return pool.transpose(0, 1, 3, 4, 5, 2).reshape(d * n_pool, 2, NSA_KV_HEADS * NSA_HD, pool.shape[2])


def _page_copy(pool_ref, buf_ref, sem_ref, page, slot, idx):
    return pltpu.make_async_copy(pool_ref.at[page], buf_ref.at[slot, idx], sem_ref.at[slot])


def _gather_schedule(issue_fn, wait_fn):
    s = pl.program_id(0)

    @pl.when(s == 0)
    def _():
        issue_fn(s, 0)

    @pl.when(s + 1 < pl.num_programs(0))
    def _():
        issue_fn(s + 1, (s + 1) % 2)

    wait_fn(s, s % 2)


def _compress_paged_kernel(pt_ref, pool_ref, pos_ref, w1_ref, w2_ref, o_ref, buf, sem, tok_s,
                           *, page_base, n_pages):
    bf, f32 = jnp.bfloat16, jnp.float32
    rows = HALF_PAGES * PAGE_ROWS

    def copies(step, slot, fn):
        b, half = step // 2, step % 2

        def body(i, c):
            p = jnp.minimum(half * HALF_PAGES + i, n_pages - 1)
            fn(_page_copy(pool_ref, buf, sem, page_base + pt_ref[b, p], slot, i))
            return c

        lax.fori_loop(0, HALF_PAGES + 1, body, 0)

    _gather_schedule(lambda s, slot: copies(s, slot, lambda cp: cp.start()),
                     lambda s, slot: copies(s, slot, lambda cp: cp.wait()))
    slot = pl.program_id(0) % 2

    half = NSA_KV_HEADS * NSA_HD

    def to_token_rows(i, c):
        dst = pl.ds(pl.multiple_of(i * PAGE_SIZE, PAGE_SIZE), PAGE_SIZE)
        for kv in range(2):
            tok_s[kv, dst, :] = buf[slot, i, kv].T
        return c

    lax.fori_loop(0, HALF_PAGES + 1, to_token_rows, 0, unroll=5)
    acc = [jnp.zeros((rows, CMP_HID // 2), f32) for _ in range(2)]
    bias = [jnp.zeros((SUBLANES, CMP_HID // 2), f32) for _ in range(2)]
    for l in range(0, CMP_BLOCK, 2):
        for kv in range(2):
            x = jnp.concatenate([tok_s.at[kv][pl.ds(l + d, rows, stride=CMP_STRIDE), :] for d in range(2)],
                                axis=1).astype(bf)
            w = w1_ref[l // 2, kv]
            p = jnp.concatenate([pos_ref[l + d, :, kv * half:(kv + 1) * half] for d in range(2)], axis=1)
            acc[kv] = acc[kv] + jnp.dot(x, w, preferred_element_type=f32)
            bias[kv] = bias[kv] + jnp.dot(p.astype(bf), w, preferred_element_type=f32)
    hid = _gelu_tanh(jnp.concatenate(acc, axis=1) + jnp.concatenate(bias, axis=1)[0:1, :])
    o_ref[...] = jnp.dot(hid.astype(bf), w2_ref[...], preferred_element_type=f32)


def _compress_paged(page_table, pool, layer, cw):
    nbatch, n_pages = page_table.shape
    assert n_pages == 2 * HALF_PAGES
    rows = HALF_PAGES * PAGE_ROWS
    const = lambda shape: pl.BlockSpec(shape, lambda s, pt: (0,) * len(shape), pipeline_mode=pl.Buffered(1))
    return pl.pallas_call(
        functools.partial(_compress_paged_kernel, page_base=layer * (pool.shape[0] // DEPTH), n_pages=n_pages),
        grid_spec=pltpu.PrefetchScalarGridSpec(
            num_scalar_prefetch=1,
            grid=(2 * nbatch,),
            in_specs=[
                pl.BlockSpec(memory_space=pl.ANY),
                const((CMP_BLOCK, SUBLANES, NSA_KV_WIDTH)),
                const((CMP_BLOCK // 2, 2, NSA_KV_WIDTH, CMP_HID // 2)),
                const((CMP_HID, NSA_KV_WIDTH)),
            ],
            out_specs=pl.BlockSpec((rows, NSA_KV_WIDTH), lambda s, pt: (s, 0)),
            scratch_shapes=[
                pltpu.VMEM((2, HALF_PAGES + 1) + pool.shape[1:], jnp.float32),
                pltpu.SemaphoreType.DMA((2,)),
                pltpu.VMEM((2, (HALF_PAGES + 1) * PAGE_SIZE, NSA_KV_WIDTH // 2), jnp.float32),
            ]),
        out_shape=jax.ShapeDtypeStruct((2 * nbatch * rows, NSA_KV_WIDTH), jnp.float32),
        compiler_params=pltpu.CompilerParams(
            dimension_semantics=("arbitrary",), vmem_limit_bytes=VMEM_LIMIT_BYTES),
        name="nsa_compress_paged",
    )(page_table, pool, cw['pos'], cw['w1'], cw['w2'])


def _nsa_sample_kernel(pt_ref, pool_ref, qbd_ref, gate_ref, kc_ref, vc_ref, nslc_ref, wcache_ref, nwin_ref,
                       o_ref, buf, sem, sc_s, sel_s, m_s, l_s, acc_s, *, page_base, n_pages, n_sel, t_new):
    f32, bf = jnp.float32, jnp.bfloat16
    HD, LN = NSA_HD, SAMPLE_LANES
    half = NSA_KV_HEADS * HD
    past = n_pages * PAGE_SIZE
    contract_rows = (((0,), (0,)), ((), ()))

    def copies(step, slot, fn):
        def body(p, c):
            fn(_page_copy(pool_ref, buf, sem, page_base + pt_ref[step, p], slot, p))
            return c

        lax.fori_loop(0, n_pages, body, 0)

    _gather_schedule(lambda s, slot: copies(s, slot, lambda cp: cp.start()),
                     lambda s, slot: copies(s, slot, lambda cp: cp.wait()))
    slot = pl.program_id(0) % 2

    qbd = qbd_ref[0]
    lane = lax.broadcasted_iota(jnp.int32, (1, LN), 1)
    pos_l = past + lane % SQ
    group0 = lane < LN // NSA_KV_HEADS
    jrow = lax.broadcasted_iota(jnp.int32, (n_sel, 1), 0)
    jrow_f = jrow.astype(f32)

    def group_rows(full):
        return jnp.where(group0, full[0:HD, :], full[HD:2 * HD, :])

    def pv(v, pt):
        return group_rows(lax.dot_general(v.astype(bf), pt, contract_rows, preferred_element_type=f32))

    def scores_t(kt):
        return lax.dot_general(kt.astype(bf), qbd, contract_rows, preferred_element_type=f32)

    li = lax.broadcasted_iota(jnp.int32, (LN, LN), 0)
    lj = lax.broadcasted_iota(jnp.int32, (LN, LN), 1)
    same = ((li // (NSA_R * SQ) == lj // (NSA_R * SQ)) & (li % SQ == lj % SQ)).astype(bf)

    def head_sum(p):
        hi = p.astype(bf)
        r1 = p - hi.astype(f32)
        mid = r1.astype(bf)
        lo = (r1 - mid.astype(f32)).astype(bf)
        return (jnp.dot(hi, same, preferred_element_type=f32) + jnp.dot(mid, same, preferred_element_type=f32)
                + jnp.dot(lo, same, preferred_element_type=f32))

    s_c, mk_c = [], []
    m = jnp.full((1, LN), NEG, f32)
    for c in range(CMP_PER_SEL):
        s = jnp.dot(kc_ref[0, c * n_sel:(c + 1) * n_sel, :], qbd, preferred_element_type=f32)
        mk = jrow * SEL_BLOCK + (c * CMP_STRIDE + CMP_BLOCK - 1) <= pos_l
        s = jnp.where(mk, s, NEG)
        m = jnp.maximum(m, jnp.max(s, axis=0, keepdims=True))
        s_c.append(s)
        mk_c.append(mk)
    e_c = [jnp.where(mk_c[c], jnp.exp2(s_c[c] - m), 0.0) for c in range(CMP_PER_SEL)]
    l = e_c[0].sum(axis=0, keepdims=True)
    for c in range(1, CMP_PER_SEL):
        l = l + e_c[c].sum(axis=0, keepdims=True)
    inv = 1.0 / jnp.maximum(l, TINY)
    o_cmp = jnp.zeros((HD, LN), f32)
    pg = []
    for c in range(CMP_PER_SEL):
        p = e_c[c] * inv
        o_cmp = o_cmp + pv(vc_ref[0, c * n_sel:(c + 1) * n_sel, :], p.astype(bf))
        pg.append(head_sum(p))

    cur = pos_l // SEL_BLOCK
    forced = (jrow == 0) | (jrow == cur) | (jrow == cur - 1)
    allowed = jrow * SEL_BLOCK <= pos_l
    last = pg[CMP_PER_SEL - 1]
    prev = jnp.where(jrow == 0, 0.0, pltpu.roll(last, 1, 0))
    inner = pg[0]
    for c in range(1, CMP_PER_SEL - 1):
        inner = inner + pg[c]
    sc_s[...] = jnp.where(forced, FORCE, jnp.where(allowed, 2.0 * inner + last + prev, -1.0))
    sel_s[...] = jnp.zeros((n_sel, LN), f32)

    def pick(_, carry):
        s = sc_s[...]
        top = jnp.max(s, axis=0, keepdims=True)
        first = jnp.min(jnp.where(s == top, jrow_f, float(n_sel)), axis=0, keepdims=True)
        hit = jrow_f == first
        sc_s[...] = jnp.where(hit, -jnp.inf, s)
        sel_s[...] = jnp.where(hit, 1.0, sel_s[...])
        return carry

    lax.fori_loop(0, SEL_TOPN, pick, 0)
    sel_s[...] = jnp.where(allowed, sel_s[...], 0.0)

    def reset():
        m_s[...] = jnp.full((1, LN), M_INIT, f32)
        l_s[...] = jnp.zeros((1, LN), f32)
        acc_s[...] = jnp.zeros((HD, LN), f32)

    def update(s_blocks, pv_fn):
        m_old = m_s[...]
        m_new = m_old
        for s in s_blocks:
            m_new = jnp.maximum(m_new, jnp.max(s, axis=0, keepdims=True))
        alpha = jnp.exp2(m_old - m_new)
        e_blocks = [jnp.exp2(s - m_new) for s in s_blocks]
        l_new = l_s[...] * alpha
        for e in e_blocks:
            l_new = l_new + e.sum(axis=0, keepdims=True)
        pt = e_blocks[0] if len(e_blocks) == 1 else jnp.concatenate(e_blocks, axis=0)
        acc_s[...] = acc_s[...] * alpha + pv_fn(pt.astype(bf))
        m_s[...] = m_new
        l_s[...] = l_new

    def finish():
        return acc_s[...] * (1.0 / jnp.maximum(l_s[...], TINY))

    def scores(kv):
        return jnp.dot(kv[:, 0:half].astype(bf), qbd, preferred_element_type=f32)

    reset()
    pages_per_tile = SAMPLE_TILE // PAGE_SIZE
    blk_per_page = PAGE_SIZE // SEL_BLOCK
    blk_per_tile = SAMPLE_TILE // SEL_BLOCK

    def slc_body(kt, carry):
        selb = sel_s[pl.ds(pl.multiple_of(kt * blk_per_tile, blk_per_tile), blk_per_tile), :]
        blocks = []
        for j in range(pages_per_tile):
            s = scores_t(buf[slot, kt * pages_per_tile + j, 0])
            for h in range(blk_per_page):
                i = j * blk_per_page + h
                blocks.append(jnp.where(selb[i:i + 1, :] > 0.5, s[h * SEL_BLOCK:(h + 1) * SEL_BLOCK, :], NEG))

        def pv_pages(pt):
            full = jnp.zeros((2 * HD, LN), f32)
            for j in range(pages_per_tile):
                full = full + jnp.dot(buf[slot, kt * pages_per_tile + j, 1].astype(bf),
                                      pt[j * PAGE_SIZE:(j + 1) * PAGE_SIZE, :], preferred_element_type=f32)
            return group_rows(full)

        update(blocks, pv_pages)
        return carry

    lax.fori_loop(0, past // SAMPLE_TILE, slc_body, 0)
    rows_new = lax.broadcasted_iota(jnp.int32, (SEL_BLOCK, 1), 0)
    kv = nslc_ref[0]
    keep = (sel_s[past // SEL_BLOCK:past // SEL_BLOCK + 1, :] > 0.5) & (past + rows_new <= pos_l)
    update([jnp.where(keep, scores(kv), NEG)], functools.partial(pv, kv[:, half:]))
    o_slc = finish()

    reset()
    wb = wcache_ref.shape[-1]
    rel = pos_l - (past - wb + lax.broadcasted_iota(jnp.int32, (wb, 1), 0))
    update([jnp.where((rel >= 0) & (rel <= WINDOW), scores_t(wcache_ref[0, 0, 0]), NEG)],
           lambda pt: group_rows(jnp.dot(wcache_ref[0, 0, 1].astype(bf), pt, preferred_element_type=f32)))
    kv = nwin_ref[0]
    rel = pos_l - (past + rows_new)
    update([jnp.where((rel >= 0) & (rel <= WINDOW) & (rows_new < t_new), scores(kv), NEG)],
           functools.partial(pv, kv[:, half:]))
    o_win = finish()

    gate = jax.nn.sigmoid(gate_ref[0])
    o_ref[0] = gate[0:1, :] * o_cmp + gate[1:2, :] * o_slc + gate[2:3, :] * o_win


def _pad_rows(a, rows):
    return jnp.pad(a, ((0, 0), (0, rows - a.shape[1]), (0, 0)))


def _nsa_sample(proj, q, kv_cmp, kv_slc, kv_win, gates, pool_cmp, pool_slc, page_table, win_cache, layer, cmp_w):
    B, Tn = q.shape[:2]
    G, R, HD = NSA_KV_HEADS, NSA_R, NSA_HD
    n_pages = page_table.shape[1]
    n_pool = pool_cmp.shape[1]
    past = n_pages * PAGE_SIZE
    wb = win_cache.shape[2]
    assert Tn <= SQ and Tn <= SEL_BLOCK and past % SAMPLE_TILE == 0 and wb % SUBLANES == 0
    bf = jnp.bfloat16
    half = G * HD
    kvc = _compress_paged(page_table, _page_view(pool_cmp), layer, cmp_w)
    n_blk = past // CMP_STRIDE
    n_sel = _round_up(past // SEL_BLOCK + 1, BF16_SUBLANES)
    kvp = kvc.reshape(B, n_blk // CMP_PER_SEL, CMP_PER_SEL, NSA_KV_WIDTH).transpose(0, 2, 1, 3)
    kvp = jnp.pad(kvp, ((0, 0), (0, 0), (0, n_sel - n_blk // CMP_PER_SEL), (0, 0)))
    kvp = kvp.reshape(B, CMP_PER_SEL * n_sel, NSA_KV_WIDTH).astype(bf)
    kc, vc = kvp[..., :half], kvp[..., half:]
    qt = jnp.pad((q * QK_SCALE).reshape(B, Tn, G, R, HD), ((0, 0), (0, SQ - Tn), (0, 0), (0, 0), (0, 0)))
    qt = qt.transpose(0, 2, 4, 3, 1)
    qbd = jnp.einsum('bgdrq,gh->bgdhrq', qt, jnp.eye(G, dtype=jnp.float32)).reshape(B, half, SAMPLE_LANES).astype(bf)
    gate = jnp.pad(gates.reshape(B, Tn, G, R, 3), ((0, 0), (0, SQ - Tn), (0, 0), (0, 0), (0, 0)))
    gate = gate.transpose(0, 4, 2, 3, 1).reshape(B, 3, SAMPLE_LANES)
    nslc = _pad_rows(kv_slc.reshape(B, Tn, NSA_KV_WIDTH), SEL_BLOCK)
    nwin = _pad_rows(kv_win.reshape(B, Tn, NSA_KV_WIDTH), SEL_BLOCK)
    per_b = lambda shape: pl.BlockSpec((1,) + shape, lambda b, pt: (b,) + (0,) * len(shape))
    out = pl.pallas_call(
        functools.partial(_nsa_sample_kernel, page_base=layer * n_pool, n_pages=n_pages, n_sel=n_sel, t_new=Tn),
        grid_spec=pltpu.PrefetchScalarGridSpec(
            num_scalar_prefetch=1,
            grid=(B,),
            in_specs=[
                pl.BlockSpec(memory_space=pl.ANY),
                per_b((half, SAMPLE_LANES)), per_b((3, SAMPLE_LANES)),
                per_b((CMP_PER_SEL * n_sel, half)), per_b((CMP_PER_SEL * n_sel, half)),
                per_b((SEL_BLOCK, NSA_KV_WIDTH)),
                pl.BlockSpec((1, 1, 2, half, wb), lambda b, pt: (layer, b, 0, 0, 0)),
                per_b((SEL_BLOCK, NSA_KV_WIDTH)),
            ],
            out_specs=per_b((HD, SAMPLE_LANES)),
            scratch_shapes=[
                pltpu.VMEM((2, n_pages, 2, half, PAGE_SIZE), jnp.float32),
                pltpu.SemaphoreType.DMA((2,)),
                pltpu.VMEM((n_sel, SAMPLE_LANES), jnp.float32),
                pltpu.VMEM((n_sel, SAMPLE_LANES), jnp.float32),
                pltpu.VMEM((1, SAMPLE_LANES), jnp.float32),
                pltpu.VMEM((1, SAMPLE_LANES), jnp.float32),
                pltpu.VMEM((HD, SAMPLE_LANES), jnp.float32),
            ]),
        out_shape=jax.ShapeDtypeStruct((B, HD, SAMPLE_LANES), jnp.float32),
        compiler_params=pltpu.CompilerParams(
            dimension_semantics=("arbitrary",), vmem_limit_bytes=VMEM_LIMIT_BYTES),
        name="nsa_sample_attention",
    )(page_table, _page_view(pool_slc), qbd, gate, kc, vc, nslc,
      win_cache.transpose(0, 1, 3, 4, 5, 2).reshape(DEPTH, B, 2, half, wb), nwin)
    o = out.reshape(B, HD, G, R, SQ)[..., :Tn].transpose(0, 4, 2, 3, 1).reshape(B, Tn, NSA_WIDTH)
    win_all = jnp.concatenate([win_cache[layer], kv_win.astype(win_cache.dtype)], axis=1)
    return o, win_all[:, -min(WINDOW, wb + Tn):]


GLA_LANES = GLA_HEADS * GLA_DK
GLA_KERNEL_CHUNK = 32
GLA_BLOCK_ROWS = 512


def _head_block_mask(dtype):
    r = lax.broadcasted_iota(jnp.int32, (GLA_LANES, GLA_LANES), 0) // GLA_DK
    c = lax.broadcasted_iota(jnp.int32, (GLA_LANES, GLA_LANES), 1) // GLA_DK
    return (r == c).astype(dtype)


def _gla_kernel(q_ref, k_ref, v_ref, gg_ref, ga_ref, wa_ref, ba_ref, gn_ref, st0_ref, o_ref, st_ref,
                st_s, kp_s, bp_s, vp_s, *, c, n_chunks, valid_rows):
    f32, bf = jnp.float32, jnp.bfloat16

    @pl.when(pl.program_id(1) == 0)
    def _():
        st_s[...] = st0_ref[0]
        zeros = jnp.zeros((c, GLA_LANES), f32)
        kp_s[0:c, :] = zeros
        bp_s[0:c, :] = zeros
        vp_s[0:c, :] = zeros

    row = lax.broadcasted_iota(jnp.int32, (c, 1), 0)
    tril = (lax.broadcasted_iota(jnp.int32, (c, c), 0) >= lax.broadcasted_iota(jnp.int32, (c, c), 1)).astype(f32)
    ones_blk = _head_block_mask(bf)
    blk_f32 = _head_block_mask(f32)
    contract_last = (((1,), (1,)), ((), ()))
    contract_rows = (((0,), (0,)), ((), ()))

    def head_sum(x):
        hi = x.astype(bf)
        lo = (x - hi.astype(f32)).astype(bf)
        return (jnp.dot(hi, ones_blk, preferred_element_type=f32)
                + jnp.dot(lo, ones_blk, preferred_element_type=f32))

    def chunk(ch, carry):
        rows = pl.ds(pl.multiple_of(ch * c, c), c)
        q = q_ref[rows, :] * (GLA_DK ** -0.5)
        k = k_ref[rows, :]
        v = v_ref[rows, :]
        z = jnp.dot(ga_ref[rows, :].astype(bf), wa_ref[...], preferred_element_type=f32) + ba_ref[...]
        la = (jnp.minimum(z, 0.0) - jnp.log1p(jnp.exp(-jnp.abs(z)))) / GLA_GATE_TEMP
        if valid_rows < c:
            la = jnp.where(row < valid_rows, la, 0.0)
        b = jnp.dot(tril, la, preferred_element_type=f32, precision=lax.Precision.HIGHEST)
        st = st_s[...]
        o = lax.dot_general((q * jnp.exp(b)).astype(bf), st.astype(bf), contract_last,
                            preferred_element_type=f32)
        kp_s[c:2 * c, :] = k
        bp_s[c:2 * c, :] = b
        vp_s[c:2 * c, :] = v
        terms = [q * k]
        for d in range(1, c):
            ok = row >= d
            kr = kp_s[c - d:2 * c - d, :]
            br = bp_s[c - d:2 * c - d, :]
            terms.append(jnp.where(ok, q * kr * jnp.exp(jnp.where(ok, b - br, 0.0)), 0.0))
        att = jnp.dot(jnp.concatenate(terms, axis=0).astype(bf), ones_blk, preferred_element_type=f32)
        for d in range(c):
            vr = v if d == 0 else vp_s[c - d:2 * c - d, :]
            o = o + att[d * c:(d + 1) * c, :] * vr
        ms = head_sum(o * o) * (1.0 / GLA_DV)
        g = gg_ref[rows, :]
        o_ref[rows, :] = o * lax.rsqrt(ms + EPS) * gn_ref[...] * (g * jax.nn.sigmoid(g))
        b_last = b[c - 1:c, :]
        ke = k * jnp.exp(b_last - b)
        upd = lax.dot_general(v.astype(bf), ke.astype(bf), contract_rows, preferred_element_type=f32)
        st_s[...] = st * jnp.exp(b_last) + upd * blk_f32
        return carry

    lax.fori_loop(0, n_chunks, chunk, 0, unroll=min(2, n_chunks))
    st_ref[0] = st_s[...]


def _gla(proj, nbatch, tb, c, valid_rows, wa_pad, ba, gn, st0):
    m = proj.shape[0]
    nblk = m // nbatch // tb
    colblk = lambda p, w: PROJ_OFF[p] // w
    row_map = lambda j: (lambda b, i: (b * nblk + i, j))
    return pl.pallas_call(
        functools.partial(_gla_kernel, c=c, n_chunks=tb // c, valid_rows=valid_rows),
        grid=(nbatch, nblk),
        in_specs=[
            pl.BlockSpec((tb, GLA_LANES), row_map(colblk(0, GLA_LANES))),
            pl.BlockSpec((tb, GLA_LANES), row_map(colblk(1, GLA_LANES))),
            pl.BlockSpec((tb, GLA_LANES), row_map(colblk(2, GLA_LANES))),
            pl.BlockSpec((tb, GLA_LANES), row_map(colblk(3, GLA_LANES))),
            pl.BlockSpec((tb, LANES), row_map(colblk(4, LANES))),
            pl.BlockSpec((LANES, GLA_LANES), lambda b, i: (0, 0)),
            pl.BlockSpec((1, GLA_LANES), lambda b, i: (0, 0)),
            pl.BlockSpec((1, GLA_LANES), lambda b, i: (0, 0)),
            pl.BlockSpec((1, GLA_LANES, GLA_LANES), lambda b, i: (b, 0, 0)),
        ],
        out_specs=[pl.BlockSpec((tb, GLA_LANES), lambda b, i: (b * nblk + i, 0)),
                   pl.BlockSpec((1, GLA_LANES, GLA_LANES), lambda b, i: (b, 0, 0))],
        out_shape=[jax.ShapeDtypeStruct((m, GLA_LANES), jnp.float32),
                   jax.ShapeDtypeStruct((nbatch, GLA_LANES, GLA_LANES), jnp.float32)],
        scratch_shapes=[pltpu.VMEM((GLA_LANES, GLA_LANES), jnp.float32),
                        pltpu.VMEM((2 * c, GLA_LANES), jnp.float32),
                        pltpu.VMEM((2 * c, GLA_LANES), jnp.float32),
                        pltpu.VMEM((2 * c, GLA_LANES), jnp.float32)],
        compiler_params=pltpu.CompilerParams(
            dimension_semantics=("arbitrary", "arbitrary"), vmem_limit_bytes=VMEM_LIMIT_BYTES),
        name="gla_scan",
    )(proj, proj, proj, proj, proj, wa_pad, ba, gn, st0)


def _gla_state_in(s0):
    eye = jnp.eye(GLA_HEADS, dtype=jnp.float32)
    return jnp.einsum('bhde,hg->bhegd', s0.astype(jnp.float32), eye).reshape(-1, GLA_LANES, GLA_LANES)


def _gla_state_out(st):
    blocks = [st[:, h * GLA_DV:(h + 1) * GLA_DV, h * GLA_DK:(h + 1) * GLA_DK] for h in range(GLA_HEADS)]
    return jnp.stack(blocks, axis=1).transpose(0, 1, 3, 2)


def _nsa_prompt(proj, q, kv_cmp, kv_slc, kv_win, gates, cmp_w):
    B, T = kv_cmp.shape[:2]
    assert B == 1 and T % SLC_TILE == 0 and T >= WIN_TILE
    n_rows = T // CMP_STRIDE
    kvc = _compress(kv_cmp.reshape(n_rows, CMP_ROW), cmp_w, min(256, n_rows))
    o = _nsa_prompt_attention(proj, kvc, kv_slc.reshape(T, NSA_KV_WIDTH), kv_win.reshape(T, NSA_KV_WIDTH))
    return o.reshape(B, T, NSA_WIDTH), kv_win[:, -min(WINDOW, T):]


def _expand_rows(v, t):
    if v.shape[0] == 1:
        return v
    return jnp.repeat(v, t, axis=0)


def _trunk_layer(x, mod, lw, gla_s0, sc_hist, ffn_hist, nsa_apply, tm, final, g_final):
    B, T, _ = x.shape
    m = B * T
    grouped = B > 1
    ssh1, ssc1, sgt1, ssh2, ssc2, sgt2 = [_expand_rows(v, T) for v in jnp.split(mod, 6, axis=-1)]
    x2 = x.reshape(m, D_MODEL)
    proj = _in_proj(x2, lw['norm_mix'], ssc1, ssh1, lw['w_in'], tm).reshape(B, T, PROJ_WIDTH)
    gq, gk, gv, gg, ga, sb, scc, shh, nq, ncmp, nslc, nwin, ngate = [_proj_piece(proj, p) for p in range(13)]
    heads = lambda a, d: a.reshape(B, T, -1, d)
    if grouped:
        t_pad = _round_up(T, SUBLANES)
        gla_in = jnp.pad(proj, ((0, 0), (0, t_pad - T), (0, 0))).reshape(B * t_pad, PROJ_WIDTH)
        tb = chunk = t_pad
    else:
        t_pad, gla_in, tb, chunk = T, proj.reshape(m, PROJ_WIDTH), GLA_BLOCK_ROWS, GLA_KERNEL_CHUNK
    o_gla, st_gla = _gla(gla_in, B, tb, chunk, T if grouped else chunk,
                         lw['gla_wa'], lw['gla_ba'], lw['gla_norm'], _gla_state_in(gla_s0))
    o_gla = o_gla.reshape(B, t_pad, GLA_WIDTH)[:, :T]
    s_gla = _gla_state_out(st_gla)
    kvr = lambda a: a.reshape(B, T, 2, NSA_KV_HEADS, NSA_HD)
    kv_cmp, kv_slc, kv_win = kvr(ncmp), kvr(nslc), kvr(nwin)
    proj2 = proj.reshape(m, PROJ_WIDTH)
    o_nsa, win_state = nsa_apply(proj2, heads(nq, NSA_HD), kv_cmp, kv_slc, kv_win, heads(ngate, 3))

    def conv_hist(h):
        c = h.shape[-1]
        if grouped:
            pad = jnp.zeros((B, T - 1, c), jnp.float32)
            h1 = jnp.concatenate([h[:, 1:2], pad], axis=1).reshape(m, c)
            h2 = jnp.concatenate([h[:, 0:2], pad[:, 1:]], axis=1).reshape(m, c)
            return (h1, h2)
        return jnp.concatenate([jnp.zeros((HIST_ROWS - (CONV_W - 1), c), jnp.float32), h[0]], axis=0)

    def conv_state(st):
        c = st.shape[-1]
        return st.reshape(B, T, c)[:, -(CONV_W - 1):] if grouped else st[None, -(CONV_W - 1):]

    group = T if grouped else 0
    x2, sc_st = _out_proj(x2, o_gla.reshape(m, GLA_WIDTH), proj2, o_nsa.reshape(m, NSA_WIDTH), sgt1,
                          lw['w_out'], lw['sc_conv'], conv_hist(sc_hist), tm, group)
    sc_state = conv_state(sc_st)
    y, st = _ffn(x2, lw['norm_ffn'], ssc2, ssh2, sgt2, lw['ffn_up'], lw['ffn_conv'], lw['ffn_down'],
                 g_final, conv_hist(ffn_hist), tm, group, final)
    ffn_state = conv_state(st)
    return (y.reshape(B, T, D_MODEL), kv_cmp, kv_slc, win_state, s_gla.astype(gla_s0.dtype), sc_state, ffn_state)


def kernel(x_prompt, x_sample, cache_nsa_cmp, cache_nsa_slc, cache_nsa_win, state_gla, state_shortconv, state_ffn_conv, page_table, c_prompt, c_sample, mod_w, mod_b, norm_mix, norm_ffn, w_in, gla_wa2, gla_ba, gla_norm, sc_conv, nsa_cmp_pos, nsa_cmp_w1, nsa_cmp_w2, w_out, ffn_up, ffn_conv, ffn_down, norm_final):
    xp, xs = x_prompt, x_sample
    bp, bs = xp.shape[0], xs.shape[0]
    assert bp == 1 and xs.shape[1] == 4
    c_rows = _round_up(bp + bs, SUBLANES)
    c_all = jnp.concatenate([c_prompt, c_sample, jnp.zeros((c_rows - bp - bs, D_MODEL), jnp.float32)], axis=0)
    mod_all = _modulation(c_all, mod_w, mod_b)
    g_final = norm_final.reshape(1, D_MODEL)
    outs = [[] for _ in range(12)]
    for l in range(DEPTH):
        lw = dict(
            norm_mix=norm_mix[l].reshape(1, D_MODEL), norm_ffn=norm_ffn[l].reshape(1, D_MODEL),
            w_in=_pack_w_in(w_in[l]),
            gla_wa=jnp.zeros((LANES, GLA_LANES), jnp.bfloat16).at[:GLA_GATE_RANK].set(
                gla_wa2[l].astype(jnp.bfloat16)),
            gla_ba=gla_ba[l].reshape(1, GLA_LANES),
            gla_norm=jnp.tile(gla_norm[l], GLA_HEADS).reshape(1, GLA_LANES),
            sc_conv=sc_conv[l], w_out=w_out[l].astype(jnp.bfloat16),
            ffn_up=ffn_up[l].astype(jnp.bfloat16), ffn_conv=ffn_conv[l],
            ffn_down=ffn_down[l].astype(jnp.bfloat16))
        cmp_params = (nsa_cmp_pos[l], nsa_cmp_w1[l], nsa_cmp_w2[l])
        cmp_w = _compress_weights(*cmp_params)
        final = l == DEPTH - 1
        res_p = _trunk_layer(
            xp, mod_all[l, 0:bp], lw,
            jnp.zeros((bp, GLA_HEADS, GLA_DK, GLA_DV), xp.dtype),
            jnp.zeros((bp, CONV_W - 1, SC_WIDTH), xp.dtype),
            jnp.zeros((bp, CONV_W - 1, 2 * D_FF), xp.dtype),
            functools.partial(_nsa_prompt, cmp_w=cmp_w), 256, final, g_final)
        res_s = _trunk_layer(
            xs, mod_all[l, bp:bp + bs], lw, state_gla[l], state_shortconv[l], state_ffn_conv[l],
            functools.partial(_nsa_sample, pool_cmp=cache_nsa_cmp, pool_slc=cache_nsa_slc,
                              page_table=page_table, win_cache=cache_nsa_win, layer=l,
                              cmp_w=cmp_w), bs * xs.shape[1], final, g_final)
        xp, xs = res_p[0], res_s[0]
        for k in range(6):
            outs[2 * k].append(res_p[k + 1])
            outs[2 * k + 1].append(res_s[k + 1])
    return (xp, xs) + tuple(jnp.stack(o) for o in outs)
```

```python
import functools

import jax
import jax.numpy as jnp
from jax import lax
from jax.experimental import pallas as pl
from jax.experimental.pallas import tpu as pltpu

D_MODEL = 1024
DEPTH = 2
PAGE_SIZE = 128
GLA_HEADS = 4
GLA_DK = D_MODEL // 16
GLA_DV = D_MODEL // 16
GLA_WIDTH = GLA_HEADS * GLA_DV
GLA_GATE_RANK = 16
GLA_GATE_TEMP = 16.0
GLA_CHUNK = 64
SC_WIDTH = D_MODEL // 4
CONV_W = 3
NSA_HEADS = 8
NSA_KV_HEADS = 2
NSA_HD = D_MODEL // 16
NSA_WIDTH = NSA_HEADS * NSA_HD
NSA_KV_WIDTH = 2 * NSA_KV_HEADS * NSA_HD
CMP_STRIDE = 16
CMP_BLOCK = 2 * CMP_STRIDE
CMP_HIDDEN = 128
SEL_BLOCK = 64
SEL_TOPN = 16
WINDOW = 512
Q_BLOCK = 128
D_FF = 2816
EPS = 1e-6
NEG = -1e30
TINY = 1e-30
FORCE = 1e9

IN_SIZES = (
    GLA_HEADS * GLA_DK, GLA_HEADS * GLA_DK, GLA_WIDTH, GLA_WIDTH, GLA_GATE_RANK,
    SC_WIDTH, SC_WIDTH, SC_WIDTH,
    NSA_WIDTH, NSA_KV_WIDTH, NSA_KV_WIDTH, NSA_KV_WIDTH, NSA_HEADS * 3,
)

LANES = 128
SUBLANES = 8
BF16_SUBLANES = 16
VMEM_LIMIT_BYTES = 56 * 1024 * 1024

PROJ_ORDER = (0, 1, 2, 3, 8, 5, 6, 7, 9, 10, 11, 4, 12)


def _round_up(n, m):
    return -(-n // m) * m


def _proj_layout():
    src, acc = [], 0
    for s in IN_SIZES:
        src.append(acc)
        acc += s
    offs, dst = {}, 0
    for p in PROJ_ORDER:
        offs[p] = dst
        dst += _round_up(IN_SIZES[p], LANES)
    return src, offs, dst


PROJ_SRC, PROJ_OFF, PROJ_WIDTH = _proj_layout()


def _pack_w_in(w_in):
    out = jnp.zeros((D_MODEL, PROJ_WIDTH), jnp.bfloat16)
    for p in PROJ_ORDER:
        piece = w_in[:, PROJ_SRC[p]:PROJ_SRC[p] + IN_SIZES[p]].astype(jnp.bfloat16)
        out = lax.dynamic_update_slice(out, piece, (0, PROJ_OFF[p]))
    return out


def _proj_piece(proj, p):
    return proj[..., PROJ_OFF[p]:PROJ_OFF[p] + IN_SIZES[p]]


def _mod_kernel(c_ref, w_ref, b_ref, o_ref):
    c = c_ref[...]
    a = c * jax.nn.sigmoid(c)
    o_ref[0] = jnp.dot(a, w_ref[0], preferred_element_type=jnp.float32,
                       precision=lax.Precision.HIGHEST) + b_ref[0]


def _modulation(c_all, mod_w, mod_b):
    rows = c_all.shape[0]
    tn = 1024
    n = mod_w.shape[-1]
    return pl.pallas_call(
        _mod_kernel,
        grid=(DEPTH, n // tn),
        in_specs=[
            pl.BlockSpec((rows, D_MODEL), lambda l, j: (0, 0)),
            pl.BlockSpec((1, D_MODEL, tn), lambda l, j: (l, 0, j)),
            pl.BlockSpec((1, 1, tn), lambda l, j: (l, 0, j)),
        ],
        out_specs=pl.BlockSpec((1, rows, tn), lambda l, j: (l, 0, j)),
        out_shape=jax.ShapeDtypeStruct((DEPTH, rows, n), jnp.float32),
        name="adaln_modulation",
    )(c_all, mod_w, mod_b.reshape(DEPTH, 1, n))


def _norm_mod(x, g, sc, sh):
    r = lax.rsqrt(jnp.mean(x * x, axis=-1, keepdims=True) + EPS)
    return (x * r * g) * (1.0 + sc) + sh


def _in_proj_kernel(x_ref, g_ref, sc_ref, sh_ref, w_ref, o_ref):
    h = _norm_mod(x_ref[...], g_ref[...], sc_ref[...], sh_ref[...])
    o_ref[...] = jnp.dot(h.astype(jnp.bfloat16), w_ref[...], preferred_element_type=jnp.float32)


def _row_spec(tm, per_row):
    if per_row:
        return pl.BlockSpec((tm, D_MODEL), lambda i: (i, 0))
    return pl.BlockSpec((1, D_MODEL), lambda i: (0, 0))


def _resident(shape):
    return pl.BlockSpec(shape, lambda i: (0,) * len(shape), pipeline_mode=pl.Buffered(1))


def _in_proj(x, g, sc, sh, w_packed, tm):
    m = x.shape[0]
    per_row = sc.shape[0] != 1
    return pl.pallas_call(
        _in_proj_kernel,
        grid=(m // tm,),
        in_specs=[
            pl.BlockSpec((tm, D_MODEL), lambda i: (i, 0)),
            _resident((1, D_MODEL)),
            _row_spec(tm, per_row),
            _row_spec(tm, per_row),
            _resident((D_MODEL, PROJ_WIDTH)),
        ],
        out_specs=pl.BlockSpec((tm, PROJ_WIDTH), lambda i: (i, 0)),
        out_shape=jax.ShapeDtypeStruct((m, PROJ_WIDTH), jnp.float32),
        compiler_params=pltpu.CompilerParams(
            dimension_semantics=("arbitrary",), vmem_limit_bytes=VMEM_LIMIT_BYTES),
        name="norm_in_proj",
    )(x, g, sc, sh, w_packed)


HIST_ROWS = SUBLANES


def _out_proj_kernel(*refs, tm, group):
    grouped = group > 0
    if grouped:
        (x_ref, gla_ref, sb_ref, scc_ref, shh_ref, nsa_ref, gt_ref, w_ref, cw_ref, h1_ref, h2_ref,
         o_ref, st_ref, u_s) = refs
    else:
        (x_ref, gla_ref, sb_ref, scc_ref, shh_ref, nsa_ref, gt_ref, w_ref, cw_ref, h0_ref,
         o_ref, st_ref, u_s) = refs

        @pl.when(pl.program_id(0) == 0)
        def _():
            u_s[0:HIST_ROWS, :] = h0_ref[...]

    u = scc_ref[...] * shh_ref[...]
    u_s[HIST_ROWS:HIST_ROWS + tm, :] = u
    p1 = u_s[HIST_ROWS - 1:HIST_ROWS - 1 + tm, :]
    p2 = u_s[HIST_ROWS - 2:HIST_ROWS - 2 + tm, :]
    if grouped:
        t = lax.broadcasted_iota(jnp.int32, (tm, 1), 0) % group
        p1 = jnp.where(t == 0, h1_ref[...], p1)
        p2 = jnp.where(t <= 1, h2_ref[...], p2)
    o_sc = sb_ref[...] * (cw_ref[0:1, :] * p2 + cw_ref[1:2, :] * p1 + cw_ref[2:3, :] * u)
    mix = jnp.concatenate([gla_ref[...], o_sc, nsa_ref[...]], axis=1).astype(jnp.bfloat16)
    o_ref[...] = x_ref[...] + gt_ref[...] * jnp.dot(mix, w_ref[...], preferred_element_type=jnp.float32)
    if grouped:
        st_ref[...] = u
    else:
        tail = u_s[tm:tm + HIST_ROWS, :]
        st_ref[...] = tail
        u_s[0:HIST_ROWS, :] = tail


def _out_proj(x, o_gla, proj, o_nsa, gt, w_bf16, cw, hist, tm, group):
    m = x.shape[0]
    per_row = gt.shape[0] != 1
    grouped = group > 0
    sc_cols = lambda p: pl.BlockSpec((tm, SC_WIDTH), lambda i: (i, PROJ_OFF[p] // SC_WIDTH))
    in_specs = [
        pl.BlockSpec((tm, D_MODEL), lambda i: (i, 0)),
        pl.BlockSpec((tm, GLA_WIDTH), lambda i: (i, 0)),
        sc_cols(5), sc_cols(6), sc_cols(7),
        pl.BlockSpec((tm, NSA_WIDTH), lambda i: (i, 0)),
        _row_spec(tm, per_row),
        _resident((D_MODEL, D_MODEL)),
        _resident((CONV_W, SC_WIDTH)),
    ]
    if grouped:
        assert m == tm
        in_specs += [_resident((tm, SC_WIDTH)), _resident((tm, SC_WIDTH))]
        hist_args, st_rows = tuple(hist), tm
    else:
        in_specs += [_resident((HIST_ROWS, SC_WIDTH))]
        hist_args, st_rows = (hist,), HIST_ROWS
    return pl.pallas_call(
        functools.partial(_out_proj_kernel, tm=tm, group=group),
        grid=(m // tm,),
        in_specs=in_specs,
        out_specs=[pl.BlockSpec((tm, D_MODEL), lambda i: (i, 0)),
                   pl.BlockSpec((st_rows, SC_WIDTH), lambda i: (0, 0))],
        out_shape=[jax.ShapeDtypeStruct((m, D_MODEL), jnp.float32),
                   jax.ShapeDtypeStruct((st_rows, SC_WIDTH), jnp.float32)],
        scratch_shapes=[pltpu.VMEM((HIST_ROWS + tm, SC_WIDTH), jnp.float32)],
        compiler_params=pltpu.CompilerParams(
            dimension_semantics=("arbitrary",), vmem_limit_bytes=VMEM_LIMIT_BYTES),
        name="shortconv_out_proj",
    )(x, o_gla, proj, proj, proj, o_nsa, gt, w_bf16, cw, *hist_args)


FFN_UP_CHUNK = 512
FFN_ACT_CHUNK = 256


def _ffn_kernel(*refs, tm, group, final):
    grouped = group > 0
    if grouped:
        (x_ref, g_ref, sc_ref, sh_ref, gt_ref, wup_ref, cw_ref, wdn_ref, gf_ref,
         h1_ref, h2_ref, o_ref, st_ref, up_s) = refs
    else:
        (x_ref, g_ref, sc_ref, sh_ref, gt_ref, wup_ref, cw_ref, wdn_ref, gf_ref,
         h0_ref, o_ref, st_ref, up_s) = refs

        @pl.when(pl.program_id(0) == 0)
        def _():
            up_s[0:HIST_ROWS, :] = h0_ref[...]

    x = x_ref[...]
    h = _norm_mod(x, g_ref[...], sc_ref[...], sh_ref[...]).astype(jnp.bfloat16)
    for c in range(2 * D_FF // FFN_UP_CHUNK):
        cols = slice(c * FFN_UP_CHUNK, (c + 1) * FFN_UP_CHUNK)
        up_s[HIST_ROWS:HIST_ROWS + tm, cols] = jnp.dot(
            h, wup_ref[:, cols], preferred_element_type=jnp.float32)

    if grouped:
        t = lax.broadcasted_iota(jnp.int32, (tm, 1), 0) % group

    def conv(cols):
        cur = up_s[HIST_ROWS:HIST_ROWS + tm, cols]
        p1 = up_s[HIST_ROWS - 1:HIST_ROWS - 1 + tm, cols]
        p2 = up_s[HIST_ROWS - 2:HIST_ROWS - 2 + tm, cols]
        if grouped:
            p1 = jnp.where(t == 0, h1_ref[:, cols], p1)
            p2 = jnp.where(t <= 1, h2_ref[:, cols], p2)
        return cw_ref[0:1, cols] * p2 + cw_ref[1:2, cols] * p1 + cw_ref[2:3, cols] * cur

    acc = jnp.zeros((tm, D_MODEL), jnp.float32)
    for c in range(D_FF // FFN_ACT_CHUNK):
        a = conv(slice(c * FFN_ACT_CHUNK, (c + 1) * FFN_ACT_CHUNK))
        b = conv(slice(D_FF + c * FFN_ACT_CHUNK, D_FF + (c + 1) * FFN_ACT_CHUNK))
        act = (a * jax.nn.sigmoid(a) * b).astype(jnp.bfloat16)
        acc = acc + jnp.dot(act, wdn_ref[c * FFN_ACT_CHUNK:(c + 1) * FFN_ACT_CHUNK, :],
                            preferred_element_type=jnp.float32)
    y = x + gt_ref[...] * acc
    if final:
        r = lax.rsqrt(jnp.mean(y * y, axis=-1, keepdims=True) + EPS)
        y = y * r * gf_ref[...]
    o_ref[...] = y

    if grouped:
        st_ref[...] = up_s[HIST_ROWS:HIST_ROWS + tm, :]
    else:
        tail = up_s[tm:tm + HIST_ROWS, :]
        st_ref[...] = tail
        up_s[0:HIST_ROWS, :] = tail


def _ffn(x, g, sc, sh, gt, wup, cw, wdn, g_final, hist, tm, group, final):
    m = x.shape[0]
    grouped = group > 0
    per_row = sc.shape[0] != 1
    ff2 = 2 * D_FF
    in_specs = [
        pl.BlockSpec((tm, D_MODEL), lambda i: (i, 0)),
        _resident((1, D_MODEL)),
        _row_spec(tm, per_row), _row_spec(tm, per_row), _row_spec(tm, per_row),
        _resident((D_MODEL, ff2)),
        _resident((CONV_W, ff2)),
        _resident((D_FF, D_MODEL)),
        _resident((1, D_MODEL)),
    ]
    if grouped:
        assert m == tm
        in_specs += [_resident((tm, ff2)), _resident((tm, ff2))]
        hist_args = tuple(hist)
        st_rows = tm
    else:
        in_specs += [_resident((HIST_ROWS, ff2))]
        hist_args = (hist,)
        st_rows = HIST_ROWS
    return pl.pallas_call(
        functools.partial(_ffn_kernel, tm=tm, group=group, final=final),
        grid=(m // tm,),
        in_specs=in_specs,
        out_specs=[pl.BlockSpec((tm, D_MODEL), lambda i: (i, 0)),
                   pl.BlockSpec((st_rows, ff2), lambda i: (0, 0))],
        out_shape=[jax.ShapeDtypeStruct((m, D_MODEL), jnp.float32),
                   jax.ShapeDtypeStruct((st_rows, ff2), jnp.float32)],
        scratch_shapes=[pltpu.VMEM((HIST_ROWS + tm, ff2), jnp.float32)],
        compiler_params=pltpu.CompilerParams(
            dimension_semantics=("arbitrary",), vmem_limit_bytes=VMEM_LIMIT_BYTES),
        name="conv_ffn",
    )(x, g, sc, sh, gt, wup, cw, wdn, g_final, *hist_args)


CMP_ROW = CMP_STRIDE * NSA_KV_WIDTH
CMP_HID = 2 * NSA_KV_HEADS * CMP_HIDDEN


def _gelu_tanh(x):
    return 0.5 * x * (1.0 + jnp.tanh(0.7978845608028654 * (x + 0.044715 * (x * x * x))))


def _compress_kernel(x_ref, xn_ref, pos_ref, wl_ref, wt_ref, w2_ref, o_ref, tr_s, *, tm):
    bf = jnp.bfloat16
    f32 = jnp.float32
    x = x_ref[...].astype(bf)
    lead = jnp.dot(x, wl_ref[...], preferred_element_type=f32)
    tr_s[0:tm, :] = jnp.dot(x, wt_ref[...], preferred_element_type=f32)
    tr_s[tm:tm + SUBLANES, :] = jnp.dot(xn_ref[...].astype(bf), wt_ref[...], preferred_element_type=f32)
    bias = (jnp.dot(pos_ref[0].astype(bf), wl_ref[...], preferred_element_type=f32)
            + jnp.dot(pos_ref[1].astype(bf), wt_ref[...], preferred_element_type=f32))[0:1, :]
    hid = _gelu_tanh(lead + tr_s[1:tm + 1, :] + bias)
    o_ref[...] = jnp.dot(hid.astype(bf), w2_ref[...], preferred_element_type=f32)


def _compress_weights(pos_emb, w1, w2):
    eye = jnp.eye(2, dtype=jnp.float32)
    w1f = jnp.einsum('kldh,kK,gG->lkgdKGh', w1, eye, eye).reshape(CMP_BLOCK, NSA_KV_WIDTH, CMP_HID)
    w1f = w1f.astype(jnp.bfloat16)
    w2b = jnp.einsum('khd,kK,gG->kghKGd', w2, eye, eye).reshape(CMP_HID, NSA_KV_WIDTH).astype(jnp.bfloat16)
    posf = jnp.broadcast_to(pos_emb.transpose(1, 0, 2)[:, :, None, :], (CMP_BLOCK, 2, NSA_KV_HEADS, NSA_HD))
    posf = posf.reshape(CMP_BLOCK, NSA_KV_WIDTH)
    pos = jnp.zeros((CMP_BLOCK, SUBLANES, NSA_KV_WIDTH), jnp.float32).at[:, 0].set(posf)
    pos_rows = jnp.zeros((2, SUBLANES, CMP_ROW), jnp.float32).at[:, 0].set(posf.reshape(2, CMP_ROW))
    w1kv = jnp.einsum('kldh,gG->lkgdGh', w1, eye).reshape(
        CMP_BLOCK // 2, 2, 2, NSA_KV_WIDTH // 2, CMP_HID // 2).transpose(0, 2, 1, 3, 4).reshape(
        CMP_BLOCK // 2, 2, NSA_KV_WIDTH, CMP_HID // 2).astype(jnp.bfloat16)
    return dict(pos=pos, w1=w1kv, w2=w2b, pos_rows=pos_rows,
                wl=w1f[:CMP_STRIDE].reshape(CMP_ROW, CMP_HID), wt=w1f[CMP_STRIDE:].reshape(CMP_ROW, CMP_HID))


def _compress(x, cw, tm):
    pos, wl, wt, w2b = cw['pos_rows'], cw['wl'], cw['wt'], cw['w2']
    n = x.shape[0]
    nb8 = n // SUBLANES
    return pl.pallas_call(
        functools.partial(_compress_kernel, tm=tm),
        grid=(n // tm,),
        in_specs=[
            pl.BlockSpec((tm, CMP_ROW), lambda i: (i, 0)),
            pl.BlockSpec((SUBLANES, CMP_ROW), lambda i: (jnp.minimum((i + 1) * (tm // SUBLANES), nb8 - 1), 0)),
            _resident((2, SUBLANES, CMP_ROW)),
            _resident((CMP_ROW, CMP_HID)), _resident((CMP_ROW, CMP_HID)),
            _resident((CMP_HID, NSA_KV_WIDTH)),
        ],
        out_specs=pl.BlockSpec((tm, NSA_KV_WIDTH), lambda i: (i, 0)),
        out_shape=jax.ShapeDtypeStruct((n, NSA_KV_WIDTH), jnp.float32),
        scratch_shapes=[pltpu.VMEM((tm + SUBLANES, CMP_HID), jnp.float32)],
        compiler_params=pltpu.CompilerParams(
            dimension_semantics=("arbitrary",), vmem_limit_bytes=VMEM_LIMIT_BYTES),
        name="nsa_compress",
    )(x, x, pos, wl, wt, w2b)


NSA_R = NSA_HEADS // NSA_KV_HEADS
QL = NSA_R * Q_BLOCK
QLL = NSA_KV_HEADS * QL
SLC_TILE = 512
BLK_PER_TILE = SLC_TILE // SEL_BLOCK
WIN_TILE = WINDOW + Q_BLOCK
CMP_PER_SEL = SEL_BLOCK // CMP_STRIDE
M_INIT = -1e29
QK_SCALE = NSA_HD ** -0.5 * 1.4426950408889634
ONES_ROWS = BF16_SUBLANES


def _tile_lanes(v, reps):
    return jnp.concatenate([v] * reps, axis=1)


def _nsa_prompt_kernel(q_ref, gate_ref, kc_ref, vct_ref, kslc_ref, vtslc_ref, kwin_ref, vtwin_ref,
                       o_ref, sc_s, sel_s, m_s, l_s, acc_s, mt_s, al_s, s_s, pt_s, *, n_sel):
    f32, bf = jnp.float32, jnp.bfloat16
    G, HD = NSA_KV_HEADS, NSA_HD
    n = pl.program_id(0)
    qn = q_ref[...] * QK_SCALE
    tq = [qn[:, j * LANES:(j + 1) * LANES].T for j in range(NSA_WIDTH // LANES)]
    heads_per_t = LANES // HD
    head_t = lambda h: tq[h // heads_per_t][(h % heads_per_t) * HD:(h % heads_per_t + 1) * HD, :]
    zero = jnp.zeros((HD, Q_BLOCK), f32)
    qbd = jnp.concatenate(
        [jnp.concatenate([head_t(g * NSA_R + r) if g == gp else zero for g in range(G) for r in range(NSA_R)],
                         axis=1) for gp in range(G)], axis=0).astype(bf)
    lane = lax.broadcasted_iota(jnp.int32, (1, Q_BLOCK), 1)
    pos_q = n * Q_BLOCK + lane
    pos_l = _tile_lanes(pos_q, QLL // Q_BLOCK)
    jrow = lax.broadcasted_iota(jnp.int32, (n_sel, 1), 0)

    s_c, mk_c = [], []
    m = jnp.full((1, QLL), NEG, f32)
    for c in range(CMP_PER_SEL):
        s = jnp.dot(kc_ref[c * n_sel:(c + 1) * n_sel, :], qbd, preferred_element_type=f32)
        mk = jrow * SEL_BLOCK + (c * CMP_STRIDE + CMP_BLOCK - 1) <= pos_l
        s = jnp.where(mk, s, NEG)
        m = jnp.maximum(m, jnp.max(s, axis=0, keepdims=True))
        s_c.append(s)
        mk_c.append(mk)
    e_c = [jnp.where(mk_c[c], jnp.exp2(s_c[c] - m), 0.0) for c in range(CMP_PER_SEL)]
    l = e_c[0].sum(axis=0, keepdims=True)
    for c in range(1, CMP_PER_SEL):
        l = l + e_c[c].sum(axis=0, keepdims=True)
    inv = 1.0 / jnp.maximum(l, TINY)
    o_cmp = [jnp.zeros((HD, QL), f32) for _ in range(G)]
    pg = []
    for c in range(CMP_PER_SEL):
        p = e_c[c] * inv
        pb = p.astype(bf)
        for g in range(G):
            o_cmp[g] = o_cmp[g] + jnp.dot(vct_ref[g * HD:(g + 1) * HD, c * n_sel:(c + 1) * n_sel],
                                          pb[:, g * QL:(g + 1) * QL], preferred_element_type=f32)
        pg.append([sum(p[:, g * QL + r * Q_BLOCK:g * QL + (r + 1) * Q_BLOCK] for r in range(NSA_R))
                   for g in range(G)])
    o_cmp = jnp.concatenate(o_cmp, axis=1)

    cur = pos_q // SEL_BLOCK
    forced = (jrow == 0) | (jrow == cur) | (jrow == cur - 1)
    allowed = jrow * SEL_BLOCK <= pos_q
    jrow_f = jrow.astype(f32)
    for g in range(G):
        last = pg[CMP_PER_SEL - 1][g]
        prev = jnp.where(jrow == 0, 0.0, pltpu.roll(last, 1, 0))
        inner = pg[0][g]
        for c in range(1, CMP_PER_SEL - 1):
            inner = inner + pg[c][g]
        p_slc = 2.0 * inner + last + prev
        sc_s[g] = jnp.where(forced, FORCE, jnp.where(allowed, p_slc, -1.0))
        sel_s[g] = jnp.zeros((n_sel, Q_BLOCK), f32)

    def pick(_, carry):
        for g in range(G):
            s = sc_s[g]
            top = jnp.max(s, axis=0, keepdims=True)
            first = jnp.min(jnp.where(s == top, jrow_f, float(n_sel)), axis=0, keepdims=True)
            hit = jrow_f == first
            sc_s[g] = jnp.where(hit, -jnp.inf, s)
            sel_s[g] = jnp.where(hit, 1.0, sel_s[g])
        return carry

    lax.fori_loop(0, min(SEL_TOPN, n_sel), pick, 0)
    for g in range(G):
        sel_s[g] = jnp.where(allowed & (sel_s[g] > 0.5), 0.0, NEG)

    def reset():
        m_s[...] = jnp.full((1, QLL), M_INIT, f32)
        l_s[...] = jnp.zeros((1, QLL), f32)
        acc_s[...] = jnp.zeros((HD, QLL), f32)

    def pass1(slot, k_ref, start, rows, bias_fn):
        s = jnp.dot(k_ref[pl.ds(start, rows), :].astype(bf), qbd, preferred_element_type=f32)
        top = jnp.full((SUBLANES, QLL), NEG, f32)
        for i in range(rows // SEL_BLOCK):
            blk = slice(i * SEL_BLOCK, (i + 1) * SEL_BLOCK)
            sb = s[blk, :] + bias_fn(i)
            s_s[slot, blk, :] = sb
            for r in range(SEL_BLOCK // SUBLANES):
                top = jnp.maximum(top, sb[r * SUBLANES:(r + 1) * SUBLANES, :])
        m_old = m_s[...]
        m_new = jnp.maximum(m_old, jnp.max(top, axis=0, keepdims=True))
        mt_s[slot] = m_new
        al_s[slot] = jnp.exp2(m_old - m_new)
        m_s[...] = m_new

    def pass2(slot, vta_ref, start, rows):
        m_new = mt_s[slot]
        for i in range(rows // SEL_BLOCK):
            blk = slice(i * SEL_BLOCK, (i + 1) * SEL_BLOCK)
            pt_s[slot, blk, :] = jnp.exp2(s_s[slot, blk, :] - m_new).astype(bf)
        pv, psum = [], []
        for g in range(G):
            r = jnp.dot(vta_ref[g, :, pl.ds(start, rows)], pt_s[slot, 0:rows, g * QL:(g + 1) * QL],
                        preferred_element_type=f32)
            pv.append(r[0:HD, :])
            psum.append(r[HD:HD + 1, :])
        alpha = al_s[slot]
        acc_s[...] = acc_s[...] * alpha + jnp.concatenate(pv, axis=1)
        l_s[...] = l_s[...] * alpha + jnp.concatenate(psum, axis=1)

    def finish():
        return acc_s[...] * (1.0 / jnp.maximum(l_s[...], TINY))

    def tile_start(kt):
        return pl.multiple_of(kt * SLC_TILE, SLC_TILE)

    def slc_pass1(slot, kt, causal):
        start = tile_start(kt)
        selb = [sel_s[g, pl.ds(pl.multiple_of(kt * BLK_PER_TILE, BLK_PER_TILE), BLK_PER_TILE), :]
                for g in range(G)]

        def bias_fn(i):
            row = jnp.concatenate([_tile_lanes(selb[g][i:i + 1, :], NSA_R) for g in range(G)], axis=1)
            if not causal:
                return row
            tok = start + i * SEL_BLOCK + lax.broadcasted_iota(jnp.int32, (SEL_BLOCK, 1), 0)
            return jnp.where(tok <= pos_l, row, NEG)

        pass1(slot, kslc_ref, start, SLC_TILE, bias_fn)

    def slc_pass2(slot, kt):
        pass2(slot, vtslc_ref, tile_start(kt), SLC_TILE)

    reset()
    diag = (n * Q_BLOCK) // SLC_TILE
    slc_pass1(0, diag, True)

    def slc_pair(j, carry):
        slc_pass1(1, 2 * j, False)
        slc_pass2(0, jnp.where(j == 0, diag, 2 * j - 1))
        slc_pass1(0, 2 * j + 1, False)
        slc_pass2(1, 2 * j)
        return carry

    pairs = diag // 2
    lax.fori_loop(0, pairs, slc_pair, 0)
    pending = jnp.where(pairs == 0, diag, 2 * pairs - 1)

    @pl.when(diag % 2 == 1)
    def _():
        slc_pass1(1, diag - 1, False)
        slc_pass2(0, pending)
        slc_pass2(1, diag - 1)

    @pl.when(diag % 2 == 0)
    def _():
        slc_pass2(0, pending)

    o_slc = finish()

    reset()
    wstart = pl.multiple_of(jnp.maximum(n * Q_BLOCK - WINDOW, 0), Q_BLOCK)
    rel = pos_q - (wstart + lax.broadcasted_iota(jnp.int32, (WIN_TILE, 1), 0))
    wbias = jnp.where((rel >= 0) & (rel <= WINDOW), 0.0, NEG)
    pass1(0, kwin_ref, wstart, WIN_TILE,
          lambda i: _tile_lanes(wbias[i * SEL_BLOCK:(i + 1) * SEL_BLOCK, :], QLL // Q_BLOCK))
    pass2(0, vtwin_ref, wstart, WIN_TILE)
    o_win = finish()

    gt = jax.nn.sigmoid(gate_ref[...].T)
    gate = lambda c: jnp.concatenate([gt[h * 3 + c:h * 3 + c + 1, :] for h in range(NSA_HEADS)], axis=1)
    o = gate(0) * o_cmp + gate(1) * o_slc + gate(2) * o_win
    o_ref[...] = jnp.concatenate(
        [jnp.concatenate([o[:, (heads_per_t * j + t) * Q_BLOCK:(heads_per_t * j + t + 1) * Q_BLOCK]
                          for t in range(heads_per_t)], axis=0).T for j in range(NSA_WIDTH // LANES)], axis=1)


def _nsa_prompt_attention(proj, kvc, nslc, nwin):
    T = proj.shape[0]
    nb, n_sel = T // Q_BLOCK, T // SEL_BLOCK
    G, R, HD = NSA_KV_HEADS, NSA_R, NSA_HD
    bf = jnp.bfloat16
    half = G * HD
    assert PROJ_OFF[8] % NSA_WIDTH == 0 and PROJ_OFF[12] % LANES == 0
    kvp = kvc.reshape(n_sel, CMP_PER_SEL, NSA_KV_WIDTH).transpose(1, 0, 2).reshape(T // CMP_STRIDE, NSA_KV_WIDTH)
    kc, vct = kvp[:, :half].astype(bf), kvp[:, half:].T.astype(bf)
    def vt_ones(v):
        vt = v.T.reshape(G, HD, T)
        return jnp.concatenate([vt, jnp.ones((G, ONES_ROWS, T), vt.dtype)], axis=1).astype(bf)
    vtslc, vtwin = vt_ones(nslc[:, half:]), vt_ones(nwin[:, half:])
    k_cols = lambda p: pl.BlockSpec((T, half), lambda i: (0, PROJ_OFF[p] // half), pipeline_mode=pl.Buffered(1))
    out = pl.pallas_call(
        functools.partial(_nsa_prompt_kernel, n_sel=n_sel),
        grid=(nb,),
        in_specs=[
            pl.BlockSpec((Q_BLOCK, NSA_WIDTH), lambda i: (i, PROJ_OFF[8] // NSA_WIDTH)),
            pl.BlockSpec((Q_BLOCK, LANES), lambda i: (i, PROJ_OFF[12] // LANES)),
            _resident((T // CMP_STRIDE, half)), _resident((half, T // CMP_STRIDE)),
            k_cols(10), _resident((G, HD + ONES_ROWS, T)),
            k_cols(11), _resident((G, HD + ONES_ROWS, T)),
        ],
        out_specs=pl.BlockSpec((Q_BLOCK, NSA_WIDTH), lambda i: (i, 0)),
        out_shape=jax.ShapeDtypeStruct((T, NSA_WIDTH), jnp.float32),
        scratch_shapes=[
            pltpu.VMEM((G, n_sel, Q_BLOCK), jnp.float32),
            pltpu.VMEM((G, n_sel, Q_BLOCK), jnp.float32),
            pltpu.VMEM((1, QLL), jnp.float32),
            pltpu.VMEM((1, QLL), jnp.float32),
            pltpu.VMEM((HD, QLL), jnp.float32),
            pltpu.VMEM((2, 1, QLL), jnp.float32),
            pltpu.VMEM((2, 1, QLL), jnp.float32),
            pltpu.VMEM((2, WIN_TILE, QLL), jnp.float32),
            pltpu.VMEM((2, WIN_TILE, QLL), jnp.bfloat16),
        ],
        compiler_params=pltpu.CompilerParams(
            dimension_semantics=("arbitrary",), vmem_limit_bytes=VMEM_LIMIT_BYTES),
        name="nsa_prompt_attention",
    )(proj, proj, kc, vct, proj, vtslc, proj, vtwin)
    return out


PAGE_ROWS = PAGE_SIZE // CMP_STRIDE
SQ = 16
SAMPLE_LANES = NSA_HEADS * SQ
HALF_PAGES = 64
SAMPLE_TILE = 4096


def _page_view(pool):
    d, n_pool = pool.shape[:2]
    return pool.transpose(0, 1, 3, 4, 5, 2).reshape(d * n_pool, 2, NSA_KV_HEADS * NSA_HD, pool.shape[2])


def _page_copy(pool_ref, buf_ref, sem_ref, page, slot, idx):
    return pltpu.make_async_copy(pool_ref.at[page], buf_ref.at[slot, idx], sem_ref.at[slot])


def _gather_schedule(issue_fn, wait_fn):
    s = pl.program_id(0)

    @pl.when(s == 0)
    def _():
        issue_fn(s, 0)

    @pl.when(s + 1 < pl.num_programs(0))
    def _():
        issue_fn(s + 1, (s + 1) % 2)

    wait_fn(s, s % 2)


def _compress_paged_kernel(pt_ref, pool_ref, pos_ref, w1_ref, w2_ref, o_ref, buf, sem, tok_s,
                           *, page_base, n_pages):
    bf, f32 = jnp.bfloat16, jnp.float32
    rows = HALF_PAGES * PAGE_ROWS

    def copies(step, slot, fn):
        b, half = step // 2, step % 2

        def body(i, c):
            p = jnp.minimum(half * HALF_PAGES + i, n_pages - 1)
            fn(_page_copy(pool_ref, buf, sem, page_base + pt_ref[b, p], slot, i))
            return c

        lax.fori_loop(0, HALF_PAGES + 1, body, 0)

    _gather_schedule(lambda s, slot: copies(s, slot, lambda cp: cp.start()),
                     lambda s, slot: copies(s, slot, lambda cp: cp.wait()))
    slot = pl.program_id(0) % 2

    half = NSA_KV_HEADS * NSA_HD

    r_out = lax.broadcasted_iota(jnp.int32, (PAGE_SIZE, 1), 0)
    t_in = lax.broadcasted_iota(jnp.int32, (1, PAGE_SIZE), 1)
    regroup = ((r_out % PAGE_ROWS) * CMP_STRIDE + r_out // PAGE_ROWS == t_in).astype(bf)

    def to_offset_rows(i, c):
        dst = pl.ds(pl.multiple_of(i * PAGE_ROWS, PAGE_ROWS), PAGE_ROWS)
        for kv in range(2):
            t = lax.dot_general(regroup, buf[slot, i, kv].astype(bf), (((1,), (1,)), ((), ())),
                                preferred_element_type=f32)
            for l in range(CMP_STRIDE):
                tok_s[kv, l, dst, :] = t[l * PAGE_ROWS:(l + 1) * PAGE_ROWS, :]
        return c

    lax.fori_loop(0, HALF_PAGES + 1, to_offset_rows, 0, unroll=5)
    acc = [jnp.zeros((rows, CMP_HID // 2), f32) for _ in range(2)]
    bias = [jnp.zeros((SUBLANES, CMP_HID // 2), f32) for _ in range(2)]
    for l in range(0, CMP_BLOCK, 2):
        first = l // CMP_STRIDE
        for kv in range(2):
            x = jnp.concatenate([tok_s[kv, l % CMP_STRIDE + d, first:first + rows, :] for d in range(2)],
                                axis=1).astype(bf)
            w = w1_ref[l // 2, kv]
            p = jnp.concatenate([pos_ref[l + d, :, kv * half:(kv + 1) * half] for d in range(2)], axis=1)
            acc[kv] = acc[kv] + jnp.dot(x, w, preferred_element_type=f32)
            bias[kv] = bias[kv] + jnp.dot(p.astype(bf), w, preferred_element_type=f32)
    hid = _gelu_tanh(jnp.concatenate(acc, axis=1) + jnp.concatenate(bias, axis=1)[0:1, :])
    o_ref[...] = jnp.dot(hid.astype(bf), w2_ref[...], preferred_element_type=f32)


def _compress_paged(page_table, pool, layer, cw):
    nbatch, n_pages = page_table.shape
    assert n_pages == 2 * HALF_PAGES
    rows = HALF_PAGES * PAGE_ROWS
    const = lambda shape: pl.BlockSpec(shape, lambda s, pt: (0,) * len(shape), pipeline_mode=pl.Buffered(1))
    return pl.pallas_call(
        functools.partial(_compress_paged_kernel, page_base=layer * (pool.shape[0] // DEPTH), n_pages=n_pages),
        grid_spec=pltpu.PrefetchScalarGridSpec(
            num_scalar_prefetch=1,
            grid=(2 * nbatch,),
            in_specs=[
                pl.BlockSpec(memory_space=pl.ANY),
                const((CMP_BLOCK, SUBLANES, NSA_KV_WIDTH)),
                const((CMP_BLOCK // 2, 2, NSA_KV_WIDTH, CMP_HID // 2)),
                const((CMP_HID, NSA_KV_WIDTH)),
            ],
            out_specs=pl.BlockSpec((rows, NSA_KV_WIDTH), lambda s, pt: (s, 0)),
            scratch_shapes=[
                pltpu.VMEM((2, HALF_PAGES + 1) + pool.shape[1:], jnp.float32),
                pltpu.SemaphoreType.DMA((2,)),
                pltpu.VMEM((2, CMP_STRIDE, (HALF_PAGES + 1) * PAGE_ROWS, NSA_KV_WIDTH // 2), jnp.float32),
            ]),
        out_shape=jax.ShapeDtypeStruct((2 * nbatch * rows, NSA_KV_WIDTH), jnp.float32),
        compiler_params=pltpu.CompilerParams(
            dimension_semantics=("arbitrary",), vmem_limit_bytes=VMEM_LIMIT_BYTES),
        name="nsa_compress_paged",
    )(page_table, pool, cw['pos'], cw['w1'], cw['w2'])


def _nsa_sample_kernel(pt_ref, pool_ref, qbd_ref, gate_ref, kc_ref, vc_ref, nslc_ref, wcache_ref, nwin_ref,
                       o_ref, buf, sem, sc_s, sel_s, m_s, l_s, acc_s, *, page_base, n_pages, n_sel, t_new):
    f32, bf = jnp.float32, jnp.bfloat16
    HD, LN = NSA_HD, SAMPLE_LANES
    half = NSA_KV_HEADS * HD
    past = n_pages * PAGE_SIZE
    contract_rows = (((0,), (0,)), ((), ()))

    def copies(step, slot, fn):
        def body(p, c):
            fn(_page_copy(pool_ref, buf, sem, page_base + pt_ref[step, p], slot, p))
            return c

        lax.fori_loop(0, n_pages, body, 0)

    _gather_schedule(lambda s, slot: copies(s, slot, lambda cp: cp.start()),
                     lambda s, slot: copies(s, slot, lambda cp: cp.wait()))
    slot = pl.program_id(0) % 2

    qbd = qbd_ref[0]
    lane = lax.broadcasted_iota(jnp.int32, (1, LN), 1)
    pos_l = past + lane % SQ
    group0 = lane < LN // NSA_KV_HEADS
    jrow = lax.broadcasted_iota(jnp.int32, (n_sel, 1), 0)
    jrow_f = jrow.astype(f32)

    def group_rows(full):
        return jnp.where(group0, full[0:HD, :], full[HD:2 * HD, :])

    def pv(v, pt):
        return group_rows(lax.dot_general(v.astype(bf), pt, contract_rows, preferred_element_type=f32))

    def scores_t(kt):
        return lax.dot_general(kt.astype(bf), qbd, contract_rows, preferred_element_type=f32)

    li = lax.broadcasted_iota(jnp.int32, (LN, LN), 0)
    lj = lax.broadcasted_iota(jnp.int32, (LN, LN), 1)
    same = ((li // (NSA_R * SQ) == lj // (NSA_R * SQ)) & (li % SQ == lj % SQ)).astype(bf)

    def head_sum(p):
        hi = p.astype(bf)
        r1 = p - hi.astype(f32)
        mid = r1.astype(bf)
        lo = (r1 - mid.astype(f32)).astype(bf)
        return (jnp.dot(hi, same, preferred_element_type=f32) + jnp.dot(mid, same, preferred_element_type=f32)
                + jnp.dot(lo, same, preferred_element_type=f32))

    s_c, mk_c = [], []
    m = jnp.full((1, LN), NEG, f32)
    for c in range(CMP_PER_SEL):
        s = jnp.dot(kc_ref[0, c * n_sel:(c + 1) * n_sel, :], qbd, preferred_element_type=f32)
        mk = jrow * SEL_BLOCK + (c * CMP_STRIDE + CMP_BLOCK - 1) <= pos_l
        s = jnp.where(mk, s, NEG)
        m = jnp.maximum(m, jnp.max(s, axis=0, keepdims=True))
        s_c.append(s)
        mk_c.append(mk)
    e_c = [jnp.where(mk_c[c], jnp.exp2(s_c[c] - m), 0.0) for c in range(CMP_PER_SEL)]
    l = e_c[0].sum(axis=0, keepdims=True)
    for c in range(1, CMP_PER_SEL):
        l = l + e_c[c].sum(axis=0, keepdims=True)
    inv = 1.0 / jnp.maximum(l, TINY)
    o_cmp = jnp.zeros((HD, LN), f32)
    pg = []
    for c in range(CMP_PER_SEL):
        p = e_c[c] * inv
        o_cmp = o_cmp + pv(vc_ref[0, c * n_sel:(c + 1) * n_sel, :], p.astype(bf))
        pg.append(head_sum(p))

    cur = pos_l // SEL_BLOCK
    forced = (jrow == 0) | (jrow == cur) | (jrow == cur - 1)
    allowed = jrow * SEL_BLOCK <= pos_l
    last = pg[CMP_PER_SEL - 1]
    prev = jnp.where(jrow == 0, 0.0, pltpu.roll(last, 1, 0))
    inner = pg[0]
    for c in range(1, CMP_PER_SEL - 1):
        inner = inner + pg[c]
    sc_s[...] = jnp.where(forced, FORCE, jnp.where(allowed, 2.0 * inner + last + prev, -1.0))
    sel_s[...] = jnp.zeros((n_sel, LN), f32)

    def pick(_, carry):
        s = sc_s[...]
        top = jnp.max(s, axis=0, keepdims=True)
        first = jnp.min(jnp.where(s == top, jrow_f, float(n_sel)), axis=0, keepdims=True)
        hit = jrow_f == first
        sc_s[...] = jnp.where(hit, -jnp.inf, s)
        sel_s[...] = jnp.where(hit, 1.0, sel_s[...])
        return carry

    lax.fori_loop(0, SEL_TOPN, pick, 0)
    sel_s[...] = jnp.where(allowed, sel_s[...], 0.0)

    def reset():
        m_s[...] = jnp.full((1, LN), M_INIT, f32)
        l_s[...] = jnp.zeros((1, LN), f32)
        acc_s[...] = jnp.zeros((HD, LN), f32)

    def update(s_blocks, pv_fn):
        m_old = m_s[...]
        m_new = m_old
        for s in s_blocks:
            m_new = jnp.maximum(m_new, jnp.max(s, axis=0, keepdims=True))
        alpha = jnp.exp2(m_old - m_new)
        e_blocks = [jnp.exp2(s - m_new) for s in s_blocks]
        l_new = l_s[...] * alpha
        for e in e_blocks:
            l_new = l_new + e.sum(axis=0, keepdims=True)
        pt = e_blocks[0] if len(e_blocks) == 1 else jnp.concatenate(e_blocks, axis=0)
        acc_s[...] = acc_s[...] * alpha + pv_fn(pt.astype(bf))
        m_s[...] = m_new
        l_s[...] = l_new

    def finish():
        return acc_s[...] * (1.0 / jnp.maximum(l_s[...], TINY))

    def scores(kv):
        return jnp.dot(kv[:, 0:half].astype(bf), qbd, preferred_element_type=f32)

    reset()
    pages_per_tile = SAMPLE_TILE // PAGE_SIZE
    blk_per_page = PAGE_SIZE // SEL_BLOCK
    blk_per_tile = SAMPLE_TILE // SEL_BLOCK

    def slc_body(kt, carry):
        selb = sel_s[pl.ds(pl.multiple_of(kt * blk_per_tile, blk_per_tile), blk_per_tile), :]
        blocks = []
        for j in range(pages_per_tile):
            s = scores_t(buf[slot, kt * pages_per_tile + j, 0])
            for h in range(blk_per_page):
                i = j * blk_per_page + h
                blocks.append(jnp.where(selb[i:i + 1, :] > 0.5, s[h * SEL_BLOCK:(h + 1) * SEL_BLOCK, :], NEG))

        def pv_pages(pt):
            full = jnp.zeros((2 * HD, LN), f32)
            for j in range(pages_per_tile):
                full = full + jnp.dot(buf[slot, kt * pages_per_tile + j, 1].astype(bf),
                                      pt[j * PAGE_SIZE:(j + 1) * PAGE_SIZE, :], preferred_element_type=f32)
            return group_rows(full)

        update(blocks, pv_pages)
        return carry

    lax.fori_loop(0, past // SAMPLE_TILE, slc_body, 0)
    rows_new = lax.broadcasted_iota(jnp.int32, (SEL_BLOCK, 1), 0)
    kv = nslc_ref[0]
    keep = (sel_s[past // SEL_BLOCK:past // SEL_BLOCK + 1, :] > 0.5) & (past + rows_new <= pos_l)
    update([jnp.where(keep, scores(kv), NEG)], functools.partial(pv, kv[:, half:]))
    o_slc = finish()

    reset()
    wb = wcache_ref.shape[-1]
    rel = pos_l - (past - wb + lax.broadcasted_iota(jnp.int32, (wb, 1), 0))
    update([jnp.where((rel >= 0) & (rel <= WINDOW), scores_t(wcache_ref[0, 0, 0]), NEG)],
           lambda pt: group_rows(jnp.dot(wcache_ref[0, 0, 1].astype(bf), pt, preferred_element_type=f32)))
    kv = nwin_ref[0]
    rel = pos_l - (past + rows_new)
    update([jnp.where((rel >= 0) & (rel <= WINDOW) & (rows_new < t_new), scores(kv), NEG)],
           functools.partial(pv, kv[:, half:]))
    o_win = finish()

    gate = jax.nn.sigmoid(gate_ref[0])
    o_ref[0] = gate[0:1, :] * o_cmp + gate[1:2, :] * o_slc + gate[2:3, :] * o_win


def _pad_rows(a, rows):
    return jnp.pad(a, ((0, 0), (0, rows - a.shape[1]), (0, 0)))


def _nsa_sample(proj, q, kv_cmp, kv_slc, kv_win, gates, pool_cmp, pool_slc, page_table, win_cache, layer, cmp_w):
    B, Tn = q.shape[:2]
    G, R, HD = NSA_KV_HEADS, NSA_R, NSA_HD
    n_pages = page_table.shape[1]
    n_pool = pool_cmp.shape[1]
    past = n_pages * PAGE_SIZE
    wb = win_cache.shape[2]
    assert Tn <= SQ and Tn <= SEL_BLOCK and past % SAMPLE_TILE == 0 and wb % SUBLANES == 0
    bf = jnp.bfloat16
    half = G * HD
    kvc = _compress_paged(page_table, _page_view(pool_cmp), layer, cmp_w)
    n_blk = past // CMP_STRIDE
    n_sel = _round_up(past // SEL_BLOCK + 1, BF16_SUBLANES)
    kvp = kvc.reshape(B, n_blk // CMP_PER_SEL, CMP_PER_SEL, NSA_KV_WIDTH).transpose(0, 2, 1, 3)
    kvp = jnp.pad(kvp, ((0, 0), (0, 0), (0, n_sel - n_blk // CMP_PER_SEL), (0, 0)))
    kvp = kvp.reshape(B, CMP_PER_SEL * n_sel, NSA_KV_WIDTH).astype(bf)
    kc, vc = kvp[..., :half], kvp[..., half:]
    qt = jnp.pad((q * QK_SCALE).reshape(B, Tn, G, R, HD), ((0, 0), (0, SQ - Tn), (0, 0), (0, 0), (0, 0)))
    qt = qt.transpose(0, 2, 4, 3, 1)
    qbd = jnp.einsum('bgdrq,gh->bgdhrq', qt, jnp.eye(G, dtype=jnp.float32)).reshape(B, half, SAMPLE_LANES).astype(bf)
    gate = jnp.pad(gates.reshape(B, Tn, G, R, 3), ((0, 0), (0, SQ - Tn), (0, 0), (0, 0), (0, 0)))
    gate = gate.transpose(0, 4, 2, 3, 1).reshape(B, 3, SAMPLE_LANES)
    nslc = _pad_rows(kv_slc.reshape(B, Tn, NSA_KV_WIDTH), SEL_BLOCK)
    nwin = _pad_rows(kv_win.reshape(B, Tn, NSA_KV_WIDTH), SEL_BLOCK)
    per_b = lambda shape: pl.BlockSpec((1,) + shape, lambda b, pt: (b,) + (0,) * len(shape))
    out = pl.pallas_call(
        functools.partial(_nsa_sample_kernel, page_base=layer * n_pool, n_pages=n_pages, n_sel=n_sel, t_new=Tn),
        grid_spec=pltpu.PrefetchScalarGridSpec(
            num_scalar_prefetch=1,
            grid=(B,),
            in_specs=[
                pl.BlockSpec(memory_space=pl.ANY),
                per_b((half, SAMPLE_LANES)), per_b((3, SAMPLE_LANES)),
                per_b((CMP_PER_SEL * n_sel, half)), per_b((CMP_PER_SEL * n_sel, half)),
                per_b((SEL_BLOCK, NSA_KV_WIDTH)),
                pl.BlockSpec((1, 1, 2, half, wb), lambda b, pt: (layer, b, 0, 0, 0)),
                per_b((SEL_BLOCK, NSA_KV_WIDTH)),
            ],
            out_specs=per_b((HD, SAMPLE_LANES)),
            scratch_shapes=[
                pltpu.VMEM((2, n_pages, 2, half, PAGE_SIZE), jnp.float32),
                pltpu.SemaphoreType.DMA((2,)),
                pltpu.VMEM((n_sel, SAMPLE_LANES), jnp.float32),
                pltpu.VMEM((n_sel, SAMPLE_LANES), jnp.float32),
                pltpu.VMEM((1, SAMPLE_LANES), jnp.float32),
                pltpu.VMEM((1, SAMPLE_LANES), jnp.float32),
                pltpu.VMEM((HD, SAMPLE_LANES), jnp.float32),
            ]),
        out_shape=jax.ShapeDtypeStruct((B, HD, SAMPLE_LANES), jnp.float32),
        compiler_params=pltpu.CompilerParams(
            dimension_semantics=("arbitrary",), vmem_limit_bytes=VMEM_LIMIT_BYTES),
        name="nsa_sample_attention",
    )(page_table, _page_view(pool_slc), qbd, gate, kc, vc, nslc,
      win_cache.transpose(0, 1, 3, 4, 5, 2).reshape(DEPTH, B, 2, half, wb), nwin)
    o = out.reshape(B, HD, G, R, SQ)[..., :Tn].transpose(0, 4, 2, 3, 1).reshape(B, Tn, NSA_WIDTH)
    win_all = jnp.concatenate([win_cache[layer], kv_win.astype(win_cache.dtype)], axis=1)
    return o, win_all[:, -min(WINDOW, wb + Tn):]


GLA_LANES = GLA_HEADS * GLA_DK
GLA_KERNEL_CHUNK = 32
GLA_BLOCK_ROWS = 512


def _head_block_mask(dtype):
    r = lax.broadcasted_iota(jnp.int32, (GLA_LANES, GLA_LANES), 0) // GLA_DK
    c = lax.broadcasted_iota(jnp.int32, (GLA_LANES, GLA_LANES), 1) // GLA_DK
    return (r == c).astype(dtype)


def _gla_kernel(q_ref, k_ref, v_ref, gg_ref, ga_ref, wa_ref, ba_ref, gn_ref, st0_ref, o_ref, st_ref,
                st_s, kp_s, bp_s, vp_s, *, c, n_chunks, valid_rows):
    f32, bf = jnp.float32, jnp.bfloat16

    @pl.when(pl.program_id(1) == 0)
    def _():
        st_s[...] = st0_ref[0]
        zeros = jnp.zeros((c, GLA_LANES), f32)
        kp_s[0:c, :] = zeros
        bp_s[0:c, :] = zeros
        vp_s[0:c, :] = zeros

    row = lax.broadcasted_iota(jnp.int32, (c, 1), 0)
    tril = (lax.broadcasted_iota(jnp.int32, (c, c), 0) >= lax.broadcasted_iota(jnp.int32, (c, c), 1)).astype(f32)
    ones_blk = _head_block_mask(bf)
    blk_f32 = _head_block_mask(f32)
    contract_last = (((1,), (1,)), ((), ()))
    contract_rows = (((0,), (0,)), ((), ()))

    def head_sum(x):
        hi = x.astype(bf)
        lo = (x - hi.astype(f32)).astype(bf)
        return (jnp.dot(hi, ones_blk, preferred_element_type=f32)
                + jnp.dot(lo, ones_blk, preferred_element_type=f32))

    def chunk(ch, carry):
        rows = pl.ds(pl.multiple_of(ch * c, c), c)
        q = q_ref[rows, :] * (GLA_DK ** -0.5)
        k = k_ref[rows, :]
        v = v_ref[rows, :]
        z = jnp.dot(ga_ref[rows, :].astype(bf), wa_ref[...], preferred_element_type=f32) + ba_ref[...]
        la = (jnp.minimum(z, 0.0) - jnp.log1p(jnp.exp(-jnp.abs(z)))) / GLA_GATE_TEMP
        if valid_rows < c:
            la = jnp.where(row < valid_rows, la, 0.0)
        b = jnp.dot(tril, la, preferred_element_type=f32, precision=lax.Precision.HIGHEST)
        st = st_s[...]
        o = lax.dot_general((q * jnp.exp(b)).astype(bf), st.astype(bf), contract_last,
                            preferred_element_type=f32)
        kp_s[c:2 * c, :] = k
        bp_s[c:2 * c, :] = b
        vp_s[c:2 * c, :] = v
        terms = [q * k]
        for d in range(1, c):
            ok = row >= d
            kr = kp_s[c - d:2 * c - d, :]
            br = bp_s[c - d:2 * c - d, :]
            terms.append(jnp.where(ok, q * kr * jnp.exp(jnp.where(ok, b - br, 0.0)), 0.0))
        att = jnp.dot(jnp.concatenate(terms, axis=0).astype(bf), ones_blk, preferred_element_type=f32)
        for d in range(c):
            vr = v if d == 0 else vp_s[c - d:2 * c - d, :]
            o = o + att[d * c:(d + 1) * c, :] * vr
        ms = head_sum(o * o) * (1.0 / GLA_DV)
        g = gg_ref[rows, :]
        o_ref[rows, :] = o * lax.rsqrt(ms + EPS) * gn_ref[...] * (g * jax.nn.sigmoid(g))
        b_last = b[c - 1:c, :]
        ke = k * jnp.exp(b_last - b)
        upd = lax.dot_general(v.astype(bf), ke.astype(bf), contract_rows, preferred_element_type=f32)
        st_s[...] = st * jnp.exp(b_last) + upd * blk_f32
        return carry

    lax.fori_loop(0, n_chunks, chunk, 0, unroll=min(2, n_chunks))
    st_ref[0] = st_s[...]


def _gla(proj, nbatch, tb, c, valid_rows, wa_pad, ba, gn, st0):
    m = proj.shape[0]
    nblk = m // nbatch // tb
    colblk = lambda p, w: PROJ_OFF[p] // w
    row_map = lambda j: (lambda b, i: (b * nblk + i, j))
    return pl.pallas_call(
        functools.partial(_gla_kernel, c=c, n_chunks=tb // c, valid_rows=valid_rows),
        grid=(nbatch, nblk),
        in_specs=[
            pl.BlockSpec((tb, GLA_LANES), row_map(colblk(0, GLA_LANES))),
            pl.BlockSpec((tb, GLA_LANES), row_map(colblk(1, GLA_LANES))),
            pl.BlockSpec((tb, GLA_LANES), row_map(colblk(2, GLA_LANES))),
            pl.BlockSpec((tb, GLA_LANES), row_map(colblk(3, GLA_LANES))),
            pl.BlockSpec((tb, LANES), row_map(colblk(4, LANES))),
            pl.BlockSpec((LANES, GLA_LANES), lambda b, i: (0, 0)),
            pl.BlockSpec((1, GLA_LANES), lambda b, i: (0, 0)),
            pl.BlockSpec((1, GLA_LANES), lambda b, i: (0, 0)),
            pl.BlockSpec((1, GLA_LANES, GLA_LANES), lambda b, i: (b, 0, 0)),
        ],
        out_specs=[pl.BlockSpec((tb, GLA_LANES), lambda b, i: (b * nblk + i, 0)),
                   pl.BlockSpec((1, GLA_LANES, GLA_LANES), lambda b, i: (b, 0, 0))],
        out_shape=[jax.ShapeDtypeStruct((m, GLA_LANES), jnp.float32),
                   jax.ShapeDtypeStruct((nbatch, GLA_LANES, GLA_LANES), jnp.float32)],
        scratch_shapes=[pltpu.VMEM((GLA_LANES, GLA_LANES), jnp.float32),
                        pltpu.VMEM((2 * c, GLA_LANES), jnp.float32),
                        pltpu.VMEM((2 * c, GLA_LANES), jnp.float32),
                        pltpu.VMEM((2 * c, GLA_LANES), jnp.float32)],
        compiler_params=pltpu.CompilerParams(
            dimension_semantics=("arbitrary", "arbitrary"), vmem_limit_bytes=VMEM_LIMIT_BYTES),
        name="gla_scan",
    )(proj, proj, proj, proj, proj, wa_pad, ba, gn, st0)


def _gla_state_in(s0):
    eye = jnp.eye(GLA_HEADS, dtype=jnp.float32)
    return jnp.einsum('bhde,hg->bhegd', s0.astype(jnp.float32), eye).reshape(-1, GLA_LANES, GLA_LANES)


def _gla_state_out(st):
    blocks = [st[:, h * GLA_DV:(h + 1) * GLA_DV, h * GLA_DK:(h + 1) * GLA_DK] for h in range(GLA_HEADS)]
    return jnp.stack(blocks, axis=1).transpose(0, 1, 3, 2)


def _nsa_prompt(proj, q, kv_cmp, kv_slc, kv_win, gates, cmp_w):
    B, T = kv_cmp.shape[:2]
    assert B == 1 and T % SLC_TILE == 0 and T >= WIN_TILE
    n_rows = T // CMP_STRIDE
    kvc = _compress(kv_cmp.reshape(n_rows, CMP_ROW), cmp_w, min(256, n_rows))
    o = _nsa_prompt_attention(proj, kvc, kv_slc.reshape(T, NSA_KV_WIDTH), kv_win.reshape(T, NSA_KV_WIDTH))
    return o.reshape(B, T, NSA_WIDTH), kv_win[:, -min(WINDOW, T):]


def _expand_rows(v, t):
    if v.shape[0] == 1:
        return v
    return jnp.repeat(v, t, axis=0)


def _trunk_layer(x, mod, lw, gla_s0, sc_hist, ffn_hist, nsa_apply, tm, final, g_final):
    B, T, _ = x.shape
    m = B * T
    grouped = B > 1
    ssh1, ssc1, sgt1, ssh2, ssc2, sgt2 = [_expand_rows(v, T) for v in jnp.split(mod, 6, axis=-1)]
    x2 = x.reshape(m, D_MODEL)
    proj = _in_proj(x2, lw['norm_mix'], ssc1, ssh1, lw['w_in'], tm).reshape(B, T, PROJ_WIDTH)
    gq, gk, gv, gg, ga, sb, scc, shh, nq, ncmp, nslc, nwin, ngate = [_proj_piece(proj, p) for p in range(13)]
    heads = lambda a, d: a.reshape(B, T, -1, d)
    if grouped:
        t_pad = _round_up(T, SUBLANES)
        gla_in = jnp.pad(proj, ((0, 0), (0, t_pad - T), (0, 0))).reshape(B * t_pad, PROJ_WIDTH)
        tb = chunk = t_pad
    else:
        t_pad, gla_in, tb, chunk = T, proj.reshape(m, PROJ_WIDTH), GLA_BLOCK_ROWS, GLA_KERNEL_CHUNK
    o_gla, st_gla = _gla(gla_in, B, tb, chunk, T if grouped else chunk,
                         lw['gla_wa'], lw['gla_ba'], lw['gla_norm'], _gla_state_in(gla_s0))
    o_gla = o_gla.reshape(B, t_pad, GLA_WIDTH)[:, :T]
    s_gla = _gla_state_out(st_gla)
    kvr = lambda a: a.reshape(B, T, 2, NSA_KV_HEADS, NSA_HD)
    kv_cmp, kv_slc, kv_win = kvr(ncmp), kvr(nslc), kvr(nwin)
    proj2 = proj.reshape(m, PROJ_WIDTH)
    o_nsa, win_state = nsa_apply(proj2, heads(nq, NSA_HD), kv_cmp, kv_slc, kv_win, heads(ngate, 3))

    def conv_hist(h):
        c = h.shape[-1]
        if grouped:
            pad = jnp.zeros((B, T - 1, c), jnp.float32)
            h1 = jnp.concatenate([h[:, 1:2], pad], axis=1).reshape(m, c)
            h2 = jnp.concatenate([h[:, 0:2], pad[:, 1:]], axis=1).reshape(m, c)
            return (h1, h2)
        return jnp.concatenate([jnp.zeros((HIST_ROWS - (CONV_W - 1), c), jnp.float32), h[0]], axis=0)

    def conv_state(st):
        c = st.shape[-1]
        return st.reshape(B, T, c)[:, -(CONV_W - 1):] if grouped else st[None, -(CONV_W - 1):]

    group = T if grouped else 0
    x2, sc_st = _out_proj(x2, o_gla.reshape(m, GLA_WIDTH), proj2, o_nsa.reshape(m, NSA_WIDTH), sgt1,
                          lw['w_out'], lw['sc_conv'], conv_hist(sc_hist), tm, group)
    sc_state = conv_state(sc_st)
    y, st = _ffn(x2, lw['norm_ffn'], ssc2, ssh2, sgt2, lw['ffn_up'], lw['ffn_conv'], lw['ffn_down'],
                 g_final, conv_hist(ffn_hist), tm, group, final)
    ffn_state = conv_state(st)
    return (y.reshape(B, T, D_MODEL), kv_cmp, kv_slc, win_state, s_gla.astype(gla_s0.dtype), sc_state, ffn_state)


def kernel(x_prompt, x_sample, cache_nsa_cmp, cache_nsa_slc, cache_nsa_win, state_gla, state_shortconv, state_ffn_conv, page_table, c_prompt, c_sample, mod_w, mod_b, norm_mix, norm_ffn, w_in, gla_wa2, gla_ba, gla_norm, sc_conv, nsa_cmp_pos, nsa_cmp_w1, nsa_cmp_w2, w_out, ffn_up, ffn_conv, ffn_down, norm_final):
    xp, xs = x_prompt, x_sample
    bp, bs = xp.shape[0], xs.shape[0]
    assert bp == 1 and xs.shape[1] == 4
    c_rows = _round_up(bp + bs, SUBLANES)
    c_all = jnp.concatenate([c_prompt, c_sample, jnp.zeros((c_rows - bp - bs, D_MODEL), jnp.float32)], axis=0)
    mod_all = _modulation(c_all, mod_w, mod_b)
    g_final = norm_final.reshape(1, D_MODEL)
    outs = [[] for _ in range(12)]
    for l in range(DEPTH):
        lw = dict(
            norm_mix=norm_mix[l].reshape(1, D_MODEL), norm_ffn=norm_ffn[l].reshape(1, D_MODEL),
            w_in=_pack_w_in(w_in[l]),
            gla_wa=jnp.zeros((LANES, GLA_LANES), jnp.bfloat16).at[:GLA_GATE_RANK].set(
                gla_wa2[l].astype(jnp.bfloat16)),
            gla_ba=gla_ba[l].reshape(1, GLA_LANES),
            gla_norm=jnp.tile(gla_norm[l], GLA_HEADS).reshape(1, GLA_LANES),
            sc_conv=sc_conv[l], w_out=w_out[l].astype(jnp.bfloat16),
            ffn_up=ffn_up[l].astype(jnp.bfloat16), ffn_conv=ffn_conv[l],
            ffn_down=ffn_down[l].astype(jnp.bfloat16))
        cmp_params = (nsa_cmp_pos[l], nsa_cmp_w1[l], nsa_cmp_w2[l])
        cmp_w = _compress_weights(*cmp_params)
        final = l == DEPTH - 1
        res_p = _trunk_layer(
            xp, mod_all[l, 0:bp], lw,
            jnp.zeros((bp, GLA_HEADS, GLA_DK, GLA_DV), xp.dtype),
            jnp.zeros((bp, CONV_W - 1, SC_WIDTH), xp.dtype),
            jnp.zeros((bp, CONV_W - 1, 2 * D_FF), xp.dtype),
            functools.partial(_nsa_prompt, cmp_w=cmp_w), 256, final, g_final)
        res_s = _trunk_layer(
            xs, mod_all[l, bp:bp + bs], lw, state_gla[l], state_shortconv[l], state_ffn_conv[l],
            functools.partial(_nsa_sample, pool_cmp=cache_nsa_cmp, pool_slc=cache_nsa_slc,
                              page_table=page_table, win_cache=cache_nsa_win, layer=l,
                              cmp_w=cmp_w), bs * xs.shape[1], final, g_final)
        xp, xs = res_p[0], res_s[0]
        for k in range(6):
            outs[2 * k].append(res_p[k + 1])
            outs[2 * k + 1].append(res_s[k + 1])
    return (xp, xs) + tuple(jnp.stack(o) for o in outs)
```

```python
import functools

import jax
import jax.numpy as jnp
from jax import lax
from jax.experimental import pallas as pl
from jax.experimental.pallas import tpu as pltpu

D_MODEL = 1024
DEPTH = 2
PAGE_SIZE = 128
GLA_HEADS = 4
GLA_DK = D_MODEL // 16
GLA_DV = D_MODEL // 16
GLA_WIDTH = GLA_HEADS * GLA_DV
GLA_GATE_RANK = 16
GLA_GATE_TEMP = 16.0
GLA_CHUNK = 64
SC_WIDTH = D_MODEL // 4
CONV_W = 3
NSA_HEADS = 8
NSA_KV_HEADS = 2
NSA_HD = D_MODEL // 16
NSA_WIDTH = NSA_HEADS * NSA_HD
NSA_KV_WIDTH = 2 * NSA_KV_HEADS * NSA_HD
CMP_STRIDE = 16
CMP_BLOCK = 2 * CMP_STRIDE
CMP_HIDDEN = 128
SEL_BLOCK = 64
SEL_TOPN = 16
WINDOW = 512
Q_BLOCK = 128
D_FF = 2816
EPS = 1e-6
NEG = -1e30
TINY = 1e-30
FORCE = 1e9

IN_SIZES = (
    GLA_HEADS * GLA_DK, GLA_HEADS * GLA_DK, GLA_WIDTH, GLA_WIDTH, GLA_GATE_RANK,
    SC_WIDTH, SC_WIDTH, SC_WIDTH,
    NSA_WIDTH, NSA_KV_WIDTH, NSA_KV_WIDTH, NSA_KV_WIDTH, NSA_HEADS * 3,
)

LANES = 128
SUBLANES = 8
BF16_SUBLANES = 16
VMEM_LIMIT_BYTES = 56 * 1024 * 1024

PROJ_ORDER = (0, 1, 2, 3, 8, 5, 6, 7, 9, 10, 11, 4, 12)


def _round_up(n, m):
    return -(-n // m) * m


def _proj_layout():
    src, acc = [], 0
    for s in IN_SIZES:
        src.append(acc)
        acc += s
    offs, dst = {}, 0
    for p in PROJ_ORDER:
        offs[p] = dst
        dst += _round_up(IN_SIZES[p], LANES)
    return src, offs, dst


PROJ_SRC, PROJ_OFF, PROJ_WIDTH = _proj_layout()


def _pack_w_in(w_in):
    out = jnp.zeros((D_MODEL, PROJ_WIDTH), jnp.bfloat16)
    for p in PROJ_ORDER:
        piece = w_in[:, PROJ_SRC[p]:PROJ_SRC[p] + IN_SIZES[p]].astype(jnp.bfloat16)
        out = lax.dynamic_update_slice(out, piece, (0, PROJ_OFF[p]))
    return out


def _proj_piece(proj, p):
    return proj[..., PROJ_OFF[p]:PROJ_OFF[p] + IN_SIZES[p]]


def _mod_kernel(c_ref, w_ref, b_ref, o_ref):
    c = c_ref[...]
    a = c * jax.nn.sigmoid(c)
    o_ref[0] = jnp.dot(a, w_ref[0], preferred_element_type=jnp.float32,
                       precision=lax.Precision.HIGHEST) + b_ref[0]


def _modulation(c_all, mod_w, mod_b):
    rows = c_all.shape[0]
    tn = 1024
    n = mod_w.shape[-1]
    return pl.pallas_call(
        _mod_kernel,
        grid=(DEPTH, n // tn),
        in_specs=[
            pl.BlockSpec((rows, D_MODEL), lambda l, j: (0, 0)),
            pl.BlockSpec((1, D_MODEL, tn), lambda l, j: (l, 0, j)),
            pl.BlockSpec((1, 1, tn), lambda l, j: (l, 0, j)),
        ],
        out_specs=pl.BlockSpec((1, rows, tn), lambda l, j: (l, 0, j)),
        out_shape=jax.ShapeDtypeStruct((DEPTH, rows, n), jnp.float32),
        name="adaln_modulation",
    )(c_all, mod_w, mod_b.reshape(DEPTH, 1, n))


def _norm_mod(x, g, sc, sh):
    r = lax.rsqrt(jnp.mean(x * x, axis=-1, keepdims=True) + EPS)
    return (x * r * g) * (1.0 + sc) + sh


def _in_proj_kernel(x_ref, g_ref, sc_ref, sh_ref, w_ref, o_ref):
    h = _norm_mod(x_ref[...], g_ref[...], sc_ref[...], sh_ref[...])
    o_ref[...] = jnp.dot(h.astype(jnp.bfloat16), w_ref[...], preferred_element_type=jnp.float32)


def _row_spec(tm, per_row):
    if per_row:
        return pl.BlockSpec((tm, D_MODEL), lambda i: (i, 0))
    return pl.BlockSpec((1, D_MODEL), lambda i: (0, 0))


def _resident(shape):
    return pl.BlockSpec(shape, lambda i: (0,) * len(shape), pipeline_mode=pl.Buffered(1))


def _in_proj(x, g, sc, sh, w_packed, tm):
    m = x.shape[0]
    per_row = sc.shape[0] != 1
    return pl.pallas_call(
        _in_proj_kernel,
        grid=(m // tm,),
        in_specs=[
            pl.BlockSpec((tm, D_MODEL), lambda i: (i, 0)),
            _resident((1, D_MODEL)),
            _row_spec(tm, per_row),
            _row_spec(tm, per_row),
            _resident((D_MODEL, PROJ_WIDTH)),
        ],
        out_specs=pl.BlockSpec((tm, PROJ_WIDTH), lambda i: (i, 0)),
        out_shape=jax.ShapeDtypeStruct((m, PROJ_WIDTH), jnp.float32),
        compiler_params=pltpu.CompilerParams(
            dimension_semantics=("arbitrary",), vmem_limit_bytes=VMEM_LIMIT_BYTES),
        name="norm_in_proj",
    )(x, g, sc, sh, w_packed)


HIST_ROWS = SUBLANES


def _out_proj_kernel(*refs, tm, group):
    grouped = group > 0
    if grouped:
        (x_ref, gla_ref, sb_ref, scc_ref, shh_ref, nsa_ref, gt_ref, w_ref, cw_ref, h1_ref, h2_ref,
         o_ref, st_ref, u_s) = refs
    else:
        (x_ref, gla_ref, sb_ref, scc_ref, shh_ref, nsa_ref, gt_ref, w_ref, cw_ref, h0_ref,
         o_ref, st_ref, u_s) = refs

        @pl.when(pl.program_id(0) == 0)
        def _():
            u_s[0:HIST_ROWS, :] = h0_ref[...]

    u = scc_ref[...] * shh_ref[...]
    u_s[HIST_ROWS:HIST_ROWS + tm, :] = u
    p1 = u_s[HIST_ROWS - 1:HIST_ROWS - 1 + tm, :]
    p2 = u_s[HIST_ROWS - 2:HIST_ROWS - 2 + tm, :]
    if grouped:
        t = lax.broadcasted_iota(jnp.int32, (tm, 1), 0) % group
        p1 = jnp.where(t == 0, h1_ref[...], p1)
        p2 = jnp.where(t <= 1, h2_ref[...], p2)
    o_sc = sb_ref[...] * (cw_ref[0:1, :] * p2 + cw_ref[1:2, :] * p1 + cw_ref[2:3, :] * u)
    mix = jnp.concatenate([gla_ref[...], o_sc, nsa_ref[...]], axis=1).astype(jnp.bfloat16)
    o_ref[...] = x_ref[...] + gt_ref[...] * jnp.dot(mix, w_ref[...], preferred_element_type=jnp.float32)
    if grouped:
        st_ref[...] = u
    else:
        tail = u_s[tm:tm + HIST_ROWS, :]
        st_ref[...] = tail
        u_s[0:HIST_ROWS, :] = tail


def _out_proj(x, o_gla, proj, o_nsa, gt, w_bf16, cw, hist, tm, group):
    m = x.shape[0]
    per_row = gt.shape[0] != 1
    grouped = group > 0
    sc_cols = lambda p: pl.BlockSpec((tm, SC_WIDTH), lambda i: (i, PROJ_OFF[p] // SC_WIDTH))
    in_specs = [
        pl.BlockSpec((tm, D_MODEL), lambda i: (i, 0)),
        pl.BlockSpec((tm, GLA_WIDTH), lambda i: (i, 0)),
        sc_cols(5), sc_cols(6), sc_cols(7),
        pl.BlockSpec((tm, NSA_WIDTH), lambda i: (i, 0)),
        _row_spec(tm, per_row),
        _resident((D_MODEL, D_MODEL)),
        _resident((CONV_W, SC_WIDTH)),
    ]
    if grouped:
        assert m == tm
        in_specs += [_resident((tm, SC_WIDTH)), _resident((tm, SC_WIDTH))]
        hist_args, st_rows = tuple(hist), tm
    else:
        in_specs += [_resident((HIST_ROWS, SC_WIDTH))]
        hist_args, st_rows = (hist,), HIST_ROWS
    return pl.pallas_call(
        functools.partial(_out_proj_kernel, tm=tm, group=group),
        grid=(m // tm,),
        in_specs=in_specs,
        out_specs=[pl.BlockSpec((tm, D_MODEL), lambda i: (i, 0)),
                   pl.BlockSpec((st_rows, SC_WIDTH), lambda i: (0, 0))],
        out_shape=[jax.ShapeDtypeStruct((m, D_MODEL), jnp.float32),
                   jax.ShapeDtypeStruct((st_rows, SC_WIDTH), jnp.float32)],
        scratch_shapes=[pltpu.VMEM((HIST_ROWS + tm, SC_WIDTH), jnp.float32)],
        compiler_params=pltpu.CompilerParams(
            dimension_semantics=("arbitrary",), vmem_limit_bytes=VMEM_LIMIT_BYTES),
        name="shortconv_out_proj",
    )(x, o_gla, proj, proj, proj, o_nsa, gt, w_bf16, cw, *hist_args)


FFN_UP_CHUNK = 512
FFN_ACT_CHUNK = 256


def _ffn_kernel(*refs, tm, group, final):
    grouped = group > 0
    if grouped:
        (x_ref, g_ref, sc_ref, sh_ref, gt_ref, wup_ref, cw_ref, wdn_ref, gf_ref,
         h1_ref, h2_ref, o_ref, st_ref, up_s) = refs
    else:
        (x_ref, g_ref, sc_ref, sh_ref, gt_ref, wup_ref, cw_ref, wdn_ref, gf_ref,
         h0_ref, o_ref, st_ref, up_s) = refs

        @pl.when(pl.program_id(0) == 0)
        def _():
            up_s[0:HIST_ROWS, :] = h0_ref[...]

    x = x_ref[...]
    h = _norm_mod(x, g_ref[...], sc_ref[...], sh_ref[...]).astype(jnp.bfloat16)
    for c in range(2 * D_FF // FFN_UP_CHUNK):
        cols = slice(c * FFN_UP_CHUNK, (c + 1) * FFN_UP_CHUNK)
        up_s[HIST_ROWS:HIST_ROWS + tm, cols] = jnp.dot(
            h, wup_ref[:, cols], preferred_element_type=jnp.float32)

    if grouped:
        t = lax.broadcasted_iota(jnp.int32, (tm, 1), 0) % group

    def conv(cols):
        cur = up_s[HIST_ROWS:HIST_ROWS + tm, cols]
        p1 = up_s[HIST_ROWS - 1:HIST_ROWS - 1 + tm, cols]
        p2 = up_s[HIST_ROWS - 2:HIST_ROWS - 2 + tm, cols]
        if grouped:
            p1 = jnp.where(t == 0, h1_ref[:, cols], p1)
            p2 = jnp.where(t <= 1, h2_ref[:, cols], p2)
        return cw_ref[0:1, cols] * p2 + cw_ref[1:2, cols] * p1 + cw_ref[2:3, cols] * cur

    acc = jnp.zeros((tm, D_MODEL), jnp.float32)
    for c in range(D_FF // FFN_ACT_CHUNK):
        a = conv(slice(c * FFN_ACT_CHUNK, (c + 1) * FFN_ACT_CHUNK))
        b = conv(slice(D_FF + c * FFN_ACT_CHUNK, D_FF + (c + 1) * FFN_ACT_CHUNK))
        act = (a * jax.nn.sigmoid(a) * b).astype(jnp.bfloat16)
        acc = acc + jnp.dot(act, wdn_ref[c * FFN_ACT_CHUNK:(c + 1) * FFN_ACT_CHUNK, :],
                            preferred_element_type=jnp.float32)
    y = x + gt_ref[...] * acc
    if final:
        r = lax.rsqrt(jnp.mean(y * y, axis=-1, keepdims=True) + EPS)
        y = y * r * gf_ref[...]
    o_ref[...] = y

    if grouped:
        st_ref[...] = up_s[HIST_ROWS:HIST_ROWS + tm, :]
    else:
        tail = up_s[tm:tm + HIST_ROWS, :]
        st_ref[...] = tail
        up_s[0:HIST_ROWS, :] = tail


def _ffn(x, g, sc, sh, gt, wup, cw, wdn, g_final, hist, tm, group, final):
    m = x.shape[0]
    grouped = group > 0
    per_row = sc.shape[0] != 1
    ff2 = 2 * D_FF
    in_specs = [
        pl.BlockSpec((tm, D_MODEL), lambda i: (i, 0)),
        _resident((1, D_MODEL)),
        _row_spec(tm, per_row), _row_spec(tm, per_row), _row_spec(tm, per_row),
        _resident((D_MODEL, ff2)),
        _resident((CONV_W, ff2)),
        _resident((D_FF, D_MODEL)),
        _resident((1, D_MODEL)),
    ]
    if grouped:
        assert m == tm
        in_specs += [_resident((tm, ff2)), _resident((tm, ff2))]
        hist_args = tuple(hist)
        st_rows = tm
    else:
        in_specs += [_resident((HIST_ROWS, ff2))]
        hist_args = (hist,)
        st_rows = HIST_ROWS
    return pl.pallas_call(
        functools.partial(_ffn_kernel, tm=tm, group=group, final=final),
        grid=(m // tm,),
        in_specs=in_specs,
        out_specs=[pl.BlockSpec((tm, D_MODEL), lambda i: (i, 0)),
                   pl.BlockSpec((st_rows, ff2), lambda i: (0, 0))],
        out_shape=[jax.ShapeDtypeStruct((m, D_MODEL), jnp.float32),
                   jax.ShapeDtypeStruct((st_rows, ff2), jnp.float32)],
        scratch_shapes=[pltpu.VMEM((HIST_ROWS + tm, ff2), jnp.float32)],
        compiler_params=pltpu.CompilerParams(
            dimension_semantics=("arbitrary",), vmem_limit_bytes=VMEM_LIMIT_BYTES),
        name="conv_ffn",
    )(x, g, sc, sh, gt, wup, cw, wdn, g_final, *hist_args)


CMP_ROW = CMP_STRIDE * NSA_KV_WIDTH
CMP_HID = 2 * NSA_KV_HEADS * CMP_HIDDEN


def _gelu_tanh(x):
    return 0.5 * x * (1.0 + jnp.tanh(0.7978845608028654 * (x + 0.044715 * (x * x * x))))


def _compress_kernel(x_ref, xn_ref, pos_ref, wl_ref, wt_ref, w2_ref, o_ref, tr_s, *, tm):
    bf = jnp.bfloat16
    f32 = jnp.float32
    x = x_ref[...].astype(bf)
    lead = jnp.dot(x, wl_ref[...], preferred_element_type=f32)
    tr_s[0:tm, :] = jnp.dot(x, wt_ref[...], preferred_element_type=f32)
    tr_s[tm:tm + SUBLANES, :] = jnp.dot(xn_ref[...].astype(bf), wt_ref[...], preferred_element_type=f32)
    bias = (jnp.dot(pos_ref[0].astype(bf), wl_ref[...], preferred_element_type=f32)
            + jnp.dot(pos_ref[1].astype(bf), wt_ref[...], preferred_element_type=f32))[0:1, :]
    hid = _gelu_tanh(lead + tr_s[1:tm + 1, :] + bias)
    o_ref[...] = jnp.dot(hid.astype(bf), w2_ref[...], preferred_element_type=f32)


def _compress_weights(pos_emb, w1, w2):
    eye = jnp.eye(2, dtype=jnp.float32)
    w1f = jnp.einsum('kldh,kK,gG->lkgdKGh', w1, eye, eye).reshape(CMP_BLOCK, NSA_KV_WIDTH, CMP_HID)
    w1f = w1f.astype(jnp.bfloat16)
    w2b = jnp.einsum('khd,kK,gG->kghKGd', w2, eye, eye).reshape(CMP_HID, NSA_KV_WIDTH).astype(jnp.bfloat16)
    posf = jnp.broadcast_to(pos_emb.transpose(1, 0, 2)[:, :, None, :], (CMP_BLOCK, 2, NSA_KV_HEADS, NSA_HD))
    posf = posf.reshape(CMP_BLOCK, NSA_KV_WIDTH)
    pos = jnp.zeros((CMP_BLOCK, SUBLANES, NSA_KV_WIDTH), jnp.float32).at[:, 0].set(posf)
    pos_rows = jnp.zeros((2, SUBLANES, CMP_ROW), jnp.float32).at[:, 0].set(posf.reshape(2, CMP_ROW))
    w1kv = jnp.einsum('kldh,gG->lkgdGh', w1, eye).reshape(
        CMP_BLOCK // 2, 2, 2, NSA_KV_WIDTH // 2, CMP_HID // 2).transpose(0, 2, 1, 3, 4).reshape(
        CMP_BLOCK // 2, 2, NSA_KV_WIDTH, CMP_HID // 2).astype(jnp.bfloat16)
    return dict(pos=pos, w1=w1kv, w2=w2b, pos_rows=pos_rows,
                wl=w1f[:CMP_STRIDE].reshape(CMP_ROW, CMP_HID), wt=w1f[CMP_STRIDE:].reshape(CMP_ROW, CMP_HID))


def _compress(x, cw, tm):
    pos, wl, wt, w2b = cw['pos_rows'], cw['wl'], cw['wt'], cw['w2']
    n = x.shape[0]
    nb8 = n // SUBLANES
    return pl.pallas_call(
        functools.partial(_compress_kernel, tm=tm),
        grid=(n // tm,),
        in_specs=[
            pl.BlockSpec((tm, CMP_ROW), lambda i: (i, 0)),
            pl.BlockSpec((SUBLANES, CMP_ROW), lambda i: (jnp.minimum((i + 1) * (tm // SUBLANES), nb8 - 1), 0)),
            _resident((2, SUBLANES, CMP_ROW)),
            _resident((CMP_ROW, CMP_HID)), _resident((CMP_ROW, CMP_HID)),
            _resident((CMP_HID, NSA_KV_WIDTH)),
        ],
        out_specs=pl.BlockSpec((tm, NSA_KV_WIDTH), lambda i: (i, 0)),
        out_shape=jax.ShapeDtypeStruct((n, NSA_KV_WIDTH), jnp.float32),
        scratch_shapes=[pltpu.VMEM((tm + SUBLANES, CMP_HID), jnp.float32)],
        compiler_params=pltpu.CompilerParams(
            dimension_semantics=("arbitrary",), vmem_limit_bytes=VMEM_LIMIT_BYTES),
        name="nsa_compress",
    )(x, x, pos, wl, wt, w2b)


NSA_R = NSA_HEADS // NSA_KV_HEADS
QL = NSA_R * Q_BLOCK
QLL = NSA_KV_HEADS * QL
SLC_TILE = 512
BLK_PER_TILE = SLC_TILE // SEL_BLOCK
WIN_TILE = WINDOW + Q_BLOCK
CMP_PER_SEL = SEL_BLOCK // CMP_STRIDE
M_INIT = -1e29
QK_SCALE = NSA_HD ** -0.5 * 1.4426950408889634
ONES_ROWS = BF16_SUBLANES
SELECT_ROW_BUCKETS = (64, 128)


def _tile_lanes(v, reps):
    return jnp.concatenate([v] * reps, axis=1)


def _nsa_prompt_kernel(q_ref, gate_ref, kc_ref, vct_ref, kslc_ref, vtslc_ref, kwin_ref, vtwin_ref,
                       o_ref, sc_s, sel_s, m_s, l_s, acc_s, ow_s, oc_s, mt_s, al_s, s_s, pt_s, *, n_sel):
    f32, bf = jnp.float32, jnp.bfloat16
    G, HD = NSA_KV_HEADS, NSA_HD
    n = pl.program_id(0)
    qn = q_ref[...] * QK_SCALE
    tq = [qn[:, j * LANES:(j + 1) * LANES].T for j in range(NSA_WIDTH // LANES)]
    heads_per_t = LANES // HD
    head_t = lambda h: tq[h // heads_per_t][(h % heads_per_t) * HD:(h % heads_per_t + 1) * HD, :]
    zero = jnp.zeros((HD, Q_BLOCK), f32)
    qbd = jnp.concatenate(
        [jnp.concatenate([head_t(g * NSA_R + r) if g == gp else zero for g in range(G) for r in range(NSA_R)],
                         axis=1) for gp in range(G)], axis=0).astype(bf)
    lane = lax.broadcasted_iota(jnp.int32, (1, Q_BLOCK), 1)
    pos_q = n * Q_BLOCK + lane
    pos_l = _tile_lanes(pos_q, QLL // Q_BLOCK)
    jrow = lax.broadcasted_iota(jnp.int32, (n_sel, 1), 0)

    def reset():
        m_s[...] = jnp.full((1, QLL), M_INIT, f32)
        l_s[...] = jnp.zeros((1, QLL), f32)
        acc_s[...] = jnp.zeros((HD, QLL), f32)

    def pass1(slot, k_ref, start, rows, bias_fn):
        s = jnp.dot(k_ref[pl.ds(start, rows), :].astype(bf), qbd, preferred_element_type=f32)
        top = jnp.full((SUBLANES, QLL), NEG, f32)
        for i in range(rows // SEL_BLOCK):
            blk = slice(i * SEL_BLOCK, (i + 1) * SEL_BLOCK)
            sb = s[blk, :] + bias_fn(i)
            s_s[slot, blk, :] = sb
            for r in range(SEL_BLOCK // SUBLANES):
                top = jnp.maximum(top, sb[r * SUBLANES:(r + 1) * SUBLANES, :])
        m_old = m_s[...]
        m_new = jnp.maximum(m_old, jnp.max(top, axis=0, keepdims=True))
        mt_s[slot] = m_new
        al_s[slot] = jnp.exp2(m_old - m_new)
        m_s[...] = m_new

    def pass2(slot, vta_ref, start, rows):
        m_new = mt_s[slot]
        for i in range(rows // SEL_BLOCK):
            blk = slice(i * SEL_BLOCK, (i + 1) * SEL_BLOCK)
            pt_s[slot, blk, :] = jnp.exp2(s_s[slot, blk, :] - m_new).astype(bf)
        pv, psum = [], []
        for g in range(G):
            r = jnp.dot(vta_ref[g, :, pl.ds(start, rows)], pt_s[slot, 0:rows, g * QL:(g + 1) * QL],
                        preferred_element_type=f32)
            pv.append(r[0:HD, :])
            psum.append(r[HD:HD + 1, :])
        alpha = al_s[slot]
        acc_s[...] = acc_s[...] * alpha + jnp.concatenate(pv, axis=1)
        l_s[...] = l_s[...] * alpha + jnp.concatenate(psum, axis=1)

    def finish():
        return acc_s[...] * (1.0 / jnp.maximum(l_s[...], TINY))

    reset()
    wstart = pl.multiple_of(jnp.maximum(n * Q_BLOCK - WINDOW, 0), Q_BLOCK)
    rel = pos_q - (wstart + lax.broadcasted_iota(jnp.int32, (WIN_TILE, 1), 0))
    wbias = jnp.where((rel >= 0) & (rel <= WINDOW), 0.0, NEG)
    pass1(1, kwin_ref, wstart, WIN_TILE,
          lambda i: _tile_lanes(wbias[i * SEL_BLOCK:(i + 1) * SEL_BLOCK, :], QLL // Q_BLOCK))
    pass2(1, vtwin_ref, wstart, WIN_TILE)
    ow_s[...] = finish()

    def compress_and_select(rows):
        jrow = lax.broadcasted_iota(jnp.int32, (rows, 1), 0)
        s_c, mk_c = [], []
        m = jnp.full((1, QLL), NEG, f32)
        for c in range(CMP_PER_SEL):
            s = jnp.dot(kc_ref[c * n_sel:c * n_sel + rows, :], qbd, preferred_element_type=f32)
            mk = jrow * SEL_BLOCK + (c * CMP_STRIDE + CMP_BLOCK - 1) <= pos_l
            s = jnp.where(mk, s, NEG)
            m = jnp.maximum(m, jnp.max(s, axis=0, keepdims=True))
            s_c.append(s)
            mk_c.append(mk)
        e_c = [jnp.where(mk_c[c], jnp.exp2(s_c[c] - m), 0.0) for c in range(CMP_PER_SEL)]
        l = e_c[0].sum(axis=0, keepdims=True)
        for c in range(1, CMP_PER_SEL):
            l = l + e_c[c].sum(axis=0, keepdims=True)
        inv = 1.0 / jnp.maximum(l, TINY)
        o_cmp = [jnp.zeros((HD, QL), f32) for _ in range(G)]
        pg = []
        for c in range(CMP_PER_SEL):
            p = e_c[c] * inv
            pb = p.astype(bf)
            for g in range(G):
                o_cmp[g] = o_cmp[g] + jnp.dot(vct_ref[g * HD:(g + 1) * HD, c * n_sel:c * n_sel + rows],
                                              pb[:, g * QL:(g + 1) * QL], preferred_element_type=f32)
            pg.append([sum(p[:, g * QL + r * Q_BLOCK:g * QL + (r + 1) * Q_BLOCK] for r in range(NSA_R))
                       for g in range(G)])
        oc_s[...] = jnp.concatenate(o_cmp, axis=1)

        cur = pos_q // SEL_BLOCK
        forced = (jrow == 0) | (jrow == cur) | (jrow == cur - 1)
        allowed = jrow * SEL_BLOCK <= pos_q
        jrow_f = jrow.astype(f32)
        live = slice(0, rows)
        for g in range(G):
            last = pg[CMP_PER_SEL - 1][g]
            prev = jnp.where(jrow == 0, 0.0, pltpu.roll(last, 1, 0))
            inner = pg[0][g]
            for c in range(1, CMP_PER_SEL - 1):
                inner = inner + pg[c][g]
            p_slc = 2.0 * inner + last + prev
            sc_s[g, live, :] = jnp.where(forced, FORCE, jnp.where(allowed, p_slc, -1.0))
            sel_s[g, live, :] = jnp.zeros((rows, Q_BLOCK), f32)

        def pick(_, carry):
            for g in range(G):
                s = sc_s[g, live, :]
                top = jnp.max(s, axis=0, keepdims=True)
                first = jnp.min(jnp.where(s == top, jrow_f, float(rows)), axis=0, keepdims=True)
                hit = jrow_f == first
                sc_s[g, live, :] = jnp.where(hit, -jnp.inf, s)
                sel_s[g, live, :] = jnp.where(hit, 1.0, sel_s[g, live, :])
            return carry

        lax.fori_loop(0, min(SEL_TOPN, rows), pick, 0)
        for g in range(G):
            sel_s[g, live, :] = jnp.where(allowed & (sel_s[g, live, :] > 0.5), 0.0, NEG)
            if rows < n_sel:
                sel_s[g, rows:n_sel, :] = jnp.full((n_sel - rows, Q_BLOCK), NEG, f32)

    visible = (n + 1) * (Q_BLOCK // SEL_BLOCK)
    lo = 0
    for rows in sorted({min(r, n_sel) for r in SELECT_ROW_BUCKETS} | {n_sel}):
        pl.when((visible > lo) & (visible <= rows))(functools.partial(compress_and_select, rows))
        lo = rows
    o_cmp = oc_s[...]

    def tile_start(kt):
        return pl.multiple_of(kt * SLC_TILE, SLC_TILE)

    def slc_pass1(slot, kt, causal):
        start = tile_start(kt)
        selb = [sel_s[g, pl.ds(pl.multiple_of(kt * BLK_PER_TILE, BLK_PER_TILE), BLK_PER_TILE), :]
                for g in range(G)]

        def bias_fn(i):
            row = jnp.concatenate([_tile_lanes(selb[g][i:i + 1, :], NSA_R) for g in range(G)], axis=1)
            if not causal:
                return row
            tok = start + i * SEL_BLOCK + lax.broadcasted_iota(jnp.int32, (SEL_BLOCK, 1), 0)
            return jnp.where(tok <= pos_l, row, NEG)

        pass1(slot, kslc_ref, start, SLC_TILE, bias_fn)

    def slc_pass2(slot, kt):
        pass2(slot, vtslc_ref, tile_start(kt), SLC_TILE)

    reset()
    diag = (n * Q_BLOCK) // SLC_TILE
    slc_pass1(0, diag, True)

    def slc_pair(j, carry):
        slc_pass1(1, 2 * j, False)
        slc_pass2(0, jnp.where(j == 0, diag, 2 * j - 1))
        slc_pass1(0, 2 * j + 1, False)
        slc_pass2(1, 2 * j)
        return carry

    pairs = diag // 2
    lax.fori_loop(0, pairs, slc_pair, 0)
    pending = jnp.where(pairs == 0, diag, 2 * pairs - 1)

    @pl.when(diag % 2 == 1)
    def _():
        slc_pass1(1, diag - 1, False)
        slc_pass2(0, pending)
        slc_pass2(1, diag - 1)

    @pl.when(diag % 2 == 0)
    def _():
        slc_pass2(0, pending)

    o_slc = finish()

    o_win = ow_s[...]

    gt = jax.nn.sigmoid(gate_ref[...].T)
    gate = lambda c: jnp.concatenate([gt[h * 3 + c:h * 3 + c + 1, :] for h in range(NSA_HEADS)], axis=1)
    o = gate(0) * o_cmp + gate(1) * o_slc + gate(2) * o_win
    o_ref[...] = jnp.concatenate(
        [jnp.concatenate([o[:, (heads_per_t * j + t) * Q_BLOCK:(heads_per_t * j + t + 1) * Q_BLOCK]
                          for t in range(heads_per_t)], axis=0).T for j in range(NSA_WIDTH // LANES)], axis=1)


def _nsa_prompt_attention(proj, kvc, nslc, nwin):
    T = proj.shape[0]
    nb, n_sel = T // Q_BLOCK, T // SEL_BLOCK
    G, R, HD = NSA_KV_HEADS, NSA_R, NSA_HD
    bf = jnp.bfloat16
    half = G * HD
    assert PROJ_OFF[8] % NSA_WIDTH == 0 and PROJ_OFF[12] % LANES == 0
    kvp = kvc.reshape(n_sel, CMP_PER_SEL, NSA_KV_WIDTH).transpose(1, 0, 2).reshape(T // CMP_STRIDE, NSA_KV_WIDTH)
    kc, vct = kvp[:, :half].astype(bf), kvp[:, half:].T.astype(bf)
    def vt_ones(v):
        vt = v.T.reshape(G, HD, T)
        return jnp.concatenate([vt, jnp.ones((G, ONES_ROWS, T), vt.dtype)], axis=1).astype(bf)
    vtslc, vtwin = vt_ones(nslc[:, half:]), vt_ones(nwin[:, half:])
    k_cols = lambda p: pl.BlockSpec((T, half), lambda i: (0, PROJ_OFF[p] // half), pipeline_mode=pl.Buffered(1))
    out = pl.pallas_call(
        functools.partial(_nsa_prompt_kernel, n_sel=n_sel),
        grid=(nb,),
        in_specs=[
            pl.BlockSpec((Q_BLOCK, NSA_WIDTH), lambda i: (i, PROJ_OFF[8] // NSA_WIDTH)),
            pl.BlockSpec((Q_BLOCK, LANES), lambda i: (i, PROJ_OFF[12] // LANES)),
            _resident((T // CMP_STRIDE, half)), _resident((half, T // CMP_STRIDE)),
            k_cols(10), _resident((G, HD + ONES_ROWS, T)),
            k_cols(11), _resident((G, HD + ONES_ROWS, T)),
        ],
        out_specs=pl.BlockSpec((Q_BLOCK, NSA_WIDTH), lambda i: (i, 0)),
        out_shape=jax.ShapeDtypeStruct((T, NSA_WIDTH), jnp.float32),
        scratch_shapes=[
            pltpu.VMEM((G, n_sel, Q_BLOCK), jnp.float32),
            pltpu.VMEM((G, n_sel, Q_BLOCK), jnp.float32),
            pltpu.VMEM((1, QLL), jnp.float32),
            pltpu.VMEM((1, QLL), jnp.float32),
            pltpu.VMEM((HD, QLL), jnp.float32),
            pltpu.VMEM((HD, QLL), jnp.float32),
            pltpu.VMEM((HD, QLL), jnp.float32),
            pltpu.VMEM((2, 1, QLL), jnp.float32),
            pltpu.VMEM((2, 1, QLL), jnp.float32),
            pltpu.VMEM((2, max(WIN_TILE, SLC_TILE), QLL), jnp.float32),
            pltpu.VMEM((2, max(WIN_TILE, SLC_TILE), QLL), jnp.bfloat16),
        ],
        compiler_params=pltpu.CompilerParams(
            dimension_semantics=("arbitrary",), vmem_limit_bytes=VMEM_LIMIT_BYTES),
        name="nsa_prompt_attention",
    )(proj, proj, kc, vct, proj, vtslc, proj, vtwin)
    return out


PAGE_ROWS = PAGE_SIZE // CMP_STRIDE
SQ = 16
SAMPLE_LANES = NSA_HEADS * SQ
HALF_PAGES = 64
SAMPLE_TILE = 4096


def _page_view(pool):
    d, n_pool = pool.shape[:2]
    return pool.transpose(0, 1, 3, 4, 5, 2).reshape(d * n_pool, 2, NSA_KV_HEADS * NSA_HD, pool.shape[2])


def _page_copy(pool_ref, buf_ref, sem_ref, page, slot, idx):
    return pltpu.make_async_copy(pool_ref.at[page], buf_ref.at[slot, idx], sem_ref.at[slot])


def _gather_schedule(issue_fn, wait_fn):
    s = pl.program_id(0)

    @pl.when(s == 0)
    def _():
        issue_fn(s, 0)

    @pl.when(s + 1 < pl.num_programs(0))
    def _():
        issue_fn(s + 1, (s + 1) % 2)

    wait_fn(s, s % 2)


def _compress_paged_kernel(pt_ref, pool_ref, pos_ref, w1_ref, w2_ref, o_ref, buf, sem, tok_s,
                           *, page_base, n_pages):
    bf, f32 = jnp.bfloat16, jnp.float32
    rows = HALF_PAGES * PAGE_ROWS

    def copies(step, slot, fn):
        b, half = step // 2, step % 2

        def body(i, c):
            p = jnp.minimum(half * HALF_PAGES + i, n_pages - 1)
            fn(_page_copy(pool_ref, buf, sem, page_base + pt_ref[b, p], slot, i))
            return c

        lax.fori_loop(0, HALF_PAGES + 1, body, 0)

    _gather_schedule(lambda s, slot: copies(s, slot, lambda cp: cp.start()),
                     lambda s, slot: copies(s, slot, lambda cp: cp.wait()))
    slot = pl.program_id(0) % 2

    half = NSA_KV_HEADS * NSA_HD

    r_out = lax.broadcasted_iota(jnp.int32, (PAGE_SIZE, 1), 0)
    t_in = lax.broadcasted_iota(jnp.int32, (1, PAGE_SIZE), 1)
    regroup = ((r_out % PAGE_ROWS) * CMP_STRIDE + r_out // PAGE_ROWS == t_in).astype(bf)

    def to_offset_rows(i, c):
        dst = pl.ds(pl.multiple_of(i * PAGE_ROWS, PAGE_ROWS), PAGE_ROWS)
        for kv in range(2):
            t = lax.dot_general(regroup, buf[slot, i, kv].astype(bf), (((1,), (1,)), ((), ())),
                                preferred_element_type=f32)
            for l in range(CMP_STRIDE):
                tok_s[kv, l, dst, :] = t[l * PAGE_ROWS:(l + 1) * PAGE_ROWS, :]
        return c

    lax.fori_loop(0, HALF_PAGES + 1, to_offset_rows, 0, unroll=5)
    acc = [jnp.zeros((rows, CMP_HID // 2), f32) for _ in range(2)]
    bias = [jnp.zeros((SUBLANES, CMP_HID // 2), f32) for _ in range(2)]
    for l in range(0, CMP_BLOCK, 2):
        first = l // CMP_STRIDE
        for kv in range(2):
            x = jnp.concatenate([tok_s[kv, l % CMP_STRIDE + d, first:first + rows, :] for d in range(2)],
                                axis=1).astype(bf)
            w = w1_ref[l // 2, kv]
            p = jnp.concatenate([pos_ref[l + d, :, kv * half:(kv + 1) * half] for d in range(2)], axis=1)
            acc[kv] = acc[kv] + jnp.dot(x, w, preferred_element_type=f32)
            bias[kv] = bias[kv] + jnp.dot(p.astype(bf), w, preferred_element_type=f32)
    hid = _gelu_tanh(jnp.concatenate(acc, axis=1) + jnp.concatenate(bias, axis=1)[0:1, :])
    o_ref[...] = jnp.dot(hid.astype(bf), w2_ref[...], preferred_element_type=f32)


def _compress_paged(page_table, pool, layer, cw):
    nbatch, n_pages = page_table.shape
    assert n_pages == 2 * HALF_PAGES
    rows = HALF_PAGES * PAGE_ROWS
    const = lambda shape: pl.BlockSpec(shape, lambda s, pt: (0,) * len(shape), pipeline_mode=pl.Buffered(1))
    return pl.pallas_call(
        functools.partial(_compress_paged_kernel, page_base=layer * (pool.shape[0] // DEPTH), n_pages=n_pages),
        grid_spec=pltpu.PrefetchScalarGridSpec(
            num_scalar_prefetch=1,
            grid=(2 * nbatch,),
            in_specs=[
                pl.BlockSpec(memory_space=pl.ANY),
                const((CMP_BLOCK, SUBLANES, NSA_KV_WIDTH)),
                const((CMP_BLOCK // 2, 2, NSA_KV_WIDTH, CMP_HID // 2)),
                const((CMP_HID, NSA_KV_WIDTH)),
            ],
            out_specs=pl.BlockSpec((rows, NSA_KV_WIDTH), lambda s, pt: (s, 0)),
            scratch_shapes=[
                pltpu.VMEM((2, HALF_PAGES + 1) + pool.shape[1:], jnp.float32),
                pltpu.SemaphoreType.DMA((2,)),
                pltpu.VMEM((2, CMP_STRIDE, (HALF_PAGES + 1) * PAGE_ROWS, NSA_KV_WIDTH // 2), jnp.float32),
            ]),
        out_shape=jax.ShapeDtypeStruct((2 * nbatch * rows, NSA_KV_WIDTH), jnp.float32),
        compiler_params=pltpu.CompilerParams(
            dimension_semantics=("arbitrary",), vmem_limit_bytes=VMEM_LIMIT_BYTES),
        name="nsa_compress_paged",
    )(page_table, pool, cw['pos'], cw['w1'], cw['w2'])


def _nsa_sample_kernel(pt_ref, pool_ref, qbd_ref, gate_ref, kc_ref, vc_ref, nslc_ref, wcache_ref, nwin_ref,
                       o_ref, buf, sem, sc_s, sel_s, m_s, l_s, acc_s, *, page_base, n_pages, n_sel, t_new):
    f32, bf = jnp.float32, jnp.bfloat16
    HD, LN = NSA_HD, SAMPLE_LANES
    half = NSA_KV_HEADS * HD
    past = n_pages * PAGE_SIZE
    contract_rows = (((0,), (0,)), ((), ()))

    def copies(step, slot, fn):
        def body(p, c):
            fn(_page_copy(pool_ref, buf, sem, page_base + pt_ref[step, p], slot, p))
            return c

        lax.fori_loop(0, n_pages, body, 0)

    _gather_schedule(lambda s, slot: copies(s, slot, lambda cp: cp.start()),
                     lambda s, slot: copies(s, slot, lambda cp: cp.wait()))
    slot = pl.program_id(0) % 2

    qbd = qbd_ref[0]
    lane = lax.broadcasted_iota(jnp.int32, (1, LN), 1)
    pos_l = past + lane % SQ
    group0 = lane < LN // NSA_KV_HEADS
    jrow = lax.broadcasted_iota(jnp.int32, (n_sel, 1), 0)
    jrow_f = jrow.astype(f32)

    def group_rows(full):
        return jnp.where(group0, full[0:HD, :], full[HD:2 * HD, :])

    def pv(v, pt):
        return group_rows(lax.dot_general(v.astype(bf), pt, contract_rows, preferred_element_type=f32))

    def scores_t(kt):
        return lax.dot_general(kt.astype(bf), qbd, contract_rows, preferred_element_type=f32)

    li = lax.broadcasted_iota(jnp.int32, (LN, LN), 0)
    lj = lax.broadcasted_iota(jnp.int32, (LN, LN), 1)
    same = ((li // (NSA_R * SQ) == lj // (NSA_R * SQ)) & (li % SQ == lj % SQ)).astype(bf)

    def head_sum(p):
        hi = p.astype(bf)
        r1 = p - hi.astype(f32)
        mid = r1.astype(bf)
        lo = (r1 - mid.astype(f32)).astype(bf)
        return (jnp.dot(hi, same, preferred_element_type=f32) + jnp.dot(mid, same, preferred_element_type=f32)
                + jnp.dot(lo, same, preferred_element_type=f32))

    s_c, mk_c = [], []
    m = jnp.full((1, LN), NEG, f32)
    for c in range(CMP_PER_SEL):
        s = jnp.dot(kc_ref[0, c * n_sel:(c + 1) * n_sel, :], qbd, preferred_element_type=f32)
        mk = jrow * SEL_BLOCK + (c * CMP_STRIDE + CMP_BLOCK - 1) <= pos_l
        s = jnp.where(mk, s, NEG)
        m = jnp.maximum(m, jnp.max(s, axis=0, keepdims=True))
        s_c.append(s)
        mk_c.append(mk)
    e_c = [jnp.where(mk_c[c], jnp.exp2(s_c[c] - m), 0.0) for c in range(CMP_PER_SEL)]
    l = e_c[0].sum(axis=0, keepdims=True)
    for c in range(1, CMP_PER_SEL):
        l = l + e_c[c].sum(axis=0, keepdims=True)
    inv = 1.0 / jnp.maximum(l, TINY)
    o_cmp = jnp.zeros((HD, LN), f32)
    pg = []
    for c in range(CMP_PER_SEL):
        p = e_c[c] * inv
        o_cmp = o_cmp + pv(vc_ref[0, c * n_sel:(c + 1) * n_sel, :], p.astype(bf))
        pg.append(head_sum(p))

    cur = pos_l // SEL_BLOCK
    forced = (jrow == 0) | (jrow == cur) | (jrow == cur - 1)
    allowed = jrow * SEL_BLOCK <= pos_l
    last = pg[CMP_PER_SEL - 1]
    prev = jnp.where(jrow == 0, 0.0, pltpu.roll(last, 1, 0))
    inner = pg[0]
    for c in range(1, CMP_PER_SEL - 1):
        inner = inner + pg[c]
    sc_s[...] = jnp.where(forced, FORCE, jnp.where(allowed, 2.0 * inner + last + prev, -1.0))
    sel_s[...] = jnp.zeros((n_sel, LN), f32)

    def pick(_, carry):
        s = sc_s[...]
        top = jnp.max(s, axis=0, keepdims=True)
        first = jnp.min(jnp.where(s == top, jrow_f, float(n_sel)), axis=0, keepdims=True)
        hit = jrow_f == first
        sc_s[...] = jnp.where(hit, -jnp.inf, s)
        sel_s[...] = jnp.where(hit, 1.0, sel_s[...])
        return carry

    lax.fori_loop(0, SEL_TOPN, pick, 0)
    sel_s[...] = jnp.where(allowed, sel_s[...], 0.0)

    def reset():
        m_s[...] = jnp.full((1, LN), M_INIT, f32)
        l_s[...] = jnp.zeros((1, LN), f32)
        acc_s[...] = jnp.zeros((HD, LN), f32)

    def update(s_blocks, pv_fn):
        m_old = m_s[...]
        m_new = m_old
        for s in s_blocks:
            m_new = jnp.maximum(m_new, jnp.max(s, axis=0, keepdims=True))
        alpha = jnp.exp2(m_old - m_new)
        e_blocks = [jnp.exp2(s - m_new) for s in s_blocks]
        l_new = l_s[...] * alpha
        for e in e_blocks:
            l_new = l_new + e.sum(axis=0, keepdims=True)
        pt = e_blocks[0] if len(e_blocks) == 1 else jnp.concatenate(e_blocks, axis=0)
        acc_s[...] = acc_s[...] * alpha + pv_fn(pt.astype(bf))
        m_s[...] = m_new
        l_s[...] = l_new

    def finish():
        return acc_s[...] * (1.0 / jnp.maximum(l_s[...], TINY))

    def scores(kv):
        return jnp.dot(kv[:, 0:half].astype(bf), qbd, preferred_element_type=f32)

    reset()
    pages_per_tile = SAMPLE_TILE // PAGE_SIZE
    blk_per_page = PAGE_SIZE // SEL_BLOCK
    blk_per_tile = SAMPLE_TILE // SEL_BLOCK

    def slc_body(kt, carry):
        selb = sel_s[pl.ds(pl.multiple_of(kt * blk_per_tile, blk_per_tile), blk_per_tile), :]
        blocks = []
        for j in range(pages_per_tile):
            s = scores_t(buf[slot, kt * pages_per_tile + j, 0])
            for h in range(blk_per_page):
                i = j * blk_per_page + h
                blocks.append(jnp.where(selb[i:i + 1, :] > 0.5, s[h * SEL_BLOCK:(h + 1) * SEL_BLOCK, :], NEG))

        def pv_pages(pt):
            full = jnp.zeros((2 * HD, LN), f32)
            for j in range(pages_per_tile):
                full = full + jnp.dot(buf[slot, kt * pages_per_tile + j, 1].astype(bf),
                                      pt[j * PAGE_SIZE:(j + 1) * PAGE_SIZE, :], preferred_element_type=f32)
            return group_rows(full)

        update(blocks, pv_pages)
        return carry

    lax.fori_loop(0, past // SAMPLE_TILE, slc_body, 0)
    rows_new = lax.broadcasted_iota(jnp.int32, (SEL_BLOCK, 1), 0)
    kv = nslc_ref[0]
    keep = (sel_s[past // SEL_BLOCK:past // SEL_BLOCK + 1, :] > 0.5) & (past + rows_new <= pos_l)
    update([jnp.where(keep, scores(kv), NEG)], functools.partial(pv, kv[:, half:]))
    o_slc = finish()

    reset()
    wb = wcache_ref.shape[-1]
    rel = pos_l - (past - wb + lax.broadcasted_iota(jnp.int32, (wb, 1), 0))
    update([jnp.where((rel >= 0) & (rel <= WINDOW), scores_t(wcache_ref[0, 0, 0]), NEG)],
           lambda pt: group_rows(jnp.dot(wcache_ref[0, 0, 1].astype(bf), pt, preferred_element_type=f32)))
    kv = nwin_ref[0]
    rel = pos_l - (past + rows_new)
    update([jnp.where((rel >= 0) & (rel <= WINDOW) & (rows_new < t_new), scores(kv), NEG)],
           functools.partial(pv, kv[:, half:]))
    o_win = finish()

    gate = jax.nn.sigmoid(gate_ref[0])
    o_ref[0] = gate[0:1, :] * o_cmp + gate[1:2, :] * o_slc + gate[2:3, :] * o_win


def _pad_rows(a, rows):
    return jnp.pad(a, ((0, 0), (0, rows - a.shape[1]), (0, 0)))


def _nsa_sample(proj, q, kv_cmp, kv_slc, kv_win, gates, pool_cmp, pool_slc, page_table, win_cache, layer, cmp_w):
    B, Tn = q.shape[:2]
    G, R, HD = NSA_KV_HEADS, NSA_R, NSA_HD
    n_pages = page_table.shape[1]
    n_pool = pool_cmp.shape[1]
    past = n_pages * PAGE_SIZE
    wb = win_cache.shape[2]
    assert Tn <= SQ and Tn <= SEL_BLOCK and past % SAMPLE_TILE == 0 and wb % SUBLANES == 0
    bf = jnp.bfloat16
    half = G * HD
    kvc = _compress_paged(page_table, _page_view(pool_cmp), layer, cmp_w)
    n_blk = past // CMP_STRIDE
    n_sel = _round_up(past // SEL_BLOCK + 1, BF16_SUBLANES)
    kvp = kvc.reshape(B, n_blk // CMP_PER_SEL, CMP_PER_SEL, NSA_KV_WIDTH).transpose(0, 2, 1, 3)
    kvp = jnp.pad(kvp, ((0, 0), (0, 0), (0, n_sel - n_blk // CMP_PER_SEL), (0, 0)))
    kvp = kvp.reshape(B, CMP_PER_SEL * n_sel, NSA_KV_WIDTH).astype(bf)
    kc, vc = kvp[..., :half], kvp[..., half:]
    qt = jnp.pad((q * QK_SCALE).reshape(B, Tn, G, R, HD), ((0, 0), (0, SQ - Tn), (0, 0), (0, 0), (0, 0)))
    qt = qt.transpose(0, 2, 4, 3, 1)
    qbd = jnp.einsum('bgdrq,gh->bgdhrq', qt, jnp.eye(G, dtype=jnp.float32)).reshape(B, half, SAMPLE_LANES).astype(bf)
    gate = jnp.pad(gates.reshape(B, Tn, G, R, 3), ((0, 0), (0, SQ - Tn), (0, 0), (0, 0), (0, 0)))
    gate = gate.transpose(0, 4, 2, 3, 1).reshape(B, 3, SAMPLE_LANES)
    nslc = _pad_rows(kv_slc.reshape(B, Tn, NSA_KV_WIDTH), SEL_BLOCK)
    nwin = _pad_rows(kv_win.reshape(B, Tn, NSA_KV_WIDTH), SEL_BLOCK)
    per_b = lambda shape: pl.BlockSpec((1,) + shape, lambda b, pt: (b,) + (0,) * len(shape))
    out = pl.pallas_call(
        functools.partial(_nsa_sample_kernel, page_base=layer * n_pool, n_pages=n_pages, n_sel=n_sel, t_new=Tn),
        grid_spec=pltpu.PrefetchScalarGridSpec(
            num_scalar_prefetch=1,
            grid=(B,),
            in_specs=[
                pl.BlockSpec(memory_space=pl.ANY),
                per_b((half, SAMPLE_LANES)), per_b((3, SAMPLE_LANES)),
                per_b((CMP_PER_SEL * n_sel, half)), per_b((CMP_PER_SEL * n_sel, half)),
                per_b((SEL_BLOCK, NSA_KV_WIDTH)),
                pl.BlockSpec((1, 1, 2, half, wb), lambda b, pt: (layer, b, 0, 0, 0)),
                per_b((SEL_BLOCK, NSA_KV_WIDTH)),
            ],
            out_specs=per_b((HD, SAMPLE_LANES)),
            scratch_shapes=[
                pltpu.VMEM((2, n_pages, 2, half, PAGE_SIZE), jnp.float32),
                pltpu.SemaphoreType.DMA((2,)),
                pltpu.VMEM((n_sel, SAMPLE_LANES), jnp.float32),
                pltpu.VMEM((n_sel, SAMPLE_LANES), jnp.float32),
                pltpu.VMEM((1, SAMPLE_LANES), jnp.float32),
                pltpu.VMEM((1, SAMPLE_LANES), jnp.float32),
                pltpu.VMEM((HD, SAMPLE_LANES), jnp.float32),
            ]),
        out_shape=jax.ShapeDtypeStruct((B, HD, SAMPLE_LANES), jnp.float32),
        compiler_params=pltpu.CompilerParams(
            dimension_semantics=("arbitrary",), vmem_limit_bytes=VMEM_LIMIT_BYTES),
        name="nsa_sample_attention",
    )(page_table, _page_view(pool_slc), qbd, gate, kc, vc, nslc,
      win_cache.transpose(0, 1, 3, 4, 5, 2).reshape(DEPTH, B, 2, half, wb), nwin)
    o = out.reshape(B, HD, G, R, SQ)[..., :Tn].transpose(0, 4, 2, 3, 1).reshape(B, Tn, NSA_WIDTH)
    win_all = jnp.concatenate([win_cache[layer], kv_win.astype(win_cache.dtype)], axis=1)
    return o, win_all[:, -min(WINDOW, wb + Tn):]


GLA_LANES = GLA_HEADS * GLA_DK
GLA_KERNEL_CHUNK = 32
GLA_BLOCK_ROWS = 512


def _head_block_mask(dtype):
    r = lax.broadcasted_iota(jnp.int32, (GLA_LANES, GLA_LANES), 0) // GLA_DK
    c = lax.broadcasted_iota(jnp.int32, (GLA_LANES, GLA_LANES), 1) // GLA_DK
    return (r == c).astype(dtype)


def _gla_kernel(q_ref, k_ref, v_ref, gg_ref, ga_ref, wa_ref, ba_ref, gn_ref, st0_ref, o_ref, st_ref,
                st_s, kp_s, bp_s, vp_s, *, c, n_chunks, valid_rows):
    f32, bf = jnp.float32, jnp.bfloat16

    @pl.when(pl.program_id(1) == 0)
    def _():
        st_s[...] = st0_ref[0]
        zeros = jnp.zeros((c, GLA_LANES), f32)
        kp_s[0:c, :] = zeros
        bp_s[0:c, :] = zeros
        vp_s[0:c, :] = zeros

    row = lax.broadcasted_iota(jnp.int32, (c, 1), 0)
    tril = (lax.broadcasted_iota(jnp.int32, (c, c), 0) >= lax.broadcasted_iota(jnp.int32, (c, c), 1)).astype(f32)
    ones_blk = _head_block_mask(bf)
    blk_f32 = _head_block_mask(f32)
    contract_last = (((1,), (1,)), ((), ()))
    contract_rows = (((0,), (0,)), ((), ()))

    def head_sum(x):
        hi = x.astype(bf)
        lo = (x - hi.astype(f32)).astype(bf)
        return (jnp.dot(hi, ones_blk, preferred_element_type=f32)
                + jnp.dot(lo, ones_blk, preferred_element_type=f32))

    def chunk(ch, carry):
        rows = pl.ds(pl.multiple_of(ch * c, c), c)
        q = q_ref[rows, :] * (GLA_DK ** -0.5)
        k = k_ref[rows, :]
        v = v_ref[rows, :]
        z = jnp.dot(ga_ref[rows, :].astype(bf), wa_ref[...], preferred_element_type=f32) + ba_ref[...]
        la = (jnp.minimum(z, 0.0) - jnp.log1p(jnp.exp(-jnp.abs(z)))) / GLA_GATE_TEMP
        if valid_rows < c:
            la = jnp.where(row < valid_rows, la, 0.0)
        b = jnp.dot(tril, la, preferred_element_type=f32, precision=lax.Precision.HIGHEST)
        st = st_s[...]
        o = lax.dot_general((q * jnp.exp(b)).astype(bf), st.astype(bf), contract_last,
                            preferred_element_type=f32)
        kp_s[c:2 * c, :] = k
        bp_s[c:2 * c, :] = b
        vp_s[c:2 * c, :] = v
        terms = [q * k]
        for d in range(1, c):
            ok = row >= d
            kr = kp_s[c - d:2 * c - d, :]
            br = bp_s[c - d:2 * c - d, :]
            terms.append(jnp.where(ok, q * kr * jnp.exp(jnp.where(ok, b - br, 0.0)), 0.0))
        att = jnp.dot(jnp.concatenate(terms, axis=0).astype(bf), ones_blk, preferred_element_type=f32)
        for d in range(c):
            vr = v if d == 0 else vp_s[c - d:2 * c - d, :]
            o = o + att[d * c:(d + 1) * c, :] * vr
        ms = head_sum(o * o) * (1.0 / GLA_DV)
        g = gg_ref[rows, :]
        o_ref[rows, :] = o * lax.rsqrt(ms + EPS) * gn_ref[...] * (g * jax.nn.sigmoid(g))
        b_last = b[c - 1:c, :]
        ke = k * jnp.exp(b_last - b)
        upd = lax.dot_general(v.astype(bf), ke.astype(bf), contract_rows, preferred_element_type=f32)
        st_s[...] = st * jnp.exp(b_last) + upd * blk_f32
        return carry

    lax.fori_loop(0, n_chunks, chunk, 0, unroll=min(2, n_chunks))
    st_ref[0] = st_s[...]


def _gla(proj, nbatch, tb, c, valid_rows, wa_pad, ba, gn, st0):
    m = proj.shape[0]
    nblk = m // nbatch // tb
    colblk = lambda p, w: PROJ_OFF[p] // w
    row_map = lambda j: (lambda b, i: (b * nblk + i, j))
    return pl.pallas_call(
        functools.partial(_gla_kernel, c=c, n_chunks=tb // c, valid_rows=valid_rows),
        grid=(nbatch, nblk),
        in_specs=[
            pl.BlockSpec((tb, GLA_LANES), row_map(colblk(0, GLA_LANES))),
            pl.BlockSpec((tb, GLA_LANES), row_map(colblk(1, GLA_LANES))),
            pl.BlockSpec((tb, GLA_LANES), row_map(colblk(2, GLA_LANES))),
            pl.BlockSpec((tb, GLA_LANES), row_map(colblk(3, GLA_LANES))),
            pl.BlockSpec((tb, LANES), row_map(colblk(4, LANES))),
            pl.BlockSpec((LANES, GLA_LANES), lambda b, i: (0, 0)),
            pl.BlockSpec((1, GLA_LANES), lambda b, i: (0, 0)),
            pl.BlockSpec((1, GLA_LANES), lambda b, i: (0, 0)),
            pl.BlockSpec((1, GLA_LANES, GLA_LANES), lambda b, i: (b, 0, 0)),
        ],
        out_specs=[pl.BlockSpec((tb, GLA_LANES), lambda b, i: (b * nblk + i, 0)),
                   pl.BlockSpec((1, GLA_LANES, GLA_LANES), lambda b, i: (b, 0, 0))],
        out_shape=[jax.ShapeDtypeStruct((m, GLA_LANES), jnp.float32),
                   jax.ShapeDtypeStruct((nbatch, GLA_LANES, GLA_LANES), jnp.float32)],
        scratch_shapes=[pltpu.VMEM((GLA_LANES, GLA_LANES), jnp.float32),
                        pltpu.VMEM((2 * c, GLA_LANES), jnp.float32),
                        pltpu.VMEM((2 * c, GLA_LANES), jnp.float32),
                        pltpu.VMEM((2 * c, GLA_LANES), jnp.float32)],
        compiler_params=pltpu.CompilerParams(
            dimension_semantics=("arbitrary", "arbitrary"), vmem_limit_bytes=VMEM_LIMIT_BYTES),
        name="gla_scan",
    )(proj, proj, proj, proj, proj, wa_pad, ba, gn, st0)


def _gla_state_in(s0):
    eye = jnp.eye(GLA_HEADS, dtype=jnp.float32)
    return jnp.einsum('bhde,hg->bhegd', s0.astype(jnp.float32), eye).reshape(-1, GLA_LANES, GLA_LANES)


def _gla_state_out(st):
    blocks = [st[:, h * GLA_DV:(h + 1) * GLA_DV, h * GLA_DK:(h + 1) * GLA_DK] for h in range(GLA_HEADS)]
    return jnp.stack(blocks, axis=1).transpose(0, 1, 3, 2)


def _nsa_prompt(proj, q, kv_cmp, kv_slc, kv_win, gates, cmp_w):
    B, T = kv_cmp.shape[:2]
    assert B == 1 and T % SLC_TILE == 0 and T >= WIN_TILE
    n_rows = T // CMP_STRIDE
    kvc = _compress(kv_cmp.reshape(n_rows, CMP_ROW), cmp_w, min(256, n_rows))
    o = _nsa_prompt_attention(proj, kvc, kv_slc.reshape(T, NSA_KV_WIDTH), kv_win.reshape(T, NSA_KV_WIDTH))
    return o.reshape(B, T, NSA_WIDTH), kv_win[:, -min(WINDOW, T):]


def _expand_rows(v, t):
    if v.shape[0] == 1:
        return v
    return jnp.repeat(v, t, axis=0)


def _trunk_layer(x, mod, lw, gla_s0, sc_hist, ffn_hist, nsa_apply, tm, final, g_final):
    B, T, _ = x.shape
    m = B * T
    grouped = B > 1
    ssh1, ssc1, sgt1, ssh2, ssc2, sgt2 = [_expand_rows(v, T) for v in jnp.split(mod, 6, axis=-1)]
    x2 = x.reshape(m, D_MODEL)
    proj = _in_proj(x2, lw['norm_mix'], ssc1, ssh1, lw['w_in'], tm).reshape(B, T, PROJ_WIDTH)
    gq, gk, gv, gg, ga, sb, scc, shh, nq, ncmp, nslc, nwin, ngate = [_proj_piece(proj, p) for p in range(13)]
    heads = lambda a, d: a.reshape(B, T, -1, d)
    if grouped:
        t_pad = _round_up(T, SUBLANES)
        gla_in = jnp.pad(proj, ((0, 0), (0, t_pad - T), (0, 0))).reshape(B * t_pad, PROJ_WIDTH)
        tb = chunk = t_pad
    else:
        t_pad, gla_in, tb, chunk = T, proj.reshape(m, PROJ_WIDTH), GLA_BLOCK_ROWS, GLA_KERNEL_CHUNK
    o_gla, st_gla = _gla(gla_in, B, tb, chunk, T if grouped else chunk,
                         lw['gla_wa'], lw['gla_ba'], lw['gla_norm'], _gla_state_in(gla_s0))
    o_gla = o_gla.reshape(B, t_pad, GLA_WIDTH)[:, :T]
    s_gla = _gla_state_out(st_gla)
    kvr = lambda a: a.reshape(B, T, 2, NSA_KV_HEADS, NSA_HD)
    kv_cmp, kv_slc, kv_win = kvr(ncmp), kvr(nslc), kvr(nwin)
    proj2 = proj.reshape(m, PROJ_WIDTH)
    o_nsa, win_state = nsa_apply(proj2, heads(nq, NSA_HD), kv_cmp, kv_slc, kv_win, heads(ngate, 3))

    def conv_hist(h):
        c = h.shape[-1]
        if grouped:
            pad = jnp.zeros((B, T - 1, c), jnp.float32)
            h1 = jnp.concatenate([h[:, 1:2], pad], axis=1).reshape(m, c)
            h2 = jnp.concatenate([h[:, 0:2], pad[:, 1:]], axis=1).reshape(m, c)
            return (h1, h2)
        return jnp.concatenate([jnp.zeros((HIST_ROWS - (CONV_W - 1), c), jnp.float32), h[0]], axis=0)

    def conv_state(st):
        c = st.shape[-1]
        return st.reshape(B, T, c)[:, -(CONV_W - 1):] if grouped else st[None, -(CONV_W - 1):]

    group = T if grouped else 0
    x2, sc_st = _out_proj(x2, o_gla.reshape(m, GLA_WIDTH), proj2, o_nsa.reshape(m, NSA_WIDTH), sgt1,
                          lw['w_out'], lw['sc_conv'], conv_hist(sc_hist), tm, group)
    sc_state = conv_state(sc_st)
    y, st = _ffn(x2, lw['norm_ffn'], ssc2, ssh2, sgt2, lw['ffn_up'], lw['ffn_conv'], lw['ffn_down'],
                 g_final, conv_hist(ffn_hist), tm, group, final)
    ffn_state = conv_state(st)
    return (y.reshape(B, T, D_MODEL), kv_cmp, kv_slc, win_state, s_gla.astype(gla_s0.dtype), sc_state, ffn_state)


def kernel(x_prompt, x_sample, cache_nsa_cmp, cache_nsa_slc, cache_nsa_win, state_gla, state_shortconv, state_ffn_conv, page_table, c_prompt, c_sample, mod_w, mod_b, norm_mix, norm_ffn, w_in, gla_wa2, gla_ba, gla_norm, sc_conv, nsa_cmp_pos, nsa_cmp_w1, nsa_cmp_w2, w_out, ffn_up, ffn_conv, ffn_down, norm_final):
    xp, xs = x_prompt, x_sample
    bp, bs = xp.shape[0], xs.shape[0]
    assert bp == 1 and xs.shape[1] == 4
    c_rows = _round_up(bp + bs, SUBLANES)
    c_all = jnp.concatenate([c_prompt, c_sample, jnp.zeros((c_rows - bp - bs, D_MODEL), jnp.float32)], axis=0)
    mod_all = _modulation(c_all, mod_w, mod_b)
    g_final = norm_final.reshape(1, D_MODEL)
    outs = [[] for _ in range(12)]
    for l in range(DEPTH):
        lw = dict(
            norm_mix=norm_mix[l].reshape(1, D_MODEL), norm_ffn=norm_ffn[l].reshape(1, D_MODEL),
            w_in=_pack_w_in(w_in[l]),
            gla_wa=jnp.zeros((LANES, GLA_LANES), jnp.bfloat16).at[:GLA_GATE_RANK].set(
                gla_wa2[l].astype(jnp.bfloat16)),
            gla_ba=gla_ba[l].reshape(1, GLA_LANES),
            gla_norm=jnp.tile(gla_norm[l], GLA_HEADS).reshape(1, GLA_LANES),
            sc_conv=sc_conv[l], w_out=w_out[l].astype(jnp.bfloat16),
            ffn_up=ffn_up[l].astype(jnp.bfloat16), ffn_conv=ffn_conv[l],
            ffn_down=ffn_down[l].astype(jnp.bfloat16))
        cmp_params = (nsa_cmp_pos[l], nsa_cmp_w1[l], nsa_cmp_w2[l])
        cmp_w = _compress_weights(*cmp_params)
        final = l == DEPTH - 1
        res_p = _trunk_layer(
            xp, mod_all[l, 0:bp], lw,
            jnp.zeros((bp, GLA_HEADS, GLA_DK, GLA_DV), xp.dtype),
            jnp.zeros((bp, CONV_W - 1, SC_WIDTH), xp.dtype),
            jnp.zeros((bp, CONV_W - 1, 2 * D_FF), xp.dtype),
            functools.partial(_nsa_prompt, cmp_w=cmp_w), 256, final, g_final)
        res_s = _trunk_layer(
            xs, mod_all[l, bp:bp + bs], lw, state_gla[l], state_shortconv[l], state_ffn_conv[l],
            functools.partial(_nsa_sample, pool_cmp=cache_nsa_cmp, pool_slc=cache_nsa_slc,
                              page_table=page_table, win_cache=cache_nsa_win, layer=l,
                              cmp_w=cmp_w), bs * xs.shape[1], final, g_final)
        xp, xs = res_p[0], res_s[0]
        for k in range(6):
            outs[2 * k].append(res_p[k + 1])
            outs[2 * k + 1].append(res_s[k + 1])
    return (xp, xs) + tuple(jnp.stack(o) for o in outs)
```

```python
import functools

import jax
import jax.numpy as jnp
from jax import lax
from jax.experimental import pallas as pl
from jax.experimental.pallas import tpu as pltpu

D_MODEL = 1024
DEPTH = 2
PAGE_SIZE = 128
GLA_HEADS = 4
GLA_DK = D_MODEL // 16
GLA_DV = D_MODEL // 16
GLA_WIDTH = GLA_HEADS * GLA_DV
GLA_GATE_RANK = 16
GLA_GATE_TEMP = 16.0
GLA_CHUNK = 64
SC_WIDTH = D_MODEL // 4
CONV_W = 3
NSA_HEADS = 8
NSA_KV_HEADS = 2
NSA_HD = D_MODEL // 16
NSA_WIDTH = NSA_HEADS * NSA_HD
NSA_KV_WIDTH = 2 * NSA_KV_HEADS * NSA_HD
CMP_STRIDE = 16
CMP_BLOCK = 2 * CMP_STRIDE
CMP_HIDDEN = 128
SEL_BLOCK = 64
SEL_TOPN = 16
WINDOW = 512
Q_BLOCK = 128
D_FF = 2816
EPS = 1e-6
NEG = -1e30
TINY = 1e-30
FORCE = 1e9

IN_SIZES = (
    GLA_HEADS * GLA_DK, GLA_HEADS * GLA_DK, GLA_WIDTH, GLA_WIDTH, GLA_GATE_RANK,
    SC_WIDTH, SC_WIDTH, SC_WIDTH,
    NSA_WIDTH, NSA_KV_WIDTH, NSA_KV_WIDTH, NSA_KV_WIDTH, NSA_HEADS * 3,
)

LANES = 128
SUBLANES = 8
BF16_SUBLANES = 16
VMEM_LIMIT_BYTES = 56 * 1024 * 1024

PROJ_ORDER = (0, 1, 2, 3, 8, 5, 6, 7, 9, 10, 11, 4, 12)


def _round_up(n, m):
    return -(-n // m) * m


def _proj_layout():
    src, acc = [], 0
    for s in IN_SIZES:
        src.append(acc)
        acc += s
    offs, dst = {}, 0
    for p in PROJ_ORDER:
        offs[p] = dst
        dst += _round_up(IN_SIZES[p], LANES)
    return src, offs, dst


PROJ_SRC, PROJ_OFF, PROJ_WIDTH = _proj_layout()


def _pack_w_in(w_in):
    out = jnp.zeros((D_MODEL, PROJ_WIDTH), jnp.bfloat16)
    for p in PROJ_ORDER:
        piece = w_in[:, PROJ_SRC[p]:PROJ_SRC[p] + IN_SIZES[p]].astype(jnp.bfloat16)
        out = lax.dynamic_update_slice(out, piece, (0, PROJ_OFF[p]))
    return out


def _proj_piece(proj, p):
    return proj[..., PROJ_OFF[p]:PROJ_OFF[p] + IN_SIZES[p]]


def _mod_kernel(c_ref, w_ref, b_ref, o_ref):
    c = c_ref[...]
    a = c * jax.nn.sigmoid(c)
    o_ref[0] = jnp.dot(a, w_ref[0], preferred_element_type=jnp.float32,
                       precision=lax.Precision.HIGHEST) + b_ref[0]


def _modulation(c_all, mod_w, mod_b):
    rows = c_all.shape[0]
    tn = 1024
    n = mod_w.shape[-1]
    return pl.pallas_call(
        _mod_kernel,
        grid=(DEPTH, n // tn),
        in_specs=[
            pl.BlockSpec((rows, D_MODEL), lambda l, j: (0, 0)),
            pl.BlockSpec((1, D_MODEL, tn), lambda l, j: (l, 0, j)),
            pl.BlockSpec((1, 1, tn), lambda l, j: (l, 0, j)),
        ],
        out_specs=pl.BlockSpec((1, rows, tn), lambda l, j: (l, 0, j)),
        out_shape=jax.ShapeDtypeStruct((DEPTH, rows, n), jnp.float32),
        name="adaln_modulation",
    )(c_all, mod_w, mod_b.reshape(DEPTH, 1, n))


def _norm_mod(x, g, sc, sh):
    r = lax.rsqrt(jnp.mean(x * x, axis=-1, keepdims=True) + EPS)
    return (x * r * g) * (1.0 + sc) + sh


def _in_proj_kernel(x_ref, g_ref, sc_ref, sh_ref, w_ref, o_ref):
    h = _norm_mod(x_ref[...], g_ref[...], sc_ref[...], sh_ref[...])
    o_ref[...] = jnp.dot(h.astype(jnp.bfloat16), w_ref[...], preferred_element_type=jnp.float32)


def _row_spec(tm, per_row):
    if per_row:
        return pl.BlockSpec((tm, D_MODEL), lambda i: (i, 0))
    return pl.BlockSpec((1, D_MODEL), lambda i: (0, 0))


def _resident(shape):
    return pl.BlockSpec(shape, lambda i: (0,) * len(shape), pipeline_mode=pl.Buffered(1))


def _in_proj(x, g, sc, sh, w_packed, tm):
    m = x.shape[0]
    per_row = sc.shape[0] != 1
    return pl.pallas_call(
        _in_proj_kernel,
        grid=(m // tm,),
        in_specs=[
            pl.BlockSpec((tm, D_MODEL), lambda i: (i, 0)),
            _resident((1, D_MODEL)),
            _row_spec(tm, per_row),
            _row_spec(tm, per_row),
            _resident((D_MODEL, PROJ_WIDTH)),
        ],
        out_specs=pl.BlockSpec((tm, PROJ_WIDTH), lambda i: (i, 0)),
        out_shape=jax.ShapeDtypeStruct((m, PROJ_WIDTH), jnp.float32),
        compiler_params=pltpu.CompilerParams(
            dimension_semantics=("arbitrary",), vmem_limit_bytes=VMEM_LIMIT_BYTES),
        name="norm_in_proj",
    )(x, g, sc, sh, w_packed)


HIST_ROWS = SUBLANES


def _out_proj_kernel(*refs, tm, group):
    grouped = group > 0
    if grouped:
        (x_ref, gla_ref, sb_ref, scc_ref, shh_ref, nsa_ref, gt_ref, w_ref, cw_ref, h1_ref, h2_ref,
         o_ref, st_ref, u_s) = refs
    else:
        (x_ref, gla_ref, sb_ref, scc_ref, shh_ref, nsa_ref, gt_ref, w_ref, cw_ref, h0_ref,
         o_ref, st_ref, u_s) = refs

        @pl.when(pl.program_id(0) == 0)
        def _():
            u_s[0:HIST_ROWS, :] = h0_ref[...]

    u = scc_ref[...] * shh_ref[...]
    u_s[HIST_ROWS:HIST_ROWS + tm, :] = u
    p1 = u_s[HIST_ROWS - 1:HIST_ROWS - 1 + tm, :]
    p2 = u_s[HIST_ROWS - 2:HIST_ROWS - 2 + tm, :]
    if grouped:
        t = lax.broadcasted_iota(jnp.int32, (tm, 1), 0) % group
        p1 = jnp.where(t == 0, h1_ref[...], p1)
        p2 = jnp.where(t <= 1, h2_ref[...], p2)
    o_sc = sb_ref[...] * (cw_ref[0:1, :] * p2 + cw_ref[1:2, :] * p1 + cw_ref[2:3, :] * u)
    mix = jnp.concatenate([gla_ref[...], o_sc, nsa_ref[...]], axis=1).astype(jnp.bfloat16)
    o_ref[...] = x_ref[...] + gt_ref[...] * jnp.dot(mix, w_ref[...], preferred_element_type=jnp.float32)
    if grouped:
        st_ref[...] = u
    else:
        tail = u_s[tm:tm + HIST_ROWS, :]
        st_ref[...] = tail
        u_s[0:HIST_ROWS, :] = tail


def _out_proj(x, o_gla, proj, o_nsa, gt, w_bf16, cw, hist, tm, group):
    m = x.shape[0]
    per_row = gt.shape[0] != 1
    grouped = group > 0
    sc_cols = lambda p: pl.BlockSpec((tm, SC_WIDTH), lambda i: (i, PROJ_OFF[p] // SC_WIDTH))
    in_specs = [
        pl.BlockSpec((tm, D_MODEL), lambda i: (i, 0)),
        pl.BlockSpec((tm, GLA_WIDTH), lambda i: (i, 0)),
        sc_cols(5), sc_cols(6), sc_cols(7),
        pl.BlockSpec((tm, NSA_WIDTH), lambda i: (i, 0)),
        _row_spec(tm, per_row),
        _resident((D_MODEL, D_MODEL)),
        _resident((CONV_W, SC_WIDTH)),
    ]
    if grouped:
        assert m == tm
        in_specs += [_resident((tm, SC_WIDTH)), _resident((tm, SC_WIDTH))]
        hist_args, st_rows = tuple(hist), tm
    else:
        in_specs += [_resident((HIST_ROWS, SC_WIDTH))]
        hist_args, st_rows = (hist,), HIST_ROWS
    return pl.pallas_call(
        functools.partial(_out_proj_kernel, tm=tm, group=group),
        grid=(m // tm,),
        in_specs=in_specs,
        out_specs=[pl.BlockSpec((tm, D_MODEL), lambda i: (i, 0)),
                   pl.BlockSpec((st_rows, SC_WIDTH), lambda i: (0, 0))],
        out_shape=[jax.ShapeDtypeStruct((m, D_MODEL), jnp.float32),
                   jax.ShapeDtypeStruct((st_rows, SC_WIDTH), jnp.float32)],
        scratch_shapes=[pltpu.VMEM((HIST_ROWS + tm, SC_WIDTH), jnp.float32)],
        compiler_params=pltpu.CompilerParams(
            dimension_semantics=("arbitrary",), vmem_limit_bytes=VMEM_LIMIT_BYTES),
        name="shortconv_out_proj",
    )(x, o_gla, proj, proj, proj, o_nsa, gt, w_bf16, cw, *hist_args)


FFN_UP_CHUNK = 512
FFN_ACT_CHUNK = 256


def _ffn_kernel(*refs, tm, group, final):
    grouped = group > 0
    if grouped:
        (x_ref, g_ref, sc_ref, sh_ref, gt_ref, wup_ref, cw_ref, wdn_ref, gf_ref,
         h1_ref, h2_ref, o_ref, st_ref, up_s) = refs
    else:
        (x_ref, g_ref, sc_ref, sh_ref, gt_ref, wup_ref, cw_ref, wdn_ref, gf_ref,
         h0_ref, o_ref, st_ref, up_s) = refs

        @pl.when(pl.program_id(0) == 0)
        def _():
            up_s[0:HIST_ROWS, :] = h0_ref[...]

    x = x_ref[...]
    h = _norm_mod(x, g_ref[...], sc_ref[...], sh_ref[...]).astype(jnp.bfloat16)
    for c in range(2 * D_FF // FFN_UP_CHUNK):
        cols = slice(c * FFN_UP_CHUNK, (c + 1) * FFN_UP_CHUNK)
        up_s[HIST_ROWS:HIST_ROWS + tm, cols] = jnp.dot(
            h, wup_ref[:, cols], preferred_element_type=jnp.float32)

    if grouped:
        t = lax.broadcasted_iota(jnp.int32, (tm, 1), 0) % group

    def conv(cols):
        cur = up_s[HIST_ROWS:HIST_ROWS + tm, cols]
        p1 = up_s[HIST_ROWS - 1:HIST_ROWS - 1 + tm, cols]
        p2 = up_s[HIST_ROWS - 2:HIST_ROWS - 2 + tm, cols]
        if grouped:
            p1 = jnp.where(t == 0, h1_ref[:, cols], p1)
            p2 = jnp.where(t <= 1, h2_ref[:, cols], p2)
        return cw_ref[0:1, cols] * p2 + cw_ref[1:2, cols] * p1 + cw_ref[2:3, cols] * cur

    acc = jnp.zeros((tm, D_MODEL), jnp.float32)
    for c in range(D_FF // FFN_ACT_CHUNK):
        a = conv(slice(c * FFN_ACT_CHUNK, (c + 1) * FFN_ACT_CHUNK))
        b = conv(slice(D_FF + c * FFN_ACT_CHUNK, D_FF + (c + 1) * FFN_ACT_CHUNK))
        act = (a * jax.nn.sigmoid(a) * b).astype(jnp.bfloat16)
        acc = acc + jnp.dot(act, wdn_ref[c * FFN_ACT_CHUNK:(c + 1) * FFN_ACT_CHUNK, :],
                            preferred_element_type=jnp.float32)
    y = x + gt_ref[...] * acc
    if final:
        r = lax.rsqrt(jnp.mean(y * y, axis=-1, keepdims=True) + EPS)
        y = y * r * gf_ref[...]
    o_ref[...] = y

    if grouped:
        st_ref[...] = up_s[HIST_ROWS:HIST_ROWS + tm, :]
    else:
        tail = up_s[tm:tm + HIST_ROWS, :]
        st_ref[...] = tail
        up_s[0:HIST_ROWS, :] = tail


def _ffn(x, g, sc, sh, gt, wup, cw, wdn, g_final, hist, tm, group, final):
    m = x.shape[0]
    grouped = group > 0
    per_row = sc.shape[0] != 1
    ff2 = 2 * D_FF
    in_specs = [
        pl.BlockSpec((tm, D_MODEL), lambda i: (i, 0)),
        _resident((1, D_MODEL)),
        _row_spec(tm, per_row), _row_spec(tm, per_row), _row_spec(tm, per_row),
        _resident((D_MODEL, ff2)),
        _resident((CONV_W, ff2)),
        _resident((D_FF, D_MODEL)),
        _resident((1, D_MODEL)),
    ]
    if grouped:
        assert m == tm
        in_specs += [_resident((tm, ff2)), _resident((tm, ff2))]
        hist_args = tuple(hist)
        st_rows = tm
    else:
        in_specs += [_resident((HIST_ROWS, ff2))]
        hist_args = (hist,)
        st_rows = HIST_ROWS
    return pl.pallas_call(
        functools.partial(_ffn_kernel, tm=tm, group=group, final=final),
        grid=(m // tm,),
        in_specs=in_specs,
        out_specs=[pl.BlockSpec((tm, D_MODEL), lambda i: (i, 0)),
                   pl.BlockSpec((st_rows, ff2), lambda i: (0, 0))],
        out_shape=[jax.ShapeDtypeStruct((m, D_MODEL), jnp.float32),
                   jax.ShapeDtypeStruct((st_rows, ff2), jnp.float32)],
        scratch_shapes=[pltpu.VMEM((HIST_ROWS + tm, ff2), jnp.float32)],
        compiler_params=pltpu.CompilerParams(
            dimension_semantics=("arbitrary",), vmem_limit_bytes=VMEM_LIMIT_BYTES),
        name="conv_ffn",
    )(x, g, sc, sh, gt, wup, cw, wdn, g_final, *hist_args)


CMP_ROW = CMP_STRIDE * NSA_KV_WIDTH
CMP_HID = 2 * NSA_KV_HEADS * CMP_HIDDEN


def _gelu_tanh(x):
    return 0.5 * x * (1.0 + jnp.tanh(0.7978845608028654 * (x + 0.044715 * (x * x * x))))


def _compress_kernel(x_ref, xn_ref, pos_ref, wl_ref, wt_ref, w2_ref, o_ref, tr_s, *, tm):
    bf = jnp.bfloat16
    f32 = jnp.float32
    x = x_ref[...].astype(bf)
    lead = jnp.dot(x, wl_ref[...], preferred_element_type=f32)
    tr_s[0:tm, :] = jnp.dot(x, wt_ref[...], preferred_element_type=f32)
    tr_s[tm:tm + SUBLANES, :] = jnp.dot(xn_ref[...].astype(bf), wt_ref[...], preferred_element_type=f32)
    bias = (jnp.dot(pos_ref[0].astype(bf), wl_ref[...], preferred_element_type=f32)
            + jnp.dot(pos_ref[1].astype(bf), wt_ref[...], preferred_element_type=f32))[0:1, :]
    hid = _gelu_tanh(lead + tr_s[1:tm + 1, :] + bias)
    o_ref[...] = jnp.dot(hid.astype(bf), w2_ref[...], preferred_element_type=f32)


def _compress_weights(pos_emb, w1, w2):
    eye = jnp.eye(2, dtype=jnp.float32)
    w1f = jnp.einsum('kldh,kK,gG->lkgdKGh', w1, eye, eye).reshape(CMP_BLOCK, NSA_KV_WIDTH, CMP_HID)
    w1f = w1f.astype(jnp.bfloat16)
    w2b = jnp.einsum('khd,kK,gG->kghKGd', w2, eye, eye).reshape(CMP_HID, NSA_KV_WIDTH).astype(jnp.bfloat16)
    posf = jnp.broadcast_to(pos_emb.transpose(1, 0, 2)[:, :, None, :], (CMP_BLOCK, 2, NSA_KV_HEADS, NSA_HD))
    posf = posf.reshape(CMP_BLOCK, NSA_KV_WIDTH)
    pos = jnp.zeros((CMP_BLOCK, SUBLANES, NSA_KV_WIDTH), jnp.float32).at[:, 0].set(posf)
    pos_rows = jnp.zeros((2, SUBLANES, CMP_ROW), jnp.float32).at[:, 0].set(posf.reshape(2, CMP_ROW))
    w1kv = jnp.einsum('kldh,gG->lkgdGh', w1, eye).reshape(
        CMP_BLOCK // 2, 2, 2, NSA_KV_WIDTH // 2, CMP_HID // 2).transpose(0, 2, 1, 3, 4).reshape(
        CMP_BLOCK // 2, 2, NSA_KV_WIDTH, CMP_HID // 2).astype(jnp.bfloat16)
    return dict(pos=pos, w1=w1kv, w2=w2b, pos_rows=pos_rows,
                wl=w1f[:CMP_STRIDE].reshape(CMP_ROW, CMP_HID), wt=w1f[CMP_STRIDE:].reshape(CMP_ROW, CMP_HID))


def _compress(x, cw, tm):
    pos, wl, wt, w2b = cw['pos_rows'], cw['wl'], cw['wt'], cw['w2']
    n = x.shape[0]
    nb8 = n // SUBLANES
    return pl.pallas_call(
        functools.partial(_compress_kernel, tm=tm),
        grid=(n // tm,),
        in_specs=[
            pl.BlockSpec((tm, CMP_ROW), lambda i: (i, 0)),
            pl.BlockSpec((SUBLANES, CMP_ROW), lambda i: (jnp.minimum((i + 1) * (tm // SUBLANES), nb8 - 1), 0)),
            _resident((2, SUBLANES, CMP_ROW)),
            _resident((CMP_ROW, CMP_HID)), _resident((CMP_ROW, CMP_HID)),
            _resident((CMP_HID, NSA_KV_WIDTH)),
        ],
        out_specs=pl.BlockSpec((tm, NSA_KV_WIDTH), lambda i: (i, 0)),
        out_shape=jax.ShapeDtypeStruct((n, NSA_KV_WIDTH), jnp.float32),
        scratch_shapes=[pltpu.VMEM((tm + SUBLANES, CMP_HID), jnp.float32)],
        compiler_params=pltpu.CompilerParams(
            dimension_semantics=("arbitrary",), vmem_limit_bytes=VMEM_LIMIT_BYTES),
        name="nsa_compress",
    )(x, x, pos, wl, wt, w2b)


NSA_R = NSA_HEADS // NSA_KV_HEADS
QL = NSA_R * Q_BLOCK
QLL = NSA_KV_HEADS * QL
SLC_TILE = 512
BLK_PER_TILE = SLC_TILE // SEL_BLOCK
WIN_TILE = WINDOW + Q_BLOCK
CMP_PER_SEL = SEL_BLOCK // CMP_STRIDE
M_INIT = -1e29
QK_SCALE = NSA_HD ** -0.5 * 1.4426950408889634
ONES_ROWS = BF16_SUBLANES
SELECT_ROW_BUCKETS = (32, 64, 96, 128, 192)


def _tile_lanes(v, reps):
    return jnp.concatenate([v] * reps, axis=1)


def _nsa_prompt_kernel(q_ref, gate_ref, kc_ref, vct_ref, kslc_ref, vtslc_ref, kwin_ref, vtwin_ref,
                       o_ref, sc_s, sel_s, m_s, l_s, acc_s, ow_s, oc_s, mt_s, al_s, s_s, pt_s, *, n_sel):
    f32, bf = jnp.float32, jnp.bfloat16
    G, HD = NSA_KV_HEADS, NSA_HD
    n = pl.program_id(0)
    qn = q_ref[...] * QK_SCALE
    tq = [qn[:, j * LANES:(j + 1) * LANES].T for j in range(NSA_WIDTH // LANES)]
    heads_per_t = LANES // HD
    head_t = lambda h: tq[h // heads_per_t][(h % heads_per_t) * HD:(h % heads_per_t + 1) * HD, :]
    zero = jnp.zeros((HD, Q_BLOCK), f32)
    qbd = jnp.concatenate(
        [jnp.concatenate([head_t(g * NSA_R + r) if g == gp else zero for g in range(G) for r in range(NSA_R)],
                         axis=1) for gp in range(G)], axis=0).astype(bf)
    lane = lax.broadcasted_iota(jnp.int32, (1, Q_BLOCK), 1)
    pos_q = n * Q_BLOCK + lane
    pos_l = _tile_lanes(pos_q, QLL // Q_BLOCK)
    jrow = lax.broadcasted_iota(jnp.int32, (n_sel, 1), 0)

    def reset():
        m_s[...] = jnp.full((1, QLL), M_INIT, f32)
        l_s[...] = jnp.zeros((1, QLL), f32)
        acc_s[...] = jnp.zeros((HD, QLL), f32)

    def pass1(slot, k_ref, start, rows, bias_fn):
        s = jnp.dot(k_ref[pl.ds(start, rows), :].astype(bf), qbd, preferred_element_type=f32)
        top = jnp.full((SUBLANES, QLL), NEG, f32)
        for i in range(rows // SEL_BLOCK):
            blk = slice(i * SEL_BLOCK, (i + 1) * SEL_BLOCK)
            sb = s[blk, :] + bias_fn(i)
            s_s[slot, blk, :] = sb
            for r in range(SEL_BLOCK // SUBLANES):
                top = jnp.maximum(top, sb[r * SUBLANES:(r + 1) * SUBLANES, :])
        m_old = m_s[...]
        m_new = jnp.maximum(m_old, jnp.max(top, axis=0, keepdims=True))
        mt_s[slot] = m_new
        al_s[slot] = jnp.exp2(m_old - m_new)
        m_s[...] = m_new

    def pass2(slot, vta_ref, start, rows):
        m_new = mt_s[slot]
        for i in range(rows // SEL_BLOCK):
            blk = slice(i * SEL_BLOCK, (i + 1) * SEL_BLOCK)
            pt_s[slot, blk, :] = jnp.exp2(s_s[slot, blk, :] - m_new).astype(bf)
        pv, psum = [], []
        for g in range(G):
            r = jnp.dot(vta_ref[g, :, pl.ds(start, rows)], pt_s[slot, 0:rows, g * QL:(g + 1) * QL],
                        preferred_element_type=f32)
            pv.append(r[0:HD, :])
            psum.append(r[HD:HD + 1, :])
        alpha = al_s[slot]
        acc_s[...] = acc_s[...] * alpha + jnp.concatenate(pv, axis=1)
        l_s[...] = l_s[...] * alpha + jnp.concatenate(psum, axis=1)

    def finish():
        return acc_s[...] * (1.0 / jnp.maximum(l_s[...], TINY))

    reset()
    wstart = pl.multiple_of(jnp.maximum(n * Q_BLOCK - WINDOW, 0), Q_BLOCK)
    rel = pos_q - (wstart + lax.broadcasted_iota(jnp.int32, (WIN_TILE, 1), 0))
    wbias = jnp.where((rel >= 0) & (rel <= WINDOW), 0.0, NEG)
    pass1(1, kwin_ref, wstart, WIN_TILE,
          lambda i: _tile_lanes(wbias[i * SEL_BLOCK:(i + 1) * SEL_BLOCK, :], QLL // Q_BLOCK))
    pass2(1, vtwin_ref, wstart, WIN_TILE)
    ow_s[...] = finish()

    def compress_and_select(rows):
        jrow = lax.broadcasted_iota(jnp.int32, (rows, 1), 0)
        s_c, mk_c = [], []
        m = jnp.full((1, QLL), NEG, f32)
        for c in range(CMP_PER_SEL):
            s = jnp.dot(kc_ref[c * n_sel:c * n_sel + rows, :], qbd, preferred_element_type=f32)
            mk = jrow * SEL_BLOCK + (c * CMP_STRIDE + CMP_BLOCK - 1) <= pos_l
            s = jnp.where(mk, s, NEG)
            m = jnp.maximum(m, jnp.max(s, axis=0, keepdims=True))
            s_c.append(s)
            mk_c.append(mk)
        e_c = [jnp.where(mk_c[c], jnp.exp2(s_c[c] - m), 0.0) for c in range(CMP_PER_SEL)]
        l = e_c[0].sum(axis=0, keepdims=True)
        for c in range(1, CMP_PER_SEL):
            l = l + e_c[c].sum(axis=0, keepdims=True)
        inv = 1.0 / jnp.maximum(l, TINY)
        o_cmp = [jnp.zeros((HD, QL), f32) for _ in range(G)]
        pg = []
        for c in range(CMP_PER_SEL):
            p = e_c[c] * inv
            pb = p.astype(bf)
            for g in range(G):
                o_cmp[g] = o_cmp[g] + jnp.dot(vct_ref[g * HD:(g + 1) * HD, c * n_sel:c * n_sel + rows],
                                              pb[:, g * QL:(g + 1) * QL], preferred_element_type=f32)
            pg.append([sum(p[:, g * QL + r * Q_BLOCK:g * QL + (r + 1) * Q_BLOCK] for r in range(NSA_R))
                       for g in range(G)])
        oc_s[...] = jnp.concatenate(o_cmp, axis=1)

        cur = pos_q // SEL_BLOCK
        forced = (jrow == 0) | (jrow == cur) | (jrow == cur - 1)
        allowed = jrow * SEL_BLOCK <= pos_q
        jrow_f = jrow.astype(f32)
        live = slice(0, rows)
        for g in range(G):
            last = pg[CMP_PER_SEL - 1][g]
            prev = jnp.where(jrow == 0, 0.0, pltpu.roll(last, 1, 0))
            inner = pg[0][g]
            for c in range(1, CMP_PER_SEL - 1):
                inner = inner + pg[c][g]
            p_slc = 2.0 * inner + last + prev
            sc_s[g, live, :] = jnp.where(forced, FORCE, jnp.where(allowed, p_slc, -1.0))
            sel_s[g, live, :] = jnp.zeros((rows, Q_BLOCK), f32)

        def pick(_, carry):
            for g in range(G):
                s = sc_s[g, live, :]
                top = jnp.max(s, axis=0, keepdims=True)
                first = jnp.min(jnp.where(s == top, jrow_f, float(rows)), axis=0, keepdims=True)
                hit = jrow_f == first
                sc_s[g, live, :] = jnp.where(hit, -jnp.inf, s)
                sel_s[g, live, :] = jnp.where(hit, 1.0, sel_s[g, live, :])
            return carry

        lax.fori_loop(0, min(SEL_TOPN, rows), pick, 0)
        for g in range(G):
            sel_s[g, live, :] = jnp.where(allowed & (sel_s[g, live, :] > 0.5), 0.0, NEG)
            if rows < n_sel:
                sel_s[g, rows:n_sel, :] = jnp.full((n_sel - rows, Q_BLOCK), NEG, f32)

    visible = (n + 1) * (Q_BLOCK // SEL_BLOCK)
    lo = 0
    for rows in sorted({min(r, n_sel) for r in SELECT_ROW_BUCKETS} | {n_sel}):
        pl.when((visible > lo) & (visible <= rows))(functools.partial(compress_and_select, rows))
        lo = rows
    o_cmp = oc_s[...]

    def tile_start(kt):
        return pl.multiple_of(kt * SLC_TILE, SLC_TILE)

    def slc_pass1(slot, kt, causal):
        start = tile_start(kt)
        selb = [sel_s[g, pl.ds(pl.multiple_of(kt * BLK_PER_TILE, BLK_PER_TILE), BLK_PER_TILE), :]
                for g in range(G)]

        def bias_fn(i):
            row = jnp.concatenate([_tile_lanes(selb[g][i:i + 1, :], NSA_R) for g in range(G)], axis=1)
            if not causal:
                return row
            tok = start + i * SEL_BLOCK + lax.broadcasted_iota(jnp.int32, (SEL_BLOCK, 1), 0)
            return jnp.where(tok <= pos_l, row, NEG)

        pass1(slot, kslc_ref, start, SLC_TILE, bias_fn)

    def slc_pass2(slot, kt):
        pass2(slot, vtslc_ref, tile_start(kt), SLC_TILE)

    reset()
    diag = (n * Q_BLOCK) // SLC_TILE
    slc_pass1(0, diag, True)

    def slc_pair(j, carry):
        slc_pass1(1, 2 * j, False)
        slc_pass2(0, jnp.where(j == 0, diag, 2 * j - 1))
        slc_pass1(0, 2 * j + 1, False)
        slc_pass2(1, 2 * j)
        return carry

    pairs = diag // 2
    lax.fori_loop(0, pairs, slc_pair, 0)
    pending = jnp.where(pairs == 0, diag, 2 * pairs - 1)

    @pl.when(diag % 2 == 1)
    def _():
        slc_pass1(1, diag - 1, False)
        slc_pass2(0, pending)
        slc_pass2(1, diag - 1)

    @pl.when(diag % 2 == 0)
    def _():
        slc_pass2(0, pending)

    o_slc = finish()

    o_win = ow_s[...]

    gt = jax.nn.sigmoid(gate_ref[...].T)
    gate = lambda c: jnp.concatenate([gt[h * 3 + c:h * 3 + c + 1, :] for h in range(NSA_HEADS)], axis=1)
    o = gate(0) * o_cmp + gate(1) * o_slc + gate(2) * o_win
    o_ref[...] = jnp.concatenate(
        [jnp.concatenate([o[:, (heads_per_t * j + t) * Q_BLOCK:(heads_per_t * j + t + 1) * Q_BLOCK]
                          for t in range(heads_per_t)], axis=0).T for j in range(NSA_WIDTH // LANES)], axis=1)


def _nsa_prompt_attention(proj, kvc, nslc, nwin):
    T = proj.shape[0]
    nb, n_sel = T // Q_BLOCK, T // SEL_BLOCK
    G, R, HD = NSA_KV_HEADS, NSA_R, NSA_HD
    bf = jnp.bfloat16
    half = G * HD
    assert PROJ_OFF[8] % NSA_WIDTH == 0 and PROJ_OFF[12] % LANES == 0
    kvp = kvc.reshape(n_sel, CMP_PER_SEL, NSA_KV_WIDTH).transpose(1, 0, 2).reshape(T // CMP_STRIDE, NSA_KV_WIDTH)
    kc, vct = kvp[:, :half].astype(bf), kvp[:, half:].T.astype(bf)
    def vt_ones(v):
        vt = v.T.reshape(G, HD, T)
        return jnp.concatenate([vt, jnp.ones((G, ONES_ROWS, T), vt.dtype)], axis=1).astype(bf)
    vtslc, vtwin = vt_ones(nslc[:, half:]), vt_ones(nwin[:, half:])
    k_cols = lambda p: pl.BlockSpec((T, half), lambda i: (0, PROJ_OFF[p] // half), pipeline_mode=pl.Buffered(1))
    out = pl.pallas_call(
        functools.partial(_nsa_prompt_kernel, n_sel=n_sel),
        grid=(nb,),
        in_specs=[
            pl.BlockSpec((Q_BLOCK, NSA_WIDTH), lambda i: (i, PROJ_OFF[8] // NSA_WIDTH)),
            pl.BlockSpec((Q_BLOCK, LANES), lambda i: (i, PROJ_OFF[12] // LANES)),
            _resident((T // CMP_STRIDE, half)), _resident((half, T // CMP_STRIDE)),
            k_cols(10), _resident((G, HD + ONES_ROWS, T)),
            k_cols(11), _resident((G, HD + ONES_ROWS, T)),
        ],
        out_specs=pl.BlockSpec((Q_BLOCK, NSA_WIDTH), lambda i: (i, 0)),
        out_shape=jax.ShapeDtypeStruct((T, NSA_WIDTH), jnp.float32),
        scratch_shapes=[
            pltpu.VMEM((G, n_sel, Q_BLOCK), jnp.float32),
            pltpu.VMEM((G, n_sel, Q_BLOCK), jnp.float32),
            pltpu.VMEM((1, QLL), jnp.float32),
            pltpu.VMEM((1, QLL), jnp.float32),
            pltpu.VMEM((HD, QLL), jnp.float32),
            pltpu.VMEM((HD, QLL), jnp.float32),
            pltpu.VMEM((HD, QLL), jnp.float32),
            pltpu.VMEM((2, 1, QLL), jnp.float32),
            pltpu.VMEM((2, 1, QLL), jnp.float32),
            pltpu.VMEM((2, max(WIN_TILE, SLC_TILE), QLL), jnp.float32),
            pltpu.VMEM((2, max(WIN_TILE, SLC_TILE), QLL), jnp.bfloat16),
        ],
        compiler_params=pltpu.CompilerParams(
            dimension_semantics=("arbitrary",), vmem_limit_bytes=VMEM_LIMIT_BYTES),
        name="nsa_prompt_attention",
    )(proj, proj, kc, vct, proj, vtslc, proj, vtwin)
    return out


PAGE_ROWS = PAGE_SIZE // CMP_STRIDE
SQ = 16
SAMPLE_LANES = NSA_HEADS * SQ
HALF_PAGES = 64
SAMPLE_TILE = 4096


def _page_view(pool):
    d, n_pool = pool.shape[:2]
    return pool.transpose(0, 1, 3, 4, 5, 2).reshape(d * n_pool, 2, NSA_KV_HEADS * NSA_HD, pool.shape[2])


def _page_copy(pool_ref, buf_ref, sem_ref, page, slot, idx):
    return pltpu.make_async_copy(pool_ref.at[page], buf_ref.at[slot, idx], sem_ref.at[slot])


def _gather_schedule(issue_fn, wait_fn):
    s = pl.program_id(0)

    @pl.when(s == 0)
    def _():
        issue_fn(s, 0)

    @pl.when(s + 1 < pl.num_programs(0))
    def _():
        issue_fn(s + 1, (s + 1) % 2)

    wait_fn(s, s % 2)


def _compress_paged_kernel(pt_ref, pool_ref, pos_ref, w1_ref, w2_ref, o_ref, buf, sem, tok_s,
                           *, page_base, n_pages):
    bf, f32 = jnp.bfloat16, jnp.float32
    rows = HALF_PAGES * PAGE_ROWS

    def copies(step, slot, fn):
        b, half = step // 2, step % 2

        def body(i, c):
            p = jnp.minimum(half * HALF_PAGES + i, n_pages - 1)
            fn(_page_copy(pool_ref, buf, sem, page_base + pt_ref[b, p], slot, i))
            return c

        lax.fori_loop(0, HALF_PAGES + 1, body, 0)

    _gather_schedule(lambda s, slot: copies(s, slot, lambda cp: cp.start()),
                     lambda s, slot: copies(s, slot, lambda cp: cp.wait()))
    slot = pl.program_id(0) % 2

    half = NSA_KV_HEADS * NSA_HD

    r_out = lax.broadcasted_iota(jnp.int32, (PAGE_SIZE, 1), 0)
    t_in = lax.broadcasted_iota(jnp.int32, (1, PAGE_SIZE), 1)
    regroup = ((r_out % PAGE_ROWS) * CMP_STRIDE + r_out // PAGE_ROWS == t_in).astype(bf)

    def to_offset_rows(i, c):
        dst = pl.ds(pl.multiple_of(i * PAGE_ROWS, PAGE_ROWS), PAGE_ROWS)
        for kv in range(2):
            t = lax.dot_general(regroup, buf[slot, i, kv].astype(bf), (((1,), (1,)), ((), ())),
                                preferred_element_type=f32)
            for l in range(CMP_STRIDE):
                tok_s[kv, l, dst, :] = t[l * PAGE_ROWS:(l + 1) * PAGE_ROWS, :]
        return c

    lax.fori_loop(0, HALF_PAGES + 1, to_offset_rows, 0, unroll=5)
    acc = [jnp.zeros((rows, CMP_HID // 2), f32) for _ in range(2)]
    bias = [jnp.zeros((SUBLANES, CMP_HID // 2), f32) for _ in range(2)]
    for l in range(0, CMP_BLOCK, 2):
        first = l // CMP_STRIDE
        for kv in range(2):
            x = jnp.concatenate([tok_s[kv, l % CMP_STRIDE + d, first:first + rows, :] for d in range(2)],
                                axis=1).astype(bf)
            w = w1_ref[l // 2, kv]
            p = jnp.concatenate([pos_ref[l + d, :, kv * half:(kv + 1) * half] for d in range(2)], axis=1)
            acc[kv] = acc[kv] + jnp.dot(x, w, preferred_element_type=f32)
            bias[kv] = bias[kv] + jnp.dot(p.astype(bf), w, preferred_element_type=f32)
    hid = _gelu_tanh(jnp.concatenate(acc, axis=1) + jnp.concatenate(bias, axis=1)[0:1, :])
    o_ref[...] = jnp.dot(hid.astype(bf), w2_ref[...], preferred_element_type=f32)


def _compress_paged(page_table, pool, layer, cw):
    nbatch, n_pages = page_table.shape
    assert n_pages == 2 * HALF_PAGES
    rows = HALF_PAGES * PAGE_ROWS
    const = lambda shape: pl.BlockSpec(shape, lambda s, pt: (0,) * len(shape), pipeline_mode=pl.Buffered(1))
    return pl.pallas_call(
        functools.partial(_compress_paged_kernel, page_base=layer * (pool.shape[0] // DEPTH), n_pages=n_pages),
        grid_spec=pltpu.PrefetchScalarGridSpec(
            num_scalar_prefetch=1,
            grid=(2 * nbatch,),
            in_specs=[
                pl.BlockSpec(memory_space=pl.ANY),
                const((CMP_BLOCK, SUBLANES, NSA_KV_WIDTH)),
                const((CMP_BLOCK // 2, 2, NSA_KV_WIDTH, CMP_HID // 2)),
                const((CMP_HID, NSA_KV_WIDTH)),
            ],
            out_specs=pl.BlockSpec((rows, NSA_KV_WIDTH), lambda s, pt: (s, 0)),
            scratch_shapes=[
                pltpu.VMEM((2, HALF_PAGES + 1) + pool.shape[1:], jnp.float32),
                pltpu.SemaphoreType.DMA((2,)),
                pltpu.VMEM((2, CMP_STRIDE, (HALF_PAGES + 1) * PAGE_ROWS, NSA_KV_WIDTH // 2), jnp.float32),
            ]),
        out_shape=jax.ShapeDtypeStruct((2 * nbatch * rows, NSA_KV_WIDTH), jnp.float32),
        compiler_params=pltpu.CompilerParams(
            dimension_semantics=("arbitrary",), vmem_limit_bytes=VMEM_LIMIT_BYTES),
        name="nsa_compress_paged",
    )(page_table, pool, cw['pos'], cw['w1'], cw['w2'])


def _nsa_sample_kernel(pt_ref, pool_ref, qbd_ref, gate_ref, kc_ref, vc_ref, nslc_ref, wcache_ref, nwin_ref,
                       o_ref, buf, sem, sc_s, sel_s, m_s, l_s, acc_s, *, page_base, n_pages, n_sel, t_new):
    f32, bf = jnp.float32, jnp.bfloat16
    HD, LN = NSA_HD, SAMPLE_LANES
    half = NSA_KV_HEADS * HD
    past = n_pages * PAGE_SIZE
    contract_rows = (((0,), (0,)), ((), ()))

    def copies(step, slot, fn):
        def body(p, c):
            fn(_page_copy(pool_ref, buf, sem, page_base + pt_ref[step, p], slot, p))
            return c

        lax.fori_loop(0, n_pages, body, 0)

    _gather_schedule(lambda s, slot: copies(s, slot, lambda cp: cp.start()),
                     lambda s, slot: copies(s, slot, lambda cp: cp.wait()))
    slot = pl.program_id(0) % 2

    qbd = qbd_ref[0]
    lane = lax.broadcasted_iota(jnp.int32, (1, LN), 1)
    pos_l = past + lane % SQ
    group0 = lane < LN // NSA_KV_HEADS
    jrow = lax.broadcasted_iota(jnp.int32, (n_sel, 1), 0)
    jrow_f = jrow.astype(f32)

    def group_rows(full):
        return jnp.where(group0, full[0:HD, :], full[HD:2 * HD, :])

    def pv(v, pt):
        return group_rows(lax.dot_general(v.astype(bf), pt, contract_rows, preferred_element_type=f32))

    def scores_t(kt):
        return lax.dot_general(kt.astype(bf), qbd, contract_rows, preferred_element_type=f32)

    li = lax.broadcasted_iota(jnp.int32, (LN, LN), 0)
    lj = lax.broadcasted_iota(jnp.int32, (LN, LN), 1)
    same = ((li // (NSA_R * SQ) == lj // (NSA_R * SQ)) & (li % SQ == lj % SQ)).astype(bf)

    def head_sum(p):
        hi = p.astype(bf)
        r1 = p - hi.astype(f32)
        mid = r1.astype(bf)
        lo = (r1 - mid.astype(f32)).astype(bf)
        return (jnp.dot(hi, same, preferred_element_type=f32) + jnp.dot(mid, same, preferred_element_type=f32)
                + jnp.dot(lo, same, preferred_element_type=f32))

    s_c, mk_c = [], []
    m = jnp.full((1, LN), NEG, f32)
    for c in range(CMP_PER_SEL):
        s = jnp.dot(kc_ref[0, c * n_sel:(c + 1) * n_sel, :], qbd, preferred_element_type=f32)
        mk = jrow * SEL_BLOCK + (c * CMP_STRIDE + CMP_BLOCK - 1) <= pos_l
        s = jnp.where(mk, s, NEG)
        m = jnp.maximum(m, jnp.max(s, axis=0, keepdims=True))
        s_c.append(s)
        mk_c.append(mk)
    e_c = [jnp.where(mk_c[c], jnp.exp2(s_c[c] - m), 0.0) for c in range(CMP_PER_SEL)]
    l = e_c[0].sum(axis=0, keepdims=True)
    for c in range(1, CMP_PER_SEL):
        l = l + e_c[c].sum(axis=0, keepdims=True)
    inv = 1.0 / jnp.maximum(l, TINY)
    o_cmp = jnp.zeros((HD, LN), f32)
    pg = []
    for c in range(CMP_PER_SEL):
        p = e_c[c] * inv
        o_cmp = o_cmp + pv(vc_ref[0, c * n_sel:(c + 1) * n_sel, :], p.astype(bf))
        pg.append(head_sum(p))

    cur = pos_l // SEL_BLOCK
    forced = (jrow == 0) | (jrow == cur) | (jrow == cur - 1)
    allowed = jrow * SEL_BLOCK <= pos_l
    last = pg[CMP_PER_SEL - 1]
    prev = jnp.where(jrow == 0, 0.0, pltpu.roll(last, 1, 0))
    inner = pg[0]
    for c in range(1, CMP_PER_SEL - 1):
        inner = inner + pg[c]
    sc_s[...] = jnp.where(forced, FORCE, jnp.where(allowed, 2.0 * inner + last + prev, -1.0))
    sel_s[...] = jnp.zeros((n_sel, LN), f32)

    def pick(_, carry):
        s = sc_s[...]
        top = jnp.max(s, axis=0, keepdims=True)
        first = jnp.min(jnp.where(s == top, jrow_f, float(n_sel)), axis=0, keepdims=True)
        hit = jrow_f == first
        sc_s[...] = jnp.where(hit, -jnp.inf, s)
        sel_s[...] = jnp.where(hit, 1.0, sel_s[...])
        return carry

    lax.fori_loop(0, SEL_TOPN, pick, 0)
    sel_s[...] = jnp.where(allowed, sel_s[...], 0.0)

    def reset():
        m_s[...] = jnp.full((1, LN), M_INIT, f32)
        l_s[...] = jnp.zeros((1, LN), f32)
        acc_s[...] = jnp.zeros((HD, LN), f32)

    def update(s_blocks, pv_fn):
        m_old = m_s[...]
        m_new = m_old
        for s in s_blocks:
            m_new = jnp.maximum(m_new, jnp.max(s, axis=0, keepdims=True))
        alpha = jnp.exp2(m_old - m_new)
        e_blocks = [jnp.exp2(s - m_new) for s in s_blocks]
        l_new = l_s[...] * alpha
        for e in e_blocks:
            l_new = l_new + e.sum(axis=0, keepdims=True)
        pt = e_blocks[0] if len(e_blocks) == 1 else jnp.concatenate(e_blocks, axis=0)
        acc_s[...] = acc_s[...] * alpha + pv_fn(pt.astype(bf))
        m_s[...] = m_new
        l_s[...] = l_new

    def finish():
        return acc_s[...] * (1.0 / jnp.maximum(l_s[...], TINY))

    def scores(kv):
        return jnp.dot(kv[:, 0:half].astype(bf), qbd, preferred_element_type=f32)

    reset()
    pages_per_tile = SAMPLE_TILE // PAGE_SIZE
    blk_per_page = PAGE_SIZE // SEL_BLOCK
    blk_per_tile = SAMPLE_TILE // SEL_BLOCK

    def slc_body(kt, carry):
        selb = sel_s[pl.ds(pl.multiple_of(kt * blk_per_tile, blk_per_tile), blk_per_tile), :]
        blocks = []
        for j in range(pages_per_tile):
            s = scores_t(buf[slot, kt * pages_per_tile + j, 0])
            for h in range(blk_per_page):
                i = j * blk_per_page + h
                blocks.append(jnp.where(selb[i:i + 1, :] > 0.5, s[h * SEL_BLOCK:(h + 1) * SEL_BLOCK, :], NEG))

        def pv_pages(pt):
            full = jnp.zeros((2 * HD, LN), f32)
            for j in range(pages_per_tile):
                full = full + jnp.dot(buf[slot, kt * pages_per_tile + j, 1].astype(bf),
                                      pt[j * PAGE_SIZE:(j + 1) * PAGE_SIZE, :], preferred_element_type=f32)
            return group_rows(full)

        update(blocks, pv_pages)
        return carry

    lax.fori_loop(0, past // SAMPLE_TILE, slc_body, 0)
    rows_new = lax.broadcasted_iota(jnp.int32, (SEL_BLOCK, 1), 0)
    kv = nslc_ref[0]
    keep = (sel_s[past // SEL_BLOCK:past // SEL_BLOCK + 1, :] > 0.5) & (past + rows_new <= pos_l)
    update([jnp.where(keep, scores(kv), NEG)], functools.partial(pv, kv[:, half:]))
    o_slc = finish()

    reset()
    wb = wcache_ref.shape[-1]
    rel = pos_l - (past - wb + lax.broadcasted_iota(jnp.int32, (wb, 1), 0))
    update([jnp.where((rel >= 0) & (rel <= WINDOW), scores_t(wcache_ref[0, 0, 0]), NEG)],
           lambda pt: group_rows(jnp.dot(wcache_ref[0, 0, 1].astype(bf), pt, preferred_element_type=f32)))
    kv = nwin_ref[0]
    rel = pos_l - (past + rows_new)
    update([jnp.where((rel >= 0) & (rel <= WINDOW) & (rows_new < t_new), scores(kv), NEG)],
           functools.partial(pv, kv[:, half:]))
    o_win = finish()

    gate = jax.nn.sigmoid(gate_ref[0])
    o_ref[0] = gate[0:1, :] * o_cmp + gate[1:2, :] * o_slc + gate[2:3, :] * o_win


def _pad_rows(a, rows):
    return jnp.pad(a, ((0, 0), (0, rows - a.shape[1]), (0, 0)))


def _nsa_sample(proj, q, kv_cmp, kv_slc, kv_win, gates, pool_cmp, pool_slc, page_table, win_cache, layer, cmp_w):
    B, Tn = q.shape[:2]
    G, R, HD = NSA_KV_HEADS, NSA_R, NSA_HD
    n_pages = page_table.shape[1]
    n_pool = pool_cmp.shape[1]
    past = n_pages * PAGE_SIZE
    wb = win_cache.shape[2]
    assert Tn <= SQ and Tn <= SEL_BLOCK and past % SAMPLE_TILE == 0 and wb % SUBLANES == 0
    bf = jnp.bfloat16
    half = G * HD
    kvc = _compress_paged(page_table, _page_view(pool_cmp), layer, cmp_w)
    n_blk = past // CMP_STRIDE
    n_sel = _round_up(past // SEL_BLOCK + 1, BF16_SUBLANES)
    kvp = kvc.reshape(B, n_blk // CMP_PER_SEL, CMP_PER_SEL, NSA_KV_WIDTH).transpose(0, 2, 1, 3)
    kvp = jnp.pad(kvp, ((0, 0), (0, 0), (0, n_sel - n_blk // CMP_PER_SEL), (0, 0)))
    kvp = kvp.reshape(B, CMP_PER_SEL * n_sel, NSA_KV_WIDTH).astype(bf)
    kc, vc = kvp[..., :half], kvp[..., half:]
    qt = jnp.pad((q * QK_SCALE).reshape(B, Tn, G, R, HD), ((0, 0), (0, SQ - Tn), (0, 0), (0, 0), (0, 0)))
    qt = qt.transpose(0, 2, 4, 3, 1)
    qbd = jnp.einsum('bgdrq,gh->bgdhrq', qt, jnp.eye(G, dtype=jnp.float32)).reshape(B, half, SAMPLE_LANES).astype(bf)
    gate = jnp.pad(gates.reshape(B, Tn, G, R, 3), ((0, 0), (0, SQ - Tn), (0, 0), (0, 0), (0, 0)))
    gate = gate.transpose(0, 4, 2, 3, 1).reshape(B, 3, SAMPLE_LANES)
    nslc = _pad_rows(kv_slc.reshape(B, Tn, NSA_KV_WIDTH), SEL_BLOCK)
    nwin = _pad_rows(kv_win.reshape(B, Tn, NSA_KV_WIDTH), SEL_BLOCK)
    per_b = lambda shape: pl.BlockSpec((1,) + shape, lambda b, pt: (b,) + (0,) * len(shape))
    out = pl.pallas_call(
        functools.partial(_nsa_sample_kernel, page_base=layer * n_pool, n_pages=n_pages, n_sel=n_sel, t_new=Tn),
        grid_spec=pltpu.PrefetchScalarGridSpec(
            num_scalar_prefetch=1,
            grid=(B,),
            in_specs=[
                pl.BlockSpec(memory_space=pl.ANY),
                per_b((half, SAMPLE_LANES)), per_b((3, SAMPLE_LANES)),
                per_b((CMP_PER_SEL * n_sel, half)), per_b((CMP_PER_SEL * n_sel, half)),
                per_b((SEL_BLOCK, NSA_KV_WIDTH)),
                pl.BlockSpec((1, 1, 2, half, wb), lambda b, pt: (layer, b, 0, 0, 0)),
                per_b((SEL_BLOCK, NSA_KV_WIDTH)),
            ],
            out_specs=per_b((HD, SAMPLE_LANES)),
            scratch_shapes=[
                pltpu.VMEM((2, n_pages, 2, half, PAGE_SIZE), jnp.float32),
                pltpu.SemaphoreType.DMA((2,)),
                pltpu.VMEM((n_sel, SAMPLE_LANES), jnp.float32),
                pltpu.VMEM((n_sel, SAMPLE_LANES), jnp.float32),
                pltpu.VMEM((1, SAMPLE_LANES), jnp.float32),
                pltpu.VMEM((1, SAMPLE_LANES), jnp.float32),
                pltpu.VMEM((HD, SAMPLE_LANES), jnp.float32),
            ]),
        out_shape=jax.ShapeDtypeStruct((B, HD, SAMPLE_LANES), jnp.float32),
        compiler_params=pltpu.CompilerParams(
            dimension_semantics=("arbitrary",), vmem_limit_bytes=VMEM_LIMIT_BYTES),
        name="nsa_sample_attention",
    )(page_table, _page_view(pool_slc), qbd, gate, kc, vc, nslc,
      win_cache.transpose(0, 1, 3, 4, 5, 2).reshape(DEPTH, B, 2, half, wb), nwin)
    o = out.reshape(B, HD, G, R, SQ)[..., :Tn].transpose(0, 4, 2, 3, 1).reshape(B, Tn, NSA_WIDTH)
    win_all = jnp.concatenate([win_cache[layer], kv_win.astype(win_cache.dtype)], axis=1)
    return o, win_all[:, -min(WINDOW, wb + Tn):]


GLA_LANES = GLA_HEADS * GLA_DK
GLA_KERNEL_CHUNK = 32
GLA_BLOCK_ROWS = 512


def _head_block_mask(dtype):
    r = lax.broadcasted_iota(jnp.int32, (GLA_LANES, GLA_LANES), 0) // GLA_DK
    c = lax.broadcasted_iota(jnp.int32, (GLA_LANES, GLA_LANES), 1) // GLA_DK
    return (r == c).astype(dtype)


def _gla_kernel(q_ref, k_ref, v_ref, gg_ref, ga_ref, wa_ref, ba_ref, gn_ref, st0_ref, o_ref, st_ref,
                st_s, kp_s, bp_s, vp_s, *, c, n_chunks, valid_rows):
    f32, bf = jnp.float32, jnp.bfloat16

    @pl.when(pl.program_id(1) == 0)
    def _():
        st_s[...] = st0_ref[0]
        zeros = jnp.zeros((c, GLA_LANES), f32)
        kp_s[0:c, :] = zeros
        bp_s[0:c, :] = zeros
        vp_s[0:c, :] = zeros

    row = lax.broadcasted_iota(jnp.int32, (c, 1), 0)
    ones_blk = _head_block_mask(bf)
    blk_f32 = _head_block_mask(f32)
    contract_last = (((1,), (1,)), ((), ()))
    contract_rows = (((0,), (0,)), ((), ()))

    def head_sum(x):
        hi = x.astype(bf)
        lo = (x - hi.astype(f32)).astype(bf)
        return (jnp.dot(hi, ones_blk, preferred_element_type=f32)
                + jnp.dot(lo, ones_blk, preferred_element_type=f32))

    def chunk(ch, carry):
        rows = pl.ds(pl.multiple_of(ch * c, c), c)
        q = q_ref[rows, :] * (GLA_DK ** -0.5)
        k = k_ref[rows, :]
        v = v_ref[rows, :]
        z = jnp.dot(ga_ref[rows, :].astype(bf), wa_ref[...], preferred_element_type=f32) + ba_ref[...]
        la = (jnp.minimum(z, 0.0) - jnp.log1p(jnp.exp(-jnp.abs(z)))) / GLA_GATE_TEMP
        if valid_rows < c:
            la = jnp.where(row < valid_rows, la, 0.0)
        b = la
        step = 1
        while step < c:
            b = b + jnp.where(row >= step, pltpu.roll(b, step, 0), 0.0)
            step *= 2
        st = st_s[...]
        o = lax.dot_general((q * jnp.exp(b)).astype(bf), st.astype(bf), contract_last,
                            preferred_element_type=f32)
        kp_s[c:2 * c, :] = k
        bp_s[c:2 * c, :] = b
        vp_s[c:2 * c, :] = v
        terms = [q * k]
        for d in range(1, c):
            ok = row >= d
            kr = kp_s[c - d:2 * c - d, :]
            br = bp_s[c - d:2 * c - d, :]
            terms.append(jnp.where(ok, q * kr * jnp.exp(jnp.where(ok, b - br, 0.0)), 0.0))
        att = jnp.dot(jnp.concatenate(terms, axis=0).astype(bf), ones_blk, preferred_element_type=f32)
        for d in range(c):
            vr = v if d == 0 else vp_s[c - d:2 * c - d, :]
            o = o + att[d * c:(d + 1) * c, :] * vr
        ms = head_sum(o * o) * (1.0 / GLA_DV)
        g = gg_ref[rows, :]
        o_ref[rows, :] = o * lax.rsqrt(ms + EPS) * gn_ref[...] * (g * jax.nn.sigmoid(g))
        b_last = b[c - 1:c, :]
        ke = k * jnp.exp(b_last - b)
        upd = lax.dot_general(v.astype(bf), ke.astype(bf), contract_rows, preferred_element_type=f32)
        st_s[...] = st * jnp.exp(b_last) + upd * blk_f32
        return carry

    lax.fori_loop(0, n_chunks, chunk, 0, unroll=min(2, n_chunks))
    st_ref[0] = st_s[...]


def _gla(proj, nbatch, tb, c, valid_rows, wa_pad, ba, gn, st0):
    m = proj.shape[0]
    nblk = m // nbatch // tb
    colblk = lambda p, w: PROJ_OFF[p] // w
    row_map = lambda j: (lambda b, i: (b * nblk + i, j))
    return pl.pallas_call(
        functools.partial(_gla_kernel, c=c, n_chunks=tb // c, valid_rows=valid_rows),
        grid=(nbatch, nblk),
        in_specs=[
            pl.BlockSpec((tb, GLA_LANES), row_map(colblk(0, GLA_LANES))),
            pl.BlockSpec((tb, GLA_LANES), row_map(colblk(1, GLA_LANES))),
            pl.BlockSpec((tb, GLA_LANES), row_map(colblk(2, GLA_LANES))),
            pl.BlockSpec((tb, GLA_LANES), row_map(colblk(3, GLA_LANES))),
            pl.BlockSpec((tb, LANES), row_map(colblk(4, LANES))),
            pl.BlockSpec((LANES, GLA_LANES), lambda b, i: (0, 0)),
            pl.BlockSpec((1, GLA_LANES), lambda b, i: (0, 0)),
            pl.BlockSpec((1, GLA_LANES), lambda b, i: (0, 0)),
            pl.BlockSpec((1, GLA_LANES, GLA_LANES), lambda b, i: (b, 0, 0)),
        ],
        out_specs=[pl.BlockSpec((tb, GLA_LANES), lambda b, i: (b * nblk + i, 0)),
                   pl.BlockSpec((1, GLA_LANES, GLA_LANES), lambda b, i: (b, 0, 0))],
        out_shape=[jax.ShapeDtypeStruct((m, GLA_LANES), jnp.float32),
                   jax.ShapeDtypeStruct((nbatch, GLA_LANES, GLA_LANES), jnp.float32)],
        scratch_shapes=[pltpu.VMEM((GLA_LANES, GLA_LANES), jnp.float32),
                        pltpu.VMEM((2 * c, GLA_LANES), jnp.float32),
                        pltpu.VMEM((2 * c, GLA_LANES), jnp.float32),
                        pltpu.VMEM((2 * c, GLA_LANES), jnp.float32)],
        compiler_params=pltpu.CompilerParams(
            dimension_semantics=("arbitrary", "arbitrary"), vmem_limit_bytes=VMEM_LIMIT_BYTES),
        name="gla_scan",
    )(proj, proj, proj, proj, proj, wa_pad, ba, gn, st0)


def _gla_state_in(s0):
    eye = jnp.eye(GLA_HEADS, dtype=jnp.float32)
    return jnp.einsum('bhde,hg->bhegd', s0.astype(jnp.float32), eye).reshape(-1, GLA_LANES, GLA_LANES)


def _gla_state_out(st):
    blocks = [st[:, h * GLA_DV:(h + 1) * GLA_DV, h * GLA_DK:(h + 1) * GLA_DK] for h in range(GLA_HEADS)]
    return jnp.stack(blocks, axis=1).transpose(0, 1, 3, 2)


def _nsa_prompt(proj, q, kv_cmp, kv_slc, kv_win, gates, cmp_w):
    B, T = kv_cmp.shape[:2]
    assert B == 1 and T % SLC_TILE == 0 and T >= WIN_TILE
    n_rows = T // CMP_STRIDE
    kvc = _compress(kv_cmp.reshape(n_rows, CMP_ROW), cmp_w, min(256, n_rows))
    o = _nsa_prompt_attention(proj, kvc, kv_slc.reshape(T, NSA_KV_WIDTH), kv_win.reshape(T, NSA_KV_WIDTH))
    return o.reshape(B, T, NSA_WIDTH), kv_win[:, -min(WINDOW, T):]


def _expand_rows(v, t):
    if v.shape[0] == 1:
        return v
    return jnp.repeat(v, t, axis=0)


def _trunk_layer(x, mod, lw, gla_s0, sc_hist, ffn_hist, nsa_apply, tm, final, g_final):
    B, T, _ = x.shape
    m = B * T
    grouped = B > 1
    ssh1, ssc1, sgt1, ssh2, ssc2, sgt2 = [_expand_rows(v, T) for v in jnp.split(mod, 6, axis=-1)]
    x2 = x.reshape(m, D_MODEL)
    proj = _in_proj(x2, lw['norm_mix'], ssc1, ssh1, lw['w_in'], tm).reshape(B, T, PROJ_WIDTH)
    gq, gk, gv, gg, ga, sb, scc, shh, nq, ncmp, nslc, nwin, ngate = [_proj_piece(proj, p) for p in range(13)]
    heads = lambda a, d: a.reshape(B, T, -1, d)
    if grouped:
        t_pad = _round_up(T, SUBLANES)
        gla_in = jnp.pad(proj, ((0, 0), (0, t_pad - T), (0, 0))).reshape(B * t_pad, PROJ_WIDTH)
        tb = chunk = t_pad
    else:
        t_pad, gla_in, tb, chunk = T, proj.reshape(m, PROJ_WIDTH), GLA_BLOCK_ROWS, GLA_KERNEL_CHUNK
    o_gla, st_gla = _gla(gla_in, B, tb, chunk, T if grouped else chunk,
                         lw['gla_wa'], lw['gla_ba'], lw['gla_norm'], _gla_state_in(gla_s0))
    o_gla = o_gla.reshape(B, t_pad, GLA_WIDTH)[:, :T]
    s_gla = _gla_state_out(st_gla)
    kvr = lambda a: a.reshape(B, T, 2, NSA_KV_HEADS, NSA_HD)
    kv_cmp, kv_slc, kv_win = kvr(ncmp), kvr(nslc), kvr(nwin)
    proj2 = proj.reshape(m, PROJ_WIDTH)
    o_nsa, win_state = nsa_apply(proj2, heads(nq, NSA_HD), kv_cmp, kv_slc, kv_win, heads(ngate, 3))

    def conv_hist(h):
        c = h.shape[-1]
        if grouped:
            pad = jnp.zeros((B, T - 1, c), jnp.float32)
            h1 = jnp.concatenate([h[:, 1:2], pad], axis=1).reshape(m, c)
            h2 = jnp.concatenate([h[:, 0:2], pad[:, 1:]], axis=1).reshape(m, c)
            return (h1, h2)
        return jnp.concatenate([jnp.zeros((HIST_ROWS - (CONV_W - 1), c), jnp.float32), h[0]], axis=0)

    def conv_state(st):
        c = st.shape[-1]
        return st.reshape(B, T, c)[:, -(CONV_W - 1):] if grouped else st[None, -(CONV_W - 1):]

    group = T if grouped else 0
    x2, sc_st = _out_proj(x2, o_gla.reshape(m, GLA_WIDTH), proj2, o_nsa.reshape(m, NSA_WIDTH), sgt1,
                          lw['w_out'], lw['sc_conv'], conv_hist(sc_hist), tm, group)
    sc_state = conv_state(sc_st)
    y, st = _ffn(x2, lw['norm_ffn'], ssc2, ssh2, sgt2, lw['ffn_up'], lw['ffn_conv'], lw['ffn_down'],
                 g_final, conv_hist(ffn_hist), tm, group, final)
    ffn_state = conv_state(st)
    return (y.reshape(B, T, D_MODEL), kv_cmp, kv_slc, win_state, s_gla.astype(gla_s0.dtype), sc_state, ffn_state)


def kernel(x_prompt, x_sample, cache_nsa_cmp, cache_nsa_slc, cache_nsa_win, state_gla, state_shortconv, state_ffn_conv, page_table, c_prompt, c_sample, mod_w, mod_b, norm_mix, norm_ffn, w_in, gla_wa2, gla_ba, gla_norm, sc_conv, nsa_cmp_pos, nsa_cmp_w1, nsa_cmp_w2, w_out, ffn_up, ffn_conv, ffn_down, norm_final):
    xp, xs = x_prompt, x_sample
    bp, bs = xp.shape[0], xs.shape[0]
    assert bp == 1 and xs.shape[1] == 4
    c_rows = _round_up(bp + bs, SUBLANES)
    c_all = jnp.concatenate([c_prompt, c_sample, jnp.zeros((c_rows - bp - bs, D_MODEL), jnp.float32)], axis=0)
    mod_all = _modulation(c_all, mod_w, mod_b)
    g_final = norm_final.reshape(1, D_MODEL)
    outs = [[] for _ in range(12)]
    for l in range(DEPTH):
        lw = dict(
            norm_mix=norm_mix[l].reshape(1, D_MODEL), norm_ffn=norm_ffn[l].reshape(1, D_MODEL),
            w_in=_pack_w_in(w_in[l]),
            gla_wa=jnp.zeros((LANES, GLA_LANES), jnp.bfloat16).at[:GLA_GATE_RANK].set(
                gla_wa2[l].astype(jnp.bfloat16)),
            gla_ba=gla_ba[l].reshape(1, GLA_LANES),
            gla_norm=jnp.tile(gla_norm[l], GLA_HEADS).reshape(1, GLA_LANES),
            sc_conv=sc_conv[l], w_out=w_out[l].astype(jnp.bfloat16),
            ffn_up=ffn_up[l].astype(jnp.bfloat16), ffn_conv=ffn_conv[l],
            ffn_down=ffn_down[l].astype(jnp.bfloat16))
        cmp_params = (nsa_cmp_pos[l], nsa_cmp_w1[l], nsa_cmp_w2[l])
        cmp_w = _compress_weights(*cmp_params)
        final = l == DEPTH - 1
        res_p = _trunk_layer(
            xp, mod_all[l, 0:bp], lw,
            jnp.zeros((bp, GLA_HEADS, GLA_DK, GLA_DV), xp.dtype),
            jnp.zeros((bp, CONV_W - 1, SC_WIDTH), xp.dtype),
            jnp.zeros((bp, CONV_W - 1, 2 * D_FF), xp.dtype),
            functools.partial(_nsa_prompt, cmp_w=cmp_w), 256, final, g_final)
        res_s = _trunk_layer(
            xs, mod_all[l, bp:bp + bs], lw, state_gla[l], state_shortconv[l], state_ffn_conv[l],
            functools.partial(_nsa_sample, pool_cmp=cache_nsa_cmp, pool_slc=cache_nsa_slc,
                              page_table=page_table, win_cache=cache_nsa_win, layer=l,
                              cmp_w=cmp_w), bs * xs.shape[1], final, g_final)
        xp, xs = res_p[0], res_s[0]
        for k in range(6):
            outs[2 * k].append(res_p[k + 1])
            outs[2 * k + 1].append(res_s[k + 1])
    return (xp, xs) + tuple(jnp.stack(o) for o in outs)
```

```python
import functools

import jax
import jax.numpy as jnp
from jax import lax
from jax.experimental import pallas as pl
from jax.experimental.pallas import tpu as pltpu

D_MODEL = 1024
DEPTH = 2
PAGE_SIZE = 128
GLA_HEADS = 4
GLA_DK = D_MODEL // 16
GLA_DV = D_MODEL // 16
GLA_WIDTH = GLA_HEADS * GLA_DV
GLA_GATE_RANK = 16
GLA_GATE_TEMP = 16.0
GLA_CHUNK = 64
SC_WIDTH = D_MODEL // 4
CONV_W = 3
NSA_HEADS = 8
NSA_KV_HEADS = 2
NSA_HD = D_MODEL // 16
NSA_WIDTH = NSA_HEADS * NSA_HD
NSA_KV_WIDTH = 2 * NSA_KV_HEADS * NSA_HD
CMP_STRIDE = 16
CMP_BLOCK = 2 * CMP_STRIDE
CMP_HIDDEN = 128
SEL_BLOCK = 64
SEL_TOPN = 16
WINDOW = 512
Q_BLOCK = 128
D_FF = 2816
EPS = 1e-6
NEG = -1e30
TINY = 1e-30
FORCE = 1e9

IN_SIZES = (
    GLA_HEADS * GLA_DK, GLA_HEADS * GLA_DK, GLA_WIDTH, GLA_WIDTH, GLA_GATE_RANK,
    SC_WIDTH, SC_WIDTH, SC_WIDTH,
    NSA_WIDTH, NSA_KV_WIDTH, NSA_KV_WIDTH, NSA_KV_WIDTH, NSA_HEADS * 3,
)

LANES = 128
SUBLANES = 8
BF16_SUBLANES = 16
VMEM_LIMIT_BYTES = 56 * 1024 * 1024

PROJ_ORDER = (0, 1, 2, 3, 8, 5, 6, 7, 9, 10, 11, 4, 12)


def _round_up(n, m):
    return -(-n // m) * m


def _proj_layout():
    src, acc = [], 0
    for s in IN_SIZES:
        src.append(acc)
        acc += s
    offs, dst = {}, 0
    for p in PROJ_ORDER:
        offs[p] = dst
        dst += _round_up(IN_SIZES[p], LANES)
    return src, offs, dst


PROJ_SRC, PROJ_OFF, PROJ_WIDTH = _proj_layout()


def _pack_w_in(w_in):
    out = jnp.zeros((D_MODEL, PROJ_WIDTH), jnp.bfloat16)
    for p in PROJ_ORDER:
        piece = w_in[:, PROJ_SRC[p]:PROJ_SRC[p] + IN_SIZES[p]].astype(jnp.bfloat16)
        out = lax.dynamic_update_slice(out, piece, (0, PROJ_OFF[p]))
    return out


def _proj_piece(proj, p):
    return proj[..., PROJ_OFF[p]:PROJ_OFF[p] + IN_SIZES[p]]


def _mod_kernel(c_ref, w_ref, b_ref, o_ref):
    c = c_ref[...]
    a = c * jax.nn.sigmoid(c)
    o_ref[0] = jnp.dot(a, w_ref[0], preferred_element_type=jnp.float32,
                       precision=lax.Precision.HIGHEST) + b_ref[0]


def _modulation(c_all, mod_w, mod_b):
    rows = c_all.shape[0]
    tn = 1024
    n = mod_w.shape[-1]
    return pl.pallas_call(
        _mod_kernel,
        grid=(DEPTH, n // tn),
        in_specs=[
            pl.BlockSpec((rows, D_MODEL), lambda l, j: (0, 0)),
            pl.BlockSpec((1, D_MODEL, tn), lambda l, j: (l, 0, j)),
            pl.BlockSpec((1, 1, tn), lambda l, j: (l, 0, j)),
        ],
        out_specs=pl.BlockSpec((1, rows, tn), lambda l, j: (l, 0, j)),
        out_shape=jax.ShapeDtypeStruct((DEPTH, rows, n), jnp.float32),
        name="adaln_modulation",
    )(c_all, mod_w, mod_b.reshape(DEPTH, 1, n))


def _norm_mod(x, g, sc, sh):
    r = lax.rsqrt(jnp.mean(x * x, axis=-1, keepdims=True) + EPS)
    return (x * r * g) * (1.0 + sc) + sh


def _in_proj_kernel(x_ref, g_ref, sc_ref, sh_ref, w_ref, o_ref):
    h = _norm_mod(x_ref[...], g_ref[...], sc_ref[...], sh_ref[...])
    o_ref[...] = jnp.dot(h.astype(jnp.bfloat16), w_ref[...], preferred_element_type=jnp.float32)


def _row_spec(tm, per_row):
    if per_row:
        return pl.BlockSpec((tm, D_MODEL), lambda i: (i, 0))
    return pl.BlockSpec((1, D_MODEL), lambda i: (0, 0))


def _resident(shape):
    return pl.BlockSpec(shape, lambda i: (0,) * len(shape), pipeline_mode=pl.Buffered(1))


def _in_proj(x, g, sc, sh, w_packed, tm):
    m = x.shape[0]
    per_row = sc.shape[0] != 1
    return pl.pallas_call(
        _in_proj_kernel,
        grid=(m // tm,),
        in_specs=[
            pl.BlockSpec((tm, D_MODEL), lambda i: (i, 0)),
            _resident((1, D_MODEL)),
            _row_spec(tm, per_row),
            _row_spec(tm, per_row),
            _resident((D_MODEL, PROJ_WIDTH)),
        ],
        out_specs=pl.BlockSpec((tm, PROJ_WIDTH), lambda i: (i, 0)),
        out_shape=jax.ShapeDtypeStruct((m, PROJ_WIDTH), jnp.float32),
        compiler_params=pltpu.CompilerParams(
            dimension_semantics=("arbitrary",), vmem_limit_bytes=VMEM_LIMIT_BYTES),
        name="norm_in_proj",
    )(x, g, sc, sh, w_packed)


HIST_ROWS = SUBLANES


def _out_proj_kernel(*refs, tm, group):
    grouped = group > 0
    if grouped:
        (x_ref, gla_ref, sb_ref, scc_ref, shh_ref, nsa_ref, gt_ref, w_ref, cw_ref, h1_ref, h2_ref,
         o_ref, st_ref, u_s) = refs
    else:
        (x_ref, gla_ref, sb_ref, scc_ref, shh_ref, nsa_ref, gt_ref, w_ref, cw_ref, h0_ref,
         o_ref, st_ref, u_s) = refs

        @pl.when(pl.program_id(0) == 0)
        def _():
            u_s[0:HIST_ROWS, :] = h0_ref[...]

    u = scc_ref[...] * shh_ref[...]
    u_s[HIST_ROWS:HIST_ROWS + tm, :] = u
    p1 = u_s[HIST_ROWS - 1:HIST_ROWS - 1 + tm, :]
    p2 = u_s[HIST_ROWS - 2:HIST_ROWS - 2 + tm, :]
    if grouped:
        t = lax.broadcasted_iota(jnp.int32, (tm, 1), 0) % group
        p1 = jnp.where(t == 0, h1_ref[...], p1)
        p2 = jnp.where(t <= 1, h2_ref[...], p2)
    o_sc = sb_ref[...] * (cw_ref[0:1, :] * p2 + cw_ref[1:2, :] * p1 + cw_ref[2:3, :] * u)
    mix = jnp.concatenate([gla_ref[...], o_sc, nsa_ref[...]], axis=1).astype(jnp.bfloat16)
    o_ref[...] = x_ref[...] + gt_ref[...] * jnp.dot(mix, w_ref[...], preferred_element_type=jnp.float32)
    if grouped:
        st_ref[...] = u
    else:
        tail = u_s[tm:tm + HIST_ROWS, :]
        st_ref[...] = tail
        u_s[0:HIST_ROWS, :] = tail


def _out_proj(x, o_gla, proj, o_nsa, gt, w_bf16, cw, hist, tm, group):
    m = x.shape[0]
    per_row = gt.shape[0] != 1
    grouped = group > 0
    sc_cols = lambda p: pl.BlockSpec((tm, SC_WIDTH), lambda i: (i, PROJ_OFF[p] // SC_WIDTH))
    in_specs = [
        pl.BlockSpec((tm, D_MODEL), lambda i: (i, 0)),
        pl.BlockSpec((tm, GLA_WIDTH), lambda i: (i, 0)),
        sc_cols(5), sc_cols(6), sc_cols(7),
        pl.BlockSpec((tm, NSA_WIDTH), lambda i: (i, 0)),
        _row_spec(tm, per_row),
        _resident((D_MODEL, D_MODEL)),
        _resident((CONV_W, SC_WIDTH)),
    ]
    if grouped:
        assert m == tm
        in_specs += [_resident((tm, SC_WIDTH)), _resident((tm, SC_WIDTH))]
        hist_args, st_rows = tuple(hist), tm
    else:
        in_specs += [_resident((HIST_ROWS, SC_WIDTH))]
        hist_args, st_rows = (hist,), HIST_ROWS
    return pl.pallas_call(
        functools.partial(_out_proj_kernel, tm=tm, group=group),
        grid=(m // tm,),
        in_specs=in_specs,
        out_specs=[pl.BlockSpec((tm, D_MODEL), lambda i: (i, 0)),
                   pl.BlockSpec((st_rows, SC_WIDTH), lambda i: (0, 0))],
        out_shape=[jax.ShapeDtypeStruct((m, D_MODEL), jnp.float32),
                   jax.ShapeDtypeStruct((st_rows, SC_WIDTH), jnp.float32)],
        scratch_shapes=[pltpu.VMEM((HIST_ROWS + tm, SC_WIDTH), jnp.float32)],
        compiler_params=pltpu.CompilerParams(
            dimension_semantics=("arbitrary",), vmem_limit_bytes=VMEM_LIMIT_BYTES),
        name="shortconv_out_proj",
    )(x, o_gla, proj, proj, proj, o_nsa, gt, w_bf16, cw, *hist_args)


FFN_UP_CHUNK = 512
FFN_ACT_CHUNK = 256


def _ffn_kernel(*refs, tm, group, final):
    grouped = group > 0
    if grouped:
        (x_ref, g_ref, sc_ref, sh_ref, gt_ref, wup_ref, cw_ref, wdn_ref, gf_ref,
         h1_ref, h2_ref, o_ref, st_ref, up_s) = refs
    else:
        (x_ref, g_ref, sc_ref, sh_ref, gt_ref, wup_ref, cw_ref, wdn_ref, gf_ref,
         h0_ref, o_ref, st_ref, up_s) = refs

        @pl.when(pl.program_id(0) == 0)
        def _():
            up_s[0:HIST_ROWS, :] = h0_ref[...]

    x = x_ref[...]
    h = _norm_mod(x, g_ref[...], sc_ref[...], sh_ref[...]).astype(jnp.bfloat16)
    for c in range(2 * D_FF // FFN_UP_CHUNK):
        cols = slice(c * FFN_UP_CHUNK, (c + 1) * FFN_UP_CHUNK)
        up_s[HIST_ROWS:HIST_ROWS + tm, cols] = jnp.dot(
            h, wup_ref[:, cols], preferred_element_type=jnp.float32)

    if grouped:
        t = lax.broadcasted_iota(jnp.int32, (tm, 1), 0) % group

    def conv(cols):
        cur = up_s[HIST_ROWS:HIST_ROWS + tm, cols]
        p1 = up_s[HIST_ROWS - 1:HIST_ROWS - 1 + tm, cols]
        p2 = up_s[HIST_ROWS - 2:HIST_ROWS - 2 + tm, cols]
        if grouped:
            p1 = jnp.where(t == 0, h1_ref[:, cols], p1)
            p2 = jnp.where(t <= 1, h2_ref[:, cols], p2)
        return cw_ref[0:1, cols] * p2 + cw_ref[1:2, cols] * p1 + cw_ref[2:3, cols] * cur

    acc = jnp.zeros((tm, D_MODEL), jnp.float32)
    for c in range(D_FF // FFN_ACT_CHUNK):
        a = conv(slice(c * FFN_ACT_CHUNK, (c + 1) * FFN_ACT_CHUNK))
        b = conv(slice(D_FF + c * FFN_ACT_CHUNK, D_FF + (c + 1) * FFN_ACT_CHUNK))
        act = (a * jax.nn.sigmoid(a) * b).astype(jnp.bfloat16)
        acc = acc + jnp.dot(act, wdn_ref[c * FFN_ACT_CHUNK:(c + 1) * FFN_ACT_CHUNK, :],
                            preferred_element_type=jnp.float32)
    y = x + gt_ref[...] * acc
    if final:
        r = lax.rsqrt(jnp.mean(y * y, axis=-1, keepdims=True) + EPS)
        y = y * r * gf_ref[...]
    o_ref[...] = y

    if grouped:
        st_ref[...] = up_s[HIST_ROWS:HIST_ROWS + tm, :]
    else:
        tail = up_s[tm:tm + HIST_ROWS, :]
        st_ref[...] = tail
        up_s[0:HIST_ROWS, :] = tail


def _ffn(x, g, sc, sh, gt, wup, cw, wdn, g_final, hist, tm, group, final):
    m = x.shape[0]
    grouped = group > 0
    per_row = sc.shape[0] != 1
    ff2 = 2 * D_FF
    in_specs = [
        pl.BlockSpec((tm, D_MODEL), lambda i: (i, 0)),
        _resident((1, D_MODEL)),
        _row_spec(tm, per_row), _row_spec(tm, per_row), _row_spec(tm, per_row),
        _resident((D_MODEL, ff2)),
        _resident((CONV_W, ff2)),
        _resident((D_FF, D_MODEL)),
        _resident((1, D_MODEL)),
    ]
    if grouped:
        assert m == tm
        in_specs += [_resident((tm, ff2)), _resident((tm, ff2))]
        hist_args = tuple(hist)
        st_rows = tm
    else:
        in_specs += [_resident((HIST_ROWS, ff2))]
        hist_args = (hist,)
        st_rows = HIST_ROWS
    return pl.pallas_call(
        functools.partial(_ffn_kernel, tm=tm, group=group, final=final),
        grid=(m // tm,),
        in_specs=in_specs,
        out_specs=[pl.BlockSpec((tm, D_MODEL), lambda i: (i, 0)),
                   pl.BlockSpec((st_rows, ff2), lambda i: (0, 0))],
        out_shape=[jax.ShapeDtypeStruct((m, D_MODEL), jnp.float32),
                   jax.ShapeDtypeStruct((st_rows, ff2), jnp.float32)],
        scratch_shapes=[pltpu.VMEM((HIST_ROWS + tm, ff2), jnp.float32)],
        compiler_params=pltpu.CompilerParams(
            dimension_semantics=("arbitrary",), vmem_limit_bytes=VMEM_LIMIT_BYTES),
        name="conv_ffn",
    )(x, g, sc, sh, gt, wup, cw, wdn, g_final, *hist_args)


CMP_ROW = CMP_STRIDE * NSA_KV_WIDTH
CMP_HID = 2 * NSA_KV_HEADS * CMP_HIDDEN


def _gelu_tanh(x):
    return 0.5 * x * (1.0 + jnp.tanh(0.7978845608028654 * (x + 0.044715 * (x * x * x))))


def _compress_kernel(x_ref, xn_ref, pos_ref, wl_ref, wt_ref, w2_ref, o_ref, tr_s, *, tm):
    bf = jnp.bfloat16
    f32 = jnp.float32
    x = x_ref[...].astype(bf)
    lead = jnp.dot(x, wl_ref[...], preferred_element_type=f32)
    tr_s[0:tm, :] = jnp.dot(x, wt_ref[...], preferred_element_type=f32)
    tr_s[tm:tm + SUBLANES, :] = jnp.dot(xn_ref[...].astype(bf), wt_ref[...], preferred_element_type=f32)
    bias = (jnp.dot(pos_ref[0].astype(bf), wl_ref[...], preferred_element_type=f32)
            + jnp.dot(pos_ref[1].astype(bf), wt_ref[...], preferred_element_type=f32))[0:1, :]
    hid = _gelu_tanh(lead + tr_s[1:tm + 1, :] + bias)
    o_ref[...] = jnp.dot(hid.astype(bf), w2_ref[...], preferred_element_type=f32)


def _compress_weights(pos_emb, w1, w2):
    eye = jnp.eye(2, dtype=jnp.float32)
    w1f = jnp.einsum('kldh,kK,gG->lkgdKGh', w1, eye, eye).reshape(CMP_BLOCK, NSA_KV_WIDTH, CMP_HID)
    w1f = w1f.astype(jnp.bfloat16)
    w2b = jnp.einsum('khd,kK,gG->kghKGd', w2, eye, eye).reshape(CMP_HID, NSA_KV_WIDTH).astype(jnp.bfloat16)
    posf = jnp.broadcast_to(pos_emb.transpose(1, 0, 2)[:, :, None, :], (CMP_BLOCK, 2, NSA_KV_HEADS, NSA_HD))
    posf = posf.reshape(CMP_BLOCK, NSA_KV_WIDTH)
    pos = jnp.zeros((CMP_BLOCK, SUBLANES, NSA_KV_WIDTH), jnp.float32).at[:, 0].set(posf)
    pos_rows = jnp.zeros((2, SUBLANES, CMP_ROW), jnp.float32).at[:, 0].set(posf.reshape(2, CMP_ROW))
    w1kv = jnp.einsum('kldh,gG->lkgdGh', w1, eye).reshape(
        CMP_BLOCK // 2, 2, 2, NSA_KV_WIDTH // 2, CMP_HID // 2).transpose(0, 2, 1, 3, 4).reshape(
        CMP_BLOCK // 2, 2, NSA_KV_WIDTH, CMP_HID // 2).astype(jnp.bfloat16)
    return dict(pos=pos, w1=w1kv, w2=w2b, pos_rows=pos_rows,
                wl=w1f[:CMP_STRIDE].reshape(CMP_ROW, CMP_HID), wt=w1f[CMP_STRIDE:].reshape(CMP_ROW, CMP_HID))


def _compress(x, cw, tm):
    pos, wl, wt, w2b = cw['pos_rows'], cw['wl'], cw['wt'], cw['w2']
    n = x.shape[0]
    nb8 = n // SUBLANES
    return pl.pallas_call(
        functools.partial(_compress_kernel, tm=tm),
        grid=(n // tm,),
        in_specs=[
            pl.BlockSpec((tm, CMP_ROW), lambda i: (i, 0)),
            pl.BlockSpec((SUBLANES, CMP_ROW), lambda i: (jnp.minimum((i + 1) * (tm // SUBLANES), nb8 - 1), 0)),
            _resident((2, SUBLANES, CMP_ROW)),
            _resident((CMP_ROW, CMP_HID)), _resident((CMP_ROW, CMP_HID)),
            _resident((CMP_HID, NSA_KV_WIDTH)),
        ],
        out_specs=pl.BlockSpec((tm, NSA_KV_WIDTH), lambda i: (i, 0)),
        out_shape=jax.ShapeDtypeStruct((n, NSA_KV_WIDTH), jnp.float32),
        scratch_shapes=[pltpu.VMEM((tm + SUBLANES, CMP_HID), jnp.float32)],
        compiler_params=pltpu.CompilerParams(
            dimension_semantics=("arbitrary",), vmem_limit_bytes=VMEM_LIMIT_BYTES),
        name="nsa_compress",
    )(x, x, pos, wl, wt, w2b)


NSA_R = NSA_HEADS // NSA_KV_HEADS
QL = NSA_R * Q_BLOCK
QLL = NSA_KV_HEADS * QL
SLC_TILE = 512
BLK_PER_TILE = SLC_TILE // SEL_BLOCK
WIN_TILE = WINDOW + Q_BLOCK
CMP_PER_SEL = SEL_BLOCK // CMP_STRIDE
M_INIT = -1e29
QK_SCALE = NSA_HD ** -0.5 * 1.4426950408889634
ONES_ROWS = BF16_SUBLANES
SELECT_ROW_BUCKETS = (32, 64, 96, 128, 192)


def _tile_lanes(v, reps):
    return jnp.concatenate([v] * reps, axis=1)


def _nsa_prompt_kernel(q_ref, gate_ref, kc_ref, vct_ref, kslc_ref, vtslc_ref, kwin_ref, vtwin_ref,
                       o_ref, sc_s, sel_s, m_s, l_s, acc_s, ow_s, oc_s, mt_s, al_s, s_s, pt_s, *, n_sel):
    f32, bf = jnp.float32, jnp.bfloat16
    G, HD = NSA_KV_HEADS, NSA_HD
    n = pl.program_id(0)
    qn = q_ref[...] * QK_SCALE
    tq = [qn[:, j * LANES:(j + 1) * LANES].T for j in range(NSA_WIDTH // LANES)]
    heads_per_t = LANES // HD
    head_t = lambda h: tq[h // heads_per_t][(h % heads_per_t) * HD:(h % heads_per_t + 1) * HD, :]
    zero = jnp.zeros((HD, Q_BLOCK), f32)
    qbd = jnp.concatenate(
        [jnp.concatenate([head_t(g * NSA_R + r) if g == gp else zero for g in range(G) for r in range(NSA_R)],
                         axis=1) for gp in range(G)], axis=0).astype(bf)
    lane = lax.broadcasted_iota(jnp.int32, (1, Q_BLOCK), 1)
    pos_q = n * Q_BLOCK + lane
    pos_l = _tile_lanes(pos_q, QLL // Q_BLOCK)
    jrow = lax.broadcasted_iota(jnp.int32, (n_sel, 1), 0)

    def reset():
        m_s[...] = jnp.full((1, QLL), M_INIT, f32)
        l_s[...] = jnp.zeros((1, QLL), f32)
        acc_s[...] = jnp.zeros((HD, QLL), f32)

    def pass1(slot, k_ref, start, rows, bias_fn):
        s = jnp.dot(k_ref[pl.ds(start, rows), :].astype(bf), qbd, preferred_element_type=f32)
        top = jnp.full((SUBLANES, QLL), NEG, f32)
        for i in range(rows // SEL_BLOCK):
            blk = slice(i * SEL_BLOCK, (i + 1) * SEL_BLOCK)
            sb = s[blk, :] + bias_fn(i)
            s_s[slot, blk, :] = sb
            for r in range(SEL_BLOCK // SUBLANES):
                top = jnp.maximum(top, sb[r * SUBLANES:(r + 1) * SUBLANES, :])
        m_old = m_s[...]
        m_new = jnp.maximum(m_old, jnp.max(top, axis=0, keepdims=True))
        mt_s[slot] = m_new
        al_s[slot] = jnp.exp2(m_old - m_new)
        m_s[...] = m_new

    def pass2(slot, vta_ref, start, rows):
        m_new = mt_s[slot]
        for i in range(rows // SEL_BLOCK):
            blk = slice(i * SEL_BLOCK, (i + 1) * SEL_BLOCK)
            pt_s[slot, blk, :] = jnp.exp2(s_s[slot, blk, :] - m_new).astype(bf)
        pv, psum = [], []
        for g in range(G):
            r = jnp.dot(vta_ref[g, :, pl.ds(start, rows)], pt_s[slot, 0:rows, g * QL:(g + 1) * QL],
                        preferred_element_type=f32)
            pv.append(r[0:HD, :])
            psum.append(r[HD:HD + 1, :])
        alpha = al_s[slot]
        acc_s[...] = acc_s[...] * alpha + jnp.concatenate(pv, axis=1)
        l_s[...] = l_s[...] * alpha + jnp.concatenate(psum, axis=1)

    def finish():
        return acc_s[...] * (1.0 / jnp.maximum(l_s[...], TINY))

    reset()
    wstart = pl.multiple_of(jnp.maximum(n * Q_BLOCK - WINDOW, 0), Q_BLOCK)
    rel = pos_q - (wstart + lax.broadcasted_iota(jnp.int32, (WIN_TILE, 1), 0))
    wbias = jnp.where((rel >= 0) & (rel <= WINDOW), 0.0, NEG)
    pass1(1, kwin_ref, wstart, WIN_TILE,
          lambda i: _tile_lanes(wbias[i * SEL_BLOCK:(i + 1) * SEL_BLOCK, :], QLL // Q_BLOCK))
    pass2(1, vtwin_ref, wstart, WIN_TILE)
    ow_s[...] = finish()

    def compress_and_select(rows):
        jrow = lax.broadcasted_iota(jnp.int32, (rows, 1), 0)
        s_c, mk_c = [], []
        m = jnp.full((1, QLL), NEG, f32)
        for c in range(CMP_PER_SEL):
            s = jnp.dot(kc_ref[c * n_sel:c * n_sel + rows, :], qbd, preferred_element_type=f32)
            mk = jrow * SEL_BLOCK + (c * CMP_STRIDE + CMP_BLOCK - 1) <= pos_l
            s = jnp.where(mk, s, NEG)
            m = jnp.maximum(m, jnp.max(s, axis=0, keepdims=True))
            s_c.append(s)
            mk_c.append(mk)
        e_c = [jnp.where(mk_c[c], jnp.exp2(s_c[c] - m), 0.0) for c in range(CMP_PER_SEL)]
        l = e_c[0].sum(axis=0, keepdims=True)
        for c in range(1, CMP_PER_SEL):
            l = l + e_c[c].sum(axis=0, keepdims=True)
        inv = 1.0 / jnp.maximum(l, TINY)
        o_cmp = [jnp.zeros((HD, QL), f32) for _ in range(G)]
        pg = []
        for c in range(CMP_PER_SEL):
            p = e_c[c] * inv
            pb = p.astype(bf)
            for g in range(G):
                o_cmp[g] = o_cmp[g] + jnp.dot(vct_ref[g * HD:(g + 1) * HD, c * n_sel:c * n_sel + rows],
                                              pb[:, g * QL:(g + 1) * QL], preferred_element_type=f32)
            pg.append([sum(p[:, g * QL + r * Q_BLOCK:g * QL + (r + 1) * Q_BLOCK] for r in range(NSA_R))
                       for g in range(G)])
        oc_s[...] = jnp.concatenate(o_cmp, axis=1)

        cur = pos_q // SEL_BLOCK
        forced = (jrow == 0) | (jrow == cur) | (jrow == cur - 1)
        allowed = jrow * SEL_BLOCK <= pos_q
        jrow_f = jrow.astype(f32)
        live = slice(0, rows)
        for g in range(G):
            last = pg[CMP_PER_SEL - 1][g]
            prev = jnp.where(jrow == 0, 0.0, pltpu.roll(last, 1, 0))
            inner = pg[0][g]
            for c in range(1, CMP_PER_SEL - 1):
                inner = inner + pg[c][g]
            p_slc = 2.0 * inner + last + prev
            sc_s[g, live, :] = jnp.where(forced, FORCE, jnp.where(allowed, p_slc, -1.0))
            sel_s[g, live, :] = jnp.zeros((rows, Q_BLOCK), f32)

        def pick(_, carry):
            for g in range(G):
                s = sc_s[g, live, :]
                top = jnp.max(s, axis=0, keepdims=True)
                first = jnp.min(jnp.where(s == top, jrow_f, float(rows)), axis=0, keepdims=True)
                hit = jrow_f == first
                sc_s[g, live, :] = jnp.where(hit, -jnp.inf, s)
                sel_s[g, live, :] = jnp.where(hit, 1.0, sel_s[g, live, :])
            return carry

        lax.fori_loop(0, min(SEL_TOPN, rows), pick, 0)
        for g in range(G):
            sel_s[g, live, :] = jnp.where(allowed & (sel_s[g, live, :] > 0.5), 0.0, NEG)
            if rows < n_sel:
                sel_s[g, rows:n_sel, :] = jnp.full((n_sel - rows, Q_BLOCK), NEG, f32)

    visible = (n + 1) * (Q_BLOCK // SEL_BLOCK)
    lo = 0
    for rows in sorted({min(r, n_sel) for r in SELECT_ROW_BUCKETS} | {n_sel}):
        pl.when((visible > lo) & (visible <= rows))(functools.partial(compress_and_select, rows))
        lo = rows
    o_cmp = oc_s[...]

    def tile_start(kt):
        return pl.multiple_of(kt * SLC_TILE, SLC_TILE)

    def slc_pass1(slot, kt, causal):
        start = tile_start(kt)
        selb = [sel_s[g, pl.ds(pl.multiple_of(kt * BLK_PER_TILE, BLK_PER_TILE), BLK_PER_TILE), :]
                for g in range(G)]

        def bias_fn(i):
            row = jnp.concatenate([_tile_lanes(selb[g][i:i + 1, :], NSA_R) for g in range(G)], axis=1)
            if not causal:
                return row
            tok = start + i * SEL_BLOCK + lax.broadcasted_iota(jnp.int32, (SEL_BLOCK, 1), 0)
            return jnp.where(tok <= pos_l, row, NEG)

        pass1(slot, kslc_ref, start, SLC_TILE, bias_fn)

    def slc_pass2(slot, kt):
        pass2(slot, vtslc_ref, tile_start(kt), SLC_TILE)

    reset()
    diag = (n * Q_BLOCK) // SLC_TILE
    slc_pass1(0, diag, True)

    def slc_pair(j, carry):
        slc_pass1(1, 2 * j, False)
        slc_pass2(0, jnp.where(j == 0, diag, 2 * j - 1))
        slc_pass1(0, 2 * j + 1, False)
        slc_pass2(1, 2 * j)
        return carry

    pairs = diag // 2
    lax.fori_loop(0, pairs, slc_pair, 0)
    pending = jnp.where(pairs == 0, diag, 2 * pairs - 1)

    @pl.when(diag % 2 == 1)
    def _():
        slc_pass1(1, diag - 1, False)
        slc_pass2(0, pending)
        slc_pass2(1, diag - 1)

    @pl.when(diag % 2 == 0)
    def _():
        slc_pass2(0, pending)

    o_slc = finish()

    o_win = ow_s[...]

    gt = jax.nn.sigmoid(gate_ref[...].T)
    gate = lambda c: jnp.concatenate([gt[h * 3 + c:h * 3 + c + 1, :] for h in range(NSA_HEADS)], axis=1)
    o = gate(0) * o_cmp + gate(1) * o_slc + gate(2) * o_win
    o_ref[...] = jnp.concatenate(
        [jnp.concatenate([o[:, (heads_per_t * j + t) * Q_BLOCK:(heads_per_t * j + t + 1) * Q_BLOCK]
                          for t in range(heads_per_t)], axis=0).T for j in range(NSA_WIDTH // LANES)], axis=1)


def _nsa_prompt_attention(proj, kvc, nslc, nwin):
    T = proj.shape[0]
    nb, n_sel = T // Q_BLOCK, T // SEL_BLOCK
    G, R, HD = NSA_KV_HEADS, NSA_R, NSA_HD
    bf = jnp.bfloat16
    half = G * HD
    assert PROJ_OFF[8] % NSA_WIDTH == 0 and PROJ_OFF[12] % LANES == 0
    kvp = kvc.reshape(n_sel, CMP_PER_SEL, NSA_KV_WIDTH).transpose(1, 0, 2).reshape(T // CMP_STRIDE, NSA_KV_WIDTH)
    kc, vct = kvp[:, :half].astype(bf), kvp[:, half:].T.astype(bf)
    def vt_ones(v):
        vt = v.T.reshape(G, HD, T)
        return jnp.concatenate([vt, jnp.ones((G, ONES_ROWS, T), vt.dtype)], axis=1).astype(bf)
    vtslc, vtwin = vt_ones(nslc[:, half:]), vt_ones(nwin[:, half:])
    k_cols = lambda p: pl.BlockSpec((T, half), lambda i: (0, PROJ_OFF[p] // half), pipeline_mode=pl.Buffered(1))
    out = pl.pallas_call(
        functools.partial(_nsa_prompt_kernel, n_sel=n_sel),
        grid=(nb,),
        in_specs=[
            pl.BlockSpec((Q_BLOCK, NSA_WIDTH), lambda i: (i, PROJ_OFF[8] // NSA_WIDTH)),
            pl.BlockSpec((Q_BLOCK, LANES), lambda i: (i, PROJ_OFF[12] // LANES)),
            _resident((T // CMP_STRIDE, half)), _resident((half, T // CMP_STRIDE)),
            k_cols(10), _resident((G, HD + ONES_ROWS, T)),
            k_cols(11), _resident((G, HD + ONES_ROWS, T)),
        ],
        out_specs=pl.BlockSpec((Q_BLOCK, NSA_WIDTH), lambda i: (i, 0)),
        out_shape=jax.ShapeDtypeStruct((T, NSA_WIDTH), jnp.float32),
        scratch_shapes=[
            pltpu.VMEM((G, n_sel, Q_BLOCK), jnp.float32),
            pltpu.VMEM((G, n_sel, Q_BLOCK), jnp.float32),
            pltpu.VMEM((1, QLL), jnp.float32),
            pltpu.VMEM((1, QLL), jnp.float32),
            pltpu.VMEM((HD, QLL), jnp.float32),
            pltpu.VMEM((HD, QLL), jnp.float32),
            pltpu.VMEM((HD, QLL), jnp.float32),
            pltpu.VMEM((2, 1, QLL), jnp.float32),
            pltpu.VMEM((2, 1, QLL), jnp.float32),
            pltpu.VMEM((2, max(WIN_TILE, SLC_TILE), QLL), jnp.float32),
            pltpu.VMEM((2, max(WIN_TILE, SLC_TILE), QLL), jnp.bfloat16),
        ],
        compiler_params=pltpu.CompilerParams(
            dimension_semantics=("arbitrary",), vmem_limit_bytes=VMEM_LIMIT_BYTES),
        name="nsa_prompt_attention",
    )(proj, proj, kc, vct, proj, vtslc, proj, vtwin)
    return out


PAGE_ROWS = PAGE_SIZE // CMP_STRIDE
SQ = 16
SAMPLE_LANES = NSA_HEADS * SQ
HALF_PAGES = 64
SAMPLE_TILE = 4096


def _page_view(pool):
    d, n_pool = pool.shape[:2]
    return pool.transpose(0, 1, 3, 4, 5, 2).reshape(d * n_pool, 2, NSA_KV_HEADS * NSA_HD, pool.shape[2])


def _page_copy(pool_ref, buf_ref, sem_ref, page, slot, idx):
    return pltpu.make_async_copy(pool_ref.at[page], buf_ref.at[slot, idx], sem_ref.at[slot])


def _gather_schedule(issue_fn, wait_fn):
    s = pl.program_id(0)

    @pl.when(s == 0)
    def _():
        issue_fn(s, 0)

    @pl.when(s + 1 < pl.num_programs(0))
    def _():
        issue_fn(s + 1, (s + 1) % 2)

    wait_fn(s, s % 2)


def _compress_paged_kernel(pt_ref, pool_ref, pos_ref, w1_ref, w2_ref, o_ref, buf, sem, tok_s,
                           *, page_base, n_pages):
    bf, f32 = jnp.bfloat16, jnp.float32
    rows = HALF_PAGES * PAGE_ROWS

    def copies(step, slot, fn):
        b, half = step // 2, step % 2

        def body(i, c):
            p = jnp.minimum(half * HALF_PAGES + i, n_pages - 1)
            fn(_page_copy(pool_ref, buf, sem, page_base + pt_ref[b, p], slot, i))
            return c

        lax.fori_loop(0, HALF_PAGES + 1, body, 0)

    _gather_schedule(lambda s, slot: copies(s, slot, lambda cp: cp.start()),
                     lambda s, slot: copies(s, slot, lambda cp: cp.wait()))
    slot = pl.program_id(0) % 2

    half = NSA_KV_HEADS * NSA_HD

    r_out = lax.broadcasted_iota(jnp.int32, (PAGE_SIZE, 1), 0)
    t_in = lax.broadcasted_iota(jnp.int32, (1, PAGE_SIZE), 1)
    regroup = ((r_out % PAGE_ROWS) * CMP_STRIDE + r_out // PAGE_ROWS == t_in).astype(bf)

    def to_offset_rows(i, c):
        dst = pl.ds(pl.multiple_of(i * PAGE_ROWS, PAGE_ROWS), PAGE_ROWS)
        page = buf[slot, i].reshape(NSA_KV_WIDTH, PAGE_SIZE).astype(bf)
        t = lax.dot_general(regroup, page, (((1,), (1,)), ((), ())), preferred_element_type=f32)
        for kv in range(2):
            for l in range(CMP_STRIDE):
                tok_s[kv, l, dst, :] = t[l * PAGE_ROWS:(l + 1) * PAGE_ROWS, kv * half:(kv + 1) * half]
        return c

    lax.fori_loop(0, HALF_PAGES + 1, to_offset_rows, 0, unroll=5)
    acc = [jnp.zeros((rows, CMP_HID // 2), f32) for _ in range(2)]
    bias = [jnp.zeros((SUBLANES, CMP_HID // 2), f32) for _ in range(2)]
    for l in range(0, CMP_BLOCK, 2):
        first = l // CMP_STRIDE
        for kv in range(2):
            x = jnp.concatenate([tok_s[kv, l % CMP_STRIDE + d, first:first + rows, :] for d in range(2)],
                                axis=1).astype(bf)
            w = w1_ref[l // 2, kv]
            p = jnp.concatenate([pos_ref[l + d, :, kv * half:(kv + 1) * half] for d in range(2)], axis=1)
            acc[kv] = acc[kv] + jnp.dot(x, w, preferred_element_type=f32)
            bias[kv] = bias[kv] + jnp.dot(p.astype(bf), w, preferred_element_type=f32)
    hid = _gelu_tanh(jnp.concatenate(acc, axis=1) + jnp.concatenate(bias, axis=1)[0:1, :])
    o_ref[...] = jnp.dot(hid.astype(bf), w2_ref[...], preferred_element_type=f32)


def _compress_paged(page_table, pool, layer, cw):
    nbatch, n_pages = page_table.shape
    assert n_pages == 2 * HALF_PAGES
    rows = HALF_PAGES * PAGE_ROWS
    const = lambda shape: pl.BlockSpec(shape, lambda s, pt: (0,) * len(shape), pipeline_mode=pl.Buffered(1))
    return pl.pallas_call(
        functools.partial(_compress_paged_kernel, page_base=layer * (pool.shape[0] // DEPTH), n_pages=n_pages),
        grid_spec=pltpu.PrefetchScalarGridSpec(
            num_scalar_prefetch=1,
            grid=(2 * nbatch,),
            in_specs=[
                pl.BlockSpec(memory_space=pl.ANY),
                const((CMP_BLOCK, SUBLANES, NSA_KV_WIDTH)),
                const((CMP_BLOCK // 2, 2, NSA_KV_WIDTH, CMP_HID // 2)),
                const((CMP_HID, NSA_KV_WIDTH)),
            ],
            out_specs=pl.BlockSpec((rows, NSA_KV_WIDTH), lambda s, pt: (s, 0)),
            scratch_shapes=[
                pltpu.VMEM((2, HALF_PAGES + 1) + pool.shape[1:], jnp.float32),
                pltpu.SemaphoreType.DMA((2,)),
                pltpu.VMEM((2, CMP_STRIDE, (HALF_PAGES + 1) * PAGE_ROWS, NSA_KV_WIDTH // 2), jnp.float32),
            ]),
        out_shape=jax.ShapeDtypeStruct((2 * nbatch * rows, NSA_KV_WIDTH), jnp.float32),
        compiler_params=pltpu.CompilerParams(
            dimension_semantics=("arbitrary",), vmem_limit_bytes=VMEM_LIMIT_BYTES),
        name="nsa_compress_paged",
    )(page_table, pool, cw['pos'], cw['w1'], cw['w2'])


def _nsa_sample_kernel(pt_ref, pool_ref, qbd_ref, gate_ref, kc_ref, vc_ref, nslc_ref, wcache_ref, nwin_ref,
                       o_ref, buf, sem, sc_s, sel_s, m_s, l_s, acc_s, *, page_base, n_pages, n_sel, t_new):
    f32, bf = jnp.float32, jnp.bfloat16
    HD, LN = NSA_HD, SAMPLE_LANES
    half = NSA_KV_HEADS * HD
    past = n_pages * PAGE_SIZE
    contract_rows = (((0,), (0,)), ((), ()))

    def copies(step, slot, fn):
        def body(p, c):
            fn(_page_copy(pool_ref, buf, sem, page_base + pt_ref[step, p], slot, p))
            return c

        lax.fori_loop(0, n_pages, body, 0)

    _gather_schedule(lambda s, slot: copies(s, slot, lambda cp: cp.start()),
                     lambda s, slot: copies(s, slot, lambda cp: cp.wait()))
    slot = pl.program_id(0) % 2

    qbd = qbd_ref[0]
    lane = lax.broadcasted_iota(jnp.int32, (1, LN), 1)
    pos_l = past + lane % SQ
    group0 = lane < LN // NSA_KV_HEADS
    jrow = lax.broadcasted_iota(jnp.int32, (n_sel, 1), 0)
    jrow_f = jrow.astype(f32)

    def group_rows(full):
        return jnp.where(group0, full[0:HD, :], full[HD:2 * HD, :])

    def pv(v, pt):
        return group_rows(lax.dot_general(v.astype(bf), pt, contract_rows, preferred_element_type=f32))

    def scores_t(kt):
        return lax.dot_general(kt.astype(bf), qbd, contract_rows, preferred_element_type=f32)

    li = lax.broadcasted_iota(jnp.int32, (LN, LN), 0)
    lj = lax.broadcasted_iota(jnp.int32, (LN, LN), 1)
    same = ((li // (NSA_R * SQ) == lj // (NSA_R * SQ)) & (li % SQ == lj % SQ)).astype(bf)

    def head_sum(p):
        hi = p.astype(bf)
        r1 = p - hi.astype(f32)
        mid = r1.astype(bf)
        lo = (r1 - mid.astype(f32)).astype(bf)
        return (jnp.dot(hi, same, preferred_element_type=f32) + jnp.dot(mid, same, preferred_element_type=f32)
                + jnp.dot(lo, same, preferred_element_type=f32))

    s_c, mk_c = [], []
    m = jnp.full((1, LN), NEG, f32)
    for c in range(CMP_PER_SEL):
        s = jnp.dot(kc_ref[0, c * n_sel:(c + 1) * n_sel, :], qbd, preferred_element_type=f32)
        mk = jrow * SEL_BLOCK + (c * CMP_STRIDE + CMP_BLOCK - 1) <= pos_l
        s = jnp.where(mk, s, NEG)
        m = jnp.maximum(m, jnp.max(s, axis=0, keepdims=True))
        s_c.append(s)
        mk_c.append(mk)
    e_c = [jnp.where(mk_c[c], jnp.exp2(s_c[c] - m), 0.0) for c in range(CMP_PER_SEL)]
    l = e_c[0].sum(axis=0, keepdims=True)
    for c in range(1, CMP_PER_SEL):
        l = l + e_c[c].sum(axis=0, keepdims=True)
    inv = 1.0 / jnp.maximum(l, TINY)
    o_cmp = jnp.zeros((HD, LN), f32)
    pg = []
    for c in range(CMP_PER_SEL):
        p = e_c[c] * inv
        o_cmp = o_cmp + pv(vc_ref[0, c * n_sel:(c + 1) * n_sel, :], p.astype(bf))
        pg.append(head_sum(p))

    cur = pos_l // SEL_BLOCK
    forced = (jrow == 0) | (jrow == cur) | (jrow == cur - 1)
    allowed = jrow * SEL_BLOCK <= pos_l
    last = pg[CMP_PER_SEL - 1]
    prev = jnp.where(jrow == 0, 0.0, pltpu.roll(last, 1, 0))
    inner = pg[0]
    for c in range(1, CMP_PER_SEL - 1):
        inner = inner + pg[c]
    sc_s[...] = jnp.where(forced, FORCE, jnp.where(allowed, 2.0 * inner + last + prev, -1.0))
    sel_s[...] = jnp.zeros((n_sel, LN), f32)

    def pick(_, carry):
        s = sc_s[...]
        top = jnp.max(s, axis=0, keepdims=True)
        first = jnp.min(jnp.where(s == top, jrow_f, float(n_sel)), axis=0, keepdims=True)
        hit = jrow_f == first
        sc_s[...] = jnp.where(hit, -jnp.inf, s)
        sel_s[...] = jnp.where(hit, 1.0, sel_s[...])
        return carry

    lax.fori_loop(0, SEL_TOPN, pick, 0)
    sel_s[...] = jnp.where(allowed, sel_s[...], 0.0)

    def reset():
        m_s[...] = jnp.full((1, LN), M_INIT, f32)
        l_s[...] = jnp.zeros((1, LN), f32)
        acc_s[...] = jnp.zeros((HD, LN), f32)

    def update(s_blocks, pv_fn):
        m_old = m_s[...]
        m_new = m_old
        for s in s_blocks:
            m_new = jnp.maximum(m_new, jnp.max(s, axis=0, keepdims=True))
        alpha = jnp.exp2(m_old - m_new)
        e_blocks = [jnp.exp2(s - m_new) for s in s_blocks]
        l_new = l_s[...] * alpha
        for e in e_blocks:
            l_new = l_new + e.sum(axis=0, keepdims=True)
        pt = e_blocks[0] if len(e_blocks) == 1 else jnp.concatenate(e_blocks, axis=0)
        acc_s[...] = acc_s[...] * alpha + pv_fn(pt.astype(bf))
        m_s[...] = m_new
        l_s[...] = l_new

    def finish():
        return acc_s[...] * (1.0 / jnp.maximum(l_s[...], TINY))

    def scores(kv):
        return jnp.dot(kv[:, 0:half].astype(bf), qbd, preferred_element_type=f32)

    reset()
    pages_per_tile = SAMPLE_TILE // PAGE_SIZE
    blk_per_page = PAGE_SIZE // SEL_BLOCK
    blk_per_tile = SAMPLE_TILE // SEL_BLOCK

    def slc_body(kt, carry):
        selb = sel_s[pl.ds(pl.multiple_of(kt * blk_per_tile, blk_per_tile), blk_per_tile), :]
        blocks = []
        for j in range(pages_per_tile):
            s = scores_t(buf[slot, kt * pages_per_tile + j, 0])
            for h in range(blk_per_page):
                i = j * blk_per_page + h
                blocks.append(jnp.where(selb[i:i + 1, :] > 0.5, s[h * SEL_BLOCK:(h + 1) * SEL_BLOCK, :], NEG))

        def pv_pages(pt):
            full = jnp.zeros((2 * HD, LN), f32)
            for j in range(pages_per_tile):
                full = full + jnp.dot(buf[slot, kt * pages_per_tile + j, 1].astype(bf),
                                      pt[j * PAGE_SIZE:(j + 1) * PAGE_SIZE, :], preferred_element_type=f32)
            return group_rows(full)

        update(blocks, pv_pages)
        return carry

    lax.fori_loop(0, past // SAMPLE_TILE, slc_body, 0)
    rows_new = lax.broadcasted_iota(jnp.int32, (SEL_BLOCK, 1), 0)
    kv = nslc_ref[0]
    keep = (sel_s[past // SEL_BLOCK:past // SEL_BLOCK + 1, :] > 0.5) & (past + rows_new <= pos_l)
    update([jnp.where(keep, scores(kv), NEG)], functools.partial(pv, kv[:, half:]))
    o_slc = finish()

    reset()
    wb = wcache_ref.shape[-1]
    rel = pos_l - (past - wb + lax.broadcasted_iota(jnp.int32, (wb, 1), 0))
    update([jnp.where((rel >= 0) & (rel <= WINDOW), scores_t(wcache_ref[0, 0, 0]), NEG)],
           lambda pt: group_rows(jnp.dot(wcache_ref[0, 0, 1].astype(bf), pt, preferred_element_type=f32)))
    kv = nwin_ref[0]
    rel = pos_l - (past + rows_new)
    update([jnp.where((rel >= 0) & (rel <= WINDOW) & (rows_new < t_new), scores(kv), NEG)],
           functools.partial(pv, kv[:, half:]))
    o_win = finish()

    gate = jax.nn.sigmoid(gate_ref[0])
    o_ref[0] = gate[0:1, :] * o_cmp + gate[1:2, :] * o_slc + gate[2:3, :] * o_win


def _pad_rows(a, rows):
    return jnp.pad(a, ((0, 0), (0, rows - a.shape[1]), (0, 0)))


def _nsa_sample(proj, q, kv_cmp, kv_slc, kv_win, gates, pool_cmp, pool_slc, page_table, win_cache, layer, cmp_w):
    B, Tn = q.shape[:2]
    G, R, HD = NSA_KV_HEADS, NSA_R, NSA_HD
    n_pages = page_table.shape[1]
    n_pool = pool_cmp.shape[1]
    past = n_pages * PAGE_SIZE
    wb = win_cache.shape[2]
    assert Tn <= SQ and Tn <= SEL_BLOCK and past % SAMPLE_TILE == 0 and wb % SUBLANES == 0
    bf = jnp.bfloat16
    half = G * HD
    kvc = _compress_paged(page_table, _page_view(pool_cmp), layer, cmp_w)
    n_blk = past // CMP_STRIDE
    n_sel = _round_up(past // SEL_BLOCK + 1, BF16_SUBLANES)
    kvp = kvc.reshape(B, n_blk // CMP_PER_SEL, CMP_PER_SEL, NSA_KV_WIDTH).transpose(0, 2, 1, 3)
    kvp = jnp.pad(kvp, ((0, 0), (0, 0), (0, n_sel - n_blk // CMP_PER_SEL), (0, 0)))
    kvp = kvp.reshape(B, CMP_PER_SEL * n_sel, NSA_KV_WIDTH).astype(bf)
    kc, vc = kvp[..., :half], kvp[..., half:]
    qt = jnp.pad((q * QK_SCALE).reshape(B, Tn, G, R, HD), ((0, 0), (0, SQ - Tn), (0, 0), (0, 0), (0, 0)))
    qt = qt.transpose(0, 2, 4, 3, 1)
    qbd = jnp.einsum('bgdrq,gh->bgdhrq', qt, jnp.eye(G, dtype=jnp.float32)).reshape(B, half, SAMPLE_LANES).astype(bf)
    gate = jnp.pad(gates.reshape(B, Tn, G, R, 3), ((0, 0), (0, SQ - Tn), (0, 0), (0, 0), (0, 0)))
    gate = gate.transpose(0, 4, 2, 3, 1).reshape(B, 3, SAMPLE_LANES)
    nslc = _pad_rows(kv_slc.reshape(B, Tn, NSA_KV_WIDTH), SEL_BLOCK)
    nwin = _pad_rows(kv_win.reshape(B, Tn, NSA_KV_WIDTH), SEL_BLOCK)
    per_b = lambda shape: pl.BlockSpec((1,) + shape, lambda b, pt: (b,) + (0,) * len(shape))
    out = pl.pallas_call(
        functools.partial(_nsa_sample_kernel, page_base=layer * n_pool, n_pages=n_pages, n_sel=n_sel, t_new=Tn),
        grid_spec=pltpu.PrefetchScalarGridSpec(
            num_scalar_prefetch=1,
            grid=(B,),
            in_specs=[
                pl.BlockSpec(memory_space=pl.ANY),
                per_b((half, SAMPLE_LANES)), per_b((3, SAMPLE_LANES)),
                per_b((CMP_PER_SEL * n_sel, half)), per_b((CMP_PER_SEL * n_sel, half)),
                per_b((SEL_BLOCK, NSA_KV_WIDTH)),
                pl.BlockSpec((1, 1, 2, half, wb), lambda b, pt: (layer, b, 0, 0, 0)),
                per_b((SEL_BLOCK, NSA_KV_WIDTH)),
            ],
            out_specs=per_b((HD, SAMPLE_LANES)),
            scratch_shapes=[
                pltpu.VMEM((2, n_pages, 2, half, PAGE_SIZE), jnp.float32),
                pltpu.SemaphoreType.DMA((2,)),
                pltpu.VMEM((n_sel, SAMPLE_LANES), jnp.float32),
                pltpu.VMEM((n_sel, SAMPLE_LANES), jnp.float32),
                pltpu.VMEM((1, SAMPLE_LANES), jnp.float32),
                pltpu.VMEM((1, SAMPLE_LANES), jnp.float32),
                pltpu.VMEM((HD, SAMPLE_LANES), jnp.float32),
            ]),
        out_shape=jax.ShapeDtypeStruct((B, HD, SAMPLE_LANES), jnp.float32),
        compiler_params=pltpu.CompilerParams(
            dimension_semantics=("arbitrary",), vmem_limit_bytes=VMEM_LIMIT_BYTES),
        name="nsa_sample_attention",
    )(page_table, _page_view(pool_slc), qbd, gate, kc, vc, nslc,
      win_cache.transpose(0, 1, 3, 4, 5, 2).reshape(DEPTH, B, 2, half, wb), nwin)
    o = out.reshape(B, HD, G, R, SQ)[..., :Tn].transpose(0, 4, 2, 3, 1).reshape(B, Tn, NSA_WIDTH)
    win_all = jnp.concatenate([win_cache[layer], kv_win.astype(win_cache.dtype)], axis=1)
    return o, win_all[:, -min(WINDOW, wb + Tn):]


GLA_LANES = GLA_HEADS * GLA_DK
GLA_KERNEL_CHUNK = 32
GLA_BLOCK_ROWS = 512


def _head_block_mask(dtype):
    r = lax.broadcasted_iota(jnp.int32, (GLA_LANES, GLA_LANES), 0) // GLA_DK
    c = lax.broadcasted_iota(jnp.int32, (GLA_LANES, GLA_LANES), 1) // GLA_DK
    return (r == c).astype(dtype)


def _gla_kernel(q_ref, k_ref, v_ref, gg_ref, ga_ref, wa_ref, ba_ref, gn_ref, st0_ref, o_ref, st_ref,
                st_s, kp_s, bp_s, vp_s, *, c, n_chunks, valid_rows):
    f32, bf = jnp.float32, jnp.bfloat16

    @pl.when(pl.program_id(1) == 0)
    def _():
        st_s[...] = st0_ref[0]
        zeros = jnp.zeros((c, GLA_LANES), f32)
        kp_s[0:c, :] = zeros
        bp_s[0:c, :] = zeros
        vp_s[0:c, :] = zeros

    row = lax.broadcasted_iota(jnp.int32, (c, 1), 0)
    ones_blk = _head_block_mask(bf)
    blk_f32 = _head_block_mask(f32)
    contract_last = (((1,), (1,)), ((), ()))
    contract_rows = (((0,), (0,)), ((), ()))

    def head_sum(x):
        hi = x.astype(bf)
        lo = (x - hi.astype(f32)).astype(bf)
        return (jnp.dot(hi, ones_blk, preferred_element_type=f32)
                + jnp.dot(lo, ones_blk, preferred_element_type=f32))

    def chunk(ch, carry):
        rows = pl.ds(pl.multiple_of(ch * c, c), c)
        q = q_ref[rows, :] * (GLA_DK ** -0.5)
        k = k_ref[rows, :]
        v = v_ref[rows, :]
        z = jnp.dot(ga_ref[rows, :].astype(bf), wa_ref[...], preferred_element_type=f32) + ba_ref[...]
        la = (jnp.minimum(z, 0.0) - jnp.log1p(jnp.exp(-jnp.abs(z)))) / GLA_GATE_TEMP
        if valid_rows < c:
            la = jnp.where(row < valid_rows, la, 0.0)
        b = la
        step = 1
        while step < c:
            b = b + jnp.where(row >= step, pltpu.roll(b, step, 0), 0.0)
            step *= 2
        st = st_s[...]
        o = lax.dot_general((q * jnp.exp(b)).astype(bf), st.astype(bf), contract_last,
                            preferred_element_type=f32)
        kp_s[c:2 * c, :] = k
        bp_s[c:2 * c, :] = b
        vp_s[c:2 * c, :] = v
        terms = [q * k]
        for d in range(1, c):
            ok = row >= d
            kr = kp_s[c - d:2 * c - d, :]
            br = bp_s[c - d:2 * c - d, :]
            terms.append(jnp.where(ok, q * kr * jnp.exp(jnp.where(ok, b - br, 0.0)), 0.0))
        att = jnp.dot(jnp.concatenate(terms, axis=0).astype(bf), ones_blk, preferred_element_type=f32)
        for d in range(c):
            vr = v if d == 0 else vp_s[c - d:2 * c - d, :]
            o = o + att[d * c:(d + 1) * c, :] * vr
        ms = head_sum(o * o) * (1.0 / GLA_DV)
        g = gg_ref[rows, :]
        o_ref[rows, :] = o * lax.rsqrt(ms + EPS) * gn_ref[...] * (g * jax.nn.sigmoid(g))
        b_last = b[c - 1:c, :]
        ke = k * jnp.exp(b_last - b)
        upd = lax.dot_general(v.astype(bf), ke.astype(bf), contract_rows, preferred_element_type=f32)
        st_s[...] = st * jnp.exp(b_last) + upd * blk_f32
        return carry

    lax.fori_loop(0, n_chunks, chunk, 0, unroll=min(4, n_chunks))
    st_ref[0] = st_s[...]


def _gla(proj, nbatch, tb, c, valid_rows, wa_pad, ba, gn, st0):
    m = proj.shape[0]
    nblk = m // nbatch // tb
    colblk = lambda p, w: PROJ_OFF[p] // w
    row_map = lambda j: (lambda b, i: (b * nblk + i, j))
    return pl.pallas_call(
        functools.partial(_gla_kernel, c=c, n_chunks=tb // c, valid_rows=valid_rows),
        grid=(nbatch, nblk),
        in_specs=[
            pl.BlockSpec((tb, GLA_LANES), row_map(colblk(0, GLA_LANES))),
            pl.BlockSpec((tb, GLA_LANES), row_map(colblk(1, GLA_LANES))),
            pl.BlockSpec((tb, GLA_LANES), row_map(colblk(2, GLA_LANES))),
            pl.BlockSpec((tb, GLA_LANES), row_map(colblk(3, GLA_LANES))),
            pl.BlockSpec((tb, LANES), row_map(colblk(4, LANES))),
            pl.BlockSpec((LANES, GLA_LANES), lambda b, i: (0, 0)),
            pl.BlockSpec((1, GLA_LANES), lambda b, i: (0, 0)),
            pl.BlockSpec((1, GLA_LANES), lambda b, i: (0, 0)),
            pl.BlockSpec((1, GLA_LANES, GLA_LANES), lambda b, i: (b, 0, 0)),
        ],
        out_specs=[pl.BlockSpec((tb, GLA_LANES), lambda b, i: (b * nblk + i, 0)),
                   pl.BlockSpec((1, GLA_LANES, GLA_LANES), lambda b, i: (b, 0, 0))],
        out_shape=[jax.ShapeDtypeStruct((m, GLA_LANES), jnp.float32),
                   jax.ShapeDtypeStruct((nbatch, GLA_LANES, GLA_LANES), jnp.float32)],
        scratch_shapes=[pltpu.VMEM((GLA_LANES, GLA_LANES), jnp.float32),
                        pltpu.VMEM((2 * c, GLA_LANES), jnp.float32),
                        pltpu.VMEM((2 * c, GLA_LANES), jnp.float32),
                        pltpu.VMEM((2 * c, GLA_LANES), jnp.float32)],
        compiler_params=pltpu.CompilerParams(
            dimension_semantics=("arbitrary", "arbitrary"), vmem_limit_bytes=VMEM_LIMIT_BYTES),
        name="gla_scan",
    )(proj, proj, proj, proj, proj, wa_pad, ba, gn, st0)


def _gla_state_in(s0):
    eye = jnp.eye(GLA_HEADS, dtype=jnp.float32)
    return jnp.einsum('bhde,hg->bhegd', s0.astype(jnp.float32), eye).reshape(-1, GLA_LANES, GLA_LANES)


def _gla_state_out(st):
    blocks = [st[:, h * GLA_DV:(h + 1) * GLA_DV, h * GLA_DK:(h + 1) * GLA_DK] for h in range(GLA_HEADS)]
    return jnp.stack(blocks, axis=1).transpose(0, 1, 3, 2)


def _nsa_prompt(proj, q, kv_cmp, kv_slc, kv_win, gates, cmp_w):
    B, T = kv_cmp.shape[:2]
    assert B == 1 and T % SLC_TILE == 0 and T >= WIN_TILE
    n_rows = T // CMP_STRIDE
    kvc = _compress(kv_cmp.reshape(n_rows, CMP_ROW), cmp_w, min(256, n_rows))
    o = _nsa_prompt_attention(proj, kvc, kv_slc.reshape(T, NSA_KV_WIDTH), kv_win.reshape(T, NSA_KV_WIDTH))
    return o.reshape(B, T, NSA_WIDTH), kv_win[:, -min(WINDOW, T):]


def _expand_rows(v, t):
    if v.shape[0] == 1:
        return v
    return jnp.repeat(v, t, axis=0)


def _trunk_layer(x, mod, lw, gla_s0, sc_hist, ffn_hist, nsa_apply, tm, final, g_final):
    B, T, _ = x.shape
    m = B * T
    grouped = B > 1
    ssh1, ssc1, sgt1, ssh2, ssc2, sgt2 = [_expand_rows(v, T) for v in jnp.split(mod, 6, axis=-1)]
    x2 = x.reshape(m, D_MODEL)
    proj = _in_proj(x2, lw['norm_mix'], ssc1, ssh1, lw['w_in'], tm).reshape(B, T, PROJ_WIDTH)
    gq, gk, gv, gg, ga, sb, scc, shh, nq, ncmp, nslc, nwin, ngate = [_proj_piece(proj, p) for p in range(13)]
    heads = lambda a, d: a.reshape(B, T, -1, d)
    if grouped:
        t_pad = _round_up(T, SUBLANES)
        gla_in = jnp.pad(proj, ((0, 0), (0, t_pad - T), (0, 0))).reshape(B * t_pad, PROJ_WIDTH)
        tb = chunk = t_pad
    else:
        t_pad, gla_in, tb, chunk = T, proj.reshape(m, PROJ_WIDTH), GLA_BLOCK_ROWS, GLA_KERNEL_CHUNK
    o_gla, st_gla = _gla(gla_in, B, tb, chunk, T if grouped else chunk,
                         lw['gla_wa'], lw['gla_ba'], lw['gla_norm'], _gla_state_in(gla_s0))
    o_gla = o_gla.reshape(B, t_pad, GLA_WIDTH)[:, :T]
    s_gla = _gla_state_out(st_gla)
    kvr = lambda a: a.reshape(B, T, 2, NSA_KV_HEADS, NSA_HD)
    kv_cmp, kv_slc, kv_win = kvr(ncmp), kvr(nslc), kvr(nwin)
    proj2 = proj.reshape(m, PROJ_WIDTH)
    o_nsa, win_state = nsa_apply(proj2, heads(nq, NSA_HD), kv_cmp, kv_slc, kv_win, heads(ngate, 3))

    def conv_hist(h):
        c = h.shape[-1]
        if grouped:
            pad = jnp.zeros((B, T - 1, c), jnp.float32)
            h1 = jnp.concatenate([h[:, 1:2], pad], axis=1).reshape(m, c)
            h2 = jnp.concatenate([h[:, 0:2], pad[:, 1:]], axis=1).reshape(m, c)
            return (h1, h2)
        return jnp.concatenate([jnp.zeros((HIST_ROWS - (CONV_W - 1), c), jnp.float32), h[0]], axis=0)

    def conv_state(st):
        c = st.shape[-1]
        return st.reshape(B, T, c)[:, -(CONV_W - 1):] if grouped else st[None, -(CONV_W - 1):]

    group = T if grouped else 0
    x2, sc_st = _out_proj(x2, o_gla.reshape(m, GLA_WIDTH), proj2, o_nsa.reshape(m, NSA_WIDTH), sgt1,
                          lw['w_out'], lw['sc_conv'], conv_hist(sc_hist), tm, group)
    sc_state = conv_state(sc_st)
    y, st = _ffn(x2, lw['norm_ffn'], ssc2, ssh2, sgt2, lw['ffn_up'], lw['ffn_conv'], lw['ffn_down'],
                 g_final, conv_hist(ffn_hist), tm, group, final)
    ffn_state = conv_state(st)
    return (y.reshape(B, T, D_MODEL), kv_cmp, kv_slc, win_state, s_gla.astype(gla_s0.dtype), sc_state, ffn_state)


def kernel(x_prompt, x_sample, cache_nsa_cmp, cache_nsa_slc, cache_nsa_win, state_gla, state_shortconv, state_ffn_conv, page_table, c_prompt, c_sample, mod_w, mod_b, norm_mix, norm_ffn, w_in, gla_wa2, gla_ba, gla_norm, sc_conv, nsa_cmp_pos, nsa_cmp_w1, nsa_cmp_w2, w_out, ffn_up, ffn_conv, ffn_down, norm_final):
    xp, xs = x_prompt, x_sample
    bp, bs = xp.shape[0], xs.shape[0]
    assert bp == 1 and xs.shape[1] == 4
    c_rows = _round_up(bp + bs, SUBLANES)
    c_all = jnp.concatenate([c_prompt, c_sample, jnp.zeros((c_rows - bp - bs, D_MODEL), jnp.float32)], axis=0)
    mod_all = _modulation(c_all, mod_w, mod_b)
    g_final = norm_final.reshape(1, D_MODEL)
    outs = [[] for _ in range(12)]
    for l in range(DEPTH):
        lw = dict(
            norm_mix=norm_mix[l].reshape(1, D_MODEL), norm_ffn=norm_ffn[l].reshape(1, D_MODEL),
            w_in=_pack_w_in(w_in[l]),
            gla_wa=jnp.zeros((LANES, GLA_LANES), jnp.bfloat16).at[:GLA_GATE_RANK].set(
                gla_wa2[l].astype(jnp.bfloat16)),
            gla_ba=gla_ba[l].reshape(1, GLA_LANES),
            gla_norm=jnp.tile(gla_norm[l], GLA_HEADS).reshape(1, GLA_LANES),
            sc_conv=sc_conv[l], w_out=w_out[l].astype(jnp.bfloat16),
            ffn_up=ffn_up[l].astype(jnp.bfloat16), ffn_conv=ffn_conv[l],
            ffn_down=ffn_down[l].astype(jnp.bfloat16))
        cmp_params = (nsa_cmp_pos[l], nsa_cmp_w1[l], nsa_cmp_w2[l])
        cmp_w = _compress_weights(*cmp_params)
        final = l == DEPTH - 1
        res_p = _trunk_layer(
            xp, mod_all[l, 0:bp], lw,
            jnp.zeros((bp, GLA_HEADS, GLA_DK, GLA_DV), xp.dtype),
            jnp.zeros((bp, CONV_W - 1, SC_WIDTH), xp.dtype),
            jnp.zeros((bp, CONV_W - 1, 2 * D_FF), xp.dtype),
            functools.partial(_nsa_prompt, cmp_w=cmp_w), 256, final, g_final)
        res_s = _trunk_layer(
            xs, mod_all[l, bp:bp + bs], lw, state_gla[l], state_shortconv[l], state_ffn_conv[l],
            functools.partial(_nsa_sample, pool_cmp=cache_nsa_cmp, pool_slc=cache_nsa_slc,
                              page_table=page_table, win_cache=cache_nsa_win, layer=l,
                              cmp_w=cmp_w), bs * xs.shape[1], final, g_final)
        xp, xs = res_p[0], res_s[0]
        for k in range(6):
            outs[2 * k].append(res_p[k + 1])
            outs[2 * k + 1].append(res_s[k + 1])
    return (xp, xs) + tuple(jnp.stack(o) for o in outs)
```

```python
import functools

import jax
import jax.numpy as jnp
from jax import lax
from jax.experimental import pallas as pl
from jax.experimental.pallas import tpu as pltpu

D_MODEL = 1024
DEPTH = 2
PAGE_SIZE = 128
GLA_HEADS = 4
GLA_DK = D_MODEL // 16
GLA_DV = D_MODEL // 16
GLA_WIDTH = GLA_HEADS * GLA_DV
GLA_GATE_RANK = 16
GLA_GATE_TEMP = 16.0
GLA_CHUNK = 64
SC_WIDTH = D_MODEL // 4
CONV_W = 3
NSA_HEADS = 8
NSA_KV_HEADS = 2
NSA_HD = D_MODEL // 16
NSA_WIDTH = NSA_HEADS * NSA_HD
NSA_KV_WIDTH = 2 * NSA_KV_HEADS * NSA_HD
CMP_STRIDE = 16
CMP_BLOCK = 2 * CMP_STRIDE
CMP_HIDDEN = 128
SEL_BLOCK = 64
SEL_TOPN = 16
WINDOW = 512
Q_BLOCK = 128
D_FF = 2816
EPS = 1e-6
NEG = -1e30
TINY = 1e-30
FORCE = 1e9

IN_SIZES = (
    GLA_HEADS * GLA_DK, GLA_HEADS * GLA_DK, GLA_WIDTH, GLA_WIDTH, GLA_GATE_RANK,
    SC_WIDTH, SC_WIDTH, SC_WIDTH,
    NSA_WIDTH, NSA_KV_WIDTH, NSA_KV_WIDTH, NSA_KV_WIDTH, NSA_HEADS * 3,
)

LANES = 128
SUBLANES = 8
BF16_SUBLANES = 16
VMEM_LIMIT_BYTES = 56 * 1024 * 1024

PROJ_ORDER = (0, 1, 2, 3, 8, 5, 6, 7, 9, 10, 11, 4, 12)


def _round_up(n, m):
    return -(-n // m) * m


def _proj_layout():
    src, acc = [], 0
    for s in IN_SIZES:
        src.append(acc)
        acc += s
    offs, dst = {}, 0
    for p in PROJ_ORDER:
        offs[p] = dst
        dst += _round_up(IN_SIZES[p], LANES)
    return src, offs, dst


PROJ_SRC, PROJ_OFF, PROJ_WIDTH = _proj_layout()


def _pack_w_in(w_in):
    out = jnp.zeros((D_MODEL, PROJ_WIDTH), jnp.bfloat16)
    for p in PROJ_ORDER:
        piece = w_in[:, PROJ_SRC[p]:PROJ_SRC[p] + IN_SIZES[p]].astype(jnp.bfloat16)
        out = lax.dynamic_update_slice(out, piece, (0, PROJ_OFF[p]))
    return out


def _proj_piece(proj, p):
    return proj[..., PROJ_OFF[p]:PROJ_OFF[p] + IN_SIZES[p]]


def _mod_kernel(c_ref, w_ref, b_ref, o_ref):
    c = c_ref[...]
    a = c * jax.nn.sigmoid(c)
    o_ref[0] = jnp.dot(a, w_ref[0], preferred_element_type=jnp.float32,
                       precision=lax.Precision.HIGHEST) + b_ref[0]


def _modulation(c_all, mod_w, mod_b):
    rows = c_all.shape[0]
    tn = 1024
    n = mod_w.shape[-1]
    return pl.pallas_call(
        _mod_kernel,
        grid=(DEPTH, n // tn),
        in_specs=[
            pl.BlockSpec((rows, D_MODEL), lambda l, j: (0, 0)),
            pl.BlockSpec((1, D_MODEL, tn), lambda l, j: (l, 0, j)),
            pl.BlockSpec((1, 1, tn), lambda l, j: (l, 0, j)),
        ],
        out_specs=pl.BlockSpec((1, rows, tn), lambda l, j: (l, 0, j)),
        out_shape=jax.ShapeDtypeStruct((DEPTH, rows, n), jnp.float32),
        name="adaln_modulation",
    )(c_all, mod_w, mod_b.reshape(DEPTH, 1, n))


def _norm_mod(x, g, sc, sh):
    r = lax.rsqrt(jnp.mean(x * x, axis=-1, keepdims=True) + EPS)
    return (x * r * g) * (1.0 + sc) + sh


def _in_proj_kernel(x_ref, g_ref, sc_ref, sh_ref, w_ref, o_ref):
    h = _norm_mod(x_ref[...], g_ref[...], sc_ref[...], sh_ref[...])
    o_ref[...] = jnp.dot(h.astype(jnp.bfloat16), w_ref[...], preferred_element_type=jnp.float32)


def _row_spec(tm, per_row):
    if per_row:
        return pl.BlockSpec((tm, D_MODEL), lambda i: (i, 0))
    return pl.BlockSpec((1, D_MODEL), lambda i: (0, 0))


def _resident(shape):
    return pl.BlockSpec(shape, lambda i: (0,) * len(shape), pipeline_mode=pl.Buffered(1))


def _in_proj(x, g, sc, sh, w_packed, tm):
    m = x.shape[0]
    per_row = sc.shape[0] != 1
    return pl.pallas_call(
        _in_proj_kernel,
        grid=(m // tm,),
        in_specs=[
            pl.BlockSpec((tm, D_MODEL), lambda i: (i, 0)),
            _resident((1, D_MODEL)),
            _row_spec(tm, per_row),
            _row_spec(tm, per_row),
            _resident((D_MODEL, PROJ_WIDTH)),
        ],
        out_specs=pl.BlockSpec((tm, PROJ_WIDTH), lambda i: (i, 0)),
        out_shape=jax.ShapeDtypeStruct((m, PROJ_WIDTH), jnp.float32),
        compiler_params=pltpu.CompilerParams(
            dimension_semantics=("arbitrary",), vmem_limit_bytes=VMEM_LIMIT_BYTES),
        name="norm_in_proj",
    )(x, g, sc, sh, w_packed)


HIST_ROWS = SUBLANES


def _out_proj_kernel(*refs, tm, group):
    grouped = group > 0
    if grouped:
        (x_ref, gla_ref, sb_ref, scc_ref, shh_ref, nsa_ref, gt_ref, w_ref, cw_ref, h1_ref, h2_ref,
         o_ref, st_ref, u_s) = refs
    else:
        (x_ref, gla_ref, sb_ref, scc_ref, shh_ref, nsa_ref, gt_ref, w_ref, cw_ref, h0_ref,
         o_ref, st_ref, u_s) = refs

        @pl.when(pl.program_id(0) == 0)
        def _():
            u_s[0:HIST_ROWS, :] = h0_ref[...]

    u = scc_ref[...] * shh_ref[...]
    u_s[HIST_ROWS:HIST_ROWS + tm, :] = u
    p1 = u_s[HIST_ROWS - 1:HIST_ROWS - 1 + tm, :]
    p2 = u_s[HIST_ROWS - 2:HIST_ROWS - 2 + tm, :]
    if grouped:
        t = lax.broadcasted_iota(jnp.int32, (tm, 1), 0) % group
        p1 = jnp.where(t == 0, h1_ref[...], p1)
        p2 = jnp.where(t <= 1, h2_ref[...], p2)
    o_sc = sb_ref[...] * (cw_ref[0:1, :] * p2 + cw_ref[1:2, :] * p1 + cw_ref[2:3, :] * u)
    mix = jnp.concatenate([gla_ref[...], o_sc, nsa_ref[...]], axis=1).astype(jnp.bfloat16)
    o_ref[...] = x_ref[...] + gt_ref[...] * jnp.dot(mix, w_ref[...], preferred_element_type=jnp.float32)
    if grouped:
        st_ref[...] = u
    else:
        tail = u_s[tm:tm + HIST_ROWS, :]
        st_ref[...] = tail
        u_s[0:HIST_ROWS, :] = tail


def _out_proj(x, o_gla, proj, o_nsa, gt, w_bf16, cw, hist, tm, group):
    m = x.shape[0]
    per_row = gt.shape[0] != 1
    grouped = group > 0
    sc_cols = lambda p: pl.BlockSpec((tm, SC_WIDTH), lambda i: (i, PROJ_OFF[p] // SC_WIDTH))
    in_specs = [
        pl.BlockSpec((tm, D_MODEL), lambda i: (i, 0)),
        pl.BlockSpec((tm, GLA_WIDTH), lambda i: (i, 0)),
        sc_cols(5), sc_cols(6), sc_cols(7),
        pl.BlockSpec((tm, NSA_WIDTH), lambda i: (i, 0)),
        _row_spec(tm, per_row),
        _resident((D_MODEL, D_MODEL)),
        _resident((CONV_W, SC_WIDTH)),
    ]
    if grouped:
        assert m == tm
        in_specs += [_resident((tm, SC_WIDTH)), _resident((tm, SC_WIDTH))]
        hist_args, st_rows = tuple(hist), tm
    else:
        in_specs += [_resident((HIST_ROWS, SC_WIDTH))]
        hist_args, st_rows = (hist,), HIST_ROWS
    return pl.pallas_call(
        functools.partial(_out_proj_kernel, tm=tm, group=group),
        grid=(m // tm,),
        in_specs=in_specs,
        out_specs=[pl.BlockSpec((tm, D_MODEL), lambda i: (i, 0)),
                   pl.BlockSpec((st_rows, SC_WIDTH), lambda i: (0, 0))],
        out_shape=[jax.ShapeDtypeStruct((m, D_MODEL), jnp.float32),
                   jax.ShapeDtypeStruct((st_rows, SC_WIDTH), jnp.float32)],
        scratch_shapes=[pltpu.VMEM((HIST_ROWS + tm, SC_WIDTH), jnp.float32)],
        compiler_params=pltpu.CompilerParams(
            dimension_semantics=("arbitrary",), vmem_limit_bytes=VMEM_LIMIT_BYTES),
        name="shortconv_out_proj",
    )(x, o_gla, proj, proj, proj, o_nsa, gt, w_bf16, cw, *hist_args)


FFN_UP_CHUNK = 512
FFN_ACT_CHUNK = 256


def _ffn_kernel(*refs, tm, group, final):
    grouped = group > 0
    if grouped:
        (x_ref, g_ref, sc_ref, sh_ref, gt_ref, wup_ref, cw_ref, wdn_ref, gf_ref,
         h1_ref, h2_ref, o_ref, st_ref, up_s) = refs
    else:
        (x_ref, g_ref, sc_ref, sh_ref, gt_ref, wup_ref, cw_ref, wdn_ref, gf_ref,
         h0_ref, o_ref, st_ref, up_s) = refs

        @pl.when(pl.program_id(0) == 0)
        def _():
            up_s[0:HIST_ROWS, :] = h0_ref[...]

    x = x_ref[...]
    h = _norm_mod(x, g_ref[...], sc_ref[...], sh_ref[...]).astype(jnp.bfloat16)
    for c in range(2 * D_FF // FFN_UP_CHUNK):
        cols = slice(c * FFN_UP_CHUNK, (c + 1) * FFN_UP_CHUNK)
        up_s[HIST_ROWS:HIST_ROWS + tm, cols] = jnp.dot(
            h, wup_ref[:, cols], preferred_element_type=jnp.float32)

    if grouped:
        t = lax.broadcasted_iota(jnp.int32, (tm, 1), 0) % group

    def conv(cols):
        cur = up_s[HIST_ROWS:HIST_ROWS + tm, cols]
        p1 = up_s[HIST_ROWS - 1:HIST_ROWS - 1 + tm, cols]
        p2 = up_s[HIST_ROWS - 2:HIST_ROWS - 2 + tm, cols]
        if grouped:
            p1 = jnp.where(t == 0, h1_ref[:, cols], p1)
            p2 = jnp.where(t <= 1, h2_ref[:, cols], p2)
        return cw_ref[0:1, cols] * p2 + cw_ref[1:2, cols] * p1 + cw_ref[2:3, cols] * cur

    acc = jnp.zeros((tm, D_MODEL), jnp.float32)
    for c in range(D_FF // FFN_ACT_CHUNK):
        a = conv(slice(c * FFN_ACT_CHUNK, (c + 1) * FFN_ACT_CHUNK))
        b = conv(slice(D_FF + c * FFN_ACT_CHUNK, D_FF + (c + 1) * FFN_ACT_CHUNK))
        act = (a * jax.nn.sigmoid(a) * b).astype(jnp.bfloat16)
        acc = acc + jnp.dot(act, wdn_ref[c * FFN_ACT_CHUNK:(c + 1) * FFN_ACT_CHUNK, :],
                            preferred_element_type=jnp.float32)
    y = x + gt_ref[...] * acc
    if final:
        r = lax.rsqrt(jnp.mean(y * y, axis=-1, keepdims=True) + EPS)
        y = y * r * gf_ref[...]
    o_ref[...] = y

    if grouped:
        st_ref[...] = up_s[HIST_ROWS:HIST_ROWS + tm, :]
    else:
        tail = up_s[tm:tm + HIST_ROWS, :]
        st_ref[...] = tail
        up_s[0:HIST_ROWS, :] = tail


def _ffn(x, g, sc, sh, gt, wup, cw, wdn, g_final, hist, tm, group, final):
    m = x.shape[0]
    grouped = group > 0
    per_row = sc.shape[0] != 1
    ff2 = 2 * D_FF
    in_specs = [
        pl.BlockSpec((tm, D_MODEL), lambda i: (i, 0)),
        _resident((1, D_MODEL)),
        _row_spec(tm, per_row), _row_spec(tm, per_row), _row_spec(tm, per_row),
        _resident((D_MODEL, ff2)),
        _resident((CONV_W, ff2)),
        _resident((D_FF, D_MODEL)),
        _resident((1, D_MODEL)),
    ]
    if grouped:
        assert m == tm
        in_specs += [_resident((tm, ff2)), _resident((tm, ff2))]
        hist_args = tuple(hist)
        st_rows = tm
    else:
        in_specs += [_resident((HIST_ROWS, ff2))]
        hist_args = (hist,)
        st_rows = HIST_ROWS
    return pl.pallas_call(
        functools.partial(_ffn_kernel, tm=tm, group=group, final=final),
        grid=(m // tm,),
        in_specs=in_specs,
        out_specs=[pl.BlockSpec((tm, D_MODEL), lambda i: (i, 0)),
                   pl.BlockSpec((st_rows, ff2), lambda i: (0, 0))],
        out_shape=[jax.ShapeDtypeStruct((m, D_MODEL), jnp.float32),
                   jax.ShapeDtypeStruct((st_rows, ff2), jnp.float32)],
        scratch_shapes=[pltpu.VMEM((HIST_ROWS + tm, ff2), jnp.float32)],
        compiler_params=pltpu.CompilerParams(
            dimension_semantics=("arbitrary",), vmem_limit_bytes=VMEM_LIMIT_BYTES),
        name="conv_ffn",
    )(x, g, sc, sh, gt, wup, cw, wdn, g_final, *hist_args)


CMP_ROW = CMP_STRIDE * NSA_KV_WIDTH
CMP_HID = 2 * NSA_KV_HEADS * CMP_HIDDEN


def _gelu_tanh(x):
    return 0.5 * x * (1.0 + jnp.tanh(0.7978845608028654 * (x + 0.044715 * (x * x * x))))


def _compress_kernel(x_ref, xn_ref, pos_ref, wl_ref, wt_ref, w2_ref, o_ref, tr_s, *, tm):
    bf = jnp.bfloat16
    f32 = jnp.float32
    x = x_ref[...].astype(bf)
    lead = jnp.dot(x, wl_ref[...], preferred_element_type=f32)
    tr_s[0:tm, :] = jnp.dot(x, wt_ref[...], preferred_element_type=f32)
    tr_s[tm:tm + SUBLANES, :] = jnp.dot(xn_ref[...].astype(bf), wt_ref[...], preferred_element_type=f32)
    bias = (jnp.dot(pos_ref[0].astype(bf), wl_ref[...], preferred_element_type=f32)
            + jnp.dot(pos_ref[1].astype(bf), wt_ref[...], preferred_element_type=f32))[0:1, :]
    hid = _gelu_tanh(lead + tr_s[1:tm + 1, :] + bias)
    o_ref[...] = jnp.dot(hid.astype(bf), w2_ref[...], preferred_element_type=f32)


def _compress_weights(pos_emb, w1, w2):
    eye = jnp.eye(2, dtype=jnp.float32)
    w1f = jnp.einsum('kldh,kK,gG->lkgdKGh', w1, eye, eye).reshape(CMP_BLOCK, NSA_KV_WIDTH, CMP_HID)
    w1f = w1f.astype(jnp.bfloat16)
    w2b = jnp.einsum('khd,kK,gG->kghKGd', w2, eye, eye).reshape(CMP_HID, NSA_KV_WIDTH).astype(jnp.bfloat16)
    posf = jnp.broadcast_to(pos_emb.transpose(1, 0, 2)[:, :, None, :], (CMP_BLOCK, 2, NSA_KV_HEADS, NSA_HD))
    posf = posf.reshape(CMP_BLOCK, NSA_KV_WIDTH)
    pos = jnp.zeros((CMP_BLOCK, SUBLANES, NSA_KV_WIDTH), jnp.float32).at[:, 0].set(posf)
    pos_rows = jnp.zeros((2, SUBLANES, CMP_ROW), jnp.float32).at[:, 0].set(posf.reshape(2, CMP_ROW))
    w1kv = jnp.einsum('kldh,gG->lkgdGh', w1, eye).reshape(
        CMP_BLOCK // 2, 2, 2, NSA_KV_WIDTH // 2, CMP_HID // 2).transpose(0, 2, 1, 3, 4).reshape(
        CMP_BLOCK // 2, 2, NSA_KV_WIDTH, CMP_HID // 2).astype(jnp.bfloat16)
    return dict(pos=pos, w1=w1kv, w2=w2b, pos_rows=pos_rows,
                wl=w1f[:CMP_STRIDE].reshape(CMP_ROW, CMP_HID), wt=w1f[CMP_STRIDE:].reshape(CMP_ROW, CMP_HID))


def _compress(x, cw, tm):
    pos, wl, wt, w2b = cw['pos_rows'], cw['wl'], cw['wt'], cw['w2']
    n = x.shape[0]
    nb8 = n // SUBLANES
    return pl.pallas_call(
        functools.partial(_compress_kernel, tm=tm),
        grid=(n // tm,),
        in_specs=[
            pl.BlockSpec((tm, CMP_ROW), lambda i: (i, 0)),
            pl.BlockSpec((SUBLANES, CMP_ROW), lambda i: (jnp.minimum((i + 1) * (tm // SUBLANES), nb8 - 1), 0)),
            _resident((2, SUBLANES, CMP_ROW)),
            _resident((CMP_ROW, CMP_HID)), _resident((CMP_ROW, CMP_HID)),
            _resident((CMP_HID, NSA_KV_WIDTH)),
        ],
        out_specs=pl.BlockSpec((tm, NSA_KV_WIDTH), lambda i: (i, 0)),
        out_shape=jax.ShapeDtypeStruct((n, NSA_KV_WIDTH), jnp.float32),
        scratch_shapes=[pltpu.VMEM((tm + SUBLANES, CMP_HID), jnp.float32)],
        compiler_params=pltpu.CompilerParams(
            dimension_semantics=("arbitrary",), vmem_limit_bytes=VMEM_LIMIT_BYTES),
        name="nsa_compress",
    )(x, x, pos, wl, wt, w2b)


NSA_R = NSA_HEADS // NSA_KV_HEADS
QL = NSA_R * Q_BLOCK
QLL = NSA_KV_HEADS * QL
SLC_TILE = 512
BLK_PER_TILE = SLC_TILE // SEL_BLOCK
WIN_TILE = WINDOW + Q_BLOCK
CMP_PER_SEL = SEL_BLOCK // CMP_STRIDE
M_INIT = -1e29
QK_SCALE = NSA_HD ** -0.5 * 1.4426950408889634
ONES_ROWS = BF16_SUBLANES
SELECT_ROW_BUCKETS = (32, 64, 96, 128, 192)


def _tile_lanes(v, reps):
    return jnp.concatenate([v] * reps, axis=1)


def _nsa_prompt_kernel(q_ref, gate_ref, kc_ref, vct_ref, kslc_ref, vtslc_ref, kwin_ref, vtwin_ref,
                       o_ref, sc_s, sel_s, m_s, l_s, acc_s, ow_s, oc_s, mt_s, al_s, s_s, pt_s, *, n_sel):
    f32, bf = jnp.float32, jnp.bfloat16
    G, HD = NSA_KV_HEADS, NSA_HD
    n = pl.program_id(0)
    qn = q_ref[...] * QK_SCALE
    tq = [qn[:, j * LANES:(j + 1) * LANES].T for j in range(NSA_WIDTH // LANES)]
    heads_per_t = LANES // HD
    head_t = lambda h: tq[h // heads_per_t][(h % heads_per_t) * HD:(h % heads_per_t + 1) * HD, :]
    zero = jnp.zeros((HD, Q_BLOCK), f32)
    qbd = jnp.concatenate(
        [jnp.concatenate([head_t(g * NSA_R + r) if g == gp else zero for g in range(G) for r in range(NSA_R)],
                         axis=1) for gp in range(G)], axis=0).astype(bf)
    lane = lax.broadcasted_iota(jnp.int32, (1, Q_BLOCK), 1)
    pos_q = n * Q_BLOCK + lane
    pos_l = _tile_lanes(pos_q, QLL // Q_BLOCK)
    jrow = lax.broadcasted_iota(jnp.int32, (n_sel, 1), 0)

    def reset():
        m_s[...] = jnp.full((1, QLL), M_INIT, f32)
        l_s[...] = jnp.zeros((1, QLL), f32)
        acc_s[...] = jnp.zeros((HD, QLL), f32)

    def pass1(slot, k_ref, start, rows, bias_fn):
        s = jnp.dot(k_ref[pl.ds(start, rows), :].astype(bf), qbd, preferred_element_type=f32)
        top = jnp.full((SUBLANES, QLL), NEG, f32)
        for i in range(rows // SEL_BLOCK):
            blk = slice(i * SEL_BLOCK, (i + 1) * SEL_BLOCK)
            sb = s[blk, :] + bias_fn(i)
            s_s[slot, blk, :] = sb
            for r in range(SEL_BLOCK // SUBLANES):
                top = jnp.maximum(top, sb[r * SUBLANES:(r + 1) * SUBLANES, :])
        m_old = m_s[...]
        m_new = jnp.maximum(m_old, jnp.max(top, axis=0, keepdims=True))
        mt_s[slot] = m_new
        al_s[slot] = jnp.exp2(m_old - m_new)
        m_s[...] = m_new

    def pass2(slot, vta_ref, start, rows):
        m_new = mt_s[slot]
        for i in range(rows // SEL_BLOCK):
            blk = slice(i * SEL_BLOCK, (i + 1) * SEL_BLOCK)
            pt_s[slot, blk, :] = jnp.exp2(s_s[slot, blk, :] - m_new).astype(bf)
        pv, psum = [], []
        for g in range(G):
            r = jnp.dot(vta_ref[g, :, pl.ds(start, rows)], pt_s[slot, 0:rows, g * QL:(g + 1) * QL],
                        preferred_element_type=f32)
            pv.append(r[0:HD, :])
            psum.append(r[HD:HD + 1, :])
        alpha = al_s[slot]
        acc_s[...] = acc_s[...] * alpha + jnp.concatenate(pv, axis=1)
        l_s[...] = l_s[...] * alpha + jnp.concatenate(psum, axis=1)

    def finish():
        return acc_s[...] * (1.0 / jnp.maximum(l_s[...], TINY))

    reset()
    wstart = pl.multiple_of(jnp.maximum(n * Q_BLOCK - WINDOW, 0), Q_BLOCK)
    rel = pos_q - (wstart + lax.broadcasted_iota(jnp.int32, (WIN_TILE, 1), 0))
    wbias = jnp.where((rel >= 0) & (rel <= WINDOW), 0.0, NEG)
    pass1(1, kwin_ref, wstart, WIN_TILE,
          lambda i: _tile_lanes(wbias[i * SEL_BLOCK:(i + 1) * SEL_BLOCK, :], QLL // Q_BLOCK))
    pass2(1, vtwin_ref, wstart, WIN_TILE)
    ow_s[...] = finish()

    def compress_and_select(rows):
        jrow = lax.broadcasted_iota(jnp.int32, (rows, 1), 0)
        s_c, mk_c = [], []
        m = jnp.full((1, QLL), NEG, f32)
        for c in range(CMP_PER_SEL):
            s = jnp.dot(kc_ref[c * n_sel:c * n_sel + rows, :], qbd, preferred_element_type=f32)
            mk = jrow * SEL_BLOCK + (c * CMP_STRIDE + CMP_BLOCK - 1) <= pos_l
            s = jnp.where(mk, s, NEG)
            m = jnp.maximum(m, jnp.max(s, axis=0, keepdims=True))
            s_c.append(s)
            mk_c.append(mk)
        e_c = [jnp.where(mk_c[c], jnp.exp2(s_c[c] - m), 0.0) for c in range(CMP_PER_SEL)]
        l = e_c[0].sum(axis=0, keepdims=True)
        for c in range(1, CMP_PER_SEL):
            l = l + e_c[c].sum(axis=0, keepdims=True)
        inv = 1.0 / jnp.maximum(l, TINY)
        o_cmp = [jnp.zeros((HD, QL), f32) for _ in range(G)]
        pg = []
        for c in range(CMP_PER_SEL):
            p = e_c[c] * inv
            pb = p.astype(bf)
            for g in range(G):
                o_cmp[g] = o_cmp[g] + jnp.dot(vct_ref[g * HD:(g + 1) * HD, c * n_sel:c * n_sel + rows],
                                              pb[:, g * QL:(g + 1) * QL], preferred_element_type=f32)
            pg.append([sum(p[:, g * QL + r * Q_BLOCK:g * QL + (r + 1) * Q_BLOCK] for r in range(NSA_R))
                       for g in range(G)])
        oc_s[...] = jnp.concatenate(o_cmp, axis=1)

        cur = pos_q // SEL_BLOCK
        forced = (jrow == 0) | (jrow == cur) | (jrow == cur - 1)
        allowed = jrow * SEL_BLOCK <= pos_q
        jrow_f = jrow.astype(f32)
        live = slice(0, rows)
        for g in range(G):
            last = pg[CMP_PER_SEL - 1][g]
            prev = jnp.where(jrow == 0, 0.0, pltpu.roll(last, 1, 0))
            inner = pg[0][g]
            for c in range(1, CMP_PER_SEL - 1):
                inner = inner + pg[c][g]
            p_slc = 2.0 * inner + last + prev
            sc_s[g, live, :] = jnp.where(forced, FORCE, jnp.where(allowed, p_slc, -1.0))
            sel_s[g, live, :] = jnp.zeros((rows, Q_BLOCK), f32)

        def pick(_, carry):
            for g in range(G):
                s = sc_s[g, live, :]
                top = jnp.max(s, axis=0, keepdims=True)
                first = jnp.min(jnp.where(s == top, jrow_f, float(rows)), axis=0, keepdims=True)
                hit = jrow_f == first
                sc_s[g, live, :] = jnp.where(hit, -jnp.inf, s)
                sel_s[g, live, :] = jnp.where(hit, 1.0, sel_s[g, live, :])
            return carry

        lax.fori_loop(0, min(SEL_TOPN, rows), pick, 0)
        for g in range(G):
            sel_s[g, live, :] = jnp.where(allowed & (sel_s[g, live, :] > 0.5), 0.0, NEG)
            if rows < n_sel:
                sel_s[g, rows:n_sel, :] = jnp.full((n_sel - rows, Q_BLOCK), NEG, f32)

    visible = (n + 1) * (Q_BLOCK // SEL_BLOCK)
    lo = 0
    for rows in sorted({min(r, n_sel) for r in SELECT_ROW_BUCKETS} | {n_sel}):
        pl.when((visible > lo) & (visible <= rows))(functools.partial(compress_and_select, rows))
        lo = rows
    o_cmp = oc_s[...]

    def tile_start(kt):
        return pl.multiple_of(kt * SLC_TILE, SLC_TILE)

    def slc_pass1(slot, kt, causal):
        start = tile_start(kt)
        selb = [sel_s[g, pl.ds(pl.multiple_of(kt * BLK_PER_TILE, BLK_PER_TILE), BLK_PER_TILE), :]
                for g in range(G)]

        def bias_fn(i):
            row = jnp.concatenate([_tile_lanes(selb[g][i:i + 1, :], NSA_R) for g in range(G)], axis=1)
            if not causal:
                return row
            tok = start + i * SEL_BLOCK + lax.broadcasted_iota(jnp.int32, (SEL_BLOCK, 1), 0)
            return jnp.where(tok <= pos_l, row, NEG)

        pass1(slot, kslc_ref, start, SLC_TILE, bias_fn)

    def slc_pass2(slot, kt):
        pass2(slot, vtslc_ref, tile_start(kt), SLC_TILE)

    reset()
    diag = (n * Q_BLOCK) // SLC_TILE
    slc_pass1(0, diag, True)

    def slc_pair(j, carry):
        slc_pass1(1, 2 * j, False)
        slc_pass2(0, jnp.where(j == 0, diag, 2 * j - 1))
        slc_pass1(0, 2 * j + 1, False)
        slc_pass2(1, 2 * j)
        return carry

    pairs = diag // 2
    lax.fori_loop(0, pairs, slc_pair, 0)
    pending = jnp.where(pairs == 0, diag, 2 * pairs - 1)

    @pl.when(diag % 2 == 1)
    def _():
        slc_pass1(1, diag - 1, False)
        slc_pass2(0, pending)
        slc_pass2(1, diag - 1)

    @pl.when(diag % 2 == 0)
    def _():
        slc_pass2(0, pending)

    o_slc = finish()

    o_win = ow_s[...]

    gt = jax.nn.sigmoid(gate_ref[...].T)
    gate = lambda c: jnp.concatenate([gt[h * 3 + c:h * 3 + c + 1, :] for h in range(NSA_HEADS)], axis=1)
    o = gate(0) * o_cmp + gate(1) * o_slc + gate(2) * o_win
    o_ref[...] = jnp.concatenate(
        [jnp.concatenate([o[:, (heads_per_t * j + t) * Q_BLOCK:(heads_per_t * j + t + 1) * Q_BLOCK]
                          for t in range(heads_per_t)], axis=0).T for j in range(NSA_WIDTH // LANES)], axis=1)


def _nsa_prompt_attention(proj, kvc, nslc, nwin):
    T = proj.shape[0]
    nb, n_sel = T // Q_BLOCK, T // SEL_BLOCK
    G, R, HD = NSA_KV_HEADS, NSA_R, NSA_HD
    bf = jnp.bfloat16
    half = G * HD
    assert PROJ_OFF[8] % NSA_WIDTH == 0 and PROJ_OFF[12] % LANES == 0
    kvp = kvc.reshape(n_sel, CMP_PER_SEL, NSA_KV_WIDTH).transpose(1, 0, 2).reshape(T // CMP_STRIDE, NSA_KV_WIDTH)
    kc, vct = kvp[:, :half].astype(bf), kvp[:, half:].T.astype(bf)
    def vt_ones(v):
        vt = v.T.reshape(G, HD, T)
        return jnp.concatenate([vt, jnp.ones((G, ONES_ROWS, T), vt.dtype)], axis=1).astype(bf)
    vtslc, vtwin = vt_ones(nslc[:, half:]), vt_ones(nwin[:, half:])
    k_cols = lambda p: pl.BlockSpec((T, half), lambda i: (0, PROJ_OFF[p] // half), pipeline_mode=pl.Buffered(1))
    out = pl.pallas_call(
        functools.partial(_nsa_prompt_kernel, n_sel=n_sel),
        grid=(nb,),
        in_specs=[
            pl.BlockSpec((Q_BLOCK, NSA_WIDTH), lambda i: (i, PROJ_OFF[8] // NSA_WIDTH)),
            pl.BlockSpec((Q_BLOCK, LANES), lambda i: (i, PROJ_OFF[12] // LANES)),
            _resident((T // CMP_STRIDE, half)), _resident((half, T // CMP_STRIDE)),
            k_cols(10), _resident((G, HD + ONES_ROWS, T)),
            k_cols(11), _resident((G, HD + ONES_ROWS, T)),
        ],
        out_specs=pl.BlockSpec((Q_BLOCK, NSA_WIDTH), lambda i: (i, 0)),
        out_shape=jax.ShapeDtypeStruct((T, NSA_WIDTH), jnp.float32),
        scratch_shapes=[
            pltpu.VMEM((G, n_sel, Q_BLOCK), jnp.float32),
            pltpu.VMEM((G, n_sel, Q_BLOCK), jnp.float32),
            pltpu.VMEM((1, QLL), jnp.float32),
            pltpu.VMEM((1, QLL), jnp.float32),
            pltpu.VMEM((HD, QLL), jnp.float32),
            pltpu.VMEM((HD, QLL), jnp.float32),
            pltpu.VMEM((HD, QLL), jnp.float32),
            pltpu.VMEM((2, 1, QLL), jnp.float32),
            pltpu.VMEM((2, 1, QLL), jnp.float32),
            pltpu.VMEM((2, max(WIN_TILE, SLC_TILE), QLL), jnp.float32),
            pltpu.VMEM((2, max(WIN_TILE, SLC_TILE), QLL), jnp.bfloat16),
        ],
        compiler_params=pltpu.CompilerParams(
            dimension_semantics=("arbitrary",), vmem_limit_bytes=VMEM_LIMIT_BYTES),
        name="nsa_prompt_attention",
    )(proj, proj, kc, vct, proj, vtslc, proj, vtwin)
    return out


PAGE_ROWS = PAGE_SIZE // CMP_STRIDE
SQ = 16
SAMPLE_LANES = NSA_HEADS * SQ
HALF_PAGES = 64
SAMPLE_TILE = 4096


def _page_view(pool):
    d, n_pool = pool.shape[:2]
    return pool.transpose(0, 1, 3, 4, 5, 2).reshape(d * n_pool, 2, NSA_KV_HEADS * NSA_HD, pool.shape[2])


def _page_copy(pool_ref, buf_ref, sem_ref, page, slot, idx):
    return pltpu.make_async_copy(pool_ref.at[page], buf_ref.at[slot, idx], sem_ref.at[slot])


def _gather_schedule(issue_fn, wait_fn):
    s = pl.program_id(0)

    @pl.when(s == 0)
    def _():
        issue_fn(s, 0)

    @pl.when(s + 1 < pl.num_programs(0))
    def _():
        issue_fn(s + 1, (s + 1) % 2)

    wait_fn(s, s % 2)


def _compress_paged_kernel(pt_ref, pool_ref, pos_ref, w1_ref, w2_ref, o_ref, buf, sem, tok_s,
                           *, page_base, n_pages):
    bf, f32 = jnp.bfloat16, jnp.float32
    rows = HALF_PAGES * PAGE_ROWS

    def copies(step, slot, fn):
        b, half = step // 2, step % 2

        def body(i, c):
            p = jnp.minimum(half * HALF_PAGES + i, n_pages - 1)
            fn(_page_copy(pool_ref, buf, sem, page_base + pt_ref[b, p], slot, i))
            return c

        lax.fori_loop(0, HALF_PAGES + 1, body, 0)

    _gather_schedule(lambda s, slot: copies(s, slot, lambda cp: cp.start()),
                     lambda s, slot: copies(s, slot, lambda cp: cp.wait()))
    slot = pl.program_id(0) % 2

    half = NSA_KV_HEADS * NSA_HD

    r_out = lax.broadcasted_iota(jnp.int32, (PAGE_SIZE, 1), 0)
    t_in = lax.broadcasted_iota(jnp.int32, (1, PAGE_SIZE), 1)
    regroup = ((r_out % PAGE_ROWS) * CMP_STRIDE + r_out // PAGE_ROWS == t_in).astype(bf)

    def to_offset_rows(i, c):
        dst = pl.ds(pl.multiple_of(i * PAGE_ROWS, PAGE_ROWS), PAGE_ROWS)
        page = buf[slot, i].reshape(NSA_KV_WIDTH, PAGE_SIZE).astype(bf)
        t = lax.dot_general(regroup, page, (((1,), (1,)), ((), ())), preferred_element_type=f32)
        for kv in range(2):
            for l in range(CMP_STRIDE):
                tok_s[kv, l, dst, :] = t[l * PAGE_ROWS:(l + 1) * PAGE_ROWS, kv * half:(kv + 1) * half]
        return c

    lax.fori_loop(0, HALF_PAGES + 1, to_offset_rows, 0, unroll=5)
    acc = [jnp.zeros((rows, CMP_HID // 2), f32) for _ in range(2)]
    bias = [jnp.zeros((SUBLANES, CMP_HID // 2), f32) for _ in range(2)]
    for l in range(0, CMP_BLOCK, 2):
        first = l // CMP_STRIDE
        for kv in range(2):
            x = jnp.concatenate([tok_s[kv, l % CMP_STRIDE + d, first:first + rows, :] for d in range(2)],
                                axis=1).astype(bf)
            w = w1_ref[l // 2, kv]
            p = jnp.concatenate([pos_ref[l + d, :, kv * half:(kv + 1) * half] for d in range(2)], axis=1)
            acc[kv] = acc[kv] + jnp.dot(x, w, preferred_element_type=f32)
            bias[kv] = bias[kv] + jnp.dot(p.astype(bf), w, preferred_element_type=f32)
    hid = _gelu_tanh(jnp.concatenate(acc, axis=1) + jnp.concatenate(bias, axis=1)[0:1, :])
    o_ref[...] = jnp.dot(hid.astype(bf), w2_ref[...], preferred_element_type=f32)


def _compress_paged(page_table, pool, layer, cw):
    nbatch, n_pages = page_table.shape
    assert n_pages == 2 * HALF_PAGES
    rows = HALF_PAGES * PAGE_ROWS
    const = lambda shape: pl.BlockSpec(shape, lambda s, pt: (0,) * len(shape), pipeline_mode=pl.Buffered(1))
    return pl.pallas_call(
        functools.partial(_compress_paged_kernel, page_base=layer * (pool.shape[0] // DEPTH), n_pages=n_pages),
        grid_spec=pltpu.PrefetchScalarGridSpec(
            num_scalar_prefetch=1,
            grid=(2 * nbatch,),
            in_specs=[
                pl.BlockSpec(memory_space=pl.ANY),
                const((CMP_BLOCK, SUBLANES, NSA_KV_WIDTH)),
                const((CMP_BLOCK // 2, 2, NSA_KV_WIDTH, CMP_HID // 2)),
                const((CMP_HID, NSA_KV_WIDTH)),
            ],
            out_specs=pl.BlockSpec((rows, NSA_KV_WIDTH), lambda s, pt: (s, 0)),
            scratch_shapes=[
                pltpu.VMEM((2, HALF_PAGES + 1) + pool.shape[1:], jnp.float32),
                pltpu.SemaphoreType.DMA((2,)),
                pltpu.VMEM((2, CMP_STRIDE, (HALF_PAGES + 1) * PAGE_ROWS, NSA_KV_WIDTH // 2), jnp.float32),
            ]),
        out_shape=jax.ShapeDtypeStruct((2 * nbatch * rows, NSA_KV_WIDTH), jnp.float32),
        compiler_params=pltpu.CompilerParams(
            dimension_semantics=("arbitrary",), vmem_limit_bytes=VMEM_LIMIT_BYTES),
        name="nsa_compress_paged",
    )(page_table, pool, cw['pos'], cw['w1'], cw['w2'])


def _nsa_sample_kernel(pt_ref, pool_ref, qbd_ref, gate_ref, kc_ref, vc_ref, nslc_ref, wcache_ref, nwin_ref,
                       o_ref, buf, sem, sc_s, sel_s, m_s, l_s, acc_s, *, page_base, n_pages, n_sel, t_new):
    f32, bf = jnp.float32, jnp.bfloat16
    HD, LN = NSA_HD, SAMPLE_LANES
    half = NSA_KV_HEADS * HD
    past = n_pages * PAGE_SIZE
    contract_rows = (((0,), (0,)), ((), ()))

    def copies(step, slot, fn):
        def body(p, c):
            fn(_page_copy(pool_ref, buf, sem, page_base + pt_ref[step, p], slot, p))
            return c

        lax.fori_loop(0, n_pages, body, 0)

    _gather_schedule(lambda s, slot: copies(s, slot, lambda cp: cp.start()),
                     lambda s, slot: copies(s, slot, lambda cp: cp.wait()))
    slot = pl.program_id(0) % 2

    qbd = qbd_ref[0]
    lane = lax.broadcasted_iota(jnp.int32, (1, LN), 1)
    pos_l = past + lane % SQ
    group0 = lane < LN // NSA_KV_HEADS
    jrow = lax.broadcasted_iota(jnp.int32, (n_sel, 1), 0)
    jrow_f = jrow.astype(f32)

    def group_rows(full):
        return jnp.where(group0, full[0:HD, :], full[HD:2 * HD, :])

    def pv(v, pt):
        return group_rows(lax.dot_general(v.astype(bf), pt, contract_rows, preferred_element_type=f32))

    def scores_t(kt):
        return lax.dot_general(kt.astype(bf), qbd, contract_rows, preferred_element_type=f32)

    li = lax.broadcasted_iota(jnp.int32, (LN, LN), 0)
    lj = lax.broadcasted_iota(jnp.int32, (LN, LN), 1)
    same = ((li // (NSA_R * SQ) == lj // (NSA_R * SQ)) & (li % SQ == lj % SQ)).astype(bf)

    def head_sum(p):
        hi = p.astype(bf)
        r1 = p - hi.astype(f32)
        mid = r1.astype(bf)
        lo = (r1 - mid.astype(f32)).astype(bf)
        return (jnp.dot(hi, same, preferred_element_type=f32) + jnp.dot(mid, same, preferred_element_type=f32)
                + jnp.dot(lo, same, preferred_element_type=f32))

    s_c, mk_c = [], []
    m = jnp.full((1, LN), NEG, f32)
    for c in range(CMP_PER_SEL):
        s = jnp.dot(kc_ref[0, c * n_sel:(c + 1) * n_sel, :], qbd, preferred_element_type=f32)
        mk = jrow * SEL_BLOCK + (c * CMP_STRIDE + CMP_BLOCK - 1) <= pos_l
        s = jnp.where(mk, s, NEG)
        m = jnp.maximum(m, jnp.max(s, axis=0, keepdims=True))
        s_c.append(s)
        mk_c.append(mk)
    e_c = [jnp.where(mk_c[c], jnp.exp2(s_c[c] - m), 0.0) for c in range(CMP_PER_SEL)]
    l = e_c[0].sum(axis=0, keepdims=True)
    for c in range(1, CMP_PER_SEL):
        l = l + e_c[c].sum(axis=0, keepdims=True)
    inv = 1.0 / jnp.maximum(l, TINY)
    o_cmp = jnp.zeros((HD, LN), f32)
    pg = []
    for c in range(CMP_PER_SEL):
        p = e_c[c] * inv
        o_cmp = o_cmp + pv(vc_ref[0, c * n_sel:(c + 1) * n_sel, :], p.astype(bf))
        pg.append(head_sum(p))

    cur = pos_l // SEL_BLOCK
    forced = (jrow == 0) | (jrow == cur) | (jrow == cur - 1)
    allowed = jrow * SEL_BLOCK <= pos_l
    last = pg[CMP_PER_SEL - 1]
    prev = jnp.where(jrow == 0, 0.0, pltpu.roll(last, 1, 0))
    inner = pg[0]
    for c in range(1, CMP_PER_SEL - 1):
        inner = inner + pg[c]
    sc_s[...] = jnp.where(forced, FORCE, jnp.where(allowed, 2.0 * inner + last + prev, -1.0))
    sel_s[...] = jnp.zeros((n_sel, LN), f32)

    def pick(_, carry):
        s = sc_s[...]
        top = jnp.max(s, axis=0, keepdims=True)
        first = jnp.min(jnp.where(s == top, jrow_f, float(n_sel)), axis=0, keepdims=True)
        hit = jrow_f == first
        sc_s[...] = jnp.where(hit, -jnp.inf, s)
        sel_s[...] = jnp.where(hit, 1.0, sel_s[...])
        return carry

    lax.fori_loop(0, SEL_TOPN, pick, 0)
    sel_s[...] = jnp.where(allowed, sel_s[...], 0.0)

    def reset():
        m_s[...] = jnp.full((1, LN), M_INIT, f32)
        l_s[...] = jnp.zeros((1, LN), f32)
        acc_s[...] = jnp.zeros((HD, LN), f32)

    def update(s_blocks, pv_fn):
        m_old = m_s[...]
        m_new = m_old
        for s in s_blocks:
            m_new = jnp.maximum(m_new, jnp.max(s, axis=0, keepdims=True))
        alpha = jnp.exp2(m_old - m_new)
        e_blocks = [jnp.exp2(s - m_new) for s in s_blocks]
        l_new = l_s[...] * alpha
        for e in e_blocks:
            l_new = l_new + e.sum(axis=0, keepdims=True)
        pt = e_blocks[0] if len(e_blocks) == 1 else jnp.concatenate(e_blocks, axis=0)
        acc_s[...] = acc_s[...] * alpha + pv_fn(pt.astype(bf))
        m_s[...] = m_new
        l_s[...] = l_new

    def finish():
        return acc_s[...] * (1.0 / jnp.maximum(l_s[...], TINY))

    def scores(kv):
        return jnp.dot(kv[:, 0:half].astype(bf), qbd, preferred_element_type=f32)

    reset()
    pages_per_tile = SAMPLE_TILE // PAGE_SIZE
    blk_per_page = PAGE_SIZE // SEL_BLOCK
    blk_per_tile = SAMPLE_TILE // SEL_BLOCK

    def slc_body(kt, carry):
        selb = sel_s[pl.ds(pl.multiple_of(kt * blk_per_tile, blk_per_tile), blk_per_tile), :]
        blocks = []
        for j in range(pages_per_tile):
            s = scores_t(buf[slot, kt * pages_per_tile + j, 0])
            for h in range(blk_per_page):
                i = j * blk_per_page + h
                blocks.append(jnp.where(selb[i:i + 1, :] > 0.5, s[h * SEL_BLOCK:(h + 1) * SEL_BLOCK, :], NEG))

        def pv_pages(pt):
            full = jnp.zeros((2 * HD, LN), f32)
            for j in range(pages_per_tile):
                full = full + jnp.dot(buf[slot, kt * pages_per_tile + j, 1].astype(bf),
                                      pt[j * PAGE_SIZE:(j + 1) * PAGE_SIZE, :], preferred_element_type=f32)
            return group_rows(full)

        update(blocks, pv_pages)
        return carry

    lax.fori_loop(0, past // SAMPLE_TILE, slc_body, 0)
    rows_new = lax.broadcasted_iota(jnp.int32, (SEL_BLOCK, 1), 0)
    kv = nslc_ref[0]
    keep = (sel_s[past // SEL_BLOCK:past // SEL_BLOCK + 1, :] > 0.5) & (past + rows_new <= pos_l)
    update([jnp.where(keep, scores(kv), NEG)], functools.partial(pv, kv[:, half:]))
    o_slc = finish()

    reset()
    wb = wcache_ref.shape[-1]
    rel = pos_l - (past - wb + lax.broadcasted_iota(jnp.int32, (wb, 1), 0))
    update([jnp.where((rel >= 0) & (rel <= WINDOW), scores_t(wcache_ref[0, 0, 0]), NEG)],
           lambda pt: group_rows(jnp.dot(wcache_ref[0, 0, 1].astype(bf), pt, preferred_element_type=f32)))
    kv = nwin_ref[0]
    rel = pos_l - (past + rows_new)
    update([jnp.where((rel >= 0) & (rel <= WINDOW) & (rows_new < t_new), scores(kv), NEG)],
           functools.partial(pv, kv[:, half:]))
    o_win = finish()

    gate = jax.nn.sigmoid(gate_ref[0])
    o_ref[0] = gate[0:1, :] * o_cmp + gate[1:2, :] * o_slc + gate[2:3, :] * o_win


def _pad_rows(a, rows):
    return jnp.pad(a, ((0, 0), (0, rows - a.shape[1]), (0, 0)))


def _nsa_sample(proj, q, kv_cmp, kv_slc, kv_win, gates, pool_cmp, pool_slc, page_table, win_cache, layer, cmp_w):
    B, Tn = q.shape[:2]
    G, R, HD = NSA_KV_HEADS, NSA_R, NSA_HD
    n_pages = page_table.shape[1]
    n_pool = pool_cmp.shape[1]
    past = n_pages * PAGE_SIZE
    wb = win_cache.shape[2]
    assert Tn <= SQ and Tn <= SEL_BLOCK and past % SAMPLE_TILE == 0 and wb % SUBLANES == 0
    bf = jnp.bfloat16
    half = G * HD
    kvc = _compress_paged(page_table, _page_view(pool_cmp), layer, cmp_w)
    n_blk = past // CMP_STRIDE
    n_sel = _round_up(past // SEL_BLOCK + 1, BF16_SUBLANES)
    kvp = kvc.reshape(B, n_blk // CMP_PER_SEL, CMP_PER_SEL, NSA_KV_WIDTH).transpose(0, 2, 1, 3)
    kvp = jnp.pad(kvp, ((0, 0), (0, 0), (0, n_sel - n_blk // CMP_PER_SEL), (0, 0)))
    kvp = kvp.reshape(B, CMP_PER_SEL * n_sel, NSA_KV_WIDTH).astype(bf)
    kc, vc = kvp[..., :half], kvp[..., half:]
    qt = jnp.pad((q * QK_SCALE).reshape(B, Tn, G, R, HD), ((0, 0), (0, SQ - Tn), (0, 0), (0, 0), (0, 0)))
    qt = qt.transpose(0, 2, 4, 3, 1)
    qbd = jnp.einsum('bgdrq,gh->bgdhrq', qt, jnp.eye(G, dtype=jnp.float32)).reshape(B, half, SAMPLE_LANES).astype(bf)
    gate = jnp.pad(gates.reshape(B, Tn, G, R, 3), ((0, 0), (0, SQ - Tn), (0, 0), (0, 0), (0, 0)))
    gate = gate.transpose(0, 4, 2, 3, 1).reshape(B, 3, SAMPLE_LANES)
    nslc = _pad_rows(kv_slc.reshape(B, Tn, NSA_KV_WIDTH), SEL_BLOCK)
    nwin = _pad_rows(kv_win.reshape(B, Tn, NSA_KV_WIDTH), SEL_BLOCK)
    per_b = lambda shape: pl.BlockSpec((1,) + shape, lambda b, pt: (b,) + (0,) * len(shape))
    out = pl.pallas_call(
        functools.partial(_nsa_sample_kernel, page_base=layer * n_pool, n_pages=n_pages, n_sel=n_sel, t_new=Tn),
        grid_spec=pltpu.PrefetchScalarGridSpec(
            num_scalar_prefetch=1,
            grid=(B,),
            in_specs=[
                pl.BlockSpec(memory_space=pl.ANY),
                per_b((half, SAMPLE_LANES)), per_b((3, SAMPLE_LANES)),
                per_b((CMP_PER_SEL * n_sel, half)), per_b((CMP_PER_SEL * n_sel, half)),
                per_b((SEL_BLOCK, NSA_KV_WIDTH)),
                pl.BlockSpec((1, 1, 2, half, wb), lambda b, pt: (layer, b, 0, 0, 0)),
                per_b((SEL_BLOCK, NSA_KV_WIDTH)),
            ],
            out_specs=per_b((HD, SAMPLE_LANES)),
            scratch_shapes=[
                pltpu.VMEM((2, n_pages, 2, half, PAGE_SIZE), jnp.float32),
                pltpu.SemaphoreType.DMA((2,)),
                pltpu.VMEM((n_sel, SAMPLE_LANES), jnp.float32),
                pltpu.VMEM((n_sel, SAMPLE_LANES), jnp.float32),
                pltpu.VMEM((1, SAMPLE_LANES), jnp.float32),
                pltpu.VMEM((1, SAMPLE_LANES), jnp.float32),
                pltpu.VMEM((HD, SAMPLE_LANES), jnp.float32),
            ]),
        out_shape=jax.ShapeDtypeStruct((B, HD, SAMPLE_LANES), jnp.float32),
        compiler_params=pltpu.CompilerParams(
            dimension_semantics=("arbitrary",), vmem_limit_bytes=VMEM_LIMIT_BYTES),
        name="nsa_sample_attention",
    )(page_table, _page_view(pool_slc), qbd, gate, kc, vc, nslc,
      win_cache.transpose(0, 1, 3, 4, 5, 2).reshape(DEPTH, B, 2, half, wb), nwin)
    o = out.reshape(B, HD, G, R, SQ)[..., :Tn].transpose(0, 4, 2, 3, 1).reshape(B, Tn, NSA_WIDTH)
    win_all = jnp.concatenate([win_cache[layer], kv_win.astype(win_cache.dtype)], axis=1)
    return o, win_all[:, -min(WINDOW, wb + Tn):]


GLA_LANES = GLA_HEADS * GLA_DK
GLA_KERNEL_CHUNK = 32
GLA_BLOCK_ROWS = 512


def _head_block_mask(dtype):
    r = lax.broadcasted_iota(jnp.int32, (GLA_LANES, GLA_LANES), 0) // GLA_DK
    c = lax.broadcasted_iota(jnp.int32, (GLA_LANES, GLA_LANES), 1) // GLA_DK
    return (r == c).astype(dtype)


def _gla_kernel(q_ref, k_ref, v_ref, gg_ref, ga_ref, wa_ref, ba_ref, gn_ref, st0_ref, o_ref, st_ref,
                st_s, kp_s, bp_s, vp_s, *, c, n_chunks, valid_rows):
    f32, bf = jnp.float32, jnp.bfloat16

    @pl.when(pl.program_id(1) == 0)
    def _():
        st_s[...] = st0_ref[0]
        zeros = jnp.zeros((c, GLA_LANES), f32)
        kp_s[0:c, :] = zeros
        bp_s[0:c, :] = zeros
        vp_s[0:c, :] = zeros

    row = lax.broadcasted_iota(jnp.int32, (c, 1), 0)
    ones_blk = _head_block_mask(bf)
    blk_f32 = _head_block_mask(f32)
    contract_last = (((1,), (1,)), ((), ()))
    contract_rows = (((0,), (0,)), ((), ()))

    def head_sum(x):
        hi = x.astype(bf)
        lo = (x - hi.astype(f32)).astype(bf)
        return (jnp.dot(hi, ones_blk, preferred_element_type=f32)
                + jnp.dot(lo, ones_blk, preferred_element_type=f32))

    def chunk(ch, carry):
        rows = pl.ds(pl.multiple_of(ch * c, c), c)
        q = q_ref[rows, :] * (GLA_DK ** -0.5)
        k = k_ref[rows, :]
        v = v_ref[rows, :]
        z = jnp.dot(ga_ref[rows, :].astype(bf), wa_ref[...], preferred_element_type=f32) + ba_ref[...]
        la = (jnp.minimum(z, 0.0) - jnp.log1p(jnp.exp(-jnp.abs(z)))) / GLA_GATE_TEMP
        if valid_rows < c:
            la = jnp.where(row < valid_rows, la, 0.0)
        b = la
        step = 1
        while step < c:
            b = b + jnp.where(row >= step, pltpu.roll(b, step, 0), 0.0)
            step *= 2
        st = st_s[...]
        o = lax.dot_general((q * jnp.exp(b)).astype(bf), st.astype(bf), contract_last,
                            preferred_element_type=f32)
        kp_s[c:2 * c, :] = k
        bp_s[c:2 * c, :] = b
        vp_s[c:2 * c, :] = v
        terms = [q * k]
        for d in range(1, c):
            ok = row >= d
            kr = kp_s[c - d:2 * c - d, :]
            br = bp_s[c - d:2 * c - d, :]
            terms.append(jnp.where(ok, q * kr * jnp.exp(jnp.where(ok, b - br, 0.0)), 0.0))
        att = jnp.dot(jnp.concatenate(terms, axis=0).astype(bf), ones_blk, preferred_element_type=f32)
        for d in range(c):
            vr = v if d == 0 else vp_s[c - d:2 * c - d, :]
            o = o + att[d * c:(d + 1) * c, :] * vr
        ms = head_sum(o * o) * (1.0 / GLA_DV)
        g = gg_ref[rows, :]
        o_ref[rows, :] = o * lax.rsqrt(ms + EPS) * gn_ref[...] * (g * jax.nn.sigmoid(g))
        b_last = b[c - 1:c, :]
        ke = k * jnp.exp(b_last - b)
        upd = lax.dot_general(v.astype(bf), ke.astype(bf), contract_rows, preferred_element_type=f32)
        st_s[...] = st * jnp.exp(b_last) + upd * blk_f32
        return carry

    lax.fori_loop(0, n_chunks, chunk, 0, unroll=min(4, n_chunks))
    st_ref[0] = st_s[...]


def _gla(proj, nbatch, tb, c, valid_rows, wa_pad, ba, gn, st0):
    m = proj.shape[0]
    nblk = m // nbatch // tb
    colblk = lambda p, w: PROJ_OFF[p] // w
    row_map = lambda j: (lambda b, i: (b * nblk + i, j))
    return pl.pallas_call(
        functools.partial(_gla_kernel, c=c, n_chunks=tb // c, valid_rows=valid_rows),
        grid=(nbatch, nblk),
        in_specs=[
            pl.BlockSpec((tb, GLA_LANES), row_map(colblk(0, GLA_LANES))),
            pl.BlockSpec((tb, GLA_LANES), row_map(colblk(1, GLA_LANES))),
            pl.BlockSpec((tb, GLA_LANES), row_map(colblk(2, GLA_LANES))),
            pl.BlockSpec((tb, GLA_LANES), row_map(colblk(3, GLA_LANES))),
            pl.BlockSpec((tb, LANES), row_map(colblk(4, LANES))),
            pl.BlockSpec((LANES, GLA_LANES), lambda b, i: (0, 0)),
            pl.BlockSpec((1, GLA_LANES), lambda b, i: (0, 0)),
            pl.BlockSpec((1, GLA_LANES), lambda b, i: (0, 0)),
            pl.BlockSpec((1, GLA_LANES, GLA_LANES), lambda b, i: (b, 0, 0)),
        ],
        out_specs=[pl.BlockSpec((tb, GLA_LANES), lambda b, i: (b * nblk + i, 0)),
                   pl.BlockSpec((1, GLA_LANES, GLA_LANES), lambda b, i: (b, 0, 0))],
        out_shape=[jax.ShapeDtypeStruct((m, GLA_LANES), jnp.float32),
                   jax.ShapeDtypeStruct((nbatch, GLA_LANES, GLA_LANES), jnp.float32)],
        scratch_shapes=[pltpu.VMEM((GLA_LANES, GLA_LANES), jnp.float32),
                        pltpu.VMEM((2 * c, GLA_LANES), jnp.float32),
                        pltpu.VMEM((2 * c, GLA_LANES), jnp.float32),
                        pltpu.VMEM((2 * c, GLA_LANES), jnp.float32)],
        compiler_params=pltpu.CompilerParams(
            dimension_semantics=("arbitrary", "arbitrary"), vmem_limit_bytes=VMEM_LIMIT_BYTES),
        name="gla_scan",
    )(proj, proj, proj, proj, proj, wa_pad, ba, gn, st0)


def _gla_state_in(s0):
    eye = jnp.eye(GLA_HEADS, dtype=jnp.float32)
    return jnp.einsum('bhde,hg->bhegd', s0.astype(jnp.float32), eye).reshape(-1, GLA_LANES, GLA_LANES)


def _gla_state_out(st):
    blocks = [st[:, h * GLA_DV:(h + 1) * GLA_DV, h * GLA_DK:(h + 1) * GLA_DK] for h in range(GLA_HEADS)]
    return jnp.stack(blocks, axis=1).transpose(0, 1, 3, 2)


def _nsa_prompt(proj, q, kv_cmp, kv_slc, kv_win, gates, cmp_w):
    B, T = kv_cmp.shape[:2]
    assert B == 1 and T % SLC_TILE == 0 and T >= WIN_TILE
    n_rows = T // CMP_STRIDE
    kvc = _compress(kv_cmp.reshape(n_rows, CMP_ROW), cmp_w, min(256, n_rows))
    o = _nsa_prompt_attention(proj, kvc, kv_slc.reshape(T, NSA_KV_WIDTH), kv_win.reshape(T, NSA_KV_WIDTH))
    return o.reshape(B, T, NSA_WIDTH), kv_win[:, -min(WINDOW, T):]


def _expand_rows(v, t):
    if v.shape[0] == 1:
        return v
    return jnp.repeat(v, t, axis=0)


def _trunk_layer(x, mod, lw, gla_s0, sc_hist, ffn_hist, nsa_apply, tm, final, g_final):
    B, T, _ = x.shape
    m = B * T
    grouped = B > 1
    ssh1, ssc1, sgt1, ssh2, ssc2, sgt2 = [_expand_rows(v, T) for v in jnp.split(mod, 6, axis=-1)]
    x2 = x.reshape(m, D_MODEL)
    proj = _in_proj(x2, lw['norm_mix'], ssc1, ssh1, lw['w_in'], tm).reshape(B, T, PROJ_WIDTH)
    gq, gk, gv, gg, ga, sb, scc, shh, nq, ncmp, nslc, nwin, ngate = [_proj_piece(proj, p) for p in range(13)]
    heads = lambda a, d: a.reshape(B, T, -1, d)
    if grouped:
        t_pad = _round_up(T, SUBLANES)
        gla_in = jnp.pad(proj, ((0, 0), (0, t_pad - T), (0, 0))).reshape(B * t_pad, PROJ_WIDTH)
        tb = chunk = t_pad
    else:
        t_pad, gla_in, tb, chunk = T, proj.reshape(m, PROJ_WIDTH), GLA_BLOCK_ROWS, GLA_KERNEL_CHUNK
    o_gla, st_gla = _gla(gla_in, B, tb, chunk, T if grouped else chunk,
                         lw['gla_wa'], lw['gla_ba'], lw['gla_norm'], _gla_state_in(gla_s0))
    o_gla = o_gla.reshape(B, t_pad, GLA_WIDTH)[:, :T]
    s_gla = _gla_state_out(st_gla)
    kvr = lambda a: a.reshape(B, T, 2, NSA_KV_HEADS, NSA_HD)
    kv_cmp, kv_slc, kv_win = kvr(ncmp), kvr(nslc), kvr(nwin)
    proj2 = proj.reshape(m, PROJ_WIDTH)
    o_nsa, win_state = nsa_apply(proj2, heads(nq, NSA_HD), kv_cmp, kv_slc, kv_win, heads(ngate, 3))

    def conv_hist(h):
        c = h.shape[-1]
        if grouped:
            pad = jnp.zeros((B, T - 1, c), jnp.float32)
            h1 = jnp.concatenate([h[:, 1:2], pad], axis=1).reshape(m, c)
            h2 = jnp.concatenate([h[:, 0:2], pad[:, 1:]], axis=1).reshape(m, c)
            return (h1, h2)
        return jnp.concatenate([jnp.zeros((HIST_ROWS - (CONV_W - 1), c), jnp.float32), h[0]], axis=0)

    def conv_state(st):
        c = st.shape[-1]
        return st.reshape(B, T, c)[:, -(CONV_W - 1):] if grouped else st[None, -(CONV_W - 1):]

    group = T if grouped else 0
    x2, sc_st = _out_proj(x2, o_gla.reshape(m, GLA_WIDTH), proj2, o_nsa.reshape(m, NSA_WIDTH), sgt1,
                          lw['w_out'], lw['sc_conv'], conv_hist(sc_hist), tm, group)
    sc_state = conv_state(sc_st)
    y, st = _ffn(x2, lw['norm_ffn'], ssc2, ssh2, sgt2, lw['ffn_up'], lw['ffn_conv'], lw['ffn_down'],
                 g_final, conv_hist(ffn_hist), tm, group, final)
    ffn_state = conv_state(st)
    return (y.reshape(B, T, D_MODEL), kv_cmp, kv_slc, win_state, s_gla.astype(gla_s0.dtype), sc_state, ffn_state)


def kernel(x_prompt, x_sample, cache_nsa_cmp, cache_nsa_slc, cache_nsa_win, state_gla, state_shortconv, state_ffn_conv, page_table, c_prompt, c_sample, mod_w, mod_b, norm_mix, norm_ffn, w_in, gla_wa2, gla_ba, gla_norm, sc_conv, nsa_cmp_pos, nsa_cmp_w1, nsa_cmp_w2, w_out, ffn_up, ffn_conv, ffn_down, norm_final):
    xp, xs = x_prompt, x_sample
    bp, bs = xp.shape[0], xs.shape[0]
    assert bp == 1 and xs.shape[1] == 4
    c_rows = _round_up(bp + bs, SUBLANES)
    c_all = jnp.concatenate([c_prompt, c_sample, jnp.zeros((c_rows - bp - bs, D_MODEL), jnp.float32)], axis=0)
    mod_all = _modulation(c_all, mod_w, mod_b)
    g_final = norm_final.reshape(1, D_MODEL)
    outs = [[] for _ in range(12)]
    for l in range(DEPTH):
        lw = dict(
            norm_mix=norm_mix[l].reshape(1, D_MODEL), norm_ffn=norm_ffn[l].reshape(1, D_MODEL),
            w_in=_pack_w_in(w_in[l]),
            gla_wa=jnp.zeros((LANES, GLA_LANES), jnp.bfloat16).at[:GLA_GATE_RANK].set(
                gla_wa2[l].astype(jnp.bfloat16)),
            gla_ba=gla_ba[l].reshape(1, GLA_LANES),
            gla_norm=jnp.tile(gla_norm[l], GLA_HEADS).reshape(1, GLA_LANES),
            sc_conv=sc_conv[l], w_out=w_out[l].astype(jnp.bfloat16),
            ffn_up=ffn_up[l].astype(jnp.bfloat16), ffn_conv=ffn_conv[l],
            ffn_down=ffn_down[l].astype(jnp.bfloat16))
        cmp_params = (nsa_cmp_pos[l], nsa_cmp_w1[l], nsa_cmp_w2[l])
        cmp_w = _compress_weights(*cmp_params)
        final = l == DEPTH - 1
        res_p = _trunk_layer(
            xp, mod_all[l, 0:bp], lw,
            jnp.zeros((bp, GLA_HEADS, GLA_DK, GLA_DV), xp.dtype),
            jnp.zeros((bp, CONV_W - 1, SC_WIDTH), xp.dtype),
            jnp.zeros((bp, CONV_W - 1, 2 * D_FF), xp.dtype),
            functools.partial(_nsa_prompt, cmp_w=cmp_w), 512, final, g_final)
        res_s = _trunk_layer(
            xs, mod_all[l, bp:bp + bs], lw, state_gla[l], state_shortconv[l], state_ffn_conv[l],
            functools.partial(_nsa_sample, pool_cmp=cache_nsa_cmp, pool_slc=cache_nsa_slc,
                              page_table=page_table, win_cache=cache_nsa_win, layer=l,
                              cmp_w=cmp_w), bs * xs.shape[1], final, g_final)
        xp, xs = res_p[0], res_s[0]
        for k in range(6):
            outs[2 * k].append(res_p[k + 1])
            outs[2 * k + 1].append(res_s[k + 1])
    return (xp, xs) + tuple(jnp.stack(o) for o in outs)
```
